```python
import jax, jax.numpy as jnp
from jax import lax
import numpy as np

D_MODEL = 1024
BATCH = 8
SEQ = 4096
DEPTH = 1

MEM_LEN = 256
LRU_WIDTH = 512
LRU_BLOCKS = 8
LRU_BLOCK_DIM = LRU_WIDTH // LRU_BLOCKS
CONV_WIDTH = 4
LRU_C = 8.0
FOX_HEADS = 8
FOX_HEAD_DIM = 64
FOX_WIDTH = FOX_HEADS * FOX_HEAD_DIM
Q_BLOCK = 128
MIX_WIDTH = LRU_WIDTH + FOX_WIDTH
IN_COLS = 2 * LRU_WIDTH + 3 * FOX_WIDTH + FOX_HEADS
MEM_HEADS = 4
MEM_HEAD_DIM = 128
MEM_WIDTH = MEM_HEADS * MEM_HEAD_DIM
N_GROUPS = 4
EXPERTS_PER_GROUP = 8
N_EXPERTS = N_GROUPS * EXPERTS_PER_GROUP
TOP_K = 2
D_EXPERT = 512
MOE_BLOCK = 128
EPS = 1e-6

kernel_name = 'hymba_rglru_fox_hmoe_layer'


def rmsnorm(x, g):
    xf = x.astype(jnp.float32)
    y = xf * lax.rsqrt(jnp.mean(xf * xf, axis=-1, keepdims=True) + EPS)
    return (y * g.astype(jnp.float32)).astype(x.dtype)


def rg_lru_group(u, gate_in, conv_w, conv_b, wa, ba, wx, bx, a_param):
    B, S, _ = u.shape
    up = jnp.pad(u, ((0, 0), (CONV_WIDTH - 1, 0), (0, 0)))
    xc = conv_b
    for tap in range(CONV_WIDTH):
        xc = xc + up[:, tap:tap + S] * conv_w[tap]
    xb = xc.reshape(B, S, LRU_BLOCKS, LRU_BLOCK_DIM)
    r = jax.nn.sigmoid(jnp.einsum('bsni,nij->bsnj', xb, wa).reshape(B, S, LRU_WIDTH) + ba)
    i = jax.nn.sigmoid(jnp.einsum('bsni,nij->bsnj', xb, wx).reshape(B, S, LRU_WIDTH) + bx)
    log_a = -LRU_C * r.astype(jnp.float32) * jax.nn.softplus(-a_param.astype(jnp.float32))
    a = jnp.exp(log_a)
    mult = jnp.sqrt(jnp.maximum(-jnp.expm1(2.0 * log_a), 0.0))
    b = mult * (i * xc).astype(jnp.float32)

    def combine(lhs, rhs):
        a1, b1 = lhs
        a2, b2 = rhs
        return a1 * a2, a2 * b1 + b2

    _, h = lax.associative_scan(combine, (a, b), axis=1)
    return (h * jax.nn.gelu(gate_in.astype(jnp.float32))).astype(u.dtype)


def forgetting_attention(q, k, v, log_f):
    B, S, H, Dh = q.shape
    c = jnp.cumsum(log_f, axis=1).transpose(0, 2, 1)
    scale = Dh ** -0.5
    outs = []
    for blk in range(S // Q_BLOCK):
        q0 = blk * Q_BLOCK
        q1 = q0 + Q_BLOCK
        s = jnp.einsum('bqhd,bkhd->bhqk', q[:, q0:q1], k[:, :q1]).astype(jnp.float32) * scale
        s = s + c[:, :, q0:q1, None] - c[:, :, None, :q1]
        qpos = q0 + jnp.arange(Q_BLOCK)[:, None]
        kpos = jnp.arange(q1)[None, :]
        s = jnp.where(kpos <= qpos, s, -jnp.inf)
        p = jax.nn.softmax(s, axis=-1)
        outs.append(jnp.einsum('bhqk,bkhd->bqhd', p.astype(v.dtype), v[:, :q1]))
    return jnp.concatenate(outs, axis=1)


def memory_cross_attention(xn, memn, wq, wkv, q_g, k_g, wo):
    B, S, _ = xn.shape
    M = memn.shape[1]
    q = (xn @ wq).reshape(B, S, MEM_HEADS, MEM_HEAD_DIM)
    kv = memn @ wkv
    k = kv[..., :MEM_WIDTH].reshape(B, M, MEM_HEADS, MEM_HEAD_DIM)
    v = kv[..., MEM_WIDTH:].reshape(B, M, MEM_HEADS, MEM_HEAD_DIM)
    q = rmsnorm(q, q_g)
    k = rmsnorm(k, k_g)
    s = jnp.einsum('bqhd,bkhd->bhqk', q, k).astype(jnp.float32) * (MEM_HEAD_DIM ** -0.5)
    p = jax.nn.softmax(s, axis=-1)
    o = jnp.einsum('bhqk,bkhd->bqhd', p.astype(v.dtype), v).reshape(B, S, MEM_WIDTH)
    return o @ wo


def hierarchical_moe(xn, wg, bg, we, be, w_gate, w_up, w_down):
    B, S, D = xn.shape
    T = B * S
    xt = xn.reshape(T, D)
    g_logits = (xt @ wg).astype(jnp.float32) + bg
    g_prob = jax.nn.softmax(g_logits, axis=-1)
    g_idx = jnp.argmax(g_logits, axis=-1)
    g_w = jnp.take_along_axis(g_prob, g_idx[:, None], axis=1)[:, 0]
    e_logits = ((xt @ we).astype(jnp.float32) + be).reshape(T, N_GROUPS, EXPERTS_PER_GROUP)
    e_logits = jnp.take_along_axis(e_logits, g_idx[:, None, None], axis=1)[:, 0]
    e_prob = jax.nn.softmax(e_logits, axis=-1)
    top_p, top_i = lax.top_k(e_prob, TOP_K)
    top_p = top_p / jnp.sum(top_p, axis=-1, keepdims=True)
    gate = g_w[:, None] * top_p
    expert = g_idx[:, None] * EXPERTS_PER_GROUP + top_i
    A = T * TOP_K
    e_flat = expert.reshape(A).astype(jnp.int32)
    tok_flat = jnp.repeat(jnp.arange(T, dtype=jnp.int32), TOP_K)
    w_flat = gate.reshape(A)
    order = jnp.argsort(e_flat)
    e_s = e_flat[order]
    tok_s = tok_flat[order]
    w_s = w_flat[order]
    counts = jnp.zeros((N_EXPERTS,), jnp.int32).at[e_flat].add(1)
    starts = jnp.cumsum(counts) - counts
    padded = (counts + MOE_BLOCK - 1) // MOE_BLOCK * MOE_BLOCK
    pstarts = jnp.cumsum(padded) - padded
    pends = pstarts + padded
    dest = pstarts[e_s] + (jnp.arange(A, dtype=jnp.int32) - starts[e_s])
    P = A + N_EXPERTS * MOE_BLOCK
    n_blocks = P // MOE_BLOCK
    tok_buf = jnp.zeros((P,), jnp.int32).at[dest].set(tok_s)
    w_buf = jnp.zeros((P,), jnp.float32).at[dest].set(w_s)
    blk_start = jnp.arange(n_blocks, dtype=jnp.int32) * MOE_BLOCK
    blk_expert = jnp.minimum(jnp.sum(blk_start[:, None] >= pends[None, :], axis=1), N_EXPERTS - 1)
    x_buf = xt[tok_buf].reshape(n_blocks, MOE_BLOCK, D)

    def run_block(args):
        xb, e = args
        hdn = jax.nn.silu(xb @ w_gate[e]) * (xb @ w_up[e])
        return hdn @ w_down[e]

    y_buf = lax.map(run_block, (x_buf, blk_expert)).reshape(P, D)
    y = jax.ops.segment_sum(y_buf * w_buf[:, None].astype(y_buf.dtype), tok_buf, num_segments=T)
    return y.reshape(B, S, D)


def setup_inputs(seed: int = 0) -> dict:
    key = jax.random.key(seed)
    ks = jax.random.split(key, 32)
    f32 = jnp.float32
    L = DEPTH

    def nrm(k, shape, scale):
        return jax.random.normal(k, shape, f32) * scale

    def gain(k, shape):
        return 1.0 + 0.01 * jax.random.normal(k, shape, f32)

    a_c = jax.random.uniform(ks[10], (L, LRU_WIDTH), f32, minval=0.9, maxval=0.999)
    sig = a_c ** (1.0 / LRU_C)
    a_param = jnp.log(sig) - jnp.log1p(-sig)
    return {
        'x': jax.random.normal(ks[0], (BATCH, SEQ, D_MODEL), f32),
        'mem': jax.random.normal(ks[1], (BATCH, MEM_LEN, D_MODEL), f32),
        'norm_mix_g': gain(ks[2], (L, D_MODEL)),
        'w_in': nrm(ks[3], (L, D_MODEL, IN_COLS), D_MODEL ** -0.5),
        'b_forget': jax.random.uniform(ks[4], (L, FOX_HEADS), f32, minval=1.0, maxval=5.0),
        'conv_w': nrm(ks[5], (L, CONV_WIDTH, LRU_WIDTH), CONV_WIDTH ** -0.5),
        'conv_b': nrm(ks[6], (L, LRU_WIDTH), 0.01),
        'lru_wa': nrm(ks[7], (L, LRU_BLOCKS, LRU_BLOCK_DIM, LRU_BLOCK_DIM), LRU_BLOCK_DIM ** -0.5),
        'lru_ba': nrm(ks[8], (L, LRU_WIDTH), 0.01),
        'lru_wx': nrm(ks[9], (L, LRU_BLOCKS, LRU_BLOCK_DIM, LRU_BLOCK_DIM), LRU_BLOCK_DIM ** -0.5),
        'lru_bx': nrm(ks[11], (L, LRU_WIDTH), 0.01),
        'lru_a_param': a_param,
        'fox_q_g': gain(ks[12], (L, FOX_HEAD_DIM)),
        'fox_k_g': gain(ks[13], (L, FOX_HEAD_DIM)),
        'lru_out_g': gain(ks[14], (L, LRU_WIDTH)),
        'fox_out_g': gain(ks[15], (L, FOX_WIDTH)),
        'w_out': nrm(ks[16], (L, MIX_WIDTH, D_MODEL), MIX_WIDTH ** -0.5),
        'norm_mem_x_g': gain(ks[17], (L, D_MODEL)),
        'norm_mem_g': gain(ks[18], (L, D_MODEL)),
        'mem_wq': nrm(ks[19], (L, D_MODEL, MEM_WIDTH), D_MODEL ** -0.5),
        'mem_wkv': nrm(ks[20], (L, D_MODEL, 2 * MEM_WIDTH), D_MODEL ** -0.5),
        'mem_q_g': gain(ks[21], (L, MEM_HEAD_DIM)),
        'mem_k_g': gain(ks[22], (L, MEM_HEAD_DIM)),
        'mem_wo': nrm(ks[23], (L, MEM_WIDTH, D_MODEL), MEM_WIDTH ** -0.5),
        'norm_ffn_g': gain(ks[24], (L, D_MODEL)),
        'router_group_w': nrm(ks[25], (L, D_MODEL, N_GROUPS), D_MODEL ** -0.5),
        'router_group_b': nrm(ks[26], (L, N_GROUPS), 0.01),
        'router_expert_w': nrm(ks[27], (L, D_MODEL, N_EXPERTS), D_MODEL ** -0.5),
        'router_expert_b': nrm(ks[28], (L, N_EXPERTS), 0.01),
        'exp_w_gate': nrm(ks[29], (L, N_EXPERTS, D_MODEL, D_EXPERT), D_MODEL ** -0.5),
        'exp_w_up': nrm(ks[30], (L, N_EXPERTS, D_MODEL, D_EXPERT), D_MODEL ** -0.5),
        'exp_w_down': nrm(ks[31], (L, N_EXPERTS, D_EXPERT, D_MODEL), D_EXPERT ** -0.5),
    }


def reference(x, mem, norm_mix_g, w_in, b_forget, conv_w, conv_b, lru_wa, lru_ba, lru_wx, lru_bx,
              lru_a_param, fox_q_g, fox_k_g, lru_out_g, fox_out_g, w_out, norm_mem_x_g, norm_mem_g,
              mem_wq, mem_wkv, mem_q_g, mem_k_g, mem_wo, norm_ffn_g, router_group_w, router_group_b,
              router_expert_w, router_expert_b, exp_w_gate, exp_w_up, exp_w_down):
    B, S, _ = x.shape
    cuts = [LRU_WIDTH, 2 * LRU_WIDTH, 2 * LRU_WIDTH + FOX_WIDTH,
            2 * LRU_WIDTH + 2 * FOX_WIDTH, 2 * LRU_WIDTH + 3 * FOX_WIDTH]
    for l in range(DEPTH):
        h = rmsnorm(x, norm_mix_g[l])
        proj = h @ w_in[l]
        u_lru, g_lru, q, k, v, f_logit = jnp.split(proj, cuts, axis=-1)
        y_lru = rg_lru_group(u_lru, g_lru, conv_w[l], conv_b[l], lru_wa[l], lru_ba[l],
                             lru_wx[l], lru_bx[l], lru_a_param[l])
        q = rmsnorm(q.reshape(B, S, FOX_HEADS, FOX_HEAD_DIM), fox_q_g[l])
        k = rmsnorm(k.reshape(B, S, FOX_HEADS, FOX_HEAD_DIM), fox_k_g[l])
        v = v.reshape(B, S, FOX_HEADS, FOX_HEAD_DIM)
        log_f = jax.nn.log_sigmoid(f_logit.astype(jnp.float32) + b_forget[l])
        y_fox = forgetting_attention(q, k, v, log_f).reshape(B, S, FOX_WIDTH)
        mixed = jnp.concatenate([rmsnorm(y_lru, lru_out_g[l]), rmsnorm(y_fox, fox_out_g[l])], axis=-1)
        x = x + mixed @ w_out[l]
        memn = rmsnorm(mem, norm_mem_g[l])
        x = x + memory_cross_attention(rmsnorm(x, norm_mem_x_g[l]), memn, mem_wq[l], mem_wkv[l],
                                       mem_q_g[l], mem_k_g[l], mem_wo[l])
        x = x + hierarchical_moe(rmsnorm(x, norm_ffn_g[l]), router_group_w[l], router_group_b[l],
                                 router_expert_w[l], router_expert_b[l], exp_w_gate[l],
                                 exp_w_up[l], exp_w_down[l])
    return x
```

```python
import functools

import jax
import jax.numpy as jnp
import numpy as np
from jax import lax
from jax.experimental import pallas as pl
from jax.experimental.pallas import tpu as pltpu

EPS = 1e-6
LRU_C = 8.0
CONV_WIDTH = 4
TOP_K = 2
LANES = 128
SUBLANES = 8
MXU_DIM = 256
HEAD_SLOT = LANES
VMEM_LIMIT = 56 * 1024 * 1024

F32 = jnp.float32
BF16 = jnp.bfloat16


def _dot(a, b):
    return jnp.dot(a, b, preferred_element_type=F32)


def _rms(x, g):
    return x * lax.rsqrt(jnp.mean(x * x, axis=-1, keepdims=True) + EPS) * g


def _softplus(x):
    return jnp.maximum(x, 0.0) + jnp.log1p(jnp.exp(-jnp.abs(x)))


def _pick_tile(n, target):
    t = min(n, target)
    while n % t:
        t //= 2
    return t


def _in_proj_kernel(x_ref, g_ref, w_ref, wf_ref, bf_ref, gq_ref, gk_ref, pe_ref, cq_ref, ck_ref,
                    cv_ref, u_ref, gate_ref, q_ref, k_ref, v_ref, carry_ref, *, lw, n_heads, dh):
    @pl.when(pl.program_id(1) == 0)
    def _():
        carry_ref[...] = jnp.zeros_like(carry_ref)

    hs = n_heads * HEAD_SLOT
    x = x_ref[0]
    tm = x.shape[0]
    hb = _rms(x, g_ref[...]).astype(BF16)

    u_ref[0] = _dot(hb, w_ref[:, 0:lw])
    gate_ref[0] = _dot(hb, w_ref[:, lw:2 * lw])

    z = _dot(hb, wf_ref[...]) + bf_ref[...]
    lane = lax.broadcasted_iota(jnp.int32, z.shape, 1)
    row = lax.broadcasted_iota(jnp.int32, z.shape, 0)
    c = jnp.where(lane < n_heads, -_softplus(-z), 0.0)
    d = 1
    while d < tm:
        c = c + jnp.where(row >= d, pltpu.roll(c, d, 0), 0.0)
        d *= 2
    c = c + carry_ref[...]
    carry_ref[...] = c[tm - 1:tm, :]
    c1 = c.astype(BF16).astype(F32)
    r1 = c - c1
    c2 = r1.astype(BF16).astype(F32)
    c3 = (r1 - c2).astype(BF16).astype(F32)
    e = (c1 + pltpu.roll(c2, n_heads, 1) + pltpu.roll(c3, 2 * n_heads, 1)).astype(BF16)
    ext = _dot(e, pe_ref[...])

    def head_norm(y, gain):
        parts = []
        for h in range(n_heads):
            blk = y[:, h * HEAD_SLOT:(h + 1) * HEAD_SLOT]
            ss = jnp.sum(blk * blk, axis=-1, keepdims=True) * (1.0 / dh)
            parts.append(blk * lax.rsqrt(ss + EPS))
        return jnp.concatenate(parts, axis=-1) * gain

    q = _dot(hb, w_ref[:, 2 * lw:2 * lw + hs])
    q_ref[0] = (head_norm(q, gq_ref[...]) + ext[:, 0:hs] + cq_ref[...]).astype(BF16)
    k = _dot(hb, w_ref[:, 2 * lw + hs:2 * lw + 2 * hs])
    k_ref[0] = (head_norm(k, gk_ref[...]) + ext[:, hs:2 * hs] + ck_ref[...]).astype(BF16)
    v = _dot(hb, w_ref[:, 2 * lw + 2 * hs:2 * lw + 3 * hs])
    v_ref[0] = (v + cv_ref[...]).astype(BF16)


def _slot_cols(w, n_heads, dh):
    d = w.shape[0]
    w = w.reshape(d, n_heads, dh)
    w = jnp.pad(w, ((0, 0), (0, 0), (0, HEAD_SLOT - dh)))
    return w.reshape(d, n_heads * HEAD_SLOT)


def _in_proj(x, norm_g, w_in, b_forget, fox_q_g, fox_k_g, *, lw, n_heads, dh, tm):
    B, S, D = x.shape
    fw = n_heads * dh
    hs = n_heads * HEAD_SLOT
    cuts = np.cumsum([lw, lw, fw, fw, fw])
    w_u, w_g, w_q, w_k, w_v, w_f = jnp.split(w_in, cuts, axis=1)
    w_all = jnp.concatenate(
        [w_u, w_g, _slot_cols(w_q, n_heads, dh), _slot_cols(w_k, n_heads, dh),
         _slot_cols(w_v, n_heads, dh)], axis=1).astype(BF16)
    w_f = jnp.pad(w_f, ((0, 0), (0, LANES - n_heads))).astype(BF16)
    b_f = jnp.pad(b_forget, (0, LANES - n_heads)).reshape(1, LANES)
    scale = dh ** -0.5
    gq = jnp.tile(jnp.pad(fox_q_g * scale, (0, HEAD_SLOT - dh)), n_heads).reshape(1, hs)
    gk = jnp.tile(jnp.pad(fox_k_g, (0, HEAD_SLOT - dh)), n_heads).reshape(1, hs)
    pe = np.zeros((LANES, 2 * hs), np.float32)
    cq = np.zeros((1, hs), np.float32)
    ck = np.zeros((1, hs), np.float32)
    cv = np.zeros((1, hs), np.float32)
    for h in range(n_heads):
        for j in range(3):
            pe[j * n_heads + h, h * HEAD_SLOT + dh + j] = 1.0
            pe[j * n_heads + h, hs + h * HEAD_SLOT + dh + 3 + j] = -1.0
            cq[0, h * HEAD_SLOT + dh + 3 + j] = 1.0
            ck[0, h * HEAD_SLOT + dh + j] = 1.0
        cv[0, h * HEAD_SLOT + dh] = 1.0
    pe = jnp.asarray(pe, BF16)

    full = lambda shape: pl.BlockSpec(shape, lambda b, s: (0,) * len(shape))
    row = lambda width: pl.BlockSpec((1, tm, width), lambda b, s: (b, s, 0))
    return pl.pallas_call(
        functools.partial(_in_proj_kernel, lw=lw, n_heads=n_heads, dh=dh),
        grid=(B, S // tm),
        in_specs=[row(D), full((1, D)), full(w_all.shape), full(w_f.shape), full((1, LANES)),
                  full((1, hs)), full((1, hs)), full(pe.shape), full((1, hs)), full((1, hs)),
                  full((1, hs))],
        out_specs=[row(lw), row(lw), row(hs), row(hs), row(hs)],
        out_shape=[jax.ShapeDtypeStruct((B, S, lw), F32), jax.ShapeDtypeStruct((B, S, lw), F32),
                   jax.ShapeDtypeStruct((B, S, hs), BF16), jax.ShapeDtypeStruct((B, S, hs), BF16),
                   jax.ShapeDtypeStruct((B, S, hs), BF16)],
        scratch_shapes=[pltpu.VMEM((1, LANES), F32)],
        compiler_params=pltpu.CompilerParams(
            dimension_semantics=("parallel", "arbitrary"), vmem_limit_bytes=VMEM_LIMIT),
        name="in_proj",
    )(x, norm_g.reshape(1, D), w_all, w_f, b_f, gq, gk, pe, jnp.asarray(cq), jnp.asarray(ck),
      jnp.asarray(cv))


def _rg_lru_kernel(u_ref, gate_ref, cw_ref, cb_ref, w_ref, ba_ref, bx_ref, ap_ref, og_ref,
                   o_ref, tail_ref, h_ref):
    @pl.when(pl.program_id(1) == 0)
    def _():
        tail_ref[...] = jnp.zeros_like(tail_ref)
        h_ref[...] = jnp.zeros_like(h_ref)

    u = u_ref[0]
    ts, lw = u.shape
    ext = jnp.concatenate([tail_ref[...], u], axis=0)
    tail_ref[...] = u[ts - SUBLANES:, :]
    xc = cb_ref[...] + u * cw_ref[CONV_WIDTH - 1:CONV_WIDTH, :]
    for back in range(1, CONV_WIDTH):
        tap = CONV_WIDTH - 1 - back
        xc = xc + pltpu.roll(ext, back, 0)[SUBLANES:, :] * cw_ref[tap:tap + 1, :]

    xb = xc.astype(BF16)
    r_parts, i_parts = [], []
    for j in range(lw // MXU_DIM):
        y = _dot(xb[:, j * MXU_DIM:(j + 1) * MXU_DIM], w_ref[j])
        r_parts.append(y[:, :MXU_DIM])
        i_parts.append(y[:, MXU_DIM:])
    r = jax.nn.sigmoid(jnp.concatenate(r_parts, axis=-1) + ba_ref[...])
    i = jax.nn.sigmoid(jnp.concatenate(i_parts, axis=-1) + bx_ref[...])
    log_a = (-LRU_C) * r * _softplus(-ap_ref[...])
    a = jnp.exp(log_a)
    mult = jnp.sqrt(jnp.maximum(-jnp.tanh(log_a) * (a * a + 1.0), 0.0))
    b = mult * (i * xc)

    row = lax.broadcasted_iota(jnp.int32, a.shape, 0)
    d = 1
    while d < ts:
        keep = row >= d
        a_prev = jnp.where(keep, pltpu.roll(a, d, 0), 1.0)
        b_prev = jnp.where(keep, pltpu.roll(b, d, 0), 0.0)
        b = a * b_prev + b
        a = a * a_prev
        d *= 2
    h = b + a * h_ref[...]
    h_ref[...] = h[ts - 1:ts, :]

    y = h * jax.nn.gelu(gate_ref[0])
    o_ref[0] = _rms(y, og_ref[...]).astype(o_ref.dtype)


def _rg_lru(u, gate, conv_w, conv_b, wa, ba, wx, bx, a_param, out_g, *, ts):
    B, S, lw = u.shape
    nb, bd, _ = wa.shape
    per = MXU_DIM // bd
    n_tiles = lw // MXU_DIM

    def tiles(w):
        w = w.reshape(n_tiles, per, bd, bd)
        eye = jnp.eye(per, dtype=w.dtype)
        return jnp.einsum('tpij,pq->tpiqj', w, eye).reshape(n_tiles, MXU_DIM, MXU_DIM)

    w_cat = jnp.concatenate([tiles(wa), tiles(wx)], axis=-1).astype(BF16)
    vec = lambda a: a.reshape(1, lw)
    full = lambda shape: pl.BlockSpec(shape, lambda b, s: (0,) * len(shape))
    row = pl.BlockSpec((1, ts, lw), lambda b, s: (b, s, 0))
    return pl.pallas_call(
        _rg_lru_kernel,
        grid=(B, S // ts),
        in_specs=[row, row, full((CONV_WIDTH, lw)), full((1, lw)), full(w_cat.shape), full((1, lw)),
                  full((1, lw)), full((1, lw)), full((1, lw))],
        out_specs=row,
        out_shape=jax.ShapeDtypeStruct((B, S, lw), BF16),
        scratch_shapes=[pltpu.VMEM((SUBLANES, lw), F32), pltpu.VMEM((1, lw), F32)],
        compiler_params=pltpu.CompilerParams(
            dimension_semantics=("parallel", "arbitrary"), vmem_limit_bytes=VMEM_LIMIT),
        name="rg_lru",
    )(u, gate, conv_w, vec(conv_b), w_cat, vec(ba), vec(bx), vec(a_param), vec(out_g))


def _fox_attn_kernel(q_ref, k_ref, v_ref, o_ref, m_ref, acc_ref, *, dh):
    qi = pl.program_id(2)
    ki = pl.program_id(3)

    @pl.when(ki == 0)
    def _():
        m_ref[...] = jnp.full_like(m_ref, -jnp.inf)
        acc_ref[...] = jnp.zeros_like(acc_ref)

    def step(masked):
        for hh in range(2):
            sl = slice(hh * HEAD_SLOT, (hh + 1) * HEAD_SLOT)
            s = lax.dot_general(q_ref[0, :, sl], k_ref[0, :, sl], (((1,), (1,)), ((), ())),
                                preferred_element_type=F32)
            if masked:
                qpos = lax.broadcasted_iota(jnp.int32, s.shape, 0)
                kpos = lax.broadcasted_iota(jnp.int32, s.shape, 1)
                s = jnp.where(kpos <= qpos, s, -jnp.inf)
            m_old = m_ref[hh]
            m_new = jnp.maximum(m_old, jnp.max(s, axis=-1, keepdims=True))
            p = jnp.exp(s - m_new)
            acc_ref[hh] = jnp.exp(m_old - m_new) * acc_ref[hh] + _dot(p.astype(BF16), v_ref[0, :, sl])
            m_ref[hh] = m_new

    @pl.when(ki < qi)
    def _():
        step(False)

    @pl.when(ki == qi)
    def _():
        step(True)
        outs = []
        for hh in range(2):
            acc = acc_ref[hh]
            outs.append(acc / acc[:, dh:dh + 1])
        lane = lax.broadcasted_iota(jnp.int32, outs[0].shape, 1)
        o_ref[0] = jnp.where(lane < dh, outs[0], pltpu.roll(outs[1], dh, 1)).astype(o_ref.dtype)


def _fox_attn(q, k, v, *, n_heads, dh, tq):
    B, S, _ = q.shape
    nq = S // tq
    pair = 2 * HEAD_SLOT
    q_spec = pl.BlockSpec((1, tq, pair), lambda b, h, i, j: (b, i, h))
    kv_spec = pl.BlockSpec((1, tq, pair), lambda b, h, i, j: (b, jnp.minimum(i, j), h))
    return pl.pallas_call(
        functools.partial(_fox_attn_kernel, dh=dh),
        grid=(B, n_heads // 2, nq, nq),
        in_specs=[q_spec, kv_spec, kv_spec],
        out_specs=pl.BlockSpec((1, tq, 2 * dh), lambda b, h, i, j: (b, i, h)),
        out_shape=jax.ShapeDtypeStruct((B, S, n_heads * dh), F32),
        scratch_shapes=[pltpu.VMEM((2, tq, 1), F32), pltpu.VMEM((2, tq, HEAD_SLOT), F32)],
        compiler_params=pltpu.CompilerParams(
            dimension_semantics=("parallel", "parallel", "parallel", "arbitrary"),
            vmem_limit_bytes=VMEM_LIMIT),
        name="fox_attn",
    )(q, k, v)


def _mem_kv_kernel(m_ref, g_ref, w_ref, kg_ref, k_ref, v_ref, *, n_heads, dh):
    mw = n_heads * dh
    mb = _rms(m_ref[0], g_ref[...]).astype(BF16)
    kv = _dot(mb, w_ref[...])
    parts = []
    for h in range(n_heads):
        parts.append(_rms(kv[:, h * dh:(h + 1) * dh], kg_ref[...]))
    k_ref[0] = jnp.concatenate(parts, axis=-1).astype(BF16)
    v_ref[0] = kv[:, mw:].astype(BF16)


def _mem_kv(mem, norm_g, wkv, k_g, *, n_heads, dh):
    B, M, D = mem.shape
    mw = n_heads * dh
    full = lambda shape: pl.BlockSpec(shape, lambda b: (0,) * len(shape))
    out = pl.BlockSpec((1, M, mw), lambda b: (b, 0, 0))
    return pl.pallas_call(
        functools.partial(_mem_kv_kernel, n_heads=n_heads, dh=dh),
        grid=(B,),
        in_specs=[pl.BlockSpec((1, M, D), lambda b: (b, 0, 0)), full((1, D)), full((D, 2 * mw)),
                  full((1, dh))],
        out_specs=[out, out],
        out_shape=[jax.ShapeDtypeStruct((B, M, mw), BF16)] * 2,
        compiler_params=pltpu.CompilerParams(
            dimension_semantics=("parallel",), vmem_limit_bytes=VMEM_LIMIT),
        name="mem_kv",
    )(mem, norm_g.reshape(1, D), wkv.astype(BF16), k_g.reshape(1, dh))


def _out_mem_kernel(x_ref, yl_ref, yf_ref, fg_ref, wol_ref, wof_ref, gx_ref, wq_ref, qg_ref,
                    km_ref, vm_ref, wo_ref, gf_ref, wr_ref, br_ref,
                    x2_ref, xn_ref, gates_ref, experts_ref, *, n_heads, dh, n_groups, per_group):
    yf = _rms(yf_ref[0], fg_ref[...]).astype(BF16)
    x1 = x_ref[0] + _dot(yl_ref[0], wol_ref[...]) + _dot(yf, wof_ref[...])

    q = _dot(_rms(x1, gx_ref[...]).astype(BF16), wq_ref[...])
    outs = []
    for h in range(n_heads):
        sl = slice(h * dh, (h + 1) * dh)
        qh = _rms(q[:, sl], qg_ref[...]).astype(BF16)
        s = lax.dot_general(qh, km_ref[0, :, sl], (((1,), (1,)), ((), ())),
                            preferred_element_type=F32)
        p = jnp.exp(s - jnp.max(s, axis=-1, keepdims=True))
        p = p / jnp.sum(p, axis=-1, keepdims=True)
        outs.append(_dot(p.astype(BF16), vm_ref[0, :, sl]))
    x2 = x1 + _dot(jnp.concatenate(outs, axis=-1).astype(BF16), wo_ref[...])
    x2_ref[0] = x2

    xn = _rms(x2, gf_ref[...])
    xn_ref[0] = xn
    logits = _dot(xn.astype(BF16), wr_ref[...]) + br_ref[...]
    lane = lax.broadcasted_iota(jnp.int32, logits.shape, 1).astype(F32)
    neg = -jnp.inf

    def top(vals):
        mx = jnp.max(vals, axis=-1, keepdims=True)
        idx = jnp.min(jnp.where(vals == mx, lane, float(LANES)), axis=-1, keepdims=True)
        return mx, idx

    gl = jnp.where(lane < n_groups, logits, neg)
    g_max, g_idx = top(gl)
    g_w = 1.0 / jnp.sum(jnp.exp(gl - g_max), axis=-1, keepdims=True)
    lo = n_groups + per_group * g_idx
    el = jnp.where((lane >= lo) & (lane < lo + per_group), logits, neg)
    e1, i1 = top(el)
    e2, i2 = top(jnp.where(lane == i1, neg, el))
    t = jnp.exp(e2 - e1)
    w1 = g_w / (1.0 + t)
    w2 = g_w * t / (1.0 + t)
    gates_ref[0] = jnp.where(lane == 0, w1, jnp.where(lane == 1, w2, 0.0))
    experts_ref[0] = jnp.where(lane == 0, i1 - n_groups,
                               jnp.where(lane == 1, i2 - n_groups, 0.0)).astype(jnp.int32)


def _out_mem(x, y_lru, y_fox, fox_out_g, w_out, norm_mem_x_g, mem_wq, mem_q_g, k_mem, v_mem, mem_wo,
             norm_ffn_g, router_group_w, router_group_b, router_expert_w, router_expert_b, *,
             n_heads, dh, tm):
    B, S, D = x.shape
    lw = y_lru.shape[-1]
    fw = y_fox.shape[-1]
    M = k_mem.shape[1]
    mw = n_heads * dh
    n_groups = router_group_w.shape[-1]
    n_experts = router_expert_w.shape[-1]
    w_r = jnp.concatenate([router_group_w, router_expert_w], axis=1)
    w_r = jnp.pad(w_r, ((0, 0), (0, LANES - w_r.shape[1]))).astype(BF16)
    b_r = jnp.concatenate([router_group_b, router_expert_b])
    b_r = jnp.pad(b_r, (0, LANES - b_r.shape[0])).reshape(1, LANES)
    q_gain = (mem_q_g * dh ** -0.5).reshape(1, dh)

    full = lambda shape: pl.BlockSpec(shape, lambda b, s: (0,) * len(shape))
    row = lambda width: pl.BlockSpec((1, tm, width), lambda b, s: (b, s, 0))
    mem_spec = pl.BlockSpec((1, M, mw), lambda b, s: (b, 0, 0))
    return pl.pallas_call(
        functools.partial(_out_mem_kernel, n_heads=n_heads, dh=dh, n_groups=n_groups,
                          per_group=n_experts // n_groups),
        grid=(B, S // tm),
        in_specs=[row(D), row(lw), row(fw), full((1, fw)), full((lw, D)), full((fw, D)),
                  full((1, D)), full((D, mw)), full((1, dh)), mem_spec, mem_spec, full((mw, D)),
                  full((1, D)), full((D, LANES)), full((1, LANES))],
        out_specs=[row(D), row(D), row(LANES), row(LANES)],
        out_shape=[jax.ShapeDtypeStruct((B, S, D), F32), jax.ShapeDtypeStruct((B, S, D), F32),
                   jax.ShapeDtypeStruct((B, S, LANES), F32),
                   jax.ShapeDtypeStruct((B, S, LANES), jnp.int32)],
        compiler_params=pltpu.CompilerParams(
            dimension_semantics=("parallel", "parallel"), vmem_limit_bytes=VMEM_LIMIT),
        name="out_mem",
    )(x, y_lru, y_fox, fox_out_g.reshape(1, fw), w_out[:lw].astype(BF16), w_out[lw:].astype(BF16),
      norm_mem_x_g.reshape(1, D), mem_wq.astype(BF16), q_gain, k_mem, v_mem, mem_wo.astype(BF16),
      norm_ffn_g.reshape(1, D), w_r, b_r)


def _gather_rows_kernel(idx_ref, src_ref, o_ref, sem, *, rows):
    def row_copy(r):
        return pltpu.make_async_copy(src_ref.at[pl.ds(idx_ref[0, 0, r], 1), :],
                                     o_ref.at[pl.ds(r, 1), :], sem)

    def issue(r, carry):
        row_copy(r).start()
        return carry

    def drain(r, carry):
        row_copy(r).wait()
        return carry

    lax.fori_loop(0, rows, issue, 0)
    lax.fori_loop(0, rows, drain, 0)


def _gather_rows(src, idx, *, rows):
    n = idx.shape[0]
    d = src.shape[1]
    return pl.pallas_call(
        functools.partial(_gather_rows_kernel, rows=rows),
        grid=(n // rows,),
        in_specs=[pl.BlockSpec((1, 1, rows), lambda i: (i, 0, 0), memory_space=pltpu.SMEM),
                  pl.BlockSpec(memory_space=pl.ANY)],
        out_specs=pl.BlockSpec((rows, d), lambda i: (i, 0)),
        out_shape=jax.ShapeDtypeStruct((n, d), src.dtype),
        scratch_shapes=[pltpu.SemaphoreType.DMA],
        compiler_params=pltpu.CompilerParams(
            dimension_semantics=("arbitrary",), vmem_limit_bytes=VMEM_LIMIT),
        name="gather_rows",
    )(idx.reshape(n // rows, 1, rows), src)


def _experts_kernel(be_ref, x_ref, w_ref, wg_ref, wu_ref, wd_ref, o_ref):
    del be_ref
    xb = x_ref[...].astype(BF16)
    hdn = jax.nn.silu(_dot(xb, wg_ref[0])) * _dot(xb, wu_ref[0])
    o_ref[...] = _dot(hdn.astype(BF16), wd_ref[0]) * w_ref[...]


def _experts(x_buf, w_buf, blk_expert, w_gate, w_up, w_down, *, blk):
    P, D = x_buf.shape
    de = w_gate.shape[-1]
    grid_spec = pltpu.PrefetchScalarGridSpec(
        num_scalar_prefetch=1,
        grid=(P // blk,),
        in_specs=[pl.BlockSpec((blk, D), lambda i, be: (i, 0)),
                  pl.BlockSpec((blk, 1), lambda i, be: (i, 0)),
                  pl.BlockSpec((1, D, de), lambda i, be: (be[i], 0, 0)),
                  pl.BlockSpec((1, D, de), lambda i, be: (be[i], 0, 0)),
                  pl.BlockSpec((1, de, D), lambda i, be: (be[i], 0, 0))],
        out_specs=pl.BlockSpec((blk, D), lambda i, be: (i, 0)),
    )
    return pl.pallas_call(
        _experts_kernel,
        grid_spec=grid_spec,
        out_shape=jax.ShapeDtypeStruct((P, D), F32),
        compiler_params=pltpu.CompilerParams(
            dimension_semantics=("arbitrary",), vmem_limit_bytes=VMEM_LIMIT),
        name="experts",
    )(blk_expert, x_buf, w_buf.reshape(P, 1), w_gate.astype(BF16), w_up.astype(BF16),
      w_down.astype(BF16))


def _combine_kernel(x_ref, a_ref, b_ref, o_ref):
    o_ref[...] = x_ref[...] + a_ref[...] + b_ref[...]


def _combine(x2, y_pairs, *, tm):
    T, D = x2.shape
    nt = T // tm
    return pl.pallas_call(
        _combine_kernel,
        grid=(nt,),
        in_specs=[pl.BlockSpec((tm, D), lambda i: (i, 0)),
                  pl.BlockSpec((tm, D), lambda i: (i, 0)),
                  pl.BlockSpec((tm, D), lambda i: (i + nt, 0))],
        out_specs=pl.BlockSpec((tm, D), lambda i: (i, 0)),
        out_shape=jax.ShapeDtypeStruct((T, D), F32),
        compiler_params=pltpu.CompilerParams(
            dimension_semantics=("parallel",), vmem_limit_bytes=VMEM_LIMIT),
        name="combine",
    )(x2, y_pairs, y_pairs)


def _dispatch(experts, gates, n_experts, blk):
    T, K = experts.shape
    A = T * K
    e_flat = experts.reshape(A)
    onehot = (e_flat[:, None] == jnp.arange(n_experts, dtype=jnp.int32)[None, :]).astype(jnp.int32)
    ranks = jnp.cumsum(onehot, axis=0)
    counts = ranks[-1]
    rank = jnp.take_along_axis(ranks, e_flat[:, None], axis=1)[:, 0] - 1
    padded = (counts + blk - 1) // blk * blk
    pends = jnp.cumsum(padded)
    pstarts = pends - padded
    dest = pstarts[e_flat] + rank
    P = A + n_experts * blk
    tok_flat = jnp.repeat(jnp.arange(T, dtype=jnp.int32), K)
    tok_buf = jnp.zeros((P,), jnp.int32).at[dest].set(tok_flat)
    w_buf = jnp.zeros((P,), F32).at[dest].set(gates.reshape(A))
    blk_start = jnp.arange(P // blk, dtype=jnp.int32) * blk
    blk_expert = jnp.minimum(jnp.sum(blk_start[:, None] >= pends[None, :], axis=1), n_experts - 1)
    return tok_buf, w_buf, blk_expert.astype(jnp.int32), dest.reshape(T, K)


def kernel(x, mem, norm_mix_g, w_in, b_forget, conv_w, conv_b, lru_wa, lru_ba, lru_wx, lru_bx,
           lru_a_param, fox_q_g, fox_k_g, lru_out_g, fox_out_g, w_out, norm_mem_x_g, norm_mem_g,
           mem_wq, mem_wkv, mem_q_g, mem_k_g, mem_wo, norm_ffn_g, router_group_w, router_group_b,
           router_expert_w, router_expert_b, exp_w_gate, exp_w_up, exp_w_down):
    B, S, D = x.shape
    depth = norm_mix_g.shape[0]
    lw = conv_w.shape[-1]
    n_heads = b_forget.shape[-1]
    dh = fox_q_g.shape[-1]
    mem_dh = mem_q_g.shape[-1]
    mem_heads = mem_wq.shape[-1] // mem_dh
    n_experts = router_expert_w.shape[-1]
    T = B * S
    tm = _pick_tile(S, 512)
    tq = _pick_tile(S, 512)
    blk = 256
    gather_rows = 256

    for l in range(depth):
        u, gate, q, k, v = _in_proj(x, norm_mix_g[l], w_in[l], b_forget[l], fox_q_g[l], fox_k_g[l],
                                    lw=lw, n_heads=n_heads, dh=dh, tm=tm)
        y_lru = _rg_lru(u, gate, conv_w[l], conv_b[l], lru_wa[l], lru_ba[l], lru_wx[l], lru_bx[l],
                        lru_a_param[l], lru_out_g[l], ts=tm)
        y_fox = _fox_attn(q, k, v, n_heads=n_heads, dh=dh, tq=tq)
        k_mem, v_mem = _mem_kv(mem, norm_mem_g[l], mem_wkv[l], mem_k_g[l], n_heads=mem_heads,
                               dh=mem_dh)
        x2, xn, gates, experts = _out_mem(
            x, y_lru, y_fox, fox_out_g[l], w_out[l], norm_mem_x_g[l], mem_wq[l], mem_q_g[l], k_mem,
            v_mem, mem_wo[l], norm_ffn_g[l], router_group_w[l], router_group_b[l],
            router_expert_w[l], router_expert_b[l], n_heads=mem_heads, dh=mem_dh, tm=tm)

        tok_buf, w_buf, blk_expert, dest = _dispatch(
            experts.reshape(T, LANES)[:, :TOP_K], gates.reshape(T, LANES)[:, :TOP_K], n_experts, blk)
        x_buf = _gather_rows(xn.reshape(T, D), tok_buf, rows=gather_rows)
        y_buf = _experts(x_buf, w_buf, blk_expert, exp_w_gate[l], exp_w_up[l], exp_w_down[l], blk=blk)
        y_pairs = _gather_rows(y_buf, dest.T.reshape(T * TOP_K), rows=gather_rows)
        x = _combine(x2.reshape(T, D), y_pairs, tm=tm).reshape(B, S, D)
    return x
```

```python
import functools

import jax
import jax.numpy as jnp
import numpy as np
from jax import lax
from jax.experimental import pallas as pl
from jax.experimental.pallas import tpu as pltpu

EPS = 1e-6
LRU_C = 8.0
CONV_WIDTH = 4
TOP_K = 2
LANES = 128
SUBLANES = 8
MXU_DIM = 256
HEAD_SLOT = LANES
VMEM_LIMIT = 56 * 1024 * 1024

F32 = jnp.float32
BF16 = jnp.bfloat16


def _dot(a, b):
    return jnp.dot(a, b, preferred_element_type=F32)


def _rms(x, g):
    return x * lax.rsqrt(jnp.mean(x * x, axis=-1, keepdims=True) + EPS) * g


def _softplus(x):
    return jnp.maximum(x, 0.0) + jnp.log1p(jnp.exp(-jnp.abs(x)))


def _pick_tile(n, target):
    t = min(n, target)
    while n % t:
        t //= 2
    return t


def _in_proj_kernel(x_ref, g_ref, w_ref, wf_ref, bf_ref, gq_ref, gk_ref, pe_ref, cq_ref, ck_ref,
                    cv_ref, u_ref, gate_ref, q_ref, k_ref, v_ref, carry_ref, *, lw, n_heads, dh):
    @pl.when(pl.program_id(1) == 0)
    def _():
        carry_ref[...] = jnp.zeros_like(carry_ref)

    hs = n_heads * HEAD_SLOT
    x = x_ref[0]
    tm = x.shape[0]
    hb = _rms(x, g_ref[...]).astype(BF16)

    u_ref[0] = _dot(hb, w_ref[:, 0:lw])
    gate_ref[0] = _dot(hb, w_ref[:, lw:2 * lw])

    z = _dot(hb, wf_ref[...]) + bf_ref[...]
    lane = lax.broadcasted_iota(jnp.int32, z.shape, 1)
    row = lax.broadcasted_iota(jnp.int32, z.shape, 0)
    c = jnp.where(lane < n_heads, -_softplus(-z), 0.0)
    d = 1
    while d < tm:
        c = c + jnp.where(row >= d, pltpu.roll(c, d, 0), 0.0)
        d *= 2
    c = c + carry_ref[...]
    carry_ref[...] = c[tm - 1:tm, :]
    c1 = c.astype(BF16).astype(F32)
    r1 = c - c1
    c2 = r1.astype(BF16).astype(F32)
    c3 = (r1 - c2).astype(BF16).astype(F32)
    e = (c1 + pltpu.roll(c2, n_heads, 1) + pltpu.roll(c3, 2 * n_heads, 1)).astype(BF16)
    ext = _dot(e, pe_ref[...])

    def head_norm(y, gain):
        parts = []
        for h in range(n_heads):
            blk = y[:, h * HEAD_SLOT:(h + 1) * HEAD_SLOT]
            ss = jnp.sum(blk * blk, axis=-1, keepdims=True) * (1.0 / dh)
            parts.append(blk * lax.rsqrt(ss + EPS))
        return jnp.concatenate(parts, axis=-1) * gain

    q = _dot(hb, w_ref[:, 2 * lw:2 * lw + hs])
    q_ref[0] = (head_norm(q, gq_ref[...]) + ext[:, 0:hs] + cq_ref[...]).astype(BF16)
    k = _dot(hb, w_ref[:, 2 * lw + hs:2 * lw + 2 * hs])
    k_ref[0] = (head_norm(k, gk_ref[...]) + ext[:, hs:2 * hs] + ck_ref[...]).astype(BF16)
    v = _dot(hb, w_ref[:, 2 * lw + 2 * hs:2 * lw + 3 * hs])
    v_ref[0] = (v + cv_ref[...]).astype(BF16)


def _slot_cols(w, n_heads, dh):
    d = w.shape[0]
    w = w.reshape(d, n_heads, dh)
    w = jnp.pad(w, ((0, 0), (0, 0), (0, HEAD_SLOT - dh)))
    return w.reshape(d, n_heads * HEAD_SLOT)


def _in_proj(x, norm_g, w_in, b_forget, fox_q_g, fox_k_g, *, lw, n_heads, dh, tm):
    B, S, D = x.shape
    fw = n_heads * dh
    hs = n_heads * HEAD_SLOT
    cuts = np.cumsum([lw, lw, fw, fw, fw])
    w_u, w_g, w_q, w_k, w_v, w_f = jnp.split(w_in, cuts, axis=1)
    w_all = jnp.concatenate(
        [w_u, w_g, _slot_cols(w_q, n_heads, dh), _slot_cols(w_k, n_heads, dh),
         _slot_cols(w_v, n_heads, dh)], axis=1).astype(BF16)
    w_f = jnp.pad(w_f, ((0, 0), (0, LANES - n_heads))).astype(BF16)
    b_f = jnp.pad(b_forget, (0, LANES - n_heads)).reshape(1, LANES)
    scale = dh ** -0.5
    gq = jnp.tile(jnp.pad(fox_q_g * scale, (0, HEAD_SLOT - dh)), n_heads).reshape(1, hs)
    gk = jnp.tile(jnp.pad(fox_k_g, (0, HEAD_SLOT - dh)), n_heads).reshape(1, hs)
    pe = np.zeros((LANES, 2 * hs), np.float32)
    cq = np.zeros((1, hs), np.float32)
    ck = np.zeros((1, hs), np.float32)
    cv = np.zeros((1, hs), np.float32)
    for h in range(n_heads):
        for j in range(3):
            pe[j * n_heads + h, h * HEAD_SLOT + dh + j] = 1.0
            pe[j * n_heads + h, hs + h * HEAD_SLOT + dh + 3 + j] = -1.0
            cq[0, h * HEAD_SLOT + dh + 3 + j] = 1.0
            ck[0, h * HEAD_SLOT + dh + j] = 1.0
        cv[0, h * HEAD_SLOT + dh] = 1.0
    pe = jnp.asarray(pe, BF16)

    full = lambda shape: pl.BlockSpec(shape, lambda b, s: (0,) * len(shape))
    row = lambda width: pl.BlockSpec((1, tm, width), lambda b, s: (b, s, 0))
    return pl.pallas_call(
        functools.partial(_in_proj_kernel, lw=lw, n_heads=n_heads, dh=dh),
        grid=(B, S // tm),
        in_specs=[row(D), full((1, D)), full(w_all.shape), full(w_f.shape), full((1, LANES)),
                  full((1, hs)), full((1, hs)), full(pe.shape), full((1, hs)), full((1, hs)),
                  full((1, hs))],
        out_specs=[row(lw), row(lw), row(hs), row(hs), row(hs)],
        out_shape=[jax.ShapeDtypeStruct((B, S, lw), F32), jax.ShapeDtypeStruct((B, S, lw), F32),
                   jax.ShapeDtypeStruct((B, S, hs), BF16), jax.ShapeDtypeStruct((B, S, hs), BF16),
                   jax.ShapeDtypeStruct((B, S, hs), BF16)],
        scratch_shapes=[pltpu.VMEM((1, LANES), F32)],
        compiler_params=pltpu.CompilerParams(
            dimension_semantics=("parallel", "arbitrary"), vmem_limit_bytes=VMEM_LIMIT),
        name="in_proj",
    )(x, norm_g.reshape(1, D), w_all, w_f, b_f, gq, gk, pe, jnp.asarray(cq), jnp.asarray(ck),
      jnp.asarray(cv))


def _rg_lru_kernel(u_ref, gate_ref, cw_ref, cb_ref, w_ref, ba_ref, bx_ref, ap_ref, og_ref,
                   o_ref, tail_ref, h_ref):
    @pl.when(pl.program_id(1) == 0)
    def _():
        tail_ref[...] = jnp.zeros_like(tail_ref)
        h_ref[...] = jnp.zeros_like(h_ref)

    u = u_ref[0]
    ts, lw = u.shape
    ext = jnp.concatenate([tail_ref[...], u], axis=0)
    tail_ref[...] = u[ts - SUBLANES:, :]
    xc = cb_ref[...] + u * cw_ref[CONV_WIDTH - 1:CONV_WIDTH, :]
    for back in range(1, CONV_WIDTH):
        tap = CONV_WIDTH - 1 - back
        xc = xc + pltpu.roll(ext, back, 0)[SUBLANES:, :] * cw_ref[tap:tap + 1, :]

    xb = xc.astype(BF16)
    r_parts, i_parts = [], []
    for j in range(lw // MXU_DIM):
        y = _dot(xb[:, j * MXU_DIM:(j + 1) * MXU_DIM], w_ref[j])
        r_parts.append(y[:, :MXU_DIM])
        i_parts.append(y[:, MXU_DIM:])
    r = jax.nn.sigmoid(jnp.concatenate(r_parts, axis=-1) + ba_ref[...])
    i = jax.nn.sigmoid(jnp.concatenate(i_parts, axis=-1) + bx_ref[...])
    log_a = (-LRU_C) * r * _softplus(-ap_ref[...])
    a = jnp.exp(log_a)
    mult = jnp.sqrt(jnp.maximum(-jnp.tanh(log_a) * (a * a + 1.0), 0.0))
    b = mult * (i * xc)

    row = lax.broadcasted_iota(jnp.int32, a.shape, 0)
    d = 1
    while d < ts:
        keep = row >= d
        a_prev = jnp.where(keep, pltpu.roll(a, d, 0), 1.0)
        b_prev = jnp.where(keep, pltpu.roll(b, d, 0), 0.0)
        b = a * b_prev + b
        a = a * a_prev
        d *= 2
    h = b + a * h_ref[...]
    h_ref[...] = h[ts - 1:ts, :]

    y = h * jax.nn.gelu(gate_ref[0])
    o_ref[0] = _rms(y, og_ref[...]).astype(o_ref.dtype)


def _rg_lru(u, gate, conv_w, conv_b, wa, ba, wx, bx, a_param, out_g, *, ts):
    B, S, lw = u.shape
    nb, bd, _ = wa.shape
    per = MXU_DIM // bd
    n_tiles = lw // MXU_DIM

    def tiles(w):
        w = w.reshape(n_tiles, per, bd, bd)
        eye = jnp.eye(per, dtype=w.dtype)
        return jnp.einsum('tpij,pq->tpiqj', w, eye).reshape(n_tiles, MXU_DIM, MXU_DIM)

    w_cat = jnp.concatenate([tiles(wa), tiles(wx)], axis=-1).astype(BF16)
    vec = lambda a: a.reshape(1, lw)
    full = lambda shape: pl.BlockSpec(shape, lambda b, s: (0,) * len(shape))
    row = pl.BlockSpec((1, ts, lw), lambda b, s: (b, s, 0))
    return pl.pallas_call(
        _rg_lru_kernel,
        grid=(B, S // ts),
        in_specs=[row, row, full((CONV_WIDTH, lw)), full((1, lw)), full(w_cat.shape), full((1, lw)),
                  full((1, lw)), full((1, lw)), full((1, lw))],
        out_specs=row,
        out_shape=jax.ShapeDtypeStruct((B, S, lw), BF16),
        scratch_shapes=[pltpu.VMEM((SUBLANES, lw), F32), pltpu.VMEM((1, lw), F32)],
        compiler_params=pltpu.CompilerParams(
            dimension_semantics=("parallel", "arbitrary"), vmem_limit_bytes=VMEM_LIMIT),
        name="rg_lru",
    )(u, gate, conv_w, vec(conv_b), w_cat, vec(ba), vec(bx), vec(a_param), vec(out_g))


def _fox_attn_kernel(q_ref, k_ref, v_ref, o_ref, vt_ref, m_ref, acc_ref, *, dh, tq):
    qi = pl.program_id(2)
    n_chunks = vt_ref.shape[0]

    @pl.when(qi == 0)
    def _():
        for c in range(n_chunks):
            vt_ref[c] = v_ref[0, c * tq:(c + 1) * tq, :].astype(F32).T.astype(BF16)

    m_ref[...] = jnp.full_like(m_ref, -jnp.inf)
    acc_ref[...] = jnp.zeros_like(acc_ref)

    def chunk(j, masked):
        start = pl.multiple_of(j * tq, tq)
        for hh in range(2):
            sl = slice(hh * HEAD_SLOT, (hh + 1) * HEAD_SLOT)
            st = lax.dot_general(k_ref[0, pl.ds(start, tq), sl], q_ref[0, :, sl],
                                 (((1,), (1,)), ((), ())), preferred_element_type=F32)
            if masked:
                kpos = lax.broadcasted_iota(jnp.int32, st.shape, 0)
                qpos = lax.broadcasted_iota(jnp.int32, st.shape, 1)
                st = jnp.where(kpos <= qpos, st, -jnp.inf)
            m_old = m_ref[hh]
            m_new = jnp.maximum(m_old, jnp.max(st, axis=0, keepdims=True))
            pt = jnp.exp(st - m_new).astype(BF16)
            acc_ref[hh] = jnp.exp(m_old - m_new) * acc_ref[hh] + _dot(vt_ref[j, sl, :], pt)
            m_ref[hh] = m_new

    def body(j, carry):
        chunk(j, False)
        return carry

    lax.fori_loop(0, qi, body, 0)
    chunk(qi, True)

    outs = []
    for hh in range(2):
        acc = acc_ref[hh]
        outs.append(acc[0:dh, :] / acc[dh:dh + 1, :])
    o_ref[0] = jnp.concatenate(outs, axis=0).T.astype(o_ref.dtype)


def _fox_attn(q, k, v, *, n_heads, dh, tq):
    B, S, _ = q.shape
    nq = S // tq
    pair = 2 * HEAD_SLOT
    kv_spec = pl.BlockSpec((1, S, pair), lambda b, h, i: (b, 0, h))
    return pl.pallas_call(
        functools.partial(_fox_attn_kernel, dh=dh, tq=tq),
        grid=(B, n_heads // 2, nq),
        in_specs=[pl.BlockSpec((1, tq, pair), lambda b, h, i: (b, i, h)), kv_spec, kv_spec],
        out_specs=pl.BlockSpec((1, tq, 2 * dh), lambda b, h, i: (b, i, h)),
        out_shape=jax.ShapeDtypeStruct((B, S, n_heads * dh), F32),
        scratch_shapes=[pltpu.VMEM((nq, pair, tq), BF16), pltpu.VMEM((2, 1, tq), F32),
                        pltpu.VMEM((2, HEAD_SLOT, tq), F32)],
        compiler_params=pltpu.CompilerParams(
            dimension_semantics=("parallel", "parallel", "arbitrary"),
            vmem_limit_bytes=VMEM_LIMIT),
        name="fox_attn",
    )(q, k, v)


def _mem_kv_kernel(m_ref, g_ref, w_ref, kg_ref, k_ref, v_ref, *, n_heads, dh):
    mw = n_heads * dh
    mb = _rms(m_ref[0], g_ref[...]).astype(BF16)
    kv = _dot(mb, w_ref[...])
    parts = []
    for h in range(n_heads):
        parts.append(_rms(kv[:, h * dh:(h + 1) * dh], kg_ref[...]))
    k_ref[0] = jnp.concatenate(parts, axis=-1).astype(BF16)
    v_ref[0] = kv[:, mw:].astype(BF16)


def _mem_kv(mem, norm_g, wkv, k_g, *, n_heads, dh):
    B, M, D = mem.shape
    mw = n_heads * dh
    full = lambda shape: pl.BlockSpec(shape, lambda b: (0,) * len(shape))
    out = pl.BlockSpec((1, M, mw), lambda b: (b, 0, 0))
    return pl.pallas_call(
        functools.partial(_mem_kv_kernel, n_heads=n_heads, dh=dh),
        grid=(B,),
        in_specs=[pl.BlockSpec((1, M, D), lambda b: (b, 0, 0)), full((1, D)), full((D, 2 * mw)),
                  full((1, dh))],
        out_specs=[out, out],
        out_shape=[jax.ShapeDtypeStruct((B, M, mw), BF16)] * 2,
        compiler_params=pltpu.CompilerParams(
            dimension_semantics=("parallel",), vmem_limit_bytes=VMEM_LIMIT),
        name="mem_kv",
    )(mem, norm_g.reshape(1, D), wkv.astype(BF16), k_g.reshape(1, dh))


def _out_mem_kernel(x_ref, yl_ref, yf_ref, fg_ref, wol_ref, wof_ref, gx_ref, wq_ref, qg_ref,
                    km_ref, vm_ref, wo_ref, gf_ref, wr_ref, br_ref,
                    x2_ref, xn_ref, gates_ref, experts_ref, *, n_heads, dh, n_groups, per_group):
    yf = _rms(yf_ref[0], fg_ref[...]).astype(BF16)
    x1 = x_ref[0] + _dot(yl_ref[0], wol_ref[...]) + _dot(yf, wof_ref[...])

    q = _dot(_rms(x1, gx_ref[...]).astype(BF16), wq_ref[...])
    outs = []
    for h in range(n_heads):
        sl = slice(h * dh, (h + 1) * dh)
        qh = _rms(q[:, sl], qg_ref[...]).astype(BF16)
        s = lax.dot_general(qh, km_ref[0, :, sl], (((1,), (1,)), ((), ())),
                            preferred_element_type=F32)
        p = jnp.exp(s - jnp.max(s, axis=-1, keepdims=True))
        p = p / jnp.sum(p, axis=-1, keepdims=True)
        outs.append(_dot(p.astype(BF16), vm_ref[0, :, sl]))
    x2 = x1 + _dot(jnp.concatenate(outs, axis=-1).astype(BF16), wo_ref[...])
    x2_ref[0] = x2

    xn = _rms(x2, gf_ref[...])
    xn_ref[0] = xn
    logits = _dot(xn.astype(BF16), wr_ref[...]) + br_ref[...]
    lane = lax.broadcasted_iota(jnp.int32, logits.shape, 1).astype(F32)
    neg = -jnp.inf

    def top(vals):
        mx = jnp.max(vals, axis=-1, keepdims=True)
        idx = jnp.min(jnp.where(vals == mx, lane, float(LANES)), axis=-1, keepdims=True)
        return mx, idx

    gl = jnp.where(lane < n_groups, logits, neg)
    g_max, g_idx = top(gl)
    g_w = 1.0 / jnp.sum(jnp.exp(gl - g_max), axis=-1, keepdims=True)
    lo = n_groups + per_group * g_idx
    el = jnp.where((lane >= lo) & (lane < lo + per_group), logits, neg)
    e1, i1 = top(el)
    e2, i2 = top(jnp.where(lane == i1, neg, el))
    t = jnp.exp(e2 - e1)
    w1 = g_w / (1.0 + t)
    w2 = g_w * t / (1.0 + t)
    gates_ref[0] = jnp.where(lane == 0, w1, jnp.where(lane == 1, w2, 0.0))
    experts_ref[0] = jnp.where(lane == 0, i1 - n_groups,
                               jnp.where(lane == 1, i2 - n_groups, 0.0)).astype(jnp.int32)


def _out_mem(x, y_lru, y_fox, fox_out_g, w_out, norm_mem_x_g, mem_wq, mem_q_g, k_mem, v_mem, mem_wo,
             norm_ffn_g, router_group_w, router_group_b, router_expert_w, router_expert_b, *,
             n_heads, dh, tm):
    B, S, D = x.shape
    lw = y_lru.shape[-1]
    fw = y_fox.shape[-1]
    M = k_mem.shape[1]
    mw = n_heads * dh
    n_groups = router_group_w.shape[-1]
    n_experts = router_expert_w.shape[-1]
    w_r = jnp.concatenate([router_group_w, router_expert_w], axis=1)
    w_r = jnp.pad(w_r, ((0, 0), (0, LANES - w_r.shape[1]))).astype(BF16)
    b_r = jnp.concatenate([router_group_b, router_expert_b])
    b_r = jnp.pad(b_r, (0, LANES - b_r.shape[0])).reshape(1, LANES)
    q_gain = (mem_q_g * dh ** -0.5).reshape(1, dh)

    full = lambda shape: pl.BlockSpec(shape, lambda b, s: (0,) * len(shape))
    row = lambda width: pl.BlockSpec((1, tm, width), lambda b, s: (b, s, 0))
    mem_spec = pl.BlockSpec((1, M, mw), lambda b, s: (b, 0, 0))
    return pl.pallas_call(
        functools.partial(_out_mem_kernel, n_heads=n_heads, dh=dh, n_groups=n_groups,
                          per_group=n_experts // n_groups),
        grid=(B, S // tm),
        in_specs=[row(D), row(lw), row(fw), full((1, fw)), full((lw, D)), full((fw, D)),
                  full((1, D)), full((D, mw)), full((1, dh)), mem_spec, mem_spec, full((mw, D)),
                  full((1, D)), full((D, LANES)), full((1, LANES))],
        out_specs=[row(D), row(D), row(LANES), row(LANES)],
        out_shape=[jax.ShapeDtypeStruct((B, S, D), F32), jax.ShapeDtypeStruct((B, S, D), F32),
                   jax.ShapeDtypeStruct((B, S, LANES), F32),
                   jax.ShapeDtypeStruct((B, S, LANES), jnp.int32)],
        compiler_params=pltpu.CompilerParams(
            dimension_semantics=("parallel", "parallel"), vmem_limit_bytes=VMEM_LIMIT),
        name="out_mem",
    )(x, y_lru, y_fox, fox_out_g.reshape(1, fw), w_out[:lw].astype(BF16), w_out[lw:].astype(BF16),
      norm_mem_x_g.reshape(1, D), mem_wq.astype(BF16), q_gain, k_mem, v_mem, mem_wo.astype(BF16),
      norm_ffn_g.reshape(1, D), w_r, b_r)


def _gather_rows_kernel(idx_ref, src_ref, o_ref, sem, *, rows):
    def row_copy(r):
        return pltpu.make_async_copy(src_ref.at[pl.ds(idx_ref[0, 0, r], 1), :],
                                     o_ref.at[pl.ds(r, 1), :], sem)

    def issue(r, carry):
        row_copy(r).start()
        return carry

    def drain(r, carry):
        row_copy(r).wait()
        return carry

    lax.fori_loop(0, rows, issue, 0)
    lax.fori_loop(0, rows, drain, 0)


def _gather_rows(src, idx, *, rows):
    n = idx.shape[0]
    d = src.shape[1]
    return pl.pallas_call(
        functools.partial(_gather_rows_kernel, rows=rows),
        grid=(n // rows,),
        in_specs=[pl.BlockSpec((1, 1, rows), lambda i: (i, 0, 0), memory_space=pltpu.SMEM),
                  pl.BlockSpec(memory_space=pl.ANY)],
        out_specs=pl.BlockSpec((rows, d), lambda i: (i, 0)),
        out_shape=jax.ShapeDtypeStruct((n, d), src.dtype),
        scratch_shapes=[pltpu.SemaphoreType.DMA],
        compiler_params=pltpu.CompilerParams(
            dimension_semantics=("arbitrary",), vmem_limit_bytes=VMEM_LIMIT),
        name="gather_rows",
    )(idx.reshape(n // rows, 1, rows), src)


def _experts_kernel(be_ref, x_ref, w_ref, wg_ref, wu_ref, wd_ref, o_ref):
    del be_ref
    xb = x_ref[...].astype(BF16)
    hdn = jax.nn.silu(_dot(xb, wg_ref[0])) * _dot(xb, wu_ref[0])
    o_ref[...] = _dot(hdn.astype(BF16), wd_ref[0]) * w_ref[...]


def _experts(x_buf, w_buf, blk_expert, w_gate, w_up, w_down, *, blk):
    P, D = x_buf.shape
    de = w_gate.shape[-1]
    grid_spec = pltpu.PrefetchScalarGridSpec(
        num_scalar_prefetch=1,
        grid=(P // blk,),
        in_specs=[pl.BlockSpec((blk, D), lambda i, be: (i, 0)),
                  pl.BlockSpec((blk, 1), lambda i, be: (i, 0)),
                  pl.BlockSpec((1, D, de), lambda i, be: (be[i], 0, 0)),
                  pl.BlockSpec((1, D, de), lambda i, be: (be[i], 0, 0)),
                  pl.BlockSpec((1, de, D), lambda i, be: (be[i], 0, 0))],
        out_specs=pl.BlockSpec((blk, D), lambda i, be: (i, 0)),
    )
    return pl.pallas_call(
        _experts_kernel,
        grid_spec=grid_spec,
        out_shape=jax.ShapeDtypeStruct((P, D), F32),
        compiler_params=pltpu.CompilerParams(
            dimension_semantics=("arbitrary",), vmem_limit_bytes=VMEM_LIMIT),
        name="experts",
    )(blk_expert, x_buf, w_buf.reshape(P, 1), w_gate.astype(BF16), w_up.astype(BF16),
      w_down.astype(BF16))


def _combine_kernel(x_ref, a_ref, b_ref, o_ref):
    o_ref[...] = x_ref[...] + a_ref[...] + b_ref[...]


def _combine(x2, y_pairs, *, tm):
    T, D = x2.shape
    nt = T // tm
    return pl.pallas_call(
        _combine_kernel,
        grid=(nt,),
        in_specs=[pl.BlockSpec((tm, D), lambda i: (i, 0)),
                  pl.BlockSpec((tm, D), lambda i: (i, 0)),
                  pl.BlockSpec((tm, D), lambda i: (i + nt, 0))],
        out_specs=pl.BlockSpec((tm, D), lambda i: (i, 0)),
        out_shape=jax.ShapeDtypeStruct((T, D), F32),
        compiler_params=pltpu.CompilerParams(
            dimension_semantics=("parallel",), vmem_limit_bytes=VMEM_LIMIT),
        name="combine",
    )(x2, y_pairs, y_pairs)


def _dispatch(experts, gates, n_experts, blk):
    T, K = experts.shape
    A = T * K
    e_flat = experts.reshape(A)
    onehot = (e_flat[:, None] == jnp.arange(n_experts, dtype=jnp.int32)[None, :]).astype(jnp.int32)
    ranks = jnp.cumsum(onehot, axis=0)
    counts = ranks[-1]
    rank = jnp.take_along_axis(ranks, e_flat[:, None], axis=1)[:, 0] - 1
    padded = (counts + blk - 1) // blk * blk
    pends = jnp.cumsum(padded)
    pstarts = pends - padded
    dest = pstarts[e_flat] + rank
    P = A + n_experts * blk
    tok_flat = jnp.repeat(jnp.arange(T, dtype=jnp.int32), K)
    tok_buf = jnp.zeros((P,), jnp.int32).at[dest].set(tok_flat)
    w_buf = jnp.zeros((P,), F32).at[dest].set(gates.reshape(A))
    blk_start = jnp.arange(P // blk, dtype=jnp.int32) * blk
    blk_expert = jnp.minimum(jnp.sum(blk_start[:, None] >= pends[None, :], axis=1), n_experts - 1)
    return tok_buf, w_buf, blk_expert.astype(jnp.int32), dest.reshape(T, K)


def kernel(x, mem, norm_mix_g, w_in, b_forget, conv_w, conv_b, lru_wa, lru_ba, lru_wx, lru_bx,
           lru_a_param, fox_q_g, fox_k_g, lru_out_g, fox_out_g, w_out, norm_mem_x_g, norm_mem_g,
           mem_wq, mem_wkv, mem_q_g, mem_k_g, mem_wo, norm_ffn_g, router_group_w, router_group_b,
           router_expert_w, router_expert_b, exp_w_gate, exp_w_up, exp_w_down):
    B, S, D = x.shape
    depth = norm_mix_g.shape[0]
    lw = conv_w.shape[-1]
    n_heads = b_forget.shape[-1]
    dh = fox_q_g.shape[-1]
    mem_dh = mem_q_g.shape[-1]
    mem_heads = mem_wq.shape[-1] // mem_dh
    n_experts = router_expert_w.shape[-1]
    T = B * S
    tm = _pick_tile(S, 512)
    tq = _pick_tile(S, 512)
    blk = 256
    gather_rows = 256

    for l in range(depth):
        u, gate, q, k, v = _in_proj(x, norm_mix_g[l], w_in[l], b_forget[l], fox_q_g[l], fox_k_g[l],
                                    lw=lw, n_heads=n_heads, dh=dh, tm=tm)
        y_lru = _rg_lru(u, gate, conv_w[l], conv_b[l], lru_wa[l], lru_ba[l], lru_wx[l], lru_bx[l],
                        lru_a_param[l], lru_out_g[l], ts=tm)
        y_fox = _fox_attn(q, k, v, n_heads=n_heads, dh=dh, tq=tq)
        k_mem, v_mem = _mem_kv(mem, norm_mem_g[l], mem_wkv[l], mem_k_g[l], n_heads=mem_heads,
                               dh=mem_dh)
        x2, xn, gates, experts = _out_mem(
            x, y_lru, y_fox, fox_out_g[l], w_out[l], norm_mem_x_g[l], mem_wq[l], mem_q_g[l], k_mem,
            v_mem, mem_wo[l], norm_ffn_g[l], router_group_w[l], router_group_b[l],
            router_expert_w[l], router_expert_b[l], n_heads=mem_heads, dh=mem_dh, tm=tm)

        tok_buf, w_buf, blk_expert, dest = _dispatch(
            experts.reshape(T, LANES)[:, :TOP_K], gates.reshape(T, LANES)[:, :TOP_K], n_experts, blk)
        x_buf = _gather_rows(xn.reshape(T, D), tok_buf, rows=gather_rows)
        y_buf = _experts(x_buf, w_buf, blk_expert, exp_w_gate[l], exp_w_up[l], exp_w_down[l], blk=blk)
        y_pairs = _gather_rows(y_buf, dest.T.reshape(T * TOP_K), rows=gather_rows)
        x = _combine(x2.reshape(T, D), y_pairs, tm=tm).reshape(B, S, D)
    return x
```

```python
import functools

import jax
import jax.numpy as jnp
import numpy as np
from jax import lax
from jax.experimental import pallas as pl
from jax.experimental.pallas import tpu as pltpu

EPS = 1e-6
LRU_C = 8.0
CONV_WIDTH = 4
TOP_K = 2
LANES = 128
SUBLANES = 8
MXU_DIM = 256
HEAD_SLOT = LANES
VMEM_LIMIT = 56 * 1024 * 1024

F32 = jnp.float32
BF16 = jnp.bfloat16


def _dot(a, b):
    return jnp.dot(a, b, preferred_element_type=F32)


def _rms(x, g):
    return x * lax.rsqrt(jnp.mean(x * x, axis=-1, keepdims=True) + EPS) * g


def _softplus(x):
    return jnp.maximum(x, 0.0) + jnp.log1p(jnp.exp(-jnp.abs(x)))


def _pick_tile(n, target):
    t = min(n, target)
    while n % t:
        t //= 2
    return t


def _in_proj_kernel(x_ref, g_ref, w_ref, wf_ref, bf_ref, gq_ref, gk_ref, pe_ref, cq_ref, ck_ref,
                    cv_ref, u_ref, gate_ref, q_ref, k_ref, v_ref, carry_ref, *, lw, n_heads, dh):
    @pl.when(pl.program_id(1) == 0)
    def _():
        carry_ref[...] = jnp.zeros_like(carry_ref)

    hs = n_heads * HEAD_SLOT
    x = x_ref[0]
    tm = x.shape[0]
    hb = _rms(x, g_ref[...]).astype(BF16)

    u_ref[0] = _dot(hb, w_ref[:, 0:lw])
    gate_ref[0] = _dot(hb, w_ref[:, lw:2 * lw])

    z = _dot(hb, wf_ref[...]) + bf_ref[...]
    lane = lax.broadcasted_iota(jnp.int32, z.shape, 1)
    row = lax.broadcasted_iota(jnp.int32, z.shape, 0)
    c = jnp.where(lane < n_heads, -_softplus(-z), 0.0)
    d = 1
    while d < tm:
        c = c + jnp.where(row >= d, pltpu.roll(c, d, 0), 0.0)
        d *= 2
    c = c + carry_ref[...]
    carry_ref[...] = c[tm - 1:tm, :]
    c1 = c.astype(BF16).astype(F32)
    r1 = c - c1
    c2 = r1.astype(BF16).astype(F32)
    c3 = (r1 - c2).astype(BF16).astype(F32)
    e = (c1 + pltpu.roll(c2, n_heads, 1) + pltpu.roll(c3, 2 * n_heads, 1)).astype(BF16)
    ext = _dot(e, pe_ref[...])

    def head_norm(y, gain):
        parts = []
        for h in range(n_heads):
            blk = y[:, h * HEAD_SLOT:(h + 1) * HEAD_SLOT]
            ss = jnp.sum(blk * blk, axis=-1, keepdims=True) * (1.0 / dh)
            parts.append(blk * lax.rsqrt(ss + EPS))
        return jnp.concatenate(parts, axis=-1) * gain

    q = _dot(hb, w_ref[:, 2 * lw:2 * lw + hs])
    q_ref[0] = (head_norm(q, gq_ref[...]) + ext[:, 0:hs] + cq_ref[...]).astype(BF16)
    k = _dot(hb, w_ref[:, 2 * lw + hs:2 * lw + 2 * hs])
    k_ref[0] = (head_norm(k, gk_ref[...]) + ext[:, hs:2 * hs] + ck_ref[...]).astype(BF16)
    v = _dot(hb, w_ref[:, 2 * lw + 2 * hs:2 * lw + 3 * hs])
    v_ref[0] = (v + cv_ref[...]).astype(BF16)


def _slot_cols(w, n_heads, dh):
    d = w.shape[0]
    w = w.reshape(d, n_heads, dh)
    w = jnp.pad(w, ((0, 0), (0, 0), (0, HEAD_SLOT - dh)))
    return w.reshape(d, n_heads * HEAD_SLOT)


def _in_proj(x, norm_g, w_in, b_forget, fox_q_g, fox_k_g, *, lw, n_heads, dh, tm):
    B, S, D = x.shape
    fw = n_heads * dh
    hs = n_heads * HEAD_SLOT
    cuts = np.cumsum([lw, lw, fw, fw, fw])
    w_u, w_g, w_q, w_k, w_v, w_f = jnp.split(w_in, cuts, axis=1)
    w_all = jnp.concatenate(
        [w_u, w_g, _slot_cols(w_q, n_heads, dh), _slot_cols(w_k, n_heads, dh),
         _slot_cols(w_v, n_heads, dh)], axis=1).astype(BF16)
    w_f = jnp.pad(w_f, ((0, 0), (0, LANES - n_heads))).astype(BF16)
    b_f = jnp.pad(b_forget, (0, LANES - n_heads)).reshape(1, LANES)
    scale = dh ** -0.5
    gq = jnp.tile(jnp.pad(fox_q_g * scale, (0, HEAD_SLOT - dh)), n_heads).reshape(1, hs)
    gk = jnp.tile(jnp.pad(fox_k_g, (0, HEAD_SLOT - dh)), n_heads).reshape(1, hs)
    pe = np.zeros((LANES, 2 * hs), np.float32)
    cq = np.zeros((1, hs), np.float32)
    ck = np.zeros((1, hs), np.float32)
    cv = np.zeros((1, hs), np.float32)
    for h in range(n_heads):
        for j in range(3):
            pe[j * n_heads + h, h * HEAD_SLOT + dh + j] = 1.0
            pe[j * n_heads + h, hs + h * HEAD_SLOT + dh + 3 + j] = -1.0
            cq[0, h * HEAD_SLOT + dh + 3 + j] = 1.0
            ck[0, h * HEAD_SLOT + dh + j] = 1.0
            ck[0, h * HEAD_SLOT + dh + 6 + j] = 1.0
        cv[0, h * HEAD_SLOT + dh] = 1.0
    pe = jnp.asarray(pe, BF16)

    full = lambda shape: pl.BlockSpec(shape, lambda b, s: (0,) * len(shape))
    row = lambda width: pl.BlockSpec((1, tm, width), lambda b, s: (b, s, 0))
    return pl.pallas_call(
        functools.partial(_in_proj_kernel, lw=lw, n_heads=n_heads, dh=dh),
        grid=(B, S // tm),
        in_specs=[row(D), full((1, D)), full(w_all.shape), full(w_f.shape), full((1, LANES)),
                  full((1, hs)), full((1, hs)), full(pe.shape), full((1, hs)), full((1, hs)),
                  full((1, hs))],
        out_specs=[row(lw), row(lw), row(hs), row(hs), row(hs)],
        out_shape=[jax.ShapeDtypeStruct((B, S, lw), F32), jax.ShapeDtypeStruct((B, S, lw), F32),
                   jax.ShapeDtypeStruct((B, S, hs), BF16), jax.ShapeDtypeStruct((B, S, hs), BF16),
                   jax.ShapeDtypeStruct((B, S, hs), BF16)],
        scratch_shapes=[pltpu.VMEM((1, LANES), F32)],
        compiler_params=pltpu.CompilerParams(
            dimension_semantics=("parallel", "arbitrary"), vmem_limit_bytes=VMEM_LIMIT),
        name="in_proj",
    )(x, norm_g.reshape(1, D), w_all, w_f, b_f, gq, gk, pe, jnp.asarray(cq), jnp.asarray(ck),
      jnp.asarray(cv))


def _rg_lru_kernel(u_ref, gate_ref, cw_ref, cb_ref, w_ref, ba_ref, bx_ref, ap_ref, og_ref,
                   o_ref, tail_ref, h_ref):
    @pl.when(pl.program_id(1) == 0)
    def _():
        tail_ref[...] = jnp.zeros_like(tail_ref)
        h_ref[...] = jnp.zeros_like(h_ref)

    u = u_ref[0]
    ts, lw = u.shape
    ext = jnp.concatenate([tail_ref[...], u], axis=0)
    tail_ref[...] = u[ts - SUBLANES:, :]
    xc = cb_ref[...] + u * cw_ref[CONV_WIDTH - 1:CONV_WIDTH, :]
    for back in range(1, CONV_WIDTH):
        tap = CONV_WIDTH - 1 - back
        xc = xc + pltpu.roll(ext, back, 0)[SUBLANES:, :] * cw_ref[tap:tap + 1, :]

    xb = xc.astype(BF16)
    r_parts, i_parts = [], []
    for j in range(lw // MXU_DIM):
        y = _dot(xb[:, j * MXU_DIM:(j + 1) * MXU_DIM], w_ref[j])
        r_parts.append(y[:, :MXU_DIM])
        i_parts.append(y[:, MXU_DIM:])
    r = jax.nn.sigmoid(jnp.concatenate(r_parts, axis=-1) + ba_ref[...])
    i = jax.nn.sigmoid(jnp.concatenate(i_parts, axis=-1) + bx_ref[...])
    log_a = (-LRU_C) * r * _softplus(-ap_ref[...])
    a = jnp.exp(log_a)
    mult = jnp.sqrt(jnp.maximum(-jnp.tanh(log_a) * (a * a + 1.0), 0.0))
    b = mult * (i * xc)

    row = lax.broadcasted_iota(jnp.int32, a.shape, 0)
    d = 1
    while d < ts:
        keep = row >= d
        a_prev = jnp.where(keep, pltpu.roll(a, d, 0), 1.0)
        b_prev = jnp.where(keep, pltpu.roll(b, d, 0), 0.0)
        b = a * b_prev + b
        a = a * a_prev
        d *= 2
    h = b + a * h_ref[...]
    h_ref[...] = h[ts - 1:ts, :]

    y = h * jax.nn.gelu(gate_ref[0])
    o_ref[0] = _rms(y, og_ref[...]).astype(o_ref.dtype)


def _rg_lru(u, gate, conv_w, conv_b, wa, ba, wx, bx, a_param, out_g, *, ts):
    B, S, lw = u.shape
    nb, bd, _ = wa.shape
    per = MXU_DIM // bd
    n_tiles = lw // MXU_DIM

    def tiles(w):
        w = w.reshape(n_tiles, per, bd, bd)
        eye = jnp.eye(per, dtype=w.dtype)
        return jnp.einsum('tpij,pq->tpiqj', w, eye).reshape(n_tiles, MXU_DIM, MXU_DIM)

    w_cat = jnp.concatenate([tiles(wa), tiles(wx)], axis=-1).astype(BF16)
    vec = lambda a: a.reshape(1, lw)
    full = lambda shape: pl.BlockSpec(shape, lambda b, s: (0,) * len(shape))
    row = pl.BlockSpec((1, ts, lw), lambda b, s: (b, s, 0))
    return pl.pallas_call(
        _rg_lru_kernel,
        grid=(B, S // ts),
        in_specs=[row, row, full((CONV_WIDTH, lw)), full((1, lw)), full(w_cat.shape), full((1, lw)),
                  full((1, lw)), full((1, lw)), full((1, lw))],
        out_specs=row,
        out_shape=jax.ShapeDtypeStruct((B, S, lw), BF16),
        scratch_shapes=[pltpu.VMEM((SUBLANES, lw), F32), pltpu.VMEM((1, lw), F32)],
        compiler_params=pltpu.CompilerParams(
            dimension_semantics=("parallel", "arbitrary"), vmem_limit_bytes=VMEM_LIMIT),
        name="rg_lru",
    )(u, gate, conv_w, vec(conv_b), w_cat, vec(ba), vec(bx), vec(a_param), vec(out_g))


OVERFLOW_GUARD = 1e30


def _fox_attn_kernel(q_ref, k_ref, v_ref, o_ref, vt_ref, qm_ref, m_ref, acc_ref, *, dh, tq):
    qi = pl.program_id(2)
    n_chunks = vt_ref.shape[0]

    @pl.when(qi == 0)
    def _():
        for c in range(n_chunks):
            vt_ref[c] = v_ref[0, c * tq:(c + 1) * tq, :].astype(F32).T.astype(BF16)

    heads = [slice(hh * HEAD_SLOT, (hh + 1) * HEAD_SLOT) for hh in range(2)]

    def scores(j, q, sl, masked):
        start = pl.multiple_of(j * tq, tq)
        st = lax.dot_general(k_ref[0, pl.ds(start, tq), sl], q, (((1,), (1,)), ((), ())),
                             preferred_element_type=F32)
        if masked:
            kpos = lax.broadcasted_iota(jnp.int32, st.shape, 0)
            qpos = lax.broadcasted_iota(jnp.int32, st.shape, 1)
            st = jnp.where(kpos <= qpos, st, -jnp.inf)
        return st

    for hh, sl in enumerate(heads):
        q = q_ref[0, :, sl]
        st = scores(qi, q, sl, True)
        m = jnp.max(st, axis=0, keepdims=True)
        acc_ref[hh] = _dot(vt_ref[qi, sl, :], jnp.exp(st - m).astype(BF16))
        m_rows = jnp.broadcast_to(m, (HEAD_SLOT, tq)).T
        m1 = m_rows.astype(BF16).astype(F32)
        r1 = m_rows - m1
        m2 = r1.astype(BF16).astype(F32)
        m3 = (r1 - m2).astype(BF16).astype(F32)
        lane = lax.broadcasted_iota(jnp.int32, m_rows.shape, 1)
        qm = jnp.where(lane == dh + 6, -m1, jnp.where(lane == dh + 7, -m2, jnp.where(
            lane == dh + 8, -m3, q.astype(F32))))
        qm_ref[hh] = qm.astype(BF16)

    def fast_chunk(j):
        for hh, sl in enumerate(heads):
            pt = jnp.exp(scores(j, qm_ref[hh], sl, False)).astype(BF16)
            acc_ref[hh] += _dot(vt_ref[j, sl, :], pt)

    def fast_pair(i, carry):
        fast_chunk(2 * i)
        fast_chunk(2 * i + 1)
        return carry

    lax.fori_loop(0, qi // 2, fast_pair, 0)

    @pl.when(qi % 2 == 1)
    def _():
        fast_chunk(qi - 1)

    acc_all = acc_ref[...]
    bad = jnp.where(jnp.abs(acc_all) < OVERFLOW_GUARD, 0.0, 1.0)
    bad = jnp.max(jnp.max(jnp.maximum(bad[0], bad[1]), axis=0, keepdims=True), axis=1, keepdims=True)

    @pl.when(bad[0, 0] > 0.0)
    def _():
        m_ref[...] = jnp.full_like(m_ref, -jnp.inf)
        acc_ref[...] = jnp.zeros_like(acc_ref)

        def chunk(j, masked):
            for hh, sl in enumerate(heads):
                st = scores(j, q_ref[0, :, sl], sl, masked)
                m_old = m_ref[hh]
                m_new = jnp.maximum(m_old, jnp.max(st, axis=0, keepdims=True))
                pt = jnp.exp(st - m_new).astype(BF16)
                acc_ref[hh] = jnp.exp(m_old - m_new) * acc_ref[hh] + _dot(vt_ref[j, sl, :], pt)
                m_ref[hh] = m_new

        def body(j, carry):
            chunk(j, False)
            return carry

        lax.fori_loop(0, qi, body, 0)
        chunk(qi, True)

    outs = []
    for hh in range(2):
        acc = acc_ref[hh]
        outs.append(acc[0:dh, :] / acc[dh:dh + 1, :])
    o_ref[0] = jnp.concatenate(outs, axis=0).T.astype(o_ref.dtype)


def _fox_attn(q, k, v, *, n_heads, dh, tq):
    B, S, _ = q.shape
    nq = S // tq
    pair = 2 * HEAD_SLOT
    kv_spec = pl.BlockSpec((1, S, pair), lambda b, h, i: (b, 0, h))
    return pl.pallas_call(
        functools.partial(_fox_attn_kernel, dh=dh, tq=tq),
        grid=(B, n_heads // 2, nq),
        in_specs=[pl.BlockSpec((1, tq, pair), lambda b, h, i: (b, i, h)), kv_spec, kv_spec],
        out_specs=pl.BlockSpec((1, tq, 2 * dh), lambda b, h, i: (b, i, h)),
        out_shape=jax.ShapeDtypeStruct((B, S, n_heads * dh), F32),
        scratch_shapes=[pltpu.VMEM((nq, pair, tq), BF16), pltpu.VMEM((2, tq, HEAD_SLOT), BF16),
                        pltpu.VMEM((2, 1, tq), F32), pltpu.VMEM((2, HEAD_SLOT, tq), F32)],
        compiler_params=pltpu.CompilerParams(
            dimension_semantics=("parallel", "parallel", "arbitrary"),
            vmem_limit_bytes=VMEM_LIMIT),
        name="fox_attn",
    )(q, k, v)


def _mem_kv_kernel(m_ref, g_ref, w_ref, kg_ref, k_ref, v_ref, *, n_heads, dh):
    mw = n_heads * dh
    mb = _rms(m_ref[0], g_ref[...]).astype(BF16)
    kv = _dot(mb, w_ref[...])
    parts = []
    for h in range(n_heads):
        parts.append(_rms(kv[:, h * dh:(h + 1) * dh], kg_ref[...]))
    k_ref[0] = jnp.concatenate(parts, axis=-1).astype(BF16)
    v_ref[0] = kv[:, mw:].astype(BF16)


def _mem_kv(mem, norm_g, wkv, k_g, *, n_heads, dh):
    B, M, D = mem.shape
    mw = n_heads * dh
    full = lambda shape: pl.BlockSpec(shape, lambda b: (0,) * len(shape))
    out = pl.BlockSpec((1, M, mw), lambda b: (b, 0, 0))
    return pl.pallas_call(
        functools.partial(_mem_kv_kernel, n_heads=n_heads, dh=dh),
        grid=(B,),
        in_specs=[pl.BlockSpec((1, M, D), lambda b: (b, 0, 0)), full((1, D)), full((D, 2 * mw)),
                  full((1, dh))],
        out_specs=[out, out],
        out_shape=[jax.ShapeDtypeStruct((B, M, mw), BF16)] * 2,
        compiler_params=pltpu.CompilerParams(
            dimension_semantics=("parallel",), vmem_limit_bytes=VMEM_LIMIT),
        name="mem_kv",
    )(mem, norm_g.reshape(1, D), wkv.astype(BF16), k_g.reshape(1, dh))


def _out_mem_kernel(x_ref, yl_ref, yf_ref, fg_ref, wol_ref, wof_ref, gx_ref, wq_ref, qg_ref,
                    km_ref, vm_ref, wo_ref, gf_ref, wr_ref, br_ref,
                    x2_ref, xn_ref, gates_ref, experts_ref, *, n_heads, dh, n_groups, per_group):
    yf = _rms(yf_ref[0], fg_ref[...]).astype(BF16)
    x1 = x_ref[0] + _dot(yl_ref[0], wol_ref[...]) + _dot(yf, wof_ref[...])

    q = _dot(_rms(x1, gx_ref[...]).astype(BF16), wq_ref[...])
    outs = []
    for h in range(n_heads):
        sl = slice(h * dh, (h + 1) * dh)
        qh = _rms(q[:, sl], qg_ref[...]).astype(BF16)
        s = lax.dot_general(qh, km_ref[0, :, sl], (((1,), (1,)), ((), ())),
                            preferred_element_type=F32)
        p = jnp.exp(s - jnp.max(s, axis=-1, keepdims=True))
        p = p / jnp.sum(p, axis=-1, keepdims=True)
        outs.append(_dot(p.astype(BF16), vm_ref[0, :, sl]))
    x2 = x1 + _dot(jnp.concatenate(outs, axis=-1).astype(BF16), wo_ref[...])
    x2_ref[0] = x2

    xn = _rms(x2, gf_ref[...])
    xn_ref[0] = xn
    logits = _dot(xn.astype(BF16), wr_ref[...]) + br_ref[...]
    lane = lax.broadcasted_iota(jnp.int32, logits.shape, 1).astype(F32)
    neg = -jnp.inf

    def top(vals):
        mx = jnp.max(vals, axis=-1, keepdims=True)
        idx = jnp.min(jnp.where(vals == mx, lane, float(LANES)), axis=-1, keepdims=True)
        return mx, idx

    gl = jnp.where(lane < n_groups, logits, neg)
    g_max, g_idx = top(gl)
    g_w = 1.0 / jnp.sum(jnp.exp(gl - g_max), axis=-1, keepdims=True)
    lo = n_groups + per_group * g_idx
    el = jnp.where((lane >= lo) & (lane < lo + per_group), logits, neg)
    e1, i1 = top(el)
    e2, i2 = top(jnp.where(lane == i1, neg, el))
    t = jnp.exp(e2 - e1)
    w1 = g_w / (1.0 + t)
    w2 = g_w * t / (1.0 + t)
    gates_ref[0] = jnp.where(lane == 0, w1, jnp.where(lane == 1, w2, 0.0))
    experts_ref[0] = jnp.where(lane == 0, i1 - n_groups,
                               jnp.where(lane == 1, i2 - n_groups, 0.0)).astype(jnp.int32)


def _out_mem(x, y_lru, y_fox, fox_out_g, w_out, norm_mem_x_g, mem_wq, mem_q_g, k_mem, v_mem, mem_wo,
             norm_ffn_g, router_group_w, router_group_b, router_expert_w, router_expert_b, *,
             n_heads, dh, tm):
    B, S, D = x.shape
    lw = y_lru.shape[-1]
    fw = y_fox.shape[-1]
    M = k_mem.shape[1]
    mw = n_heads * dh
    n_groups = router_group_w.shape[-1]
    n_experts = router_expert_w.shape[-1]
    w_r = jnp.concatenate([router_group_w, router_expert_w], axis=1)
    w_r = jnp.pad(w_r, ((0, 0), (0, LANES - w_r.shape[1]))).astype(BF16)
    b_r = jnp.concatenate([router_group_b, router_expert_b])
    b_r = jnp.pad(b_r, (0, LANES - b_r.shape[0])).reshape(1, LANES)
    q_gain = (mem_q_g * dh ** -0.5).reshape(1, dh)

    full = lambda shape: pl.BlockSpec(shape, lambda b, s: (0,) * len(shape))
    row = lambda width: pl.BlockSpec((1, tm, width), lambda b, s: (b, s, 0))
    mem_spec = pl.BlockSpec((1, M, mw), lambda b, s: (b, 0, 0))
    return pl.pallas_call(
        functools.partial(_out_mem_kernel, n_heads=n_heads, dh=dh, n_groups=n_groups,
                          per_group=n_experts // n_groups),
        grid=(B, S // tm),
        in_specs=[row(D), row(lw), row(fw), full((1, fw)), full((lw, D)), full((fw, D)),
                  full((1, D)), full((D, mw)), full((1, dh)), mem_spec, mem_spec, full((mw, D)),
                  full((1, D)), full((D, LANES)), full((1, LANES))],
        out_specs=[row(D), row(D), row(LANES), row(LANES)],
        out_shape=[jax.ShapeDtypeStruct((B, S, D), F32), jax.ShapeDtypeStruct((B, S, D), F32),
                   jax.ShapeDtypeStruct((B, S, LANES), F32),
                   jax.ShapeDtypeStruct((B, S, LANES), jnp.int32)],
        compiler_params=pltpu.CompilerParams(
            dimension_semantics=("parallel", "parallel"), vmem_limit_bytes=VMEM_LIMIT),
        name="out_mem",
    )(x, y_lru, y_fox, fox_out_g.reshape(1, fw), w_out[:lw].astype(BF16), w_out[lw:].astype(BF16),
      norm_mem_x_g.reshape(1, D), mem_wq.astype(BF16), q_gain, k_mem, v_mem, mem_wo.astype(BF16),
      norm_ffn_g.reshape(1, D), w_r, b_r)


DMA_UNROLL = 8


def _scatter_rows_kernel(idx_ref, x_ref, buf_in_ref, buf_ref, sem, *, top_k):
    del buf_in_ref
    rows = x_ref.shape[0]

    def issue(g, carry):
        for u in range(DMA_UNROLL):
            r = g * DMA_UNROLL + u
            for k in range(top_k):
                pltpu.make_async_copy(x_ref.at[pl.ds(r, 1), :],
                                      buf_ref.at[pl.ds(idx_ref[0, k, r], 1), :], sem).start()
        return carry

    lax.fori_loop(0, rows // DMA_UNROLL, issue, 0)
    for k in range(top_k):
        pltpu.make_async_copy(x_ref, buf_ref.at[pl.ds(0, rows), :], sem).wait()


def _scatter_rows(x, dest, n_rows, *, tm):
    T, D = x.shape
    nt, top_k, _ = dest.shape
    return pl.pallas_call(
        functools.partial(_scatter_rows_kernel, top_k=top_k),
        grid=(nt,),
        in_specs=[pl.BlockSpec((1, top_k, tm), lambda i: (i, 0, 0), memory_space=pltpu.SMEM),
                  pl.BlockSpec((tm, D), lambda i: (i, 0)),
                  pl.BlockSpec(memory_space=pl.ANY)],
        out_specs=pl.BlockSpec(memory_space=pl.ANY),
        out_shape=jax.ShapeDtypeStruct((n_rows, D), x.dtype),
        scratch_shapes=[pltpu.SemaphoreType.DMA],
        input_output_aliases={2: 0},
        compiler_params=pltpu.CompilerParams(
            dimension_semantics=("arbitrary",), vmem_limit_bytes=VMEM_LIMIT),
        name="scatter_rows",
    )(dest, x, jnp.zeros((n_rows, D), x.dtype))


def _experts_kernel(be_ref, x_ref, wg_ref, wu_ref, wd_ref, o_ref):
    del be_ref
    xb = x_ref[...].astype(BF16)
    hdn = jax.nn.silu(_dot(xb, wg_ref[0])) * _dot(xb, wu_ref[0])
    o_ref[...] = _dot(hdn.astype(BF16), wd_ref[0])


def _experts(x_buf, blk_expert, w_gate, w_up, w_down, *, blk):
    P, D = x_buf.shape
    de = w_gate.shape[-1]
    grid_spec = pltpu.PrefetchScalarGridSpec(
        num_scalar_prefetch=1,
        grid=(P // blk,),
        in_specs=[pl.BlockSpec((blk, D), lambda i, be: (i, 0)),
                  pl.BlockSpec((1, D, de), lambda i, be: (be[i], 0, 0)),
                  pl.BlockSpec((1, D, de), lambda i, be: (be[i], 0, 0)),
                  pl.BlockSpec((1, de, D), lambda i, be: (be[i], 0, 0))],
        out_specs=pl.BlockSpec((blk, D), lambda i, be: (i, 0)),
    )
    return pl.pallas_call(
        _experts_kernel,
        grid_spec=grid_spec,
        out_shape=jax.ShapeDtypeStruct((P, D), F32),
        compiler_params=pltpu.CompilerParams(
            dimension_semantics=("arbitrary",), vmem_limit_bytes=VMEM_LIMIT),
        name="experts",
    )(blk_expert, x_buf, w_gate.astype(BF16), w_up.astype(BF16), w_down.astype(BF16))


def _combine_kernel(idx_ref, x_ref, g_ref, y_ref, o_ref, buf_ref, sem, *, top_k):
    rows = x_ref.shape[0]

    def issue(g, carry):
        for u in range(DMA_UNROLL):
            r = g * DMA_UNROLL + u
            for k in range(top_k):
                pltpu.make_async_copy(y_ref.at[pl.ds(idx_ref[0, k, r], 1), :],
                                      buf_ref.at[k, pl.ds(r, 1), :], sem).start()
        return carry

    lax.fori_loop(0, rows // DMA_UNROLL, issue, 0)
    for k in range(top_k):
        pltpu.make_async_copy(y_ref.at[pl.ds(0, rows), :], buf_ref.at[k], sem).wait()
    out = x_ref[...]
    for k in range(top_k):
        out = out + g_ref[:, k:k + 1] * buf_ref[k]
    o_ref[...] = out


def _combine(x2, gates, y_buf, dest, *, tm):
    T, D = x2.shape
    nt, top_k, _ = dest.shape
    return pl.pallas_call(
        functools.partial(_combine_kernel, top_k=top_k),
        grid=(nt,),
        in_specs=[pl.BlockSpec((1, top_k, tm), lambda i: (i, 0, 0), memory_space=pltpu.SMEM),
                  pl.BlockSpec((tm, D), lambda i: (i, 0)),
                  pl.BlockSpec((tm, LANES), lambda i: (i, 0)),
                  pl.BlockSpec(memory_space=pl.ANY)],
        out_specs=pl.BlockSpec((tm, D), lambda i: (i, 0)),
        out_shape=jax.ShapeDtypeStruct((T, D), F32),
        scratch_shapes=[pltpu.VMEM((top_k, tm, D), F32), pltpu.SemaphoreType.DMA],
        compiler_params=pltpu.CompilerParams(
            dimension_semantics=("arbitrary",), vmem_limit_bytes=VMEM_LIMIT),
        name="combine",
    )(dest, x2, gates, y_buf)


def _dispatch(experts, n_experts, blk):
    T, K = experts.shape
    A = T * K
    e_flat = experts.reshape(A)
    onehot = (e_flat[:, None] == jnp.arange(n_experts, dtype=jnp.int32)[None, :]).astype(jnp.int32)
    ranks = jnp.cumsum(onehot, axis=0)
    counts = ranks[-1]
    padded = (counts + blk - 1) // blk * blk
    pends = jnp.cumsum(padded)
    pstarts = pends - padded
    dest = jnp.sum(onehot * (ranks - 1 + pstarts[None, :]), axis=1)
    n_rows = A + n_experts * blk
    blk_start = jnp.arange(n_rows // blk, dtype=jnp.int32) * blk
    blk_expert = jnp.minimum(jnp.sum(blk_start[:, None] >= pends[None, :], axis=1), n_experts - 1)
    return dest.reshape(T, K), blk_expert.astype(jnp.int32), n_rows


def kernel(x, mem, norm_mix_g, w_in, b_forget, conv_w, conv_b, lru_wa, lru_ba, lru_wx, lru_bx,
           lru_a_param, fox_q_g, fox_k_g, lru_out_g, fox_out_g, w_out, norm_mem_x_g, norm_mem_g,
           mem_wq, mem_wkv, mem_q_g, mem_k_g, mem_wo, norm_ffn_g, router_group_w, router_group_b,
           router_expert_w, router_expert_b, exp_w_gate, exp_w_up, exp_w_down):
    B, S, D = x.shape
    depth = norm_mix_g.shape[0]
    lw = conv_w.shape[-1]
    n_heads = b_forget.shape[-1]
    dh = fox_q_g.shape[-1]
    mem_dh = mem_q_g.shape[-1]
    mem_heads = mem_wq.shape[-1] // mem_dh
    n_experts = router_expert_w.shape[-1]
    T = B * S
    tm = _pick_tile(S, 512)
    tq = _pick_tile(S, 512)
    blk = 256
    scatter_tm = _pick_tile(T, 1024)

    for l in range(depth):
        u, gate, q, k, v = _in_proj(x, norm_mix_g[l], w_in[l], b_forget[l], fox_q_g[l], fox_k_g[l],
                                    lw=lw, n_heads=n_heads, dh=dh, tm=tm)
        y_lru = _rg_lru(u, gate, conv_w[l], conv_b[l], lru_wa[l], lru_ba[l], lru_wx[l], lru_bx[l],
                        lru_a_param[l], lru_out_g[l], ts=tm)
        y_fox = _fox_attn(q, k, v, n_heads=n_heads, dh=dh, tq=tq)
        k_mem, v_mem = _mem_kv(mem, norm_mem_g[l], mem_wkv[l], mem_k_g[l], n_heads=mem_heads,
                               dh=mem_dh)
        x2, xn, gates, experts = _out_mem(
            x, y_lru, y_fox, fox_out_g[l], w_out[l], norm_mem_x_g[l], mem_wq[l], mem_q_g[l], k_mem,
            v_mem, mem_wo[l], norm_ffn_g[l], router_group_w[l], router_group_b[l],
            router_expert_w[l], router_expert_b[l], n_heads=mem_heads, dh=mem_dh, tm=tm)

        dest, blk_expert, n_rows = _dispatch(experts.reshape(T, LANES)[:, :TOP_K], n_experts, blk)
        tiled = lambda t: dest.reshape(T // t, t, TOP_K).transpose(0, 2, 1)
        x_buf = _scatter_rows(xn.reshape(T, D), tiled(scatter_tm), n_rows, tm=scatter_tm)
        y_buf = _experts(x_buf, blk_expert, exp_w_gate[l], exp_w_up[l], exp_w_down[l], blk=blk)
        x = _combine(x2.reshape(T, D), gates.reshape(T, LANES), y_buf, tiled(tm), tm=tm)
        x = x.reshape(B, S, D)
    return x
```

```python
import functools

import jax
import jax.numpy as jnp
import numpy as np
from jax import lax
from jax.experimental import pallas as pl
from jax.experimental.pallas import tpu as pltpu

EPS = 1e-6
LRU_C = 8.0
CONV_WIDTH = 4
TOP_K = 2
LANES = 128
SUBLANES = 8
MXU_DIM = 256
HEAD_SLOT = LANES
VMEM_LIMIT = 56 * 1024 * 1024

F32 = jnp.float32
BF16 = jnp.bfloat16


def _dot(a, b):
    return jnp.dot(a, b, preferred_element_type=F32)


def _rms(x, g):
    return x * lax.rsqrt(jnp.mean(x * x, axis=-1, keepdims=True) + EPS) * g


def _softplus(x):
    return jnp.maximum(x, 0.0) + jnp.log1p(jnp.exp(-jnp.abs(x)))


def _pick_tile(n, target):
    t = min(n, target)
    while n % t:
        t //= 2
    return t


def _store_chunk_rows(ref, val):
    n, d = val.shape
    chunks = d // LANES
    for c in range(chunks):
        ref[pl.ds(c, n, stride=chunks), :] = val[:, c * LANES:(c + 1) * LANES]


def _load_chunk_rows(ref, n, d):
    chunks = d // LANES
    return jnp.concatenate([ref[pl.ds(c, n, stride=chunks), :] for c in range(chunks)], axis=1)


def _in_proj_kernel(x_ref, g_ref, w_ref, wf_ref, bf_ref, gq_ref, gk_ref, pe_ref, cv_ref,
                    u_ref, gate_ref, q_ref, k_ref, v_ref, carry_ref, *, lw, n_heads, dh):
    @pl.when(pl.program_id(1) == 0)
    def _():
        carry_ref[...] = jnp.zeros_like(carry_ref)

    hs = n_heads * HEAD_SLOT
    x = x_ref[0]
    tm = x.shape[0]
    hb = _rms(x, g_ref[...]).astype(BF16)

    u_ref[0] = _dot(hb, w_ref[:, 0:lw])
    gate_ref[0] = _dot(hb, w_ref[:, lw:2 * lw])

    z = _dot(hb, wf_ref[...]) + bf_ref[...]
    lane = lax.broadcasted_iota(jnp.int32, z.shape, 1)
    row = lax.broadcasted_iota(jnp.int32, z.shape, 0)
    c = jnp.where(lane < n_heads, -_softplus(-z), 0.0)
    d = 1
    while d < tm:
        c = c + jnp.where(row >= d, pltpu.roll(c, d, 0), 0.0)
        d *= 2
    c = c + carry_ref[...]
    carry_ref[...] = c[tm - 1:tm, :]
    c1 = c.astype(BF16).astype(F32)
    r1 = c - c1
    c2 = r1.astype(BF16).astype(F32)
    c3 = (r1 - c2).astype(BF16).astype(F32)
    e = c1 + pltpu.roll(c2, n_heads, 1) + pltpu.roll(c3, 2 * n_heads, 1)
    e = jnp.where(lane == 3 * n_heads, 1.0, e).astype(BF16)
    ext = _dot(e, pe_ref[...])

    fw = n_heads * dh
    lane_t = lax.broadcasted_iota(jnp.int32, (tm, LANES), 1)
    own = (lane_t < dh, lane_t >= dh)

    def to_slots(y):
        parts = []
        for c in range(fw // LANES):
            col = y[:, c * LANES:(c + 1) * LANES]
            parts += [jnp.where(own[0], col, 0.0), jnp.where(own[1], col, 0.0)]
        return jnp.concatenate(parts, axis=-1)

    def head_norm(y, gain):
        parts = []
        for h in range(n_heads):
            blk = y[:, h * HEAD_SLOT:(h + 1) * HEAD_SLOT]
            ss = jnp.sum(blk * blk, axis=-1, keepdims=True) * (1.0 / dh)
            parts.append(blk * lax.rsqrt(ss + EPS))
        return jnp.concatenate(parts, axis=-1) * gain

    q = to_slots(_dot(hb, w_ref[:, 2 * lw:2 * lw + fw]))
    q_ref[0] = (head_norm(q, gq_ref[...]) + ext[:, 0:hs]).astype(BF16)
    k = to_slots(_dot(hb, w_ref[:, 2 * lw + fw:2 * lw + 2 * fw]))
    k_ref[0] = (head_norm(k, gk_ref[...]) + ext[:, hs:2 * hs]).astype(BF16)
    v = to_slots(_dot(hb, w_ref[:, 2 * lw + 2 * fw:2 * lw + 3 * fw]))
    v_ref[0] = (v + cv_ref[...]).astype(BF16)


def _in_proj(x, norm_g, w_in, b_forget, fox_q_g, fox_k_g, *, lw, n_heads, dh, tm):
    B, S, D = x.shape
    fw = n_heads * dh
    hs = n_heads * HEAD_SLOT
    assert 2 * dh == HEAD_SLOT and fw % LANES == 0 and 3 * n_heads <= LANES
    n_main = 2 * lw + 3 * fw
    w_all = w_in[:, :n_main].astype(BF16)
    w_f = jnp.pad(w_in[:, n_main:], ((0, 0), (0, LANES - n_heads))).astype(BF16)
    b_f = jnp.pad(b_forget, (0, LANES - n_heads)).reshape(1, LANES)
    scale = dh ** -0.5

    def slot_gains(g):
        even = jnp.pad(g, (0, HEAD_SLOT - dh))
        odd = jnp.pad(g, (HEAD_SLOT - dh, 0))
        return jnp.tile(jnp.concatenate([even, odd]), n_heads // 2).reshape(1, hs)

    gq = slot_gains(fox_q_g * scale)
    gk = slot_gains(fox_k_g)
    ones_row = 3 * n_heads
    pe = np.zeros((LANES, 2 * hs), np.float32)
    cv = np.zeros((1, hs), np.float32)
    for h in range(n_heads):
        x0 = h * HEAD_SLOT + (dh if h % 2 == 0 else 0)
        for j in range(3):
            pe[j * n_heads + h, x0 + j] = 1.0
            pe[ones_row, x0 + 3 + j] = 1.0
            pe[ones_row, hs + x0 + j] = 1.0
            pe[j * n_heads + h, hs + x0 + 3 + j] = -1.0
        cv[0, x0] = 1.0
    pe = jnp.asarray(pe, BF16)

    full = lambda shape: pl.BlockSpec(shape, lambda b, s: (0,) * len(shape))
    row = lambda width: pl.BlockSpec((1, tm, width), lambda b, s: (b, s, 0))
    return pl.pallas_call(
        functools.partial(_in_proj_kernel, lw=lw, n_heads=n_heads, dh=dh),
        grid=(B, S // tm),
        in_specs=[row(D), full((1, D)), full(w_all.shape), full(w_f.shape), full((1, LANES)),
                  full((1, hs)), full((1, hs)), full(pe.shape), full((1, hs))],
        out_specs=[row(lw), row(lw), row(hs), row(hs), row(hs)],
        out_shape=[jax.ShapeDtypeStruct((B, S, lw), F32), jax.ShapeDtypeStruct((B, S, lw), F32),
                   jax.ShapeDtypeStruct((B, S, hs), BF16), jax.ShapeDtypeStruct((B, S, hs), BF16),
                   jax.ShapeDtypeStruct((B, S, hs), BF16)],
        scratch_shapes=[pltpu.VMEM((1, LANES), F32)],
        compiler_params=pltpu.CompilerParams(
            dimension_semantics=("parallel", "arbitrary"), vmem_limit_bytes=VMEM_LIMIT),
        name="in_proj",
    )(x, norm_g.reshape(1, D), w_all, w_f, b_f, gq, gk, pe, jnp.asarray(cv))


def _rg_lru_kernel(u_ref, gate_ref, cw_ref, cb_ref, w_ref, ba_ref, bx_ref, ap_ref, og_ref,
                   o_ref, tail_ref, h_ref):
    @pl.when(pl.program_id(1) == 0)
    def _():
        tail_ref[...] = jnp.zeros_like(tail_ref)
        h_ref[...] = jnp.zeros_like(h_ref)

    u = u_ref[0]
    ts, lw = u.shape
    ext = jnp.concatenate([tail_ref[...], u], axis=0)
    tail_ref[...] = u[ts - SUBLANES:, :]
    xc = cb_ref[...] + u * cw_ref[CONV_WIDTH - 1:CONV_WIDTH, :]
    for back in range(1, CONV_WIDTH):
        tap = CONV_WIDTH - 1 - back
        xc = xc + pltpu.roll(ext, back, 0)[SUBLANES:, :] * cw_ref[tap:tap + 1, :]

    xb = xc.astype(BF16)
    r_parts, i_parts = [], []
    for j in range(lw // MXU_DIM):
        y = _dot(xb[:, j * MXU_DIM:(j + 1) * MXU_DIM], w_ref[j])
        r_parts.append(y[:, :MXU_DIM])
        i_parts.append(y[:, MXU_DIM:])
    r = jax.nn.sigmoid(jnp.concatenate(r_parts, axis=-1) + ba_ref[...])
    i = jax.nn.sigmoid(jnp.concatenate(i_parts, axis=-1) + bx_ref[...])
    log_a = (-LRU_C) * r * _softplus(-ap_ref[...])
    a = jnp.exp(log_a)
    mult = jnp.sqrt(jnp.maximum(-jnp.tanh(log_a) * (a * a + 1.0), 0.0))
    b = mult * (i * xc)

    groups = ts // SUBLANES
    a = a.reshape(groups, SUBLANES, lw)
    b = b.reshape(groups, SUBLANES, lw)
    row = lax.broadcasted_iota(jnp.int32, a.shape, 1)
    d = 1
    while d < SUBLANES:
        keep = row >= d
        a_prev = jnp.where(keep, pltpu.roll(a, d, 1), 1.0)
        b_prev = jnp.where(keep, pltpu.roll(b, d, 1), 0.0)
        b = a * b_prev + b
        a = a * a_prev
        d *= 2
    state = h_ref[...]
    parts = []
    for g in range(groups):
        hg = b[g] + a[g] * state
        state = hg[SUBLANES - 1:, :]
        parts.append(hg)
    h = jnp.concatenate(parts, axis=0)
    h_ref[...] = state

    y = h * jax.nn.gelu(gate_ref[0])
    o_ref[0] = _rms(y, og_ref[...]).astype(o_ref.dtype)


def _rg_lru(u, gate, conv_w, conv_b, wa, ba, wx, bx, a_param, out_g, *, ts):
    B, S, lw = u.shape
    nb, bd, _ = wa.shape
    per = MXU_DIM // bd
    n_tiles = lw // MXU_DIM

    def tiles(w):
        w = w.reshape(n_tiles, per, bd, bd)
        eye = jnp.eye(per, dtype=w.dtype)
        return jnp.einsum('tpij,pq->tpiqj', w, eye).reshape(n_tiles, MXU_DIM, MXU_DIM)

    w_cat = jnp.concatenate([tiles(wa), tiles(wx)], axis=-1).astype(BF16)
    vec = lambda a: a.reshape(1, lw)
    full = lambda shape: pl.BlockSpec(shape, lambda b, s: (0,) * len(shape))
    row = pl.BlockSpec((1, ts, lw), lambda b, s: (b, s, 0))
    return pl.pallas_call(
        _rg_lru_kernel,
        grid=(B, S // ts),
        in_specs=[row, row, full((CONV_WIDTH, lw)), full((1, lw)), full(w_cat.shape), full((1, lw)),
                  full((1, lw)), full((1, lw)), full((1, lw))],
        out_specs=row,
        out_shape=jax.ShapeDtypeStruct((B, S, lw), BF16),
        scratch_shapes=[pltpu.VMEM((SUBLANES, lw), F32), pltpu.VMEM((1, lw), F32)],
        compiler_params=pltpu.CompilerParams(
            dimension_semantics=("parallel", "arbitrary"), vmem_limit_bytes=VMEM_LIMIT),
        name="rg_lru",
    )(u, gate, conv_w, vec(conv_b), w_cat, vec(ba), vec(bx), vec(a_param), vec(out_g))


OVERFLOW_GUARD = 1e30


def _fox_attn_kernel(q_ref, k_ref, v_ref, o_ref, vt_ref, m_ref, acc_ref, *, dh, tq):
    qi = pl.program_id(2)
    n_chunks = vt_ref.shape[0]

    @pl.when(qi == 0)
    def _():
        for c in range(n_chunks):
            vt_ref[c] = v_ref[0, c * tq:(c + 1) * tq, :].astype(F32).T.astype(BF16)

    heads = [slice(hh * HEAD_SLOT, (hh + 1) * HEAD_SLOT) for hh in range(2)]

    def scores(j, q, sl, masked):
        start = pl.multiple_of(j * tq, tq)
        st = lax.dot_general(k_ref[0, pl.ds(start, tq), sl], q, (((1,), (1,)), ((), ())),
                             preferred_element_type=F32)
        if masked:
            kpos = lax.broadcasted_iota(jnp.int32, st.shape, 0)
            qpos = lax.broadcasted_iota(jnp.int32, st.shape, 1)
            st = jnp.where(kpos <= qpos, st, -jnp.inf)
        return st

    for hh, sl in enumerate(heads):
        st = scores(qi, q_ref[0, :, sl], sl, True)
        m = jnp.max(st, axis=0, keepdims=True)
        m_ref[hh] = m
        acc_ref[hh] = _dot(vt_ref[qi, sl, :], jnp.exp(st - m).astype(BF16))

    def fast_chunk(j):
        for hh, sl in enumerate(heads):
            pt = jnp.exp(scores(j, q_ref[0, :, sl], sl, False) - m_ref[hh]).astype(BF16)
            acc_ref[hh] += _dot(vt_ref[j, sl, :], pt)

    def fast_pair(i, carry):
        fast_chunk(2 * i)
        fast_chunk(2 * i + 1)
        return carry

    lax.fori_loop(0, qi // 2, fast_pair, 0)

    @pl.when(qi % 2 == 1)
    def _():
        fast_chunk(qi - 1)

    def softmax_sums():
        return jnp.concatenate([acc_ref[0][dh:dh + 1, :], acc_ref[1][0:1, :]], axis=0)

    def normalised():
        sums = softmax_sums()
        return jnp.concatenate([acc_ref[0][0:dh, :] / sums[0:1, :],
                                acc_ref[1][dh:2 * dh, :] / sums[1:2, :]], axis=0)

    out = normalised()
    o_ref[0] = out.T.astype(o_ref.dtype)
    sums = softmax_sums()
    flag = lambda x: jnp.max(jnp.where(jnp.abs(x) < OVERFLOW_GUARD, 0.0, 1.0),
                             axis=0, keepdims=True)
    bad = jnp.max(jnp.maximum(flag(out), flag(sums)), axis=1, keepdims=True)

    @pl.when(bad[0, 0] > 0.0)
    def _():
        m_ref[...] = jnp.full_like(m_ref, -jnp.inf)
        acc_ref[...] = jnp.zeros_like(acc_ref)

        def chunk(j, masked):
            for hh, sl in enumerate(heads):
                st = scores(j, q_ref[0, :, sl], sl, masked)
                m_old = m_ref[hh]
                m_new = jnp.maximum(m_old, jnp.max(st, axis=0, keepdims=True))
                pt = jnp.exp(st - m_new).astype(BF16)
                acc_ref[hh] = jnp.exp(m_old - m_new) * acc_ref[hh] + _dot(vt_ref[j, sl, :], pt)
                m_ref[hh] = m_new

        def body(j, carry):
            chunk(j, False)
            return carry

        lax.fori_loop(0, qi, body, 0)
        chunk(qi, True)
        o_ref[0] = normalised().T.astype(o_ref.dtype)


def _fox_attn(q, k, v, *, n_heads, dh, tq):
    B, S, _ = q.shape
    nq = S // tq
    pair = 2 * HEAD_SLOT
    kv_spec = pl.BlockSpec((1, S, pair), lambda b, h, i: (b, 0, h))
    return pl.pallas_call(
        functools.partial(_fox_attn_kernel, dh=dh, tq=tq),
        grid=(B, n_heads // 2, nq),
        in_specs=[pl.BlockSpec((1, tq, pair), lambda b, h, i: (b, i, h)), kv_spec, kv_spec],
        out_specs=pl.BlockSpec((1, tq, 2 * dh), lambda b, h, i: (b, i, h)),
        out_shape=jax.ShapeDtypeStruct((B, S, n_heads * dh), F32),
        scratch_shapes=[pltpu.VMEM((nq, pair, tq), BF16), pltpu.VMEM((2, 1, tq), F32),
                        pltpu.VMEM((2, HEAD_SLOT, tq), F32)],
        compiler_params=pltpu.CompilerParams(
            dimension_semantics=("parallel", "parallel", "arbitrary"),
            vmem_limit_bytes=VMEM_LIMIT),
        name="fox_attn",
    )(q, k, v)


def _mem_kv_kernel(m_ref, g_ref, w_ref, kg_ref, k_ref, v_ref, *, n_heads, dh):
    mw = n_heads * dh
    mb = _rms(m_ref[0], g_ref[...]).astype(BF16)
    kv = _dot(mb, w_ref[...])
    parts = []
    for h in range(n_heads):
        parts.append(_rms(kv[:, h * dh:(h + 1) * dh], kg_ref[...]))
    k_ref[0] = jnp.concatenate(parts, axis=-1).astype(BF16)
    v_ref[0] = kv[:, mw:].astype(BF16)


def _mem_kv(mem, norm_g, wkv, k_g, *, n_heads, dh):
    B, M, D = mem.shape
    mw = n_heads * dh
    full = lambda shape: pl.BlockSpec(shape, lambda b: (0,) * len(shape))
    out = pl.BlockSpec((1, M, mw), lambda b: (b, 0, 0))
    return pl.pallas_call(
        functools.partial(_mem_kv_kernel, n_heads=n_heads, dh=dh),
        grid=(B,),
        in_specs=[pl.BlockSpec((1, M, D), lambda b: (b, 0, 0)), full((1, D)), full((D, 2 * mw)),
                  full((1, dh))],
        out_specs=[out, out],
        out_shape=[jax.ShapeDtypeStruct((B, M, mw), BF16)] * 2,
        compiler_params=pltpu.CompilerParams(
            dimension_semantics=("parallel",), vmem_limit_bytes=VMEM_LIMIT),
        name="mem_kv",
    )(mem, norm_g.reshape(1, D), wkv.astype(BF16), k_g.reshape(1, dh))


def _out_mem_kernel(x_ref, yl_ref, yf_ref, fg_ref, wol_ref, wof_ref, gx_ref, wq_ref, qg_ref,
                    km_ref, vm_ref, wo_ref, gf_ref, wr_ref, br_ref,
                    x2_ref, xn_ref, gates_ref, experts_ref, *, n_heads, dh, n_groups, per_group):
    yf = _rms(yf_ref[0], fg_ref[...]).astype(BF16)
    x1 = x_ref[0] + _dot(yl_ref[0], wol_ref[...]) + _dot(yf, wof_ref[...])

    q = _dot(_rms(x1, gx_ref[...]).astype(BF16), wq_ref[...])
    outs = []
    for h in range(n_heads):
        sl = slice(h * dh, (h + 1) * dh)
        qh = _rms(q[:, sl], qg_ref[...]).astype(BF16)
        s = lax.dot_general(qh, km_ref[0, :, sl], (((1,), (1,)), ((), ())),
                            preferred_element_type=F32)
        p = jnp.exp(s - jnp.max(s, axis=-1, keepdims=True))
        p = p / jnp.sum(p, axis=-1, keepdims=True)
        outs.append(_dot(p.astype(BF16), vm_ref[0, :, sl]))
    x2 = x1 + _dot(jnp.concatenate(outs, axis=-1).astype(BF16), wo_ref[...])
    x2_ref[0] = x2

    xn = _rms(x2, gf_ref[...])
    _store_chunk_rows(xn_ref, xn)
    logits = _dot(xn.astype(BF16), wr_ref[...]) + br_ref[...]
    lane = lax.broadcasted_iota(jnp.int32, logits.shape, 1).astype(F32)
    neg = -jnp.inf

    def top(vals):
        mx = jnp.max(vals, axis=-1, keepdims=True)
        idx = jnp.min(jnp.where(vals == mx, lane, float(LANES)), axis=-1, keepdims=True)
        return mx, idx

    gl = jnp.where(lane < n_groups, logits, neg)
    g_max, g_idx = top(gl)
    g_w = 1.0 / jnp.sum(jnp.exp(gl - g_max), axis=-1, keepdims=True)
    lo = n_groups + per_group * g_idx
    el = jnp.where((lane >= lo) & (lane < lo + per_group), logits, neg)
    e1, i1 = top(el)
    e2, i2 = top(jnp.where(lane == i1, neg, el))
    t = jnp.exp(e2 - e1)
    w1 = g_w / (1.0 + t)
    w2 = g_w * t / (1.0 + t)
    gates_ref[0] = jnp.where(lane == 0, w1, jnp.where(lane == 1, w2, 0.0))
    experts_ref[0] = jnp.where(lane == 0, i1 - n_groups,
                               jnp.where(lane == 1, i2 - n_groups, 0.0)).astype(jnp.int32)


def _out_mem(x, y_lru, y_fox, fox_out_g, w_out, norm_mem_x_g, mem_wq, mem_q_g, k_mem, v_mem, mem_wo,
             norm_ffn_g, router_group_w, router_group_b, router_expert_w, router_expert_b, *,
             n_heads, dh, tm):
    B, S, D = x.shape
    lw = y_lru.shape[-1]
    fw = y_fox.shape[-1]
    M = k_mem.shape[1]
    mw = n_heads * dh
    n_groups = router_group_w.shape[-1]
    n_experts = router_expert_w.shape[-1]
    w_r = jnp.concatenate([router_group_w, router_expert_w], axis=1)
    w_r = jnp.pad(w_r, ((0, 0), (0, LANES - w_r.shape[1]))).astype(BF16)
    b_r = jnp.concatenate([router_group_b, router_expert_b])
    b_r = jnp.pad(b_r, (0, LANES - b_r.shape[0])).reshape(1, LANES)
    q_gain = (mem_q_g * dh ** -0.5).reshape(1, dh)

    full = lambda shape: pl.BlockSpec(shape, lambda b, s: (0,) * len(shape))
    row = lambda width: pl.BlockSpec((1, tm, width), lambda b, s: (b, s, 0))
    mem_spec = pl.BlockSpec((1, M, mw), lambda b, s: (b, 0, 0))
    return pl.pallas_call(
        functools.partial(_out_mem_kernel, n_heads=n_heads, dh=dh, n_groups=n_groups,
                          per_group=n_experts // n_groups),
        grid=(B, S // tm),
        in_specs=[row(D), row(lw), row(fw), full((1, fw)), full((lw, D)), full((fw, D)),
                  full((1, D)), full((D, mw)), full((1, dh)), mem_spec, mem_spec, full((mw, D)),
                  full((1, D)), full((D, LANES)), full((1, LANES))],
        out_specs=[row(D), pl.BlockSpec((tm * D // LANES, LANES), lambda b, s: (b * (S // tm) + s, 0)),
                   row(LANES), row(LANES)],
        out_shape=[jax.ShapeDtypeStruct((B, S, D), F32),
                   jax.ShapeDtypeStruct((B * S * D // LANES, LANES), F32),
                   jax.ShapeDtypeStruct((B, S, LANES), F32),
                   jax.ShapeDtypeStruct((B, S, LANES), jnp.int32)],
        compiler_params=pltpu.CompilerParams(
            dimension_semantics=("parallel", "parallel"), vmem_limit_bytes=VMEM_LIMIT),
        name="out_mem",
    )(x, y_lru, y_fox, fox_out_g.reshape(1, fw), w_out[:lw].astype(BF16), w_out[lw:].astype(BF16),
      norm_mem_x_g.reshape(1, D), mem_wq.astype(BF16), q_gain, k_mem, v_mem, mem_wo.astype(BF16),
      norm_ffn_g.reshape(1, D), w_r, b_r)


DMA_UNROLL = 8


def _scatter_rows_kernel(idx_ref, x_ref, buf_in_ref, buf_ref, sem, *, top_k, chunks):
    del buf_in_ref
    tokens = x_ref.shape[0] // chunks

    def issue(g, carry):
        for u in range(DMA_UNROLL):
            r = g * DMA_UNROLL + u
            src = x_ref.at[pl.ds(pl.multiple_of(r * chunks, chunks), chunks), :]
            for k in range(top_k):
                row = pl.multiple_of(idx_ref[0, 0, r * top_k + k], chunks)
                pltpu.make_async_copy(src, buf_ref.at[pl.ds(row, chunks), :], sem).start()
        return carry

    lax.fori_loop(0, tokens // DMA_UNROLL, issue, 0)
    for k in range(top_k):
        pltpu.make_async_copy(x_ref, buf_ref.at[pl.ds(0, tokens * chunks), :], sem).wait()


def _scatter_rows(x, dest, n_rows, *, tm, chunks):
    nt = dest.shape[0]
    top_k = dest.shape[2] // tm
    return pl.pallas_call(
        functools.partial(_scatter_rows_kernel, top_k=top_k, chunks=chunks),
        grid=(nt,),
        in_specs=[pl.BlockSpec((1, 1, tm * top_k), lambda i: (i, 0, 0), memory_space=pltpu.SMEM),
                  pl.BlockSpec((tm * chunks, LANES), lambda i: (i, 0)),
                  pl.BlockSpec(memory_space=pl.ANY)],
        out_specs=pl.BlockSpec(memory_space=pl.ANY),
        out_shape=jax.ShapeDtypeStruct((n_rows * chunks, LANES), x.dtype),
        scratch_shapes=[pltpu.SemaphoreType.DMA],
        input_output_aliases={2: 0},
        compiler_params=pltpu.CompilerParams(
            dimension_semantics=("arbitrary",), vmem_limit_bytes=VMEM_LIMIT),
        name="scatter_rows",
    )(dest, x, jnp.zeros((n_rows * chunks, LANES), x.dtype))


def _experts_kernel(be_ref, x_ref, wg_ref, wu_ref, wd_ref, o_ref, wgb_ref, wub_ref, wdb_ref, *, blk):
    i = pl.program_id(0)

    @pl.when((i == 0) | (be_ref[i] != be_ref[jnp.maximum(i - 1, 0)]))
    def _():
        wgb_ref[...] = wg_ref[0].astype(BF16)
        wub_ref[...] = wu_ref[0].astype(BF16)
        wdb_ref[...] = wd_ref[0].astype(BF16)

    d = wg_ref.shape[1]
    xb = _load_chunk_rows(x_ref, blk, d).astype(BF16)
    hdn = jax.nn.silu(_dot(xb, wgb_ref[...])) * _dot(xb, wub_ref[...])
    _store_chunk_rows(o_ref, _dot(hdn.astype(BF16), wdb_ref[...]))


def _experts(x_buf, blk_expert, w_gate, w_up, w_down, *, blk):
    _, D, de = w_gate.shape
    chunks = D // LANES
    n_blocks = x_buf.shape[0] // (blk * chunks)
    rows_spec = pl.BlockSpec((blk * chunks, LANES), lambda i, be: (i, 0))
    grid_spec = pltpu.PrefetchScalarGridSpec(
        num_scalar_prefetch=1,
        grid=(n_blocks,),
        in_specs=[rows_spec,
                  pl.BlockSpec((1, D, de), lambda i, be: (be[i], 0, 0)),
                  pl.BlockSpec((1, D, de), lambda i, be: (be[i], 0, 0)),
                  pl.BlockSpec((1, de, D), lambda i, be: (be[i], 0, 0))],
        out_specs=rows_spec,
        scratch_shapes=[pltpu.VMEM((D, de), BF16), pltpu.VMEM((D, de), BF16),
                        pltpu.VMEM((de, D), BF16)],
    )
    return pl.pallas_call(
        functools.partial(_experts_kernel, blk=blk),
        grid_spec=grid_spec,
        out_shape=jax.ShapeDtypeStruct(x_buf.shape, F32),
        compiler_params=pltpu.CompilerParams(
            dimension_semantics=("arbitrary",), vmem_limit_bytes=VMEM_LIMIT),
        name="experts",
    )(blk_expert, x_buf, w_gate, w_up, w_down)


def _combine_kernel(idx_ref, x_ref, g_ref, y_ref, o_ref, buf_ref, sem, *, top_k, chunks):
    tokens, d = x_ref.shape

    def issue(g, carry):
        for u in range(DMA_UNROLL):
            r = g * DMA_UNROLL + u
            for k in range(top_k):
                row = pl.multiple_of(idx_ref[0, 0, r * top_k + k], chunks)
                dst = buf_ref.at[k, pl.ds(pl.multiple_of(r * chunks, chunks), chunks), :]
                pltpu.make_async_copy(y_ref.at[pl.ds(row, chunks), :], dst, sem).start()
        return carry

    lax.fori_loop(0, tokens // DMA_UNROLL, issue, 0)
    for k in range(top_k):
        pltpu.make_async_copy(y_ref.at[pl.ds(0, tokens * chunks), :], buf_ref.at[k], sem).wait()
    out = x_ref[...]
    for k in range(top_k):
        out = out + g_ref[:, k:k + 1] * _load_chunk_rows(buf_ref.at[k], tokens, d)
    o_ref[...] = out


def _combine(x2, gates, y_buf, dest, *, tm):
    T, D = x2.shape
    chunks = D // LANES
    nt = dest.shape[0]
    top_k = dest.shape[2] // tm
    return pl.pallas_call(
        functools.partial(_combine_kernel, top_k=top_k, chunks=chunks),
        grid=(nt,),
        in_specs=[pl.BlockSpec((1, 1, tm * top_k), lambda i: (i, 0, 0), memory_space=pltpu.SMEM),
                  pl.BlockSpec((tm, D), lambda i: (i, 0)),
                  pl.BlockSpec((tm, LANES), lambda i: (i, 0)),
                  pl.BlockSpec(memory_space=pl.ANY)],
        out_specs=pl.BlockSpec((tm, D), lambda i: (i, 0)),
        out_shape=jax.ShapeDtypeStruct((T, D), F32),
        scratch_shapes=[pltpu.VMEM((top_k, tm * chunks, LANES), F32), pltpu.SemaphoreType.DMA],
        compiler_params=pltpu.CompilerParams(
            dimension_semantics=("arbitrary",), vmem_limit_bytes=VMEM_LIMIT),
        name="combine",
    )(dest, x2, gates, y_buf)


def _dispatch(experts, n_experts, blk):
    T, K = experts.shape
    A = T * K
    e_flat = experts.reshape(A)
    onehot = (e_flat[:, None] == jnp.arange(n_experts, dtype=jnp.int32)[None, :]).astype(jnp.int32)
    ranks = jnp.cumsum(onehot, axis=0)
    counts = ranks[-1]
    padded = (counts + blk - 1) // blk * blk
    pends = jnp.cumsum(padded)
    pstarts = pends - padded
    dest = jnp.sum(onehot * (ranks - 1 + pstarts[None, :]), axis=1)
    n_rows = A + n_experts * blk
    blk_start = jnp.arange(n_rows // blk, dtype=jnp.int32) * blk
    blk_expert = jnp.minimum(jnp.sum(blk_start[:, None] >= pends[None, :], axis=1), n_experts - 1)
    return dest.reshape(T, K), blk_expert.astype(jnp.int32), n_rows


def kernel(x, mem, norm_mix_g, w_in, b_forget, conv_w, conv_b, lru_wa, lru_ba, lru_wx, lru_bx,
           lru_a_param, fox_q_g, fox_k_g, lru_out_g, fox_out_g, w_out, norm_mem_x_g, norm_mem_g,
           mem_wq, mem_wkv, mem_q_g, mem_k_g, mem_wo, norm_ffn_g, router_group_w, router_group_b,
           router_expert_w, router_expert_b, exp_w_gate, exp_w_up, exp_w_down):
    B, S, D = x.shape
    depth = norm_mix_g.shape[0]
    lw = conv_w.shape[-1]
    n_heads = b_forget.shape[-1]
    dh = fox_q_g.shape[-1]
    mem_dh = mem_q_g.shape[-1]
    mem_heads = mem_wq.shape[-1] // mem_dh
    n_experts = router_expert_w.shape[-1]
    T = B * S
    tm = _pick_tile(S, 512)
    tq = _pick_tile(S, 512)
    blk = 256
    scatter_tm = _pick_tile(T, 1024)

    for l in range(depth):
        u, gate, q, k, v = _in_proj(x, norm_mix_g[l], w_in[l], b_forget[l], fox_q_g[l], fox_k_g[l],
                                    lw=lw, n_heads=n_heads, dh=dh, tm=tm)
        y_lru = _rg_lru(u, gate, conv_w[l], conv_b[l], lru_wa[l], lru_ba[l], lru_wx[l], lru_bx[l],
                        lru_a_param[l], lru_out_g[l], ts=tm)
        y_fox = _fox_attn(q, k, v, n_heads=n_heads, dh=dh, tq=tq)
        k_mem, v_mem = _mem_kv(mem, norm_mem_g[l], mem_wkv[l], mem_k_g[l], n_heads=mem_heads,
                               dh=mem_dh)
        x2, xn, gates, experts = _out_mem(
            x, y_lru, y_fox, fox_out_g[l], w_out[l], norm_mem_x_g[l], mem_wq[l], mem_q_g[l], k_mem,
            v_mem, mem_wo[l], norm_ffn_g[l], router_group_w[l], router_group_b[l],
            router_expert_w[l], router_expert_b[l], n_heads=mem_heads, dh=mem_dh, tm=tm)

        dest, blk_expert, n_rows = _dispatch(experts.reshape(T, LANES)[:, :TOP_K], n_experts, blk)
        chunks = D // LANES
        tiled = lambda t: (dest * chunks).reshape(T // t, 1, t * TOP_K)
        x_buf = _scatter_rows(xn, tiled(scatter_tm), n_rows, tm=scatter_tm, chunks=chunks)
        y_buf = _experts(x_buf, blk_expert, exp_w_gate[l], exp_w_up[l], exp_w_down[l], blk=blk)
        x = _combine(x2.reshape(T, D), gates.reshape(T, LANES), y_buf, tiled(tm), tm=tm)
        x = x.reshape(B, S, D)
    return x
```

```python
import functools

import jax
import jax.numpy as jnp
import numpy as np
from jax import lax
from jax.experimental import pallas as pl
from jax.experimental.pallas import tpu as pltpu

EPS = 1e-6
LRU_C = 8.0
CONV_WIDTH = 4
TOP_K = 2
LANES = 128
SUBLANES = 8
MXU_DIM = 256
HEAD_SLOT = LANES
VMEM_LIMIT = 56 * 1024 * 1024

F32 = jnp.float32
BF16 = jnp.bfloat16


def _dot(a, b):
    return jnp.dot(a, b, preferred_element_type=F32)


def _rms(x, g):
    return x * lax.rsqrt(jnp.mean(x * x, axis=-1, keepdims=True) + EPS) * g


def _softplus(x):
    return jnp.maximum(x, 0.0) + jnp.log1p(jnp.exp(-jnp.abs(x)))


def _pick_tile(n, target):
    t = min(n, target)
    while n % t:
        t //= 2
    return t


def _store_chunk_rows(ref, val):
    n, d = val.shape
    chunks = d // LANES
    for c in range(chunks):
        ref[pl.ds(c, n, stride=chunks), :] = val[:, c * LANES:(c + 1) * LANES]


def _load_chunk_rows(ref, n, d):
    chunks = d // LANES
    return jnp.concatenate([ref[pl.ds(c, n, stride=chunks), :] for c in range(chunks)], axis=1)


def _in_proj_kernel(x_ref, g_ref, w_ref, wf_ref, bf_ref, gq_ref, gk_ref, pe_ref, cv_ref,
                    u_ref, gate_ref, q_ref, k_ref, v_ref, carry_ref, *, lw, n_heads, dh):
    @pl.when(pl.program_id(1) == 0)
    def _():
        carry_ref[...] = jnp.zeros_like(carry_ref)

    hs = n_heads * HEAD_SLOT
    x = x_ref[0]
    tm = x.shape[0]
    hb = _rms(x, g_ref[...]).astype(BF16)

    u_ref[0] = _dot(hb, w_ref[:, 0:lw])
    gate_ref[0] = _dot(hb, w_ref[:, lw:2 * lw])

    z = _dot(hb, wf_ref[...]) + bf_ref[...]
    lane = lax.broadcasted_iota(jnp.int32, z.shape, 1)
    row = lax.broadcasted_iota(jnp.int32, z.shape, 0)
    c = jnp.where(lane < n_heads, -_softplus(-z), 0.0)
    d = 1
    while d < tm:
        c = c + jnp.where(row >= d, pltpu.roll(c, d, 0), 0.0)
        d *= 2
    c = c + carry_ref[...]
    carry_ref[...] = c[tm - 1:tm, :]
    c1 = c.astype(BF16).astype(F32)
    r1 = c - c1
    c2 = r1.astype(BF16).astype(F32)
    c3 = (r1 - c2).astype(BF16).astype(F32)
    e = c1 + pltpu.roll(c2, n_heads, 1) + pltpu.roll(c3, 2 * n_heads, 1)
    e = jnp.where(lane == 3 * n_heads, 1.0, e).astype(BF16)
    ext = _dot(e, pe_ref[...])

    fw = n_heads * dh
    lane_t = lax.broadcasted_iota(jnp.int32, (tm, LANES), 1)
    own = (lane_t < dh, lane_t >= dh)

    def to_slots(y):
        parts = []
        for c in range(fw // LANES):
            col = y[:, c * LANES:(c + 1) * LANES]
            parts += [jnp.where(own[0], col, 0.0), jnp.where(own[1], col, 0.0)]
        return jnp.concatenate(parts, axis=-1)

    def head_norm(y, gain):
        parts = []
        for h in range(n_heads):
            blk = y[:, h * HEAD_SLOT:(h + 1) * HEAD_SLOT]
            ss = jnp.sum(blk * blk, axis=-1, keepdims=True) * (1.0 / dh)
            parts.append(blk * lax.rsqrt(ss + EPS))
        return jnp.concatenate(parts, axis=-1) * gain

    q = to_slots(_dot(hb, w_ref[:, 2 * lw:2 * lw + fw]))
    q_ref[0] = (head_norm(q, gq_ref[...]) + ext[:, 0:hs]).astype(BF16)
    k = to_slots(_dot(hb, w_ref[:, 2 * lw + fw:2 * lw + 2 * fw]))
    k_ref[0] = (head_norm(k, gk_ref[...]) + ext[:, hs:2 * hs]).astype(BF16)
    v = to_slots(_dot(hb, w_ref[:, 2 * lw + 2 * fw:2 * lw + 3 * fw]))
    v_ref[0] = (v + cv_ref[...]).astype(BF16)


def _in_proj(x, norm_g, w_in, b_forget, fox_q_g, fox_k_g, *, lw, n_heads, dh, tm):
    B, S, D = x.shape
    fw = n_heads * dh
    hs = n_heads * HEAD_SLOT
    assert 2 * dh == HEAD_SLOT and fw % LANES == 0 and 3 * n_heads <= LANES
    n_main = 2 * lw + 3 * fw
    w_all = w_in[:, :n_main].astype(BF16)
    w_f = jnp.pad(w_in[:, n_main:], ((0, 0), (0, LANES - n_heads))).astype(BF16)
    b_f = jnp.pad(b_forget, (0, LANES - n_heads)).reshape(1, LANES)
    scale = dh ** -0.5

    def slot_gains(g):
        even = jnp.pad(g, (0, HEAD_SLOT - dh))
        odd = jnp.pad(g, (HEAD_SLOT - dh, 0))
        return jnp.tile(jnp.concatenate([even, odd]), n_heads // 2).reshape(1, hs)

    gq = slot_gains(fox_q_g * scale)
    gk = slot_gains(fox_k_g)
    ones_row = 3 * n_heads
    pe = np.zeros((LANES, 2 * hs), np.float32)
    cv = np.zeros((1, hs), np.float32)
    for h in range(n_heads):
        x0 = h * HEAD_SLOT + (dh if h % 2 == 0 else 0)
        for j in range(3):
            pe[j * n_heads + h, x0 + j] = 1.0
            pe[ones_row, x0 + 3 + j] = 1.0
            pe[ones_row, hs + x0 + j] = 1.0
            pe[j * n_heads + h, hs + x0 + 3 + j] = -1.0
        cv[0, x0] = 1.0
    pe = jnp.asarray(pe, BF16)

    full = lambda shape: pl.BlockSpec(shape, lambda b, s: (0,) * len(shape))
    row = lambda width: pl.BlockSpec((1, tm, width), lambda b, s: (b, s, 0))
    return pl.pallas_call(
        functools.partial(_in_proj_kernel, lw=lw, n_heads=n_heads, dh=dh),
        grid=(B, S // tm),
        in_specs=[row(D), full((1, D)), full(w_all.shape), full(w_f.shape), full((1, LANES)),
                  full((1, hs)), full((1, hs)), full(pe.shape), full((1, hs))],
        out_specs=[row(lw), row(lw), row(hs), row(hs), row(hs)],
        out_shape=[jax.ShapeDtypeStruct((B, S, lw), F32), jax.ShapeDtypeStruct((B, S, lw), F32),
                   jax.ShapeDtypeStruct((B, S, hs), BF16), jax.ShapeDtypeStruct((B, S, hs), BF16),
                   jax.ShapeDtypeStruct((B, S, hs), BF16)],
        scratch_shapes=[pltpu.VMEM((1, LANES), F32)],
        compiler_params=pltpu.CompilerParams(
            dimension_semantics=("parallel", "arbitrary"), vmem_limit_bytes=VMEM_LIMIT),
        name="in_proj",
    )(x, norm_g.reshape(1, D), w_all, w_f, b_f, gq, gk, pe, jnp.asarray(cv))


def _rg_lru_kernel(u_ref, gate_ref, cw_ref, cb_ref, w_ref, ba_ref, bx_ref, ap_ref, og_ref,
                   o_ref, tail_ref, h_ref):
    @pl.when(pl.program_id(1) == 0)
    def _():
        tail_ref[...] = jnp.zeros_like(tail_ref)
        h_ref[...] = jnp.zeros_like(h_ref)

    u = u_ref[0]
    ts, lw = u.shape
    ext = jnp.concatenate([tail_ref[...], u], axis=0)
    tail_ref[...] = u[ts - SUBLANES:, :]
    xc = cb_ref[...] + u * cw_ref[CONV_WIDTH - 1:CONV_WIDTH, :]
    for back in range(1, CONV_WIDTH):
        tap = CONV_WIDTH - 1 - back
        xc = xc + pltpu.roll(ext, back, 0)[SUBLANES:, :] * cw_ref[tap:tap + 1, :]

    xb = xc.astype(BF16)
    r_parts, i_parts = [], []
    for j in range(lw // MXU_DIM):
        y = _dot(xb[:, j * MXU_DIM:(j + 1) * MXU_DIM], w_ref[j])
        r_parts.append(y[:, :MXU_DIM])
        i_parts.append(y[:, MXU_DIM:])
    r = jax.nn.sigmoid(jnp.concatenate(r_parts, axis=-1) + ba_ref[...])
    i = jax.nn.sigmoid(jnp.concatenate(i_parts, axis=-1) + bx_ref[...])
    log_a = (-LRU_C) * r * _softplus(-ap_ref[...])
    a = jnp.exp(log_a)
    mult = jnp.sqrt(jnp.maximum(-jnp.tanh(log_a) * (a * a + 1.0), 0.0))
    b = mult * (i * xc)

    groups = ts // SUBLANES
    a = a.reshape(groups, SUBLANES, lw)
    b = b.reshape(groups, SUBLANES, lw)
    row = lax.broadcasted_iota(jnp.int32, a.shape, 1)
    d = 1
    while d < SUBLANES:
        keep = row >= d
        a_prev = jnp.where(keep, pltpu.roll(a, d, 1), 1.0)
        b_prev = jnp.where(keep, pltpu.roll(b, d, 1), 0.0)
        b = a * b_prev + b
        a = a * a_prev
        d *= 2
    state = h_ref[...]
    parts = []
    for g in range(groups):
        hg = b[g] + a[g] * state
        state = hg[SUBLANES - 1:, :]
        parts.append(hg)
    h = jnp.concatenate(parts, axis=0)
    h_ref[...] = state

    y = h * jax.nn.gelu(gate_ref[0])
    o_ref[0] = _rms(y, og_ref[...]).astype(o_ref.dtype)


def _rg_lru(u, gate, conv_w, conv_b, wa, ba, wx, bx, a_param, out_g, *, ts):
    B, S, lw = u.shape
    nb, bd, _ = wa.shape
    per = MXU_DIM // bd
    n_tiles = lw // MXU_DIM

    def tiles(w):
        w = w.reshape(n_tiles, per, bd, bd)
        eye = jnp.eye(per, dtype=w.dtype)
        return jnp.einsum('tpij,pq->tpiqj', w, eye).reshape(n_tiles, MXU_DIM, MXU_DIM)

    w_cat = jnp.concatenate([tiles(wa), tiles(wx)], axis=-1).astype(BF16)
    vec = lambda a: a.reshape(1, lw)
    full = lambda shape: pl.BlockSpec(shape, lambda b, s: (0,) * len(shape))
    row = pl.BlockSpec((1, ts, lw), lambda b, s: (b, s, 0))
    return pl.pallas_call(
        _rg_lru_kernel,
        grid=(B, S // ts),
        in_specs=[row, row, full((CONV_WIDTH, lw)), full((1, lw)), full(w_cat.shape), full((1, lw)),
                  full((1, lw)), full((1, lw)), full((1, lw))],
        out_specs=row,
        out_shape=jax.ShapeDtypeStruct((B, S, lw), BF16),
        scratch_shapes=[pltpu.VMEM((SUBLANES, lw), F32), pltpu.VMEM((1, lw), F32)],
        compiler_params=pltpu.CompilerParams(
            dimension_semantics=("parallel", "arbitrary"), vmem_limit_bytes=VMEM_LIMIT),
        name="rg_lru",
    )(u, gate, conv_w, vec(conv_b), w_cat, vec(ba), vec(bx), vec(a_param), vec(out_g))


OVERFLOW_GUARD = 1e30


def _fox_attn_kernel(q_ref, k_ref, v_ref, o_ref, vt_ref, m_ref, acc_ref, *, dh, tq):
    qi = pl.program_id(2)
    n_chunks = vt_ref.shape[0]

    @pl.when(qi == 0)
    def _():
        for c in range(n_chunks):
            vt_ref[c] = v_ref[0, c * tq:(c + 1) * tq, :].astype(F32).T.astype(BF16)

    heads = [slice(hh * HEAD_SLOT, (hh + 1) * HEAD_SLOT) for hh in range(2)]

    def scores(j, q, sl, masked):
        start = pl.multiple_of(j * tq, tq)
        st = lax.dot_general(k_ref[0, pl.ds(start, tq), sl], q, (((1,), (1,)), ((), ())),
                             preferred_element_type=F32)
        if masked:
            kpos = lax.broadcasted_iota(jnp.int32, st.shape, 0)
            qpos = lax.broadcasted_iota(jnp.int32, st.shape, 1)
            st = jnp.where(kpos <= qpos, st, -jnp.inf)
        return st

    for hh, sl in enumerate(heads):
        st = scores(qi, q_ref[0, :, sl], sl, True)
        m = jnp.max(st, axis=0, keepdims=True)
        m_ref[hh] = m
        acc_ref[hh] = _dot(vt_ref[qi, sl, :], jnp.exp(st - m).astype(BF16))

    def fast_chunk(j):
        for hh, sl in enumerate(heads):
            pt = jnp.exp(scores(j, q_ref[0, :, sl], sl, False) - m_ref[hh]).astype(BF16)
            acc_ref[hh] += _dot(vt_ref[j, sl, :], pt)

    def fast_pair(i, carry):
        fast_chunk(2 * i)
        fast_chunk(2 * i + 1)
        return carry

    lax.fori_loop(0, qi // 2, fast_pair, 0)

    @pl.when(qi % 2 == 1)
    def _():
        fast_chunk(qi - 1)

    def softmax_sums():
        return jnp.concatenate([acc_ref[0][dh:dh + 1, :], acc_ref[1][0:1, :]], axis=0)

    def normalised():
        sums = softmax_sums()
        return jnp.concatenate([acc_ref[0][0:dh, :] / sums[0:1, :],
                                acc_ref[1][dh:2 * dh, :] / sums[1:2, :]], axis=0)

    out = normalised()
    o_ref[0] = out.T.astype(o_ref.dtype)
    sums = softmax_sums()
    flag = lambda x: jnp.max(jnp.where(jnp.abs(x) < OVERFLOW_GUARD, 0.0, 1.0),
                             axis=0, keepdims=True)
    bad = jnp.max(jnp.maximum(flag(out), flag(sums)), axis=1, keepdims=True)

    @pl.when(bad[0, 0] > 0.0)
    def _():
        m_ref[...] = jnp.full_like(m_ref, -jnp.inf)
        acc_ref[...] = jnp.zeros_like(acc_ref)

        def chunk(j, masked):
            for hh, sl in enumerate(heads):
                st = scores(j, q_ref[0, :, sl], sl, masked)
                m_old = m_ref[hh]
                m_new = jnp.maximum(m_old, jnp.max(st, axis=0, keepdims=True))
                pt = jnp.exp(st - m_new).astype(BF16)
                acc_ref[hh] = jnp.exp(m_old - m_new) * acc_ref[hh] + _dot(vt_ref[j, sl, :], pt)
                m_ref[hh] = m_new

        def body(j, carry):
            chunk(j, False)
            return carry

        lax.fori_loop(0, qi, body, 0)
        chunk(qi, True)
        o_ref[0] = normalised().T.astype(o_ref.dtype)


def _fox_attn(q, k, v, *, n_heads, dh, tq):
    B, S, _ = q.shape
    nq = S // tq
    pair = 2 * HEAD_SLOT
    kv_spec = pl.BlockSpec((1, S, pair), lambda b, h, i: (b, 0, h))
    return pl.pallas_call(
        functools.partial(_fox_attn_kernel, dh=dh, tq=tq),
        grid=(B, n_heads // 2, nq),
        in_specs=[pl.BlockSpec((1, tq, pair), lambda b, h, i: (b, i, h)), kv_spec, kv_spec],
        out_specs=pl.BlockSpec((1, tq, 2 * dh), lambda b, h, i: (b, i, h)),
        out_shape=jax.ShapeDtypeStruct((B, S, n_heads * dh), F32),
        scratch_shapes=[pltpu.VMEM((nq, pair, tq), BF16), pltpu.VMEM((2, 1, tq), F32),
                        pltpu.VMEM((2, HEAD_SLOT, tq), F32)],
        compiler_params=pltpu.CompilerParams(
            dimension_semantics=("parallel", "parallel", "arbitrary"),
            vmem_limit_bytes=VMEM_LIMIT),
        name="fox_attn",
    )(q, k, v)


def _mem_kv_kernel(m_ref, g_ref, w_ref, kg_ref, k_ref, v_ref, *, n_heads, dh):
    mw = n_heads * dh
    mb = _rms(m_ref[0], g_ref[...]).astype(BF16)
    kv = _dot(mb, w_ref[...])
    parts = []
    for h in range(n_heads):
        parts.append(_rms(kv[:, h * dh:(h + 1) * dh], kg_ref[...]))
    k_ref[0] = jnp.concatenate(parts, axis=-1).astype(BF16)
    v_ref[0] = kv[:, mw:].astype(BF16)


def _mem_kv(mem, norm_g, wkv, k_g, *, n_heads, dh):
    B, M, D = mem.shape
    mw = n_heads * dh
    full = lambda shape: pl.BlockSpec(shape, lambda b: (0,) * len(shape))
    out = pl.BlockSpec((1, M, mw), lambda b: (b, 0, 0))
    return pl.pallas_call(
        functools.partial(_mem_kv_kernel, n_heads=n_heads, dh=dh),
        grid=(B,),
        in_specs=[pl.BlockSpec((1, M, D), lambda b: (b, 0, 0)), full((1, D)), full((D, 2 * mw)),
                  full((1, dh))],
        out_specs=[out, out],
        out_shape=[jax.ShapeDtypeStruct((B, M, mw), BF16)] * 2,
        compiler_params=pltpu.CompilerParams(
            dimension_semantics=("parallel",), vmem_limit_bytes=VMEM_LIMIT),
        name="mem_kv",
    )(mem, norm_g.reshape(1, D), wkv.astype(BF16), k_g.reshape(1, dh))


def _out_mem_kernel(x_ref, yl_ref, yf_ref, fg_ref, wol_ref, wof_ref, gx_ref, wq_ref, qg_ref,
                    km_ref, vm_ref, wo_ref, gf_ref, wr_ref, br_ref,
                    x2_ref, xn_ref, gates_ref, experts_ref, *, n_heads, dh, n_groups, per_group):
    yf = _rms(yf_ref[0], fg_ref[...]).astype(BF16)
    x1 = x_ref[0] + _dot(yl_ref[0], wol_ref[...]) + _dot(yf, wof_ref[...])

    q = _dot(_rms(x1, gx_ref[...]).astype(BF16), wq_ref[...])
    outs = []
    for h in range(n_heads):
        sl = slice(h * dh, (h + 1) * dh)
        qh = _rms(q[:, sl], qg_ref[...]).astype(BF16)
        s = lax.dot_general(qh, km_ref[0, :, sl], (((1,), (1,)), ((), ())),
                            preferred_element_type=F32)
        p = jnp.exp(s - jnp.max(s, axis=-1, keepdims=True))
        p = p / jnp.sum(p, axis=-1, keepdims=True)
        outs.append(_dot(p.astype(BF16), vm_ref[0, :, sl]))
    x2 = x1 + _dot(jnp.concatenate(outs, axis=-1).astype(BF16), wo_ref[...])
    x2_ref[0] = x2

    xn = _rms(x2, gf_ref[...])
    _store_chunk_rows(xn_ref, xn)
    logits = _dot(xn.astype(BF16), wr_ref[...]) + br_ref[...]
    lane = lax.broadcasted_iota(jnp.int32, logits.shape, 1).astype(F32)
    neg = -jnp.inf

    def top(vals):
        mx = jnp.max(vals, axis=-1, keepdims=True)
        idx = jnp.min(jnp.where(vals == mx, lane, float(LANES)), axis=-1, keepdims=True)
        return mx, idx

    gl = jnp.where(lane < n_groups, logits, neg)
    g_max, g_idx = top(gl)
    g_w = 1.0 / jnp.sum(jnp.exp(gl - g_max), axis=-1, keepdims=True)
    lo = n_groups + per_group * g_idx
    el = jnp.where((lane >= lo) & (lane < lo + per_group), logits, neg)
    e1, i1 = top(el)
    e2, i2 = top(jnp.where(lane == i1, neg, el))
    t = jnp.exp(e2 - e1)
    w1 = g_w / (1.0 + t)
    w2 = g_w * t / (1.0 + t)
    gates_ref[0] = jnp.where(lane == 0, w1, jnp.where(lane == 1, w2, 0.0))
    experts_ref[0] = jnp.where(lane == 0, i1 - n_groups,
                               jnp.where(lane == 1, i2 - n_groups, 0.0)).astype(jnp.int32)


def _out_mem(x, y_lru, y_fox, fox_out_g, w_out, norm_mem_x_g, mem_wq, mem_q_g, k_mem, v_mem, mem_wo,
             norm_ffn_g, router_group_w, router_group_b, router_expert_w, router_expert_b, *,
             n_heads, dh, tm):
    B, S, D = x.shape
    lw = y_lru.shape[-1]
    fw = y_fox.shape[-1]
    M = k_mem.shape[1]
    mw = n_heads * dh
    n_groups = router_group_w.shape[-1]
    n_experts = router_expert_w.shape[-1]
    w_r = jnp.concatenate([router_group_w, router_expert_w], axis=1)
    w_r = jnp.pad(w_r, ((0, 0), (0, LANES - w_r.shape[1]))).astype(BF16)
    b_r = jnp.concatenate([router_group_b, router_expert_b])
    b_r = jnp.pad(b_r, (0, LANES - b_r.shape[0])).reshape(1, LANES)
    q_gain = (mem_q_g * dh ** -0.5).reshape(1, dh)

    full = lambda shape: pl.BlockSpec(shape, lambda b, s: (0,) * len(shape))
    row = lambda width: pl.BlockSpec((1, tm, width), lambda b, s: (b, s, 0))
    mem_spec = pl.BlockSpec((1, M, mw), lambda b, s: (b, 0, 0))
    return pl.pallas_call(
        functools.partial(_out_mem_kernel, n_heads=n_heads, dh=dh, n_groups=n_groups,
                          per_group=n_experts // n_groups),
        grid=(B, S // tm),
        in_specs=[row(D), row(lw), row(fw), full((1, fw)), full((lw, D)), full((fw, D)),
                  full((1, D)), full((D, mw)), full((1, dh)), mem_spec, mem_spec, full((mw, D)),
                  full((1, D)), full((D, LANES)), full((1, LANES))],
        out_specs=[row(D), pl.BlockSpec((tm * D // LANES, LANES), lambda b, s: (b * (S // tm) + s, 0)),
                   row(LANES), row(LANES)],
        out_shape=[jax.ShapeDtypeStruct((B, S, D), F32),
                   jax.ShapeDtypeStruct((B * S * D // LANES, LANES), F32),
                   jax.ShapeDtypeStruct((B, S, LANES), F32),
                   jax.ShapeDtypeStruct((B, S, LANES), jnp.int32)],
        compiler_params=pltpu.CompilerParams(
            dimension_semantics=("parallel", "parallel"), vmem_limit_bytes=VMEM_LIMIT),
        name="out_mem",
    )(x, y_lru, y_fox, fox_out_g.reshape(1, fw), w_out[:lw].astype(BF16), w_out[lw:].astype(BF16),
      norm_mem_x_g.reshape(1, D), mem_wq.astype(BF16), q_gain, k_mem, v_mem, mem_wo.astype(BF16),
      norm_ffn_g.reshape(1, D), w_r, b_r)


DMA_UNROLL = 8


def _experts_kernel(be_ref, nv_ref, src_ref, src_next_ref, dst_ref, xn_ref, wg_ref, wu_ref, wd_ref,
                    ys_ref, xbuf_ref, ybuf_ref, wgb_ref, wub_ref, wdb_ref, gsem, ssem, *, blk, chunks):
    i = pl.program_id(0)
    n = pl.num_programs(0)
    slot = lax.rem(i, 2)
    row_at = lambda r: pl.ds(pl.multiple_of(r * chunks, chunks), chunks)

    def request_rows(idx_ref, s):
        def body(g, carry):
            for u in range(DMA_UNROLL):
                r = g * DMA_UNROLL + u
                row = pl.multiple_of(idx_ref[0, 0, r], chunks)
                pltpu.make_async_copy(xn_ref.at[pl.ds(row, chunks), :], xbuf_ref.at[s, row_at(r), :],
                                      gsem.at[s]).start()
            return carry
        lax.fori_loop(0, blk // DMA_UNROLL, body, 0)

    def send_row(r, s):
        row = pl.multiple_of(dst_ref[0, 0, r], chunks)
        pltpu.make_async_copy(ybuf_ref.at[s, row_at(r), :], ys_ref.at[pl.ds(row, chunks), :],
                              ssem.at[s]).start()

    def wait_sent(count, s):
        @pl.when(count > 0)
        def _():
            size = pl.multiple_of(count * chunks, chunks)
            pltpu.make_async_copy(ybuf_ref.at[s, pl.ds(0, size), :], ys_ref.at[pl.ds(0, size), :],
                                  ssem.at[s]).wait()

    @pl.when(i == 0)
    def _():
        request_rows(src_ref, 0)

    @pl.when(i + 1 < n)
    def _():
        request_rows(src_next_ref, 1 - slot)

    @pl.when((i == 0) | (be_ref[i] != be_ref[jnp.maximum(i - 1, 0)]))
    def _():
        wgb_ref[...] = wg_ref[0].astype(BF16)
        wub_ref[...] = wu_ref[0].astype(BF16)
        wdb_ref[...] = wd_ref[0].astype(BF16)

    pltpu.make_async_copy(xn_ref.at[pl.ds(0, blk * chunks), :], xbuf_ref.at[slot], gsem.at[slot]).wait()

    @pl.when(i >= 2)
    def _():
        wait_sent(nv_ref[jnp.maximum(i - 2, 0)], slot)

    d = wg_ref.shape[1]
    xb = _load_chunk_rows(xbuf_ref.at[slot], blk, d).astype(BF16)
    hdn = jax.nn.silu(_dot(xb, wgb_ref[...])) * _dot(xb, wub_ref[...])
    _store_chunk_rows(ybuf_ref.at[slot], _dot(hdn.astype(BF16), wdb_ref[...]))

    nv = nv_ref[i]

    def send_group(g, carry):
        for u in range(DMA_UNROLL):
            send_row(g * DMA_UNROLL + u, slot)
        return carry

    def send_one(r, carry):
        send_row(r, slot)
        return carry

    full_groups = nv // DMA_UNROLL
    lax.fori_loop(0, full_groups, send_group, 0)
    lax.fori_loop(full_groups * DMA_UNROLL, nv, send_one, 0)

    @pl.when(i == n - 1)
    def _():
        wait_sent(nv, slot)

        @pl.when(i >= 1)
        def _():
            wait_sent(nv_ref[jnp.maximum(i - 1, 0)], 1 - slot)


def _experts(xn, src_rows, dst_rows, blk_expert, n_valid, w_gate, w_up, w_down, *, blk, n_out_rows):
    _, D, de = w_gate.shape
    chunks = D // LANES
    n_blocks = src_rows.shape[0]
    idx_spec = lambda step: pl.BlockSpec(
        (1, 1, blk), lambda i, be, nv: (jnp.minimum(i + step, n_blocks - 1), 0, 0),
        memory_space=pltpu.SMEM)
    grid_spec = pltpu.PrefetchScalarGridSpec(
        num_scalar_prefetch=2,
        grid=(n_blocks,),
        in_specs=[idx_spec(0), idx_spec(1), idx_spec(0),
                  pl.BlockSpec(memory_space=pl.ANY),
                  pl.BlockSpec((1, D, de), lambda i, be, nv: (be[i], 0, 0)),
                  pl.BlockSpec((1, D, de), lambda i, be, nv: (be[i], 0, 0)),
                  pl.BlockSpec((1, de, D), lambda i, be, nv: (be[i], 0, 0))],
        out_specs=pl.BlockSpec(memory_space=pl.ANY),
        scratch_shapes=[pltpu.VMEM((2, blk * chunks, LANES), F32),
                        pltpu.VMEM((2, blk * chunks, LANES), F32),
                        pltpu.VMEM((D, de), BF16), pltpu.VMEM((D, de), BF16),
                        pltpu.VMEM((de, D), BF16),
                        pltpu.SemaphoreType.DMA((2,)), pltpu.SemaphoreType.DMA((2,))],
    )
    return pl.pallas_call(
        functools.partial(_experts_kernel, blk=blk, chunks=chunks),
        grid_spec=grid_spec,
        out_shape=jax.ShapeDtypeStruct((n_out_rows * chunks, LANES), F32),
        compiler_params=pltpu.CompilerParams(
            dimension_semantics=("arbitrary",), vmem_limit_bytes=VMEM_LIMIT),
        name="experts",
    )(blk_expert, n_valid, src_rows, src_rows, dst_rows, xn, w_gate, w_up, w_down)


def _combine_kernel(x_ref, g_ref, *refs, chunks):
    *y_refs, o_ref = refs
    tokens, d = x_ref.shape
    out = x_ref[...]
    for k, y_ref in enumerate(y_refs):
        out = out + g_ref[:, k:k + 1] * _load_chunk_rows(y_ref, tokens, d)
    o_ref[...] = out


def _combine(x2, gates, ys, *, top_k, tm):
    T, D = x2.shape
    chunks = D // LANES
    nt = T // tm
    y_spec = lambda k: pl.BlockSpec((tm * chunks, LANES), lambda i: (k * nt + i, 0))
    return pl.pallas_call(
        functools.partial(_combine_kernel, chunks=chunks),
        grid=(nt,),
        in_specs=[pl.BlockSpec((tm, D), lambda i: (i, 0)),
                  pl.BlockSpec((tm, LANES), lambda i: (i, 0))] + [y_spec(k) for k in range(top_k)],
        out_specs=pl.BlockSpec((tm, D), lambda i: (i, 0)),
        out_shape=jax.ShapeDtypeStruct((T, D), F32),
        compiler_params=pltpu.CompilerParams(
            dimension_semantics=("parallel",), vmem_limit_bytes=VMEM_LIMIT),
        name="combine",
    )(x2, gates, *([ys] * top_k))


def _dispatch(experts, n_experts, blk, chunks):
    T, K = experts.shape
    A = T * K
    i32 = jnp.int32
    e_flat = experts.reshape(A)
    sorted_e, order = lax.sort((e_flat, jnp.arange(A, dtype=i32)), num_keys=1, is_stable=True)
    bounds = jnp.searchsorted(sorted_e, jnp.arange(n_experts + 1, dtype=i32)).astype(i32)
    starts, counts = bounds[:-1], bounds[1:] - bounds[:-1]
    padded = (counts + blk - 1) // blk * blk
    pends = jnp.cumsum(padded)
    pstarts = pends - padded
    n_blocks = (A + n_experts * blk) // blk
    blk_start = jnp.arange(n_blocks, dtype=i32) * blk
    blk_expert = jnp.minimum(jnp.sum(blk_start[:, None] >= pends[None, :], axis=1),
                             n_experts - 1).astype(i32)
    n_valid = jnp.clip(counts[blk_expert] - (blk_start - pstarts[blk_expert]), 0, blk).astype(i32)
    r = jnp.arange(blk, dtype=i32)[None, :]
    valid = r < n_valid[:, None]
    seg_row = (blk_start - pstarts[blk_expert])[:, None] + r
    a = order[jnp.where(valid, starts[blk_expert][:, None] + seg_row, 0)]
    tok, slot = a // K, a % K
    src_rows = (tok * chunks).reshape(n_blocks, 1, blk)
    dst_rows = ((slot * T + tok) * chunks).reshape(n_blocks, 1, blk)
    return src_rows, dst_rows, blk_expert, n_valid


def kernel(x, mem, norm_mix_g, w_in, b_forget, conv_w, conv_b, lru_wa, lru_ba, lru_wx, lru_bx,
           lru_a_param, fox_q_g, fox_k_g, lru_out_g, fox_out_g, w_out, norm_mem_x_g, norm_mem_g,
           mem_wq, mem_wkv, mem_q_g, mem_k_g, mem_wo, norm_ffn_g, router_group_w, router_group_b,
           router_expert_w, router_expert_b, exp_w_gate, exp_w_up, exp_w_down):
    B, S, D = x.shape
    depth = norm_mix_g.shape[0]
    lw = conv_w.shape[-1]
    n_heads = b_forget.shape[-1]
    dh = fox_q_g.shape[-1]
    mem_dh = mem_q_g.shape[-1]
    mem_heads = mem_wq.shape[-1] // mem_dh
    n_experts = router_expert_w.shape[-1]
    T = B * S
    tm = _pick_tile(S, 512)
    tq = _pick_tile(S, 512)
    blk = 256

    for l in range(depth):
        u, gate, q, k, v = _in_proj(x, norm_mix_g[l], w_in[l], b_forget[l], fox_q_g[l], fox_k_g[l],
                                    lw=lw, n_heads=n_heads, dh=dh, tm=tm)
        y_lru = _rg_lru(u, gate, conv_w[l], conv_b[l], lru_wa[l], lru_ba[l], lru_wx[l], lru_bx[l],
                        lru_a_param[l], lru_out_g[l], ts=tm)
        y_fox = _fox_attn(q, k, v, n_heads=n_heads, dh=dh, tq=tq)
        k_mem, v_mem = _mem_kv(mem, norm_mem_g[l], mem_wkv[l], mem_k_g[l], n_heads=mem_heads,
                               dh=mem_dh)
        x2, xn, gates, experts = _out_mem(
            x, y_lru, y_fox, fox_out_g[l], w_out[l], norm_mem_x_g[l], mem_wq[l], mem_q_g[l], k_mem,
            v_mem, mem_wo[l], norm_ffn_g[l], router_group_w[l], router_group_b[l],
            router_expert_w[l], router_expert_b[l], n_heads=mem_heads, dh=mem_dh, tm=tm)

        src_rows, dst_rows, blk_expert, n_valid = _dispatch(
            experts.reshape(T, LANES)[:, :TOP_K], n_experts, blk, D // LANES)
        ys = _experts(xn, src_rows, dst_rows, blk_expert, n_valid, exp_w_gate[l], exp_w_up[l],
                      exp_w_down[l], blk=blk, n_out_rows=TOP_K * T)
        x = _combine(x2.reshape(T, D), gates.reshape(T, LANES), ys, top_k=TOP_K, tm=tm)
        x = x.reshape(B, S, D)
    return x
```

```python
import functools

import jax
import jax.numpy as jnp
import numpy as np
from jax import lax
from jax.experimental import pallas as pl
from jax.experimental.pallas import tpu as pltpu

EPS = 1e-6
LRU_C = 8.0
CONV_WIDTH = 4
TOP_K = 2
LANES = 128
SUBLANES = 8
MXU_DIM = 256
HEAD_SLOT = LANES
VMEM_LIMIT = 56 * 1024 * 1024

F32 = jnp.float32
BF16 = jnp.bfloat16


def _dot(a, b):
    return jnp.dot(a, b, preferred_element_type=F32)


def _rms(x, g):
    return x * lax.rsqrt(jnp.mean(x * x, axis=-1, keepdims=True) + EPS) * g


def _softplus(x):
    return jnp.maximum(x, 0.0) + jnp.log1p(jnp.exp(-jnp.abs(x)))


def _pick_tile(n, target):
    t = min(n, target)
    while n % t:
        t //= 2
    return t


def _store_chunk_rows(ref, val):
    n, d = val.shape
    chunks = d // LANES
    for c in range(chunks):
        ref[pl.ds(c, n, stride=chunks), :] = val[:, c * LANES:(c + 1) * LANES]


def _load_chunk_rows(ref, n, d):
    chunks = d // LANES
    return jnp.concatenate([ref[pl.ds(c, n, stride=chunks), :] for c in range(chunks)], axis=1)


def _in_proj_kernel(x_ref, g_ref, w_ref, wf_ref, bf_ref, gq_ref, gk_ref, pe_ref, cv_ref,
                    u_ref, gate_ref, q_ref, k_ref, v_ref, carry_ref, *, lw, n_heads, dh):
    @pl.when(pl.program_id(1) == 0)
    def _():
        carry_ref[...] = jnp.zeros_like(carry_ref)

    hs = n_heads * HEAD_SLOT
    x = x_ref[0]
    tm = x.shape[0]
    hb = _rms(x, g_ref[...]).astype(BF16)

    u_ref[0] = _dot(hb, w_ref[:, 0:lw])
    gate_ref[0] = _dot(hb, w_ref[:, lw:2 * lw])

    z = _dot(hb, wf_ref[...]) + bf_ref[...]
    lane = lax.broadcasted_iota(jnp.int32, z.shape, 1)
    row = lax.broadcasted_iota(jnp.int32, z.shape, 0)
    c = jnp.where(lane < n_heads, -_softplus(-z), 0.0)
    d = 1
    while d < tm:
        c = c + jnp.where(row >= d, pltpu.roll(c, d, 0), 0.0)
        d *= 2
    c = c + carry_ref[...]
    carry_ref[...] = c[tm - 1:tm, :]
    c1 = c.astype(BF16).astype(F32)
    r1 = c - c1
    c2 = r1.astype(BF16).astype(F32)
    c3 = (r1 - c2).astype(BF16).astype(F32)
    e = c1 + pltpu.roll(c2, n_heads, 1) + pltpu.roll(c3, 2 * n_heads, 1)
    e = jnp.where(lane == 3 * n_heads, 1.0, e).astype(BF16)
    ext = _dot(e, pe_ref[...])

    fw = n_heads * dh
    lane_t = lax.broadcasted_iota(jnp.int32, (tm, LANES), 1)
    own = (lane_t < dh, lane_t >= dh)

    def to_slots(y):
        parts = []
        for c in range(fw // LANES):
            col = y[:, c * LANES:(c + 1) * LANES]
            parts += [jnp.where(own[0], col, 0.0), jnp.where(own[1], col, 0.0)]
        return jnp.concatenate(parts, axis=-1)

    def head_norm(y, gain):
        parts = []
        for h in range(n_heads):
            blk = y[:, h * HEAD_SLOT:(h + 1) * HEAD_SLOT]
            ss = jnp.sum(blk * blk, axis=-1, keepdims=True) * (1.0 / dh)
            parts.append(blk * lax.rsqrt(ss + EPS))
        return jnp.concatenate(parts, axis=-1) * gain

    q = to_slots(_dot(hb, w_ref[:, 2 * lw:2 * lw + fw]))
    q_ref[0] = (head_norm(q, gq_ref[...]) + ext[:, 0:hs]).astype(BF16)
    k = to_slots(_dot(hb, w_ref[:, 2 * lw + fw:2 * lw + 2 * fw]))
    k_ref[0] = (head_norm(k, gk_ref[...]) + ext[:, hs:2 * hs]).astype(BF16)
    v = to_slots(_dot(hb, w_ref[:, 2 * lw + 2 * fw:2 * lw + 3 * fw]))
    v_ref[0] = (v + cv_ref[...]).astype(BF16)


def _in_proj(x, norm_g, w_in, b_forget, fox_q_g, fox_k_g, *, lw, n_heads, dh, tm):
    B, S, D = x.shape
    fw = n_heads * dh
    hs = n_heads * HEAD_SLOT
    assert 2 * dh == HEAD_SLOT and fw % LANES == 0 and 3 * n_heads <= LANES
    n_main = 2 * lw + 3 * fw
    w_all = w_in[:, :n_main].astype(BF16)
    w_f = jnp.pad(w_in[:, n_main:], ((0, 0), (0, LANES - n_heads))).astype(BF16)
    b_f = jnp.pad(b_forget, (0, LANES - n_heads)).reshape(1, LANES)
    scale = dh ** -0.5

    def slot_gains(g):
        even = jnp.pad(g, (0, HEAD_SLOT - dh))
        odd = jnp.pad(g, (HEAD_SLOT - dh, 0))
        return jnp.tile(jnp.concatenate([even, odd]), n_heads // 2).reshape(1, hs)

    gq = slot_gains(fox_q_g * scale)
    gk = slot_gains(fox_k_g)
    ones_row = 3 * n_heads
    pe = np.zeros((LANES, 2 * hs), np.float32)
    cv = np.zeros((1, hs), np.float32)
    for h in range(n_heads):
        x0 = h * HEAD_SLOT + (dh if h % 2 == 0 else 0)
        for j in range(3):
            pe[j * n_heads + h, x0 + j] = 1.0
            pe[ones_row, x0 + 3 + j] = 1.0
            pe[ones_row, hs + x0 + j] = 1.0
            pe[j * n_heads + h, hs + x0 + 3 + j] = -1.0
        cv[0, x0] = 1.0
    pe = jnp.asarray(pe, BF16)

    full = lambda shape: pl.BlockSpec(shape, lambda b, s: (0,) * len(shape))
    row = lambda width: pl.BlockSpec((1, tm, width), lambda b, s: (b, s, 0))
    return pl.pallas_call(
        functools.partial(_in_proj_kernel, lw=lw, n_heads=n_heads, dh=dh),
        grid=(B, S // tm),
        in_specs=[row(D), full((1, D)), full(w_all.shape), full(w_f.shape), full((1, LANES)),
                  full((1, hs)), full((1, hs)), full(pe.shape), full((1, hs))],
        out_specs=[row(lw), row(lw), row(hs), row(hs), row(hs)],
        out_shape=[jax.ShapeDtypeStruct((B, S, lw), F32), jax.ShapeDtypeStruct((B, S, lw), F32),
                   jax.ShapeDtypeStruct((B, S, hs), BF16), jax.ShapeDtypeStruct((B, S, hs), BF16),
                   jax.ShapeDtypeStruct((B, S, hs), BF16)],
        scratch_shapes=[pltpu.VMEM((1, LANES), F32)],
        compiler_params=pltpu.CompilerParams(
            dimension_semantics=("parallel", "arbitrary"), vmem_limit_bytes=VMEM_LIMIT),
        name="in_proj",
    )(x, norm_g.reshape(1, D), w_all, w_f, b_f, gq, gk, pe, jnp.asarray(cv))


def _rg_lru_kernel(u_ref, gate_ref, cw_ref, cb_ref, w_ref, ba_ref, bx_ref, ap_ref, og_ref,
                   o_ref, tail_ref, h_ref):
    @pl.when(pl.program_id(1) == 0)
    def _():
        tail_ref[...] = jnp.zeros_like(tail_ref)
        h_ref[...] = jnp.zeros_like(h_ref)

    u = u_ref[0]
    ts, lw = u.shape
    ext = jnp.concatenate([tail_ref[...], u], axis=0)
    tail_ref[...] = u[ts - SUBLANES:, :]
    xc = cb_ref[...] + u * cw_ref[CONV_WIDTH - 1:CONV_WIDTH, :]
    for back in range(1, CONV_WIDTH):
        tap = CONV_WIDTH - 1 - back
        xc = xc + pltpu.roll(ext, back, 0)[SUBLANES:, :] * cw_ref[tap:tap + 1, :]

    xb = xc.astype(BF16)
    r_parts, i_parts = [], []
    for j in range(lw // MXU_DIM):
        y = _dot(xb[:, j * MXU_DIM:(j + 1) * MXU_DIM], w_ref[j])
        r_parts.append(y[:, :MXU_DIM])
        i_parts.append(y[:, MXU_DIM:])
    r = jax.nn.sigmoid(jnp.concatenate(r_parts, axis=-1) + ba_ref[...])
    i = jax.nn.sigmoid(jnp.concatenate(i_parts, axis=-1) + bx_ref[...])
    log_a = (-LRU_C) * r * _softplus(-ap_ref[...])
    a = jnp.exp(log_a)
    mult = jnp.sqrt(jnp.maximum(-jnp.tanh(log_a) * (a * a + 1.0), 0.0))
    b = mult * (i * xc)

    groups = ts // SUBLANES
    a = a.reshape(groups, SUBLANES, lw)
    b = b.reshape(groups, SUBLANES, lw)
    row = lax.broadcasted_iota(jnp.int32, a.shape, 1)
    d = 1
    while d < SUBLANES:
        keep = row >= d
        a_prev = jnp.where(keep, pltpu.roll(a, d, 1), 1.0)
        b_prev = jnp.where(keep, pltpu.roll(b, d, 1), 0.0)
        b = a * b_prev + b
        a = a * a_prev
        d *= 2
    state = h_ref[...]
    parts = []
    for g in range(groups):
        hg = b[g] + a[g] * state
        state = hg[SUBLANES - 1:, :]
        parts.append(hg)
    h = jnp.concatenate(parts, axis=0)
    h_ref[...] = state

    y = h * jax.nn.gelu(gate_ref[0])
    o_ref[0] = _rms(y, og_ref[...]).astype(o_ref.dtype)


def _rg_lru(u, gate, conv_w, conv_b, wa, ba, wx, bx, a_param, out_g, *, ts):
    B, S, lw = u.shape
    nb, bd, _ = wa.shape
    per = MXU_DIM // bd
    n_tiles = lw // MXU_DIM

    def tiles(w):
        w = w.reshape(n_tiles, per, bd, bd)
        eye = jnp.eye(per, dtype=w.dtype)
        return jnp.einsum('tpij,pq->tpiqj', w, eye).reshape(n_tiles, MXU_DIM, MXU_DIM)

    w_cat = jnp.concatenate([tiles(wa), tiles(wx)], axis=-1).astype(BF16)
    vec = lambda a: a.reshape(1, lw)
    full = lambda shape: pl.BlockSpec(shape, lambda b, s: (0,) * len(shape))
    row = pl.BlockSpec((1, ts, lw), lambda b, s: (b, s, 0))
    return pl.pallas_call(
        _rg_lru_kernel,
        grid=(B, S // ts),
        in_specs=[row, row, full((CONV_WIDTH, lw)), full((1, lw)), full(w_cat.shape), full((1, lw)),
                  full((1, lw)), full((1, lw)), full((1, lw))],
        out_specs=row,
        out_shape=jax.ShapeDtypeStruct((B, S, lw), BF16),
        scratch_shapes=[pltpu.VMEM((SUBLANES, lw), F32), pltpu.VMEM((1, lw), F32)],
        compiler_params=pltpu.CompilerParams(
            dimension_semantics=("parallel", "arbitrary"), vmem_limit_bytes=VMEM_LIMIT),
        name="rg_lru",
    )(u, gate, conv_w, vec(conv_b), w_cat, vec(ba), vec(bx), vec(a_param), vec(out_g))


OVERFLOW_GUARD = 1e30


def _fox_attn_kernel(q_ref, k_ref, v_ref, o_ref, vt_ref, m_ref, acc_ref, *, dh, tq):
    qi = pl.program_id(2)
    n_chunks = vt_ref.shape[0]

    @pl.when(qi == 0)
    def _():
        for c in range(n_chunks):
            vt_ref[c] = v_ref[0, c * tq:(c + 1) * tq, :].astype(F32).T.astype(BF16)

    heads = [slice(hh * HEAD_SLOT, (hh + 1) * HEAD_SLOT) for hh in range(2)]

    def scores(j, q, sl, masked):
        start = pl.multiple_of(j * tq, tq)
        st = lax.dot_general(k_ref[0, pl.ds(start, tq), sl], q, (((1,), (1,)), ((), ())),
                             preferred_element_type=F32)
        if masked:
            kpos = lax.broadcasted_iota(jnp.int32, st.shape, 0)
            qpos = lax.broadcasted_iota(jnp.int32, st.shape, 1)
            st = jnp.where(kpos <= qpos, st, -jnp.inf)
        return st

    for hh, sl in enumerate(heads):
        st = scores(qi, q_ref[0, :, sl], sl, True)
        m = jnp.max(st, axis=0, keepdims=True)
        m_ref[hh] = m
        acc_ref[hh] = _dot(vt_ref[qi, sl, :], jnp.exp(st - m).astype(BF16))

    def fast_chunk(j):
        for hh, sl in enumerate(heads):
            pt = jnp.exp(scores(j, q_ref[0, :, sl], sl, False) - m_ref[hh]).astype(BF16)
            acc_ref[hh] += _dot(vt_ref[j, sl, :], pt)

    def fast_pair(i, carry):
        start = pl.multiple_of(2 * i * tq, 2 * tq)
        for hh, sl in enumerate(heads):
            st = lax.dot_general(k_ref[0, pl.ds(start, 2 * tq), sl], q_ref[0, :, sl],
                                 (((1,), (1,)), ((), ())), preferred_element_type=F32)
            pt = jnp.exp(st - m_ref[hh]).astype(BF16)
            vt = jnp.concatenate([vt_ref[2 * i, sl, :], vt_ref[2 * i + 1, sl, :]], axis=1)
            acc_ref[hh] += _dot(vt, pt)
        return carry

    lax.fori_loop(0, qi // 2, fast_pair, 0)

    @pl.when(qi % 2 == 1)
    def _():
        fast_chunk(qi - 1)

    def softmax_sums():
        return jnp.concatenate([acc_ref[0][dh:dh + 1, :], acc_ref[1][0:1, :]], axis=0)

    def normalised():
        sums = softmax_sums()
        return jnp.concatenate([acc_ref[0][0:dh, :] / sums[0:1, :],
                                acc_ref[1][dh:2 * dh, :] / sums[1:2, :]], axis=0)

    out = normalised()
    o_ref[0] = out.T.astype(o_ref.dtype)
    sums = softmax_sums()
    flag = lambda x: jnp.max(jnp.where(jnp.abs(x) < OVERFLOW_GUARD, 0.0, 1.0),
                             axis=0, keepdims=True)
    bad = jnp.max(jnp.maximum(flag(out), flag(sums)), axis=1, keepdims=True)

    @pl.when(bad[0, 0] > 0.0)
    def _():
        m_ref[...] = jnp.full_like(m_ref, -jnp.inf)
        acc_ref[...] = jnp.zeros_like(acc_ref)

        def chunk(j, masked):
            for hh, sl in enumerate(heads):
                st = scores(j, q_ref[0, :, sl], sl, masked)
                m_old = m_ref[hh]
                m_new = jnp.maximum(m_old, jnp.max(st, axis=0, keepdims=True))
                pt = jnp.exp(st - m_new).astype(BF16)
                acc_ref[hh] = jnp.exp(m_old - m_new) * acc_ref[hh] + _dot(vt_ref[j, sl, :], pt)
                m_ref[hh] = m_new

        def body(j, carry):
            chunk(j, False)
            return carry

        lax.fori_loop(0, qi, body, 0)
        chunk(qi, True)
        o_ref[0] = normalised().T.astype(o_ref.dtype)


def _fox_attn(q, k, v, *, n_heads, dh, tq):
    B, S, _ = q.shape
    nq = S // tq
    pair = 2 * HEAD_SLOT
    kv_spec = pl.BlockSpec((1, S, pair), lambda b, h, i: (b, 0, h))
    return pl.pallas_call(
        functools.partial(_fox_attn_kernel, dh=dh, tq=tq),
        grid=(B, n_heads // 2, nq),
        in_specs=[pl.BlockSpec((1, tq, pair), lambda b, h, i: (b, i, h)), kv_spec, kv_spec],
        out_specs=pl.BlockSpec((1, tq, 2 * dh), lambda b, h, i: (b, i, h)),
        out_shape=jax.ShapeDtypeStruct((B, S, n_heads * dh), F32),
        scratch_shapes=[pltpu.VMEM((nq, pair, tq), BF16), pltpu.VMEM((2, 1, tq), F32),
                        pltpu.VMEM((2, HEAD_SLOT, tq), F32)],
        compiler_params=pltpu.CompilerParams(
            dimension_semantics=("parallel", "parallel", "arbitrary"),
            vmem_limit_bytes=VMEM_LIMIT),
        name="fox_attn",
    )(q, k, v)


def _mem_kv_kernel(m_ref, g_ref, w_ref, kg_ref, k_ref, v_ref, *, n_heads, dh):
    mw = n_heads * dh
    mb = _rms(m_ref[0], g_ref[...]).astype(BF16)
    kv = _dot(mb, w_ref[...])
    parts = []
    for h in range(n_heads):
        parts.append(_rms(kv[:, h * dh:(h + 1) * dh], kg_ref[...]))
    k_ref[0] = jnp.concatenate(parts, axis=-1).astype(BF16)
    v_ref[0] = kv[:, mw:].astype(BF16)


def _mem_kv(mem, norm_g, wkv, k_g, *, n_heads, dh):
    B, M, D = mem.shape
    mw = n_heads * dh
    full = lambda shape: pl.BlockSpec(shape, lambda b: (0,) * len(shape))
    out = pl.BlockSpec((1, M, mw), lambda b: (b, 0, 0))
    return pl.pallas_call(
        functools.partial(_mem_kv_kernel, n_heads=n_heads, dh=dh),
        grid=(B,),
        in_specs=[pl.BlockSpec((1, M, D), lambda b: (b, 0, 0)), full((1, D)), full((D, 2 * mw)),
                  full((1, dh))],
        out_specs=[out, out],
        out_shape=[jax.ShapeDtypeStruct((B, M, mw), BF16)] * 2,
        compiler_params=pltpu.CompilerParams(
            dimension_semantics=("parallel",), vmem_limit_bytes=VMEM_LIMIT),
        name="mem_kv",
    )(mem, norm_g.reshape(1, D), wkv.astype(BF16), k_g.reshape(1, dh))


def _out_mem_kernel(x_ref, yl_ref, yf_ref, fg_ref, wol_ref, wof_ref, gx_ref, wq_ref, qg_ref,
                    km_ref, vm_ref, wo_ref, gf_ref, wr_ref, br_ref,
                    x2_ref, xn_ref, gates_ref, experts_ref, *, n_heads, dh, n_groups, per_group):
    yf = _rms(yf_ref[0], fg_ref[...]).astype(BF16)
    x1 = x_ref[0] + _dot(yl_ref[0], wol_ref[...]) + _dot(yf, wof_ref[...])

    q = _dot(_rms(x1, gx_ref[...]).astype(BF16), wq_ref[...])
    outs = []
    for h in range(n_heads):
        sl = slice(h * dh, (h + 1) * dh)
        qh = _rms(q[:, sl], qg_ref[...]).astype(BF16)
        s = lax.dot_general(qh, km_ref[0, :, sl], (((1,), (1,)), ((), ())),
                            preferred_element_type=F32)
        p = jnp.exp(s - jnp.max(s, axis=-1, keepdims=True))
        p = p / jnp.sum(p, axis=-1, keepdims=True)
        outs.append(_dot(p.astype(BF16), vm_ref[0, :, sl]))
    x2 = x1 + _dot(jnp.concatenate(outs, axis=-1).astype(BF16), wo_ref[...])
    x2_ref[0] = x2

    xn = _rms(x2, gf_ref[...])
    _store_chunk_rows(xn_ref, xn)
    logits = _dot(xn.astype(BF16), wr_ref[...]) + br_ref[...]
    lane = lax.broadcasted_iota(jnp.int32, logits.shape, 1).astype(F32)
    neg = -jnp.inf

    def top(vals):
        mx = jnp.max(vals, axis=-1, keepdims=True)
        idx = jnp.min(jnp.where(vals == mx, lane, float(LANES)), axis=-1, keepdims=True)
        return mx, idx

    gl = jnp.where(lane < n_groups, logits, neg)
    g_max, g_idx = top(gl)
    g_w = 1.0 / jnp.sum(jnp.exp(gl - g_max), axis=-1, keepdims=True)
    lo = n_groups + per_group * g_idx
    el = jnp.where((lane >= lo) & (lane < lo + per_group), logits, neg)
    e1, i1 = top(el)
    e2, i2 = top(jnp.where(lane == i1, neg, el))
    t = jnp.exp(e2 - e1)
    w1 = g_w / (1.0 + t)
    w2 = g_w * t / (1.0 + t)
    gates_ref[0] = jnp.where(lane == 0, w1, jnp.where(lane == 1, w2, 0.0))
    experts_ref[0] = jnp.where(lane == 0, i1 - n_groups,
                               jnp.where(lane == 1, i2 - n_groups, 0.0)).astype(jnp.int32)


def _out_mem(x, y_lru, y_fox, fox_out_g, w_out, norm_mem_x_g, mem_wq, mem_q_g, k_mem, v_mem, mem_wo,
             norm_ffn_g, router_group_w, router_group_b, router_expert_w, router_expert_b, *,
             n_heads, dh, tm):
    B, S, D = x.shape
    lw = y_lru.shape[-1]
    fw = y_fox.shape[-1]
    M = k_mem.shape[1]
    mw = n_heads * dh
    n_groups = router_group_w.shape[-1]
    n_experts = router_expert_w.shape[-1]
    w_r = jnp.concatenate([router_group_w, router_expert_w], axis=1)
    w_r = jnp.pad(w_r, ((0, 0), (0, LANES - w_r.shape[1]))).astype(BF16)
    b_r = jnp.concatenate([router_group_b, router_expert_b])
    b_r = jnp.pad(b_r, (0, LANES - b_r.shape[0])).reshape(1, LANES)
    q_gain = (mem_q_g * dh ** -0.5).reshape(1, dh)

    full = lambda shape: pl.BlockSpec(shape, lambda b, s: (0,) * len(shape))
    row = lambda width: pl.BlockSpec((1, tm, width), lambda b, s: (b, s, 0))
    mem_spec = pl.BlockSpec((1, M, mw), lambda b, s: (b, 0, 0))
    return pl.pallas_call(
        functools.partial(_out_mem_kernel, n_heads=n_heads, dh=dh, n_groups=n_groups,
                          per_group=n_experts // n_groups),
        grid=(B, S // tm),
        in_specs=[row(D), row(lw), row(fw), full((1, fw)), full((lw, D)), full((fw, D)),
                  full((1, D)), full((D, mw)), full((1, dh)), mem_spec, mem_spec, full((mw, D)),
                  full((1, D)), full((D, LANES)), full((1, LANES))],
        out_specs=[row(D), pl.BlockSpec((tm * D // LANES, LANES), lambda b, s: (b * (S // tm) + s, 0)),
                   row(LANES), row(LANES)],
        out_shape=[jax.ShapeDtypeStruct((B, S, D), F32),
                   jax.ShapeDtypeStruct((B * S * D // LANES, LANES), F32),
                   jax.ShapeDtypeStruct((B, S, LANES), F32),
                   jax.ShapeDtypeStruct((B, S, LANES), jnp.int32)],
        compiler_params=pltpu.CompilerParams(
            dimension_semantics=("parallel", "parallel"), vmem_limit_bytes=VMEM_LIMIT),
        name="out_mem",
    )(x, y_lru, y_fox, fox_out_g.reshape(1, fw), w_out[:lw].astype(BF16), w_out[lw:].astype(BF16),
      norm_mem_x_g.reshape(1, D), mem_wq.astype(BF16), q_gain, k_mem, v_mem, mem_wo.astype(BF16),
      norm_ffn_g.reshape(1, D), w_r, b_r)


DMA_UNROLL = 8


def _scatter_rows_kernel(zb_ref, idx_ref, x_ref, buf_ref, zero_ref, sem, zsem, *, top_k, chunks):
    tokens = x_ref.shape[0] // chunks

    @pl.when(pl.program_id(0) == 0)
    def _():
        rows = zero_ref.shape[0]
        zero_ref[...] = jnp.zeros_like(zero_ref)

        def copy(j):
            start = pl.multiple_of(j * rows, rows)
            return pltpu.make_async_copy(zero_ref, buf_ref.at[pl.ds(start, rows), :], zsem)

        def issue_zero(j, carry):
            @pl.when(zb_ref[j] == 1)
            def _():
                copy(j).start()
            return carry

        def drain_zero(j, carry):
            @pl.when(zb_ref[j] == 1)
            def _():
                copy(j).wait()
            return carry

        lax.fori_loop(0, zb_ref.shape[0], issue_zero, 0)
        lax.fori_loop(0, zb_ref.shape[0], drain_zero, 0)

    def issue(g, carry):
        for u in range(DMA_UNROLL):
            r = g * DMA_UNROLL + u
            src = x_ref.at[pl.ds(pl.multiple_of(r * chunks, chunks), chunks), :]
            for k in range(top_k):
                row = pl.multiple_of(idx_ref[0, 0, r * top_k + k], chunks)
                pltpu.make_async_copy(src, buf_ref.at[pl.ds(row, chunks), :], sem).start()
        return carry

    lax.fori_loop(0, tokens // DMA_UNROLL, issue, 0)
    for k in range(top_k):
        pltpu.make_async_copy(x_ref, buf_ref.at[pl.ds(0, tokens * chunks), :], sem).wait()


def _scatter_rows(x, dest, zero_blocks, *, tm, blk, chunks):
    nt = dest.shape[0]
    top_k = dest.shape[2] // tm
    n_blocks = zero_blocks.shape[0]
    grid_spec = pltpu.PrefetchScalarGridSpec(
        num_scalar_prefetch=1,
        grid=(nt,),
        in_specs=[pl.BlockSpec((1, 1, tm * top_k), lambda i, zb: (i, 0, 0), memory_space=pltpu.SMEM),
                  pl.BlockSpec((tm * chunks, LANES), lambda i, zb: (i, 0))],
        out_specs=pl.BlockSpec(memory_space=pl.ANY),
        scratch_shapes=[pltpu.VMEM((blk * chunks, LANES), x.dtype), pltpu.SemaphoreType.DMA,
                        pltpu.SemaphoreType.DMA],
    )
    return pl.pallas_call(
        functools.partial(_scatter_rows_kernel, top_k=top_k, chunks=chunks),
        grid_spec=grid_spec,
        out_shape=jax.ShapeDtypeStruct((n_blocks * blk * chunks, LANES), x.dtype),
        compiler_params=pltpu.CompilerParams(
            dimension_semantics=("arbitrary",), vmem_limit_bytes=VMEM_LIMIT),
        name="scatter_rows",
    )(zero_blocks, dest, x)


def _experts_kernel(be_ref, x_ref, wg_ref, wu_ref, wd_ref, o_ref, wgb_ref, wub_ref, wdb_ref, *, blk):
    i = pl.program_id(0)

    @pl.when((i == 0) | (be_ref[i] != be_ref[jnp.maximum(i - 1, 0)]))
    def _():
        wgb_ref[...] = wg_ref[0].astype(BF16)
        wub_ref[...] = wu_ref[0].astype(BF16)
        wdb_ref[...] = wd_ref[0].astype(BF16)

    d = wg_ref.shape[1]
    xb = _load_chunk_rows(x_ref, blk, d).astype(BF16)
    hdn = jax.nn.silu(_dot(xb, wgb_ref[...])) * _dot(xb, wub_ref[...])
    _store_chunk_rows(o_ref, _dot(hdn.astype(BF16), wdb_ref[...]))


def _experts(x_buf, blk_expert, w_gate, w_up, w_down, *, blk):
    _, D, de = w_gate.shape
    chunks = D // LANES
    n_blocks = x_buf.shape[0] // (blk * chunks)
    rows_spec = pl.BlockSpec((blk * chunks, LANES), lambda i, be: (i, 0))
    grid_spec = pltpu.PrefetchScalarGridSpec(
        num_scalar_prefetch=1,
        grid=(n_blocks,),
        in_specs=[rows_spec,
                  pl.BlockSpec((1, D, de), lambda i, be: (be[i], 0, 0)),
                  pl.BlockSpec((1, D, de), lambda i, be: (be[i], 0, 0)),
                  pl.BlockSpec((1, de, D), lambda i, be: (be[i], 0, 0))],
        out_specs=rows_spec,
        scratch_shapes=[pltpu.VMEM((D, de), BF16), pltpu.VMEM((D, de), BF16),
                        pltpu.VMEM((de, D), BF16)],
    )
    return pl.pallas_call(
        functools.partial(_experts_kernel, blk=blk),
        grid_spec=grid_spec,
        out_shape=jax.ShapeDtypeStruct(x_buf.shape, F32),
        compiler_params=pltpu.CompilerParams(
            dimension_semantics=("arbitrary",), vmem_limit_bytes=VMEM_LIMIT),
        name="experts",
    )(blk_expert, x_buf, w_gate, w_up, w_down)


def _combine_kernel(idx_ref, idx_next_ref, x_ref, g_ref, y_ref, o_ref, buf_ref, sem, *, top_k, chunks):
    i = pl.program_id(0)
    n = pl.num_programs(0)
    slot = lax.rem(i, 2)
    tokens, d = x_ref.shape

    def request(ids_ref, s):
        def issue(g, carry):
            for u in range(DMA_UNROLL):
                r = g * DMA_UNROLL + u
                for k in range(top_k):
                    row = pl.multiple_of(ids_ref[0, 0, r * top_k + k], chunks)
                    dst = buf_ref.at[s, k, pl.ds(pl.multiple_of(r * chunks, chunks), chunks), :]
                    pltpu.make_async_copy(y_ref.at[pl.ds(row, chunks), :], dst, sem.at[s]).start()
            return carry
        lax.fori_loop(0, tokens // DMA_UNROLL, issue, 0)

    @pl.when(i == 0)
    def _():
        request(idx_ref, 0)

    @pl.when(i + 1 < n)
    def _():
        request(idx_next_ref, 1 - slot)

    for k in range(top_k):
        pltpu.make_async_copy(y_ref.at[pl.ds(0, tokens * chunks), :], buf_ref.at[slot, k],
                              sem.at[slot]).wait()
    out = x_ref[...]
    for k in range(top_k):
        out = out + g_ref[:, k:k + 1] * _load_chunk_rows(buf_ref.at[slot, k], tokens, d)
    o_ref[...] = out


def _combine(x2, gates, y_buf, dest, *, tm):
    T, D = x2.shape
    chunks = D // LANES
    nt = dest.shape[0]
    top_k = dest.shape[2] // tm
    idx_spec = lambda step: pl.BlockSpec(
        (1, 1, tm * top_k), lambda i: (jnp.minimum(i + step, nt - 1), 0, 0), memory_space=pltpu.SMEM)
    return pl.pallas_call(
        functools.partial(_combine_kernel, top_k=top_k, chunks=chunks),
        grid=(nt,),
        in_specs=[idx_spec(0), idx_spec(1),
                  pl.BlockSpec((tm, D), lambda i: (i, 0)),
                  pl.BlockSpec((tm, LANES), lambda i: (i, 0)),
                  pl.BlockSpec(memory_space=pl.ANY)],
        out_specs=pl.BlockSpec((tm, D), lambda i: (i, 0)),
        out_shape=jax.ShapeDtypeStruct((T, D), F32),
        scratch_shapes=[pltpu.VMEM((2, top_k, tm * chunks, LANES), F32),
                        pltpu.SemaphoreType.DMA((2,))],
        compiler_params=pltpu.CompilerParams(
            dimension_semantics=("arbitrary",), vmem_limit_bytes=VMEM_LIMIT),
        name="combine",
    )(dest, dest, x2, gates, y_buf)


def _dispatch(experts, n_experts, blk):
    T, K = experts.shape
    A = T * K
    e_flat = experts.reshape(A)
    onehot = (e_flat[:, None] == jnp.arange(n_experts, dtype=jnp.int32)[None, :]).astype(jnp.int32)
    ranks = jnp.cumsum(onehot, axis=0)
    counts = ranks[-1]
    padded = (counts + blk - 1) // blk * blk
    pends = jnp.cumsum(padded)
    pstarts = pends - padded
    dest = jnp.sum(onehot * (ranks - 1 + pstarts[None, :]), axis=1)
    n_blocks = (A + n_experts * blk) // blk
    blk_start = jnp.arange(n_blocks, dtype=jnp.int32) * blk
    blk_expert = jnp.minimum(jnp.sum(blk_start[:, None] >= pends[None, :], axis=1), n_experts - 1)
    last_of_segment = jnp.any((blk_start[:, None] + blk == pends[None, :]) & (padded[None, :] > 0),
                              axis=1)
    zero_blocks = (last_of_segment | (blk_start >= pends[-1])).astype(jnp.int32)
    return dest.reshape(T, K), blk_expert.astype(jnp.int32), zero_blocks


def kernel(x, mem, norm_mix_g, w_in, b_forget, conv_w, conv_b, lru_wa, lru_ba, lru_wx, lru_bx,
           lru_a_param, fox_q_g, fox_k_g, lru_out_g, fox_out_g, w_out, norm_mem_x_g, norm_mem_g,
           mem_wq, mem_wkv, mem_q_g, mem_k_g, mem_wo, norm_ffn_g, router_group_w, router_group_b,
           router_expert_w, router_expert_b, exp_w_gate, exp_w_up, exp_w_down):
    B, S, D = x.shape
    depth = norm_mix_g.shape[0]
    lw = conv_w.shape[-1]
    n_heads = b_forget.shape[-1]
    dh = fox_q_g.shape[-1]
    mem_dh = mem_q_g.shape[-1]
    mem_heads = mem_wq.shape[-1] // mem_dh
    n_experts = router_expert_w.shape[-1]
    T = B * S
    tm = _pick_tile(S, 512)
    tq = _pick_tile(S, 512)
    blk = 256
    scatter_tm = _pick_tile(T, 1024)

    for l in range(depth):
        u, gate, q, k, v = _in_proj(x, norm_mix_g[l], w_in[l], b_forget[l], fox_q_g[l], fox_k_g[l],
                                    lw=lw, n_heads=n_heads, dh=dh, tm=tm)
        y_lru = _rg_lru(u, gate, conv_w[l], conv_b[l], lru_wa[l], lru_ba[l], lru_wx[l], lru_bx[l],
                        lru_a_param[l], lru_out_g[l], ts=tm)
        y_fox = _fox_attn(q, k, v, n_heads=n_heads, dh=dh, tq=tq)
        k_mem, v_mem = _mem_kv(mem, norm_mem_g[l], mem_wkv[l], mem_k_g[l], n_heads=mem_heads,
                               dh=mem_dh)
        x2, xn, gates, experts = _out_mem(
            x, y_lru, y_fox, fox_out_g[l], w_out[l], norm_mem_x_g[l], mem_wq[l], mem_q_g[l], k_mem,
            v_mem, mem_wo[l], norm_ffn_g[l], router_group_w[l], router_group_b[l],
            router_expert_w[l], router_expert_b[l], n_heads=mem_heads, dh=mem_dh, tm=tm)

        dest, blk_expert, zero_blocks = _dispatch(
            experts.reshape(T, LANES)[:, :TOP_K], n_experts, blk)
        chunks = D // LANES
        tiled = lambda t: (dest * chunks).reshape(T // t, 1, t * TOP_K)
        x_buf = _scatter_rows(xn, tiled(scatter_tm), zero_blocks, tm=scatter_tm, blk=blk,
                              chunks=chunks)
        y_buf = _experts(x_buf, blk_expert, exp_w_gate[l], exp_w_up[l], exp_w_down[l], blk=blk)
        x = _combine(x2.reshape(T, D), gates.reshape(T, LANES), y_buf, tiled(tm), tm=tm)
        x = x.reshape(B, S, D)
    return x
```

```python
import functools

import jax
import jax.numpy as jnp
import numpy as np
from jax import lax
from jax.experimental import pallas as pl
from jax.experimental.pallas import tpu as pltpu

EPS = 1e-6
LRU_C = 8.0
CONV_WIDTH = 4
TOP_K = 2
LANES = 128
SUBLANES = 8
MXU_DIM = 256
HEAD_SLOT = LANES
VMEM_LIMIT = 56 * 1024 * 1024

F32 = jnp.float32
BF16 = jnp.bfloat16


def _dot(a, b):
    return jnp.dot(a, b, preferred_element_type=F32)


def _rms(x, g):
    return x * lax.rsqrt(jnp.mean(x * x, axis=-1, keepdims=True) + EPS) * g


def _softplus(x):
    return jnp.maximum(x, 0.0) + jnp.log1p(jnp.exp(-jnp.abs(x)))


def _pick_tile(n, target):
    t = min(n, target)
    while n % t:
        t //= 2
    return t


def _store_chunk_rows(ref, val):
    n, d = val.shape
    chunks = d // LANES
    for c in range(chunks):
        ref[pl.ds(c, n, stride=chunks), :] = val[:, c * LANES:(c + 1) * LANES]


def _load_chunk_rows(ref, n, d):
    chunks = d // LANES
    return jnp.concatenate([ref[pl.ds(c, n, stride=chunks), :] for c in range(chunks)], axis=1)


def _in_proj_kernel(x_ref, g_ref, w_ref, wf_ref, bf_ref, gq_ref, gk_ref, pe_ref, cv_ref,
                    u_ref, gate_ref, q_ref, k_ref, v_ref, carry_ref, *, lw, n_heads, dh):
    @pl.when(pl.program_id(1) == 0)
    def _():
        carry_ref[...] = jnp.zeros_like(carry_ref)

    hs = n_heads * HEAD_SLOT
    x = x_ref[0]
    tm = x.shape[0]
    hb = _rms(x, g_ref[...]).astype(BF16)

    u_ref[0] = _dot(hb, w_ref[:, 0:lw])
    gate_ref[0] = _dot(hb, w_ref[:, lw:2 * lw])

    z = _dot(hb, wf_ref[...]) + bf_ref[...]
    lane = lax.broadcasted_iota(jnp.int32, z.shape, 1)
    row = lax.broadcasted_iota(jnp.int32, z.shape, 0)
    c = jnp.where(lane < n_heads, -_softplus(-z), 0.0)
    d = 1
    while d < tm:
        c = c + jnp.where(row >= d, pltpu.roll(c, d, 0), 0.0)
        d *= 2
    c = c + carry_ref[...]
    carry_ref[...] = c[tm - 1:tm, :]
    c1 = c.astype(BF16).astype(F32)
    r1 = c - c1
    c2 = r1.astype(BF16).astype(F32)
    c3 = (r1 - c2).astype(BF16).astype(F32)
    e = c1 + pltpu.roll(c2, n_heads, 1) + pltpu.roll(c3, 2 * n_heads, 1)
    e = jnp.where(lane == 3 * n_heads, 1.0, e).astype(BF16)
    ext = _dot(e, pe_ref[...])

    fw = n_heads * dh
    lane_t = lax.broadcasted_iota(jnp.int32, (tm, LANES), 1)
    own = (lane_t < dh, lane_t >= dh)

    def to_slots(y):
        parts = []
        for c in range(fw // LANES):
            col = y[:, c * LANES:(c + 1) * LANES]
            parts += [jnp.where(own[0], col, 0.0), jnp.where(own[1], col, 0.0)]
        return jnp.concatenate(parts, axis=-1)

    def head_norm(y, gain):
        parts = []
        for h in range(n_heads):
            blk = y[:, h * HEAD_SLOT:(h + 1) * HEAD_SLOT]
            ss = jnp.sum(blk * blk, axis=-1, keepdims=True) * (1.0 / dh)
            parts.append(blk * lax.rsqrt(ss + EPS))
        return jnp.concatenate(parts, axis=-1) * gain

    q = to_slots(_dot(hb, w_ref[:, 2 * lw:2 * lw + fw]))
    q_ref[0] = (head_norm(q, gq_ref[...]) + ext[:, 0:hs]).astype(BF16)
    k = to_slots(_dot(hb, w_ref[:, 2 * lw + fw:2 * lw + 2 * fw]))
    k_ref[0] = (head_norm(k, gk_ref[...]) + ext[:, hs:2 * hs]).astype(BF16)
    v = to_slots(_dot(hb, w_ref[:, 2 * lw + 2 * fw:2 * lw + 3 * fw]))
    v_ref[0] = (v + cv_ref[...]).astype(BF16)


def _in_proj(x, norm_g, w_in, b_forget, fox_q_g, fox_k_g, *, lw, n_heads, dh, tm):
    B, S, D = x.shape
    fw = n_heads * dh
    hs = n_heads * HEAD_SLOT
    assert 2 * dh == HEAD_SLOT and fw % LANES == 0 and 3 * n_heads <= LANES
    n_main = 2 * lw + 3 * fw
    w_all = w_in[:, :n_main].astype(BF16)
    w_f = jnp.pad(w_in[:, n_main:], ((0, 0), (0, LANES - n_heads))).astype(BF16)
    b_f = jnp.pad(b_forget, (0, LANES - n_heads)).reshape(1, LANES)
    scale = dh ** -0.5

    def slot_gains(g):
        even = jnp.pad(g, (0, HEAD_SLOT - dh))
        odd = jnp.pad(g, (HEAD_SLOT - dh, 0))
        return jnp.tile(jnp.concatenate([even, odd]), n_heads // 2).reshape(1, hs)

    gq = slot_gains(fox_q_g * scale)
    gk = slot_gains(fox_k_g)
    ones_row = 3 * n_heads
    pe = np.zeros((LANES, 2 * hs), np.float32)
    cv = np.zeros((1, hs), np.float32)
    for h in range(n_heads):
        x0 = h * HEAD_SLOT + (dh if h % 2 == 0 else 0)
        for j in range(3):
            pe[j * n_heads + h, x0 + j] = 1.0
            pe[ones_row, x0 + 3 + j] = 1.0
            pe[ones_row, hs + x0 + j] = 1.0
            pe[j * n_heads + h, hs + x0 + 3 + j] = -1.0
        cv[0, x0] = 1.0
    pe = jnp.asarray(pe, BF16)

    full = lambda shape: pl.BlockSpec(shape, lambda b, s: (0,) * len(shape))
    row = lambda width: pl.BlockSpec((1, tm, width), lambda b, s: (b, s, 0))
    return pl.pallas_call(
        functools.partial(_in_proj_kernel, lw=lw, n_heads=n_heads, dh=dh),
        grid=(B, S // tm),
        in_specs=[row(D), full((1, D)), full(w_all.shape), full(w_f.shape), full((1, LANES)),
                  full((1, hs)), full((1, hs)), full(pe.shape), full((1, hs))],
        out_specs=[row(lw), row(lw), row(hs), row(hs), row(hs)],
        out_shape=[jax.ShapeDtypeStruct((B, S, lw), F32), jax.ShapeDtypeStruct((B, S, lw), F32),
                   jax.ShapeDtypeStruct((B, S, hs), BF16), jax.ShapeDtypeStruct((B, S, hs), BF16),
                   jax.ShapeDtypeStruct((B, S, hs), BF16)],
        scratch_shapes=[pltpu.VMEM((1, LANES), F32)],
        compiler_params=pltpu.CompilerParams(
            dimension_semantics=("parallel", "arbitrary"), vmem_limit_bytes=VMEM_LIMIT),
        name="in_proj",
    )(x, norm_g.reshape(1, D), w_all, w_f, b_f, gq, gk, pe, jnp.asarray(cv))


def _rg_lru_kernel(u_ref, gate_ref, cw_ref, cb_ref, w_ref, ba_ref, bx_ref, ap_ref, og_ref,
                   o_ref, tail_ref, h_ref):
    @pl.when(pl.program_id(1) == 0)
    def _():
        tail_ref[...] = jnp.zeros_like(tail_ref)
        h_ref[...] = jnp.zeros_like(h_ref)

    u = u_ref[0]
    ts, lw = u.shape
    ext = jnp.concatenate([tail_ref[...], u], axis=0)
    tail_ref[...] = u[ts - SUBLANES:, :]
    xc = cb_ref[...] + u * cw_ref[CONV_WIDTH - 1:CONV_WIDTH, :]
    for back in range(1, CONV_WIDTH):
        tap = CONV_WIDTH - 1 - back
        xc = xc + pltpu.roll(ext, back, 0)[SUBLANES:, :] * cw_ref[tap:tap + 1, :]

    xb = xc.astype(BF16)
    r_parts, i_parts = [], []
    for j in range(lw // MXU_DIM):
        y = _dot(xb[:, j * MXU_DIM:(j + 1) * MXU_DIM], w_ref[j])
        r_parts.append(y[:, :MXU_DIM])
        i_parts.append(y[:, MXU_DIM:])
    r = jax.nn.sigmoid(jnp.concatenate(r_parts, axis=-1) + ba_ref[...])
    i = jax.nn.sigmoid(jnp.concatenate(i_parts, axis=-1) + bx_ref[...])
    log_a = (-LRU_C) * r * _softplus(-ap_ref[...])
    a = jnp.exp(log_a)
    mult = jnp.sqrt(jnp.maximum(-jnp.tanh(log_a) * (a * a + 1.0), 0.0))
    b = mult * (i * xc)

    groups = ts // SUBLANES
    a = a.reshape(groups, SUBLANES, lw)
    b = b.reshape(groups, SUBLANES, lw)
    row = lax.broadcasted_iota(jnp.int32, a.shape, 1)
    d = 1
    while d < SUBLANES:
        keep = row >= d
        a_prev = jnp.where(keep, pltpu.roll(a, d, 1), 1.0)
        b_prev = jnp.where(keep, pltpu.roll(b, d, 1), 0.0)
        b = a * b_prev + b
        a = a * a_prev
        d *= 2
    state = h_ref[...]
    parts = []
    for g in range(groups):
        hg = b[g] + a[g] * state
        state = hg[SUBLANES - 1:, :]
        parts.append(hg)
    h = jnp.concatenate(parts, axis=0)
    h_ref[...] = state

    y = h * jax.nn.gelu(gate_ref[0])
    o_ref[0] = _rms(y, og_ref[...]).astype(o_ref.dtype)


def _rg_lru(u, gate, conv_w, conv_b, wa, ba, wx, bx, a_param, out_g, *, ts):
    B, S, lw = u.shape
    nb, bd, _ = wa.shape
    per = MXU_DIM // bd
    n_tiles = lw // MXU_DIM

    def tiles(w):
        w = w.reshape(n_tiles, per, bd, bd)
        eye = jnp.eye(per, dtype=w.dtype)
        return jnp.einsum('tpij,pq->tpiqj', w, eye).reshape(n_tiles, MXU_DIM, MXU_DIM)

    w_cat = jnp.concatenate([tiles(wa), tiles(wx)], axis=-1).astype(BF16)
    vec = lambda a: a.reshape(1, lw)
    full = lambda shape: pl.BlockSpec(shape, lambda b, s: (0,) * len(shape))
    row = pl.BlockSpec((1, ts, lw), lambda b, s: (b, s, 0))
    return pl.pallas_call(
        _rg_lru_kernel,
        grid=(B, S // ts),
        in_specs=[row, row, full((CONV_WIDTH, lw)), full((1, lw)), full(w_cat.shape), full((1, lw)),
                  full((1, lw)), full((1, lw)), full((1, lw))],
        out_specs=row,
        out_shape=jax.ShapeDtypeStruct((B, S, lw), BF16),
        scratch_shapes=[pltpu.VMEM((SUBLANES, lw), F32), pltpu.VMEM((1, lw), F32)],
        compiler_params=pltpu.CompilerParams(
            dimension_semantics=("parallel", "arbitrary"), vmem_limit_bytes=VMEM_LIMIT),
        name="rg_lru",
    )(u, gate, conv_w, vec(conv_b), w_cat, vec(ba), vec(bx), vec(a_param), vec(out_g))


OVERFLOW_GUARD = 1e30


def _fox_attn_kernel(q_ref, k_ref, v_ref, o_ref, vt_ref, m_ref, acc_ref, *, dh, tq):
    qi = pl.program_id(2)
    n_chunks = vt_ref.shape[0]

    @pl.when(qi == 0)
    def _():
        for c in range(n_chunks):
            vt_ref[c] = v_ref[0, c * tq:(c + 1) * tq, :].astype(F32).T.astype(BF16)

    heads = [slice(hh * HEAD_SLOT, (hh + 1) * HEAD_SLOT) for hh in range(2)]

    def scores(j, q, sl, masked):
        start = pl.multiple_of(j * tq, tq)
        st = lax.dot_general(k_ref[0, pl.ds(start, tq), sl], q, (((1,), (1,)), ((), ())),
                             preferred_element_type=F32)
        if masked:
            kpos = lax.broadcasted_iota(jnp.int32, st.shape, 0)
            qpos = lax.broadcasted_iota(jnp.int32, st.shape, 1)
            st = jnp.where(kpos <= qpos, st, -jnp.inf)
        return st

    for hh, sl in enumerate(heads):
        st = scores(qi, q_ref[0, :, sl], sl, True)
        m = jnp.max(st, axis=0, keepdims=True)
        m_ref[hh] = m
        acc_ref[hh] = _dot(vt_ref[qi, sl, :], jnp.exp(st - m).astype(BF16))

    def fast_chunk(j):
        for hh, sl in enumerate(heads):
            pt = jnp.exp(scores(j, q_ref[0, :, sl], sl, False) - m_ref[hh]).astype(BF16)
            acc_ref[hh] += _dot(vt_ref[j, sl, :], pt)

    def fast_pair(i, carry):
        start = pl.multiple_of(2 * i * tq, 2 * tq)
        for hh, sl in enumerate(heads):
            st = lax.dot_general(k_ref[0, pl.ds(start, 2 * tq), sl], q_ref[0, :, sl],
                                 (((1,), (1,)), ((), ())), preferred_element_type=F32)
            pt = jnp.exp(st - m_ref[hh]).astype(BF16)
            vt = jnp.concatenate([vt_ref[2 * i, sl, :], vt_ref[2 * i + 1, sl, :]], axis=1)
            acc_ref[hh] += _dot(vt, pt)
        return carry

    lax.fori_loop(0, qi // 2, fast_pair, 0)

    @pl.when(qi % 2 == 1)
    def _():
        fast_chunk(qi - 1)

    def softmax_sums():
        return jnp.concatenate([acc_ref[0][dh:dh + 1, :], acc_ref[1][0:1, :]], axis=0)

    def normalised():
        sums = softmax_sums()
        return jnp.concatenate([acc_ref[0][0:dh, :] / sums[0:1, :],
                                acc_ref[1][dh:2 * dh, :] / sums[1:2, :]], axis=0)

    out = normalised()
    o_ref[0] = out.T.astype(o_ref.dtype)
    sums = softmax_sums()
    flag = lambda x: jnp.max(jnp.where(jnp.abs(x) < OVERFLOW_GUARD, 0.0, 1.0),
                             axis=0, keepdims=True)
    bad = jnp.max(jnp.maximum(flag(out), flag(sums)), axis=1, keepdims=True)

    @pl.when(bad[0, 0] > 0.0)
    def _():
        m_ref[...] = jnp.full_like(m_ref, -jnp.inf)
        acc_ref[...] = jnp.zeros_like(acc_ref)

        def chunk(j, masked):
            for hh, sl in enumerate(heads):
                st = scores(j, q_ref[0, :, sl], sl, masked)
                m_old = m_ref[hh]
                m_new = jnp.maximum(m_old, jnp.max(st, axis=0, keepdims=True))
                pt = jnp.exp(st - m_new).astype(BF16)
                acc_ref[hh] = jnp.exp(m_old - m_new) * acc_ref[hh] + _dot(vt_ref[j, sl, :], pt)
                m_ref[hh] = m_new

        def body(j, carry):
            chunk(j, False)
            return carry

        lax.fori_loop(0, qi, body, 0)
        chunk(qi, True)
        o_ref[0] = normalised().T.astype(o_ref.dtype)


def _fox_attn(q, k, v, *, n_heads, dh, tq):
    B, S, _ = q.shape
    nq = S // tq
    pair = 2 * HEAD_SLOT
    kv_spec = pl.BlockSpec((1, S, pair), lambda b, h, i: (b, 0, h))
    return pl.pallas_call(
        functools.partial(_fox_attn_kernel, dh=dh, tq=tq),
        grid=(B, n_heads // 2, nq),
        in_specs=[pl.BlockSpec((1, tq, pair), lambda b, h, i: (b, i, h)), kv_spec, kv_spec],
        out_specs=pl.BlockSpec((1, tq, 2 * dh), lambda b, h, i: (b, i, h)),
        out_shape=jax.ShapeDtypeStruct((B, S, n_heads * dh), F32),
        scratch_shapes=[pltpu.VMEM((nq, pair, tq), BF16), pltpu.VMEM((2, 1, tq), F32),
                        pltpu.VMEM((2, HEAD_SLOT, tq), F32)],
        compiler_params=pltpu.CompilerParams(
            dimension_semantics=("parallel", "parallel", "arbitrary"),
            vmem_limit_bytes=VMEM_LIMIT),
        name="fox_attn",
    )(q, k, v)


def _mem_kv_kernel(m_ref, g_ref, w_ref, kg_ref, k_ref, v_ref, *, n_heads, dh):
    mw = n_heads * dh
    mb = _rms(m_ref[0], g_ref[...]).astype(BF16)
    kv = _dot(mb, w_ref[...])
    parts = []
    for h in range(n_heads):
        parts.append(_rms(kv[:, h * dh:(h + 1) * dh], kg_ref[...]))
    k_ref[0] = jnp.concatenate(parts, axis=-1).astype(BF16)
    v_ref[0] = kv[:, mw:].astype(BF16)


def _mem_kv(mem, norm_g, wkv, k_g, *, n_heads, dh):
    B, M, D = mem.shape
    mw = n_heads * dh
    full = lambda shape: pl.BlockSpec(shape, lambda b: (0,) * len(shape))
    out = pl.BlockSpec((1, M, mw), lambda b: (b, 0, 0))
    return pl.pallas_call(
        functools.partial(_mem_kv_kernel, n_heads=n_heads, dh=dh),
        grid=(B,),
        in_specs=[pl.BlockSpec((1, M, D), lambda b: (b, 0, 0)), full((1, D)), full((D, 2 * mw)),
                  full((1, dh))],
        out_specs=[out, out],
        out_shape=[jax.ShapeDtypeStruct((B, M, mw), BF16)] * 2,
        compiler_params=pltpu.CompilerParams(
            dimension_semantics=("parallel",), vmem_limit_bytes=VMEM_LIMIT),
        name="mem_kv",
    )(mem, norm_g.reshape(1, D), wkv.astype(BF16), k_g.reshape(1, dh))


def _out_mem_kernel(x_ref, yl_ref, yf_ref, fg_ref, wol_ref, wof_ref, gx_ref, wq_ref, qg_ref,
                    km_ref, vm_ref, wo_ref, gf_ref, wr_ref, br_ref,
                    x2_ref, xn_ref, gates_ref, experts_ref, *, n_heads, dh, n_groups, per_group):
    yf = _rms(yf_ref[0], fg_ref[...]).astype(BF16)
    x1 = x_ref[0] + _dot(yl_ref[0], wol_ref[...]) + _dot(yf, wof_ref[...])

    q = _dot(_rms(x1, gx_ref[...]).astype(BF16), wq_ref[...])
    outs = []
    for h in range(n_heads):
        sl = slice(h * dh, (h + 1) * dh)
        qh = _rms(q[:, sl], qg_ref[...]).astype(BF16)
        s = lax.dot_general(qh, km_ref[0, :, sl], (((1,), (1,)), ((), ())),
                            preferred_element_type=F32)
        p = jnp.exp(s - jnp.max(s, axis=-1, keepdims=True))
        p = p / jnp.sum(p, axis=-1, keepdims=True)
        outs.append(_dot(p.astype(BF16), vm_ref[0, :, sl]))
    x2 = x1 + _dot(jnp.concatenate(outs, axis=-1).astype(BF16), wo_ref[...])
    x2_ref[0] = x2

    xn = _rms(x2, gf_ref[...])
    _store_chunk_rows(xn_ref, xn)
    logits = _dot(xn.astype(BF16), wr_ref[...]) + br_ref[...]
    lane = lax.broadcasted_iota(jnp.int32, logits.shape, 1).astype(F32)
    neg = -jnp.inf

    def top(vals):
        mx = jnp.max(vals, axis=-1, keepdims=True)
        idx = jnp.min(jnp.where(vals == mx, lane, float(LANES)), axis=-1, keepdims=True)
        return mx, idx

    gl = jnp.where(lane < n_groups, logits, neg)
    g_max, g_idx = top(gl)
    g_w = 1.0 / jnp.sum(jnp.exp(gl - g_max), axis=-1, keepdims=True)
    lo = n_groups + per_group * g_idx
    el = jnp.where((lane >= lo) & (lane < lo + per_group), logits, neg)
    e1, i1 = top(el)
    e2, i2 = top(jnp.where(lane == i1, neg, el))
    t = jnp.exp(e2 - e1)
    w1 = g_w / (1.0 + t)
    w2 = g_w * t / (1.0 + t)
    gates_ref[0] = jnp.where(lane == 0, w1, jnp.where(lane == 1, w2, 0.0))
    experts_ref[0] = jnp.where(lane == 0, i1 - n_groups,
                               jnp.where(lane == 1, i2 - n_groups, 0.0)).astype(jnp.int32)


def _out_mem(x, y_lru, y_fox, fox_out_g, w_out, norm_mem_x_g, mem_wq, mem_q_g, k_mem, v_mem, mem_wo,
             norm_ffn_g, router_group_w, router_group_b, router_expert_w, router_expert_b, *,
             n_heads, dh, tm):
    B, S, D = x.shape
    lw = y_lru.shape[-1]
    fw = y_fox.shape[-1]
    M = k_mem.shape[1]
    mw = n_heads * dh
    n_groups = router_group_w.shape[-1]
    n_experts = router_expert_w.shape[-1]
    w_r = jnp.concatenate([router_group_w, router_expert_w], axis=1)
    w_r = jnp.pad(w_r, ((0, 0), (0, LANES - w_r.shape[1]))).astype(BF16)
    b_r = jnp.concatenate([router_group_b, router_expert_b])
    b_r = jnp.pad(b_r, (0, LANES - b_r.shape[0])).reshape(1, LANES)
    q_gain = (mem_q_g * dh ** -0.5).reshape(1, dh)

    full = lambda shape: pl.BlockSpec(shape, lambda b, s: (0,) * len(shape))
    row = lambda width: pl.BlockSpec((1, tm, width), lambda b, s: (b, s, 0))
    mem_spec = pl.BlockSpec((1, M, mw), lambda b, s: (b, 0, 0))
    return pl.pallas_call(
        functools.partial(_out_mem_kernel, n_heads=n_heads, dh=dh, n_groups=n_groups,
                          per_group=n_experts // n_groups),
        grid=(B, S // tm),
        in_specs=[row(D), row(lw), row(fw), full((1, fw)), full((lw, D)), full((fw, D)),
                  full((1, D)), full((D, mw)), full((1, dh)), mem_spec, mem_spec, full((mw, D)),
                  full((1, D)), full((D, LANES)), full((1, LANES))],
        out_specs=[row(D), pl.BlockSpec((tm * D // LANES, LANES), lambda b, s: (b * (S // tm) + s, 0)),
                   row(LANES), row(LANES)],
        out_shape=[jax.ShapeDtypeStruct((B, S, D), F32),
                   jax.ShapeDtypeStruct((B * S * D // LANES, LANES), F32),
                   jax.ShapeDtypeStruct((B, S, LANES), F32),
                   jax.ShapeDtypeStruct((B, S, LANES), jnp.int32)],
        compiler_params=pltpu.CompilerParams(
            dimension_semantics=("parallel", "parallel"), vmem_limit_bytes=VMEM_LIMIT),
        name="out_mem",
    )(x, y_lru, y_fox, fox_out_g.reshape(1, fw), w_out[:lw].astype(BF16), w_out[lw:].astype(BF16),
      norm_mem_x_g.reshape(1, D), mem_wq.astype(BF16), q_gain, k_mem, v_mem, mem_wo.astype(BF16),
      norm_ffn_g.reshape(1, D), w_r, b_r)


DMA_UNROLL = 8


def _scatter_rows_kernel(zb_ref, idx_ref, x_ref, buf_ref, zero_ref, sem, zsem, *, top_k, chunks):
    tokens = x_ref.shape[0] // chunks

    @pl.when(pl.program_id(0) == 0)
    def _():
        rows = zero_ref.shape[0]
        zero_ref[...] = jnp.zeros_like(zero_ref)

        def copy(j):
            start = pl.multiple_of(j * rows, rows)
            return pltpu.make_async_copy(zero_ref, buf_ref.at[pl.ds(start, rows), :], zsem)

        def issue_zero(j, carry):
            @pl.when(zb_ref[j] == 1)
            def _():
                copy(j).start()
            return carry

        def drain_zero(j, carry):
            @pl.when(zb_ref[j] == 1)
            def _():
                copy(j).wait()
            return carry

        lax.fori_loop(0, zb_ref.shape[0], issue_zero, 0)
        lax.fori_loop(0, zb_ref.shape[0], drain_zero, 0)

    def issue(g, carry):
        for u in range(DMA_UNROLL):
            r = g * DMA_UNROLL + u
            src = x_ref.at[pl.ds(pl.multiple_of(r * chunks, chunks), chunks), :]
            for k in range(top_k):
                row = pl.multiple_of(idx_ref[0, 0, r * top_k + k], chunks)
                pltpu.make_async_copy(src, buf_ref.at[pl.ds(row, chunks), :], sem).start()
        return carry

    lax.fori_loop(0, tokens // DMA_UNROLL, issue, 0)
    for k in range(top_k):
        pltpu.make_async_copy(x_ref, buf_ref.at[pl.ds(0, tokens * chunks), :], sem).wait()


def _scatter_rows(x, dest, zero_blocks, *, tm, blk, chunks):
    nt = dest.shape[0]
    top_k = dest.shape[2] // tm
    n_blocks = zero_blocks.shape[0]
    grid_spec = pltpu.PrefetchScalarGridSpec(
        num_scalar_prefetch=1,
        grid=(nt,),
        in_specs=[pl.BlockSpec((1, 1, tm * top_k), lambda i, zb: (i, 0, 0), memory_space=pltpu.SMEM),
                  pl.BlockSpec((tm * chunks, LANES), lambda i, zb: (i, 0))],
        out_specs=pl.BlockSpec(memory_space=pl.ANY),
        scratch_shapes=[pltpu.VMEM((blk * chunks, LANES), x.dtype), pltpu.SemaphoreType.DMA,
                        pltpu.SemaphoreType.DMA],
    )
    return pl.pallas_call(
        functools.partial(_scatter_rows_kernel, top_k=top_k, chunks=chunks),
        grid_spec=grid_spec,
        out_shape=jax.ShapeDtypeStruct((n_blocks * blk * chunks, LANES), x.dtype),
        compiler_params=pltpu.CompilerParams(
            dimension_semantics=("arbitrary",), vmem_limit_bytes=VMEM_LIMIT),
        name="scatter_rows",
    )(zero_blocks, dest, x)


def _experts_kernel(be_ref, x_ref, wg_ref, wu_ref, wd_ref, o_ref, wgb_ref, wub_ref, wdb_ref, *, blk):
    i = pl.program_id(0)

    @pl.when((i == 0) | (be_ref[i] != be_ref[jnp.maximum(i - 1, 0)]))
    def _():
        wgb_ref[...] = wg_ref[0].astype(BF16)
        wub_ref[...] = wu_ref[0].astype(BF16)
        wdb_ref[...] = wd_ref[0].astype(BF16)

    d = wg_ref.shape[1]
    xb = _load_chunk_rows(x_ref, blk, d).astype(BF16)
    hdn = jax.nn.silu(_dot(xb, wgb_ref[...])) * _dot(xb, wub_ref[...])
    _store_chunk_rows(o_ref, _dot(hdn.astype(BF16), wdb_ref[...]))


def _experts(x_buf, blk_expert, w_gate, w_up, w_down, *, blk):
    _, D, de = w_gate.shape
    chunks = D // LANES
    n_blocks = x_buf.shape[0] // (blk * chunks)
    rows_spec = pl.BlockSpec((blk * chunks, LANES), lambda i, be: (i, 0))
    grid_spec = pltpu.PrefetchScalarGridSpec(
        num_scalar_prefetch=1,
        grid=(n_blocks,),
        in_specs=[rows_spec,
                  pl.BlockSpec((1, D, de), lambda i, be: (be[i], 0, 0)),
                  pl.BlockSpec((1, D, de), lambda i, be: (be[i], 0, 0)),
                  pl.BlockSpec((1, de, D), lambda i, be: (be[i], 0, 0))],
        out_specs=rows_spec,
        scratch_shapes=[pltpu.VMEM((D, de), BF16), pltpu.VMEM((D, de), BF16),
                        pltpu.VMEM((de, D), BF16)],
    )
    return pl.pallas_call(
        functools.partial(_experts_kernel, blk=blk),
        grid_spec=grid_spec,
        out_shape=jax.ShapeDtypeStruct(x_buf.shape, F32),
        compiler_params=pltpu.CompilerParams(
            dimension_semantics=("arbitrary",), vmem_limit_bytes=VMEM_LIMIT),
        name="experts",
    )(blk_expert, x_buf, w_gate, w_up, w_down)


def _combine_kernel(idx_ref, idx_next_ref, x_ref, g_ref, y_ref, o_ref, buf_ref, sem, *, top_k, chunks):
    i = pl.program_id(0)
    n = pl.num_programs(0)
    slot = lax.rem(i, 2)
    tokens, d = x_ref.shape

    def request(ids_ref, s):
        def issue(g, carry):
            for u in range(DMA_UNROLL):
                r = g * DMA_UNROLL + u
                for k in range(top_k):
                    row = pl.multiple_of(ids_ref[0, 0, r * top_k + k], chunks)
                    dst = buf_ref.at[s, k, pl.ds(pl.multiple_of(r * chunks, chunks), chunks), :]
                    pltpu.make_async_copy(y_ref.at[pl.ds(row, chunks), :], dst, sem.at[s]).start()
            return carry
        lax.fori_loop(0, tokens // DMA_UNROLL, issue, 0)

    @pl.when(i == 0)
    def _():
        request(idx_ref, 0)

    @pl.when(i + 1 < n)
    def _():
        request(idx_next_ref, 1 - slot)

    for k in range(top_k):
        pltpu.make_async_copy(y_ref.at[pl.ds(0, tokens * chunks), :], buf_ref.at[slot, k],
                              sem.at[slot]).wait()
    out = x_ref[...]
    for k in range(top_k):
        out = out + g_ref[:, k:k + 1] * _load_chunk_rows(buf_ref.at[slot, k], tokens, d)
    o_ref[...] = out


def _combine(x2, gates, y_buf, dest, *, tm):
    T, D = x2.shape
    chunks = D // LANES
    nt = dest.shape[0]
    top_k = dest.shape[2] // tm
    idx_spec = lambda step: pl.BlockSpec(
        (1, 1, tm * top_k), lambda i: (jnp.minimum(i + step, nt - 1), 0, 0), memory_space=pltpu.SMEM)
    return pl.pallas_call(
        functools.partial(_combine_kernel, top_k=top_k, chunks=chunks),
        grid=(nt,),
        in_specs=[idx_spec(0), idx_spec(1),
                  pl.BlockSpec((tm, D), lambda i: (i, 0)),
                  pl.BlockSpec((tm, LANES), lambda i: (i, 0)),
                  pl.BlockSpec(memory_space=pl.ANY)],
        out_specs=pl.BlockSpec((tm, D), lambda i: (i, 0)),
        out_shape=jax.ShapeDtypeStruct((T, D), F32),
        scratch_shapes=[pltpu.VMEM((2, top_k, tm * chunks, LANES), F32),
                        pltpu.SemaphoreType.DMA((2,))],
        compiler_params=pltpu.CompilerParams(
            dimension_semantics=("arbitrary",), vmem_limit_bytes=VMEM_LIMIT),
        name="combine",
    )(dest, dest, x2, gates, y_buf)


def _dispatch(experts, n_experts, blk):
    T, K = experts.shape
    A = T * K
    e_flat = experts.reshape(A)
    onehot = (e_flat[:, None] == jnp.arange(n_experts, dtype=jnp.int32)[None, :]).astype(jnp.int32)
    ranks = jnp.cumsum(onehot, axis=0)
    counts = ranks[-1]
    padded = (counts + blk - 1) // blk * blk
    pends = jnp.cumsum(padded)
    pstarts = pends - padded
    dest = jnp.sum(onehot * (ranks - 1 + pstarts[None, :]), axis=1)
    n_blocks = (A + n_experts * blk) // blk
    blk_start = jnp.arange(n_blocks, dtype=jnp.int32) * blk
    blk_expert = jnp.minimum(jnp.sum(blk_start[:, None] >= pends[None, :], axis=1), n_experts - 1)
    last_of_segment = jnp.any((blk_start[:, None] + blk == pends[None, :]) & (padded[None, :] > 0),
                              axis=1)
    zero_blocks = (last_of_segment | (blk_start >= pends[-1])).astype(jnp.int32)
    return dest.reshape(T, K), blk_expert.astype(jnp.int32), zero_blocks


def kernel(x, mem, norm_mix_g, w_in, b_forget, conv_w, conv_b, lru_wa, lru_ba, lru_wx, lru_bx,
           lru_a_param, fox_q_g, fox_k_g, lru_out_g, fox_out_g, w_out, norm_mem_x_g, norm_mem_g,
           mem_wq, mem_wkv, mem_q_g, mem_k_g, mem_wo, norm_ffn_g, router_group_w, router_group_b,
           router_expert_w, router_expert_b, exp_w_gate, exp_w_up, exp_w_down):
    B, S, D = x.shape
    depth = norm_mix_g.shape[0]
    lw = conv_w.shape[-1]
    n_heads = b_forget.shape[-1]
    dh = fox_q_g.shape[-1]
    mem_dh = mem_q_g.shape[-1]
    mem_heads = mem_wq.shape[-1] // mem_dh
    n_experts = router_expert_w.shape[-1]
    T = B * S
    tm = _pick_tile(S, 1024)
    tq = _pick_tile(S, 512)
    blk = 512
    scatter_tm = _pick_tile(T, 1024)

    for l in range(depth):
        u, gate, q, k, v = _in_proj(x, norm_mix_g[l], w_in[l], b_forget[l], fox_q_g[l], fox_k_g[l],
                                    lw=lw, n_heads=n_heads, dh=dh, tm=tm)
        y_lru = _rg_lru(u, gate, conv_w[l], conv_b[l], lru_wa[l], lru_ba[l], lru_wx[l], lru_bx[l],
                        lru_a_param[l], lru_out_g[l], ts=tm)
        y_fox = _fox_attn(q, k, v, n_heads=n_heads, dh=dh, tq=tq)
        k_mem, v_mem = _mem_kv(mem, norm_mem_g[l], mem_wkv[l], mem_k_g[l], n_heads=mem_heads,
                               dh=mem_dh)
        x2, xn, gates, experts = _out_mem(
            x, y_lru, y_fox, fox_out_g[l], w_out[l], norm_mem_x_g[l], mem_wq[l], mem_q_g[l], k_mem,
            v_mem, mem_wo[l], norm_ffn_g[l], router_group_w[l], router_group_b[l],
            router_expert_w[l], router_expert_b[l], n_heads=mem_heads, dh=mem_dh, tm=tm)

        dest, blk_expert, zero_blocks = _dispatch(
            experts.reshape(T, LANES)[:, :TOP_K], n_experts, blk)
        chunks = D // LANES
        tiled = lambda t: (dest * chunks).reshape(T // t, 1, t * TOP_K)
        x_buf = _scatter_rows(xn, tiled(scatter_tm), zero_blocks, tm=scatter_tm, blk=blk,
                              chunks=chunks)
        y_buf = _experts(x_buf, blk_expert, exp_w_gate[l], exp_w_up[l], exp_w_down[l], blk=blk)
        x = _combine(x2.reshape(T, D), gates.reshape(T, LANES), y_buf, tiled(tm), tm=tm)
        x = x.reshape(B, S, D)
    return x
```

```python
import functools

import jax
import jax.numpy as jnp
import numpy as np
from jax import lax
from jax.experimental import pallas as pl
from jax.experimental.pallas import tpu as pltpu

EPS = 1e-6
LRU_C = 8.0
CONV_WIDTH = 4
TOP_K = 2
LANES = 128
SUBLANES = 8
MXU_DIM = 256
HEAD_SLOT = LANES
VMEM_LIMIT = 56 * 1024 * 1024

F32 = jnp.float32
BF16 = jnp.bfloat16


def _dot(a, b):
    return jnp.dot(a, b, preferred_element_type=F32)


def _rms(x, g):
    return x * lax.rsqrt(jnp.mean(x * x, axis=-1, keepdims=True) + EPS) * g


def _softplus(x):
    return jnp.maximum(x, 0.0) + jnp.log1p(jnp.exp(-jnp.abs(x)))


def _pick_tile(n, target):
    t = min(n, target)
    while n % t:
        t //= 2
    return t


def _store_chunk_rows(ref, val):
    n, d = val.shape
    chunks = d // LANES
    for c in range(chunks):
        ref[pl.ds(c, n, stride=chunks), :] = val[:, c * LANES:(c + 1) * LANES]


def _load_chunk_rows(ref, n, d):
    chunks = d // LANES
    return jnp.concatenate([ref[pl.ds(c, n, stride=chunks), :] for c in range(chunks)], axis=1)


def _in_proj_kernel(x_ref, g_ref, w_ref, wf_ref, bf_ref, gq_ref, gk_ref, pe_ref, cv_ref,
                    cw_ref, cb_ref, wl_ref, ba_ref, bx_ref, ap_ref, og_ref,
                    ylru_ref, q_ref, k_ref, v_ref, carry_ref, tail_ref, h_ref, *, lw, n_heads, dh):
    @pl.when(pl.program_id(1) == 0)
    def _():
        carry_ref[...] = jnp.zeros_like(carry_ref)
        tail_ref[...] = jnp.zeros_like(tail_ref)
        h_ref[...] = jnp.zeros_like(h_ref)

    hs = n_heads * HEAD_SLOT
    x = x_ref[0]
    tm = x.shape[0]
    hb = _rms(x, g_ref[...]).astype(BF16)

    u = _dot(hb, w_ref[:, 0:lw])
    gate = _dot(hb, w_ref[:, lw:2 * lw])
    ylru_ref[0] = _rg_lru_tile(u, gate, cw_ref, cb_ref, wl_ref, ba_ref, bx_ref, ap_ref, og_ref,
                               tail_ref, h_ref).astype(ylru_ref.dtype)

    z = _dot(hb, wf_ref[...]) + bf_ref[...]
    lane = lax.broadcasted_iota(jnp.int32, z.shape, 1)
    row = lax.broadcasted_iota(jnp.int32, z.shape, 0)
    c = jnp.where(lane < n_heads, -_softplus(-z), 0.0)
    d = 1
    while d < tm:
        c = c + jnp.where(row >= d, pltpu.roll(c, d, 0), 0.0)
        d *= 2
    c = c + carry_ref[...]
    carry_ref[...] = c[tm - 1:tm, :]
    c1 = c.astype(BF16).astype(F32)
    r1 = c - c1
    c2 = r1.astype(BF16).astype(F32)
    c3 = (r1 - c2).astype(BF16).astype(F32)
    e = c1 + pltpu.roll(c2, n_heads, 1) + pltpu.roll(c3, 2 * n_heads, 1)
    e = jnp.where(lane == 3 * n_heads, 1.0, e).astype(BF16)
    ext = _dot(e, pe_ref[...])

    fw = n_heads * dh
    lane_t = lax.broadcasted_iota(jnp.int32, (tm, LANES), 1)
    own = (lane_t < dh, lane_t >= dh)

    def to_slots(y):
        parts = []
        for c in range(fw // LANES):
            col = y[:, c * LANES:(c + 1) * LANES]
            parts += [jnp.where(own[0], col, 0.0), jnp.where(own[1], col, 0.0)]
        return jnp.concatenate(parts, axis=-1)

    def head_norm(y, gain):
        parts = []
        for h in range(n_heads):
            blk = y[:, h * HEAD_SLOT:(h + 1) * HEAD_SLOT]
            ss = jnp.sum(blk * blk, axis=-1, keepdims=True) * (1.0 / dh)
            parts.append(blk * lax.rsqrt(ss + EPS))
        return jnp.concatenate(parts, axis=-1) * gain

    q = to_slots(_dot(hb, w_ref[:, 2 * lw:2 * lw + fw]))
    q_ref[0] = (head_norm(q, gq_ref[...]) + ext[:, 0:hs]).astype(BF16)
    k = to_slots(_dot(hb, w_ref[:, 2 * lw + fw:2 * lw + 2 * fw]))
    k_ref[0] = (head_norm(k, gk_ref[...]) + ext[:, hs:2 * hs]).astype(BF16)
    v = to_slots(_dot(hb, w_ref[:, 2 * lw + 2 * fw:2 * lw + 3 * fw]))
    v_ref[0] = (v + cv_ref[...]).astype(BF16)


def _in_proj(x, norm_g, w_in, b_forget, fox_q_g, fox_k_g, conv_w, conv_b, wa, ba, wx, bx, a_param,
             lru_out_g, *, lw, n_heads, dh, tm):
    B, S, D = x.shape
    fw = n_heads * dh
    hs = n_heads * HEAD_SLOT
    assert 2 * dh == HEAD_SLOT and fw % LANES == 0 and 3 * n_heads <= LANES
    n_main = 2 * lw + 3 * fw
    w_all = w_in[:, :n_main].astype(BF16)
    w_f = jnp.pad(w_in[:, n_main:], ((0, 0), (0, LANES - n_heads))).astype(BF16)
    b_f = jnp.pad(b_forget, (0, LANES - n_heads)).reshape(1, LANES)
    scale = dh ** -0.5

    def slot_gains(g):
        even = jnp.pad(g, (0, HEAD_SLOT - dh))
        odd = jnp.pad(g, (HEAD_SLOT - dh, 0))
        return jnp.tile(jnp.concatenate([even, odd]), n_heads // 2).reshape(1, hs)

    gq = slot_gains(fox_q_g * scale)
    gk = slot_gains(fox_k_g)
    ones_row = 3 * n_heads
    pe = np.zeros((LANES, 2 * hs), np.float32)
    cv = np.zeros((1, hs), np.float32)
    for h in range(n_heads):
        x0 = h * HEAD_SLOT + (dh if h % 2 == 0 else 0)
        for j in range(3):
            pe[j * n_heads + h, x0 + j] = 1.0
            pe[ones_row, x0 + 3 + j] = 1.0
            pe[ones_row, hs + x0 + j] = 1.0
            pe[j * n_heads + h, hs + x0 + 3 + j] = -1.0
        cv[0, x0] = 1.0
    pe = jnp.asarray(pe, BF16)

    w_lru = _lru_gate_weights(wa, wx, lw)
    vec = lambda a: a.reshape(1, lw)
    full = lambda shape: pl.BlockSpec(shape, lambda b, s: (0,) * len(shape))
    row = lambda width: pl.BlockSpec((1, tm, width), lambda b, s: (b, s, 0))
    return pl.pallas_call(
        functools.partial(_in_proj_kernel, lw=lw, n_heads=n_heads, dh=dh),
        grid=(B, S // tm),
        in_specs=[row(D), full((1, D)), full(w_all.shape), full(w_f.shape), full((1, LANES)),
                  full((1, hs)), full((1, hs)), full(pe.shape), full((1, hs)),
                  full((CONV_WIDTH, lw)), full((1, lw)), full(w_lru.shape), full((1, lw)),
                  full((1, lw)), full((1, lw)), full((1, lw))],
        out_specs=[row(lw), row(hs), row(hs), row(hs)],
        out_shape=[jax.ShapeDtypeStruct((B, S, lw), BF16),
                   jax.ShapeDtypeStruct((B, S, hs), BF16), jax.ShapeDtypeStruct((B, S, hs), BF16),
                   jax.ShapeDtypeStruct((B, S, hs), BF16)],
        scratch_shapes=[pltpu.VMEM((1, LANES), F32), pltpu.VMEM((SUBLANES, lw), F32),
                        pltpu.VMEM((1, lw), F32)],
        compiler_params=pltpu.CompilerParams(
            dimension_semantics=("parallel", "arbitrary"), vmem_limit_bytes=VMEM_LIMIT),
        name="in_proj",
    )(x, norm_g.reshape(1, D), w_all, w_f, b_f, gq, gk, pe, jnp.asarray(cv),
      conv_w, vec(conv_b), w_lru, vec(ba), vec(bx), vec(a_param), vec(lru_out_g))


def _rg_lru_tile(u, gate, cw_ref, cb_ref, w_ref, ba_ref, bx_ref, ap_ref, og_ref, tail_ref, h_ref):
    ts, lw = u.shape
    ext = jnp.concatenate([tail_ref[...], u], axis=0)
    tail_ref[...] = u[ts - SUBLANES:, :]
    xc = cb_ref[...] + u * cw_ref[CONV_WIDTH - 1:CONV_WIDTH, :]
    for back in range(1, CONV_WIDTH):
        tap = CONV_WIDTH - 1 - back
        xc = xc + pltpu.roll(ext, back, 0)[SUBLANES:, :] * cw_ref[tap:tap + 1, :]

    xb = xc.astype(BF16)
    r_parts, i_parts = [], []
    for j in range(lw // MXU_DIM):
        y = _dot(xb[:, j * MXU_DIM:(j + 1) * MXU_DIM], w_ref[j])
        r_parts.append(y[:, :MXU_DIM])
        i_parts.append(y[:, MXU_DIM:])
    r = jax.nn.sigmoid(jnp.concatenate(r_parts, axis=-1) + ba_ref[...])
    i = jax.nn.sigmoid(jnp.concatenate(i_parts, axis=-1) + bx_ref[...])
    log_a = (-LRU_C) * r * _softplus(-ap_ref[...])
    a = jnp.exp(log_a)
    mult = jnp.sqrt(jnp.maximum(-jnp.tanh(log_a) * (a * a + 1.0), 0.0))
    b = mult * (i * xc)

    groups = ts // SUBLANES
    a = a.reshape(groups, SUBLANES, lw)
    b = b.reshape(groups, SUBLANES, lw)
    row = lax.broadcasted_iota(jnp.int32, a.shape, 1)
    d = 1
    while d < SUBLANES:
        keep = row >= d
        a_prev = jnp.where(keep, pltpu.roll(a, d, 1), 1.0)
        b_prev = jnp.where(keep, pltpu.roll(b, d, 1), 0.0)
        b = a * b_prev + b
        a = a * a_prev
        d *= 2
    state = h_ref[...]
    parts = []
    for g in range(groups):
        hg = b[g] + a[g] * state
        state = hg[SUBLANES - 1:, :]
        parts.append(hg)
    h = jnp.concatenate(parts, axis=0)
    h_ref[...] = state

    y = h * jax.nn.gelu(gate)
    return _rms(y, og_ref[...])


def _lru_gate_weights(wa, wx, lw):
    _, bd, _ = wa.shape
    per = MXU_DIM // bd
    n_tiles = lw // MXU_DIM

    def tiles(w):
        w = w.reshape(n_tiles, per, bd, bd)
        eye = jnp.eye(per, dtype=w.dtype)
        return jnp.einsum('tpij,pq->tpiqj', w, eye).reshape(n_tiles, MXU_DIM, MXU_DIM)

    return jnp.concatenate([tiles(wa), tiles(wx)], axis=-1).astype(BF16)


OVERFLOW_GUARD = 1e30


def _fox_attn_kernel(q_ref, k_ref, v_ref, o_ref, vt_ref, m_ref, acc_ref, *, dh, tq):
    n_chunks = vt_ref.shape[0]
    for c in range(n_chunks):
        vt_ref[c] = v_ref[0, c * tq:(c + 1) * tq, :].astype(F32).T.astype(BF16)

    heads = [slice(hh * HEAD_SLOT, (hh + 1) * HEAD_SLOT) for hh in range(2)]
    lax.fori_loop(0, n_chunks, functools.partial(
        _fox_attn_tile, q_ref=q_ref, k_ref=k_ref, o_ref=o_ref, vt_ref=vt_ref, m_ref=m_ref,
        acc_ref=acc_ref, heads=heads, dh=dh, tq=tq), 0)


def _fox_attn_tile(qi, carry, *, q_ref, k_ref, o_ref, vt_ref, m_ref, acc_ref, heads, dh, tq):
    q_rows = pl.ds(pl.multiple_of(qi * tq, tq), tq)

    def scores(j, q, sl, masked):
        start = pl.multiple_of(j * tq, tq)
        st = lax.dot_general(k_ref[0, pl.ds(start, tq), sl], q, (((1,), (1,)), ((), ())),
                             preferred_element_type=F32)
        if masked:
            kpos = lax.broadcasted_iota(jnp.int32, st.shape, 0)
            qpos = lax.broadcasted_iota(jnp.int32, st.shape, 1)
            st = jnp.where(kpos <= qpos, st, -jnp.inf)
        return st

    for hh, sl in enumerate(heads):
        st = scores(qi, q_ref[0, q_rows, sl], sl, True)
        m = jnp.max(st, axis=0, keepdims=True)
        m_ref[hh] = m
        acc_ref[hh] = _dot(vt_ref[qi, sl, :], jnp.exp(st - m).astype(BF16))

    def fast_chunk(j):
        for hh, sl in enumerate(heads):
            pt = jnp.exp(scores(j, q_ref[0, q_rows, sl], sl, False) - m_ref[hh]).astype(BF16)
            acc_ref[hh] += _dot(vt_ref[j, sl, :], pt)

    def fast_pair(i, carry):
        start = pl.multiple_of(2 * i * tq, 2 * tq)
        for hh, sl in enumerate(heads):
            st = lax.dot_general(k_ref[0, pl.ds(start, 2 * tq), sl], q_ref[0, q_rows, sl],
                                 (((1,), (1,)), ((), ())), preferred_element_type=F32)
            pt = jnp.exp(st - m_ref[hh]).astype(BF16)
            vt = jnp.concatenate([vt_ref[2 * i, sl, :], vt_ref[2 * i + 1, sl, :]], axis=1)
            acc_ref[hh] += _dot(vt, pt)
        return carry

    lax.fori_loop(0, qi // 2, fast_pair, 0)

    @pl.when(qi % 2 == 1)
    def _():
        fast_chunk(qi - 1)

    def softmax_sums():
        return jnp.concatenate([acc_ref[0][dh:dh + 1, :], acc_ref[1][0:1, :]], axis=0)

    def normalised():
        sums = softmax_sums()
        return jnp.concatenate([acc_ref[0][0:dh, :] / sums[0:1, :],
                                acc_ref[1][dh:2 * dh, :] / sums[1:2, :]], axis=0)

    out = normalised()
    o_ref[0, q_rows, :] = out.T.astype(o_ref.dtype)
    sums = softmax_sums()
    flag = lambda x: jnp.max(jnp.where(jnp.abs(x) < OVERFLOW_GUARD, 0.0, 1.0),
                             axis=0, keepdims=True)
    bad = jnp.max(jnp.maximum(flag(out), flag(sums)), axis=1, keepdims=True)

    @pl.when(bad[0, 0] > 0.0)
    def _():
        m_ref[...] = jnp.full_like(m_ref, -jnp.inf)
        acc_ref[...] = jnp.zeros_like(acc_ref)

        def chunk(j, masked):
            for hh, sl in enumerate(heads):
                st = scores(j, q_ref[0, q_rows, sl], sl, masked)
                m_old = m_ref[hh]
                m_new = jnp.maximum(m_old, jnp.max(st, axis=0, keepdims=True))
                pt = jnp.exp(st - m_new).astype(BF16)
                acc_ref[hh] = jnp.exp(m_old - m_new) * acc_ref[hh] + _dot(vt_ref[j, sl, :], pt)
                m_ref[hh] = m_new

        def body(j, carry):
            chunk(j, False)
            return carry

        lax.fori_loop(0, qi, body, 0)
        chunk(qi, True)
        o_ref[0, q_rows, :] = normalised().T.astype(o_ref.dtype)

    return carry


def _fox_attn(q, k, v, *, n_heads, dh, tq):
    B, S, _ = q.shape
    nq = S // tq
    pair = 2 * HEAD_SLOT
    qkv_spec = pl.BlockSpec((1, S, pair), lambda b, h: (b, 0, h))
    return pl.pallas_call(
        functools.partial(_fox_attn_kernel, dh=dh, tq=tq),
        grid=(B, n_heads // 2),
        in_specs=[qkv_spec, qkv_spec, qkv_spec],
        out_specs=pl.BlockSpec((1, S, 2 * dh), lambda b, h: (b, 0, h)),
        out_shape=jax.ShapeDtypeStruct((B, S, n_heads * dh), F32),
        scratch_shapes=[pltpu.VMEM((nq, pair, tq), BF16), pltpu.VMEM((2, 1, tq), F32),
                        pltpu.VMEM((2, HEAD_SLOT, tq), F32)],
        compiler_params=pltpu.CompilerParams(
            dimension_semantics=("parallel", "parallel"), vmem_limit_bytes=VMEM_LIMIT),
        name="fox_attn",
    )(q, k, v)


def _mem_kv_kernel(m_ref, g_ref, w_ref, kg_ref, k_ref, v_ref, *, n_heads, dh):
    mw = n_heads * dh
    mb = _rms(m_ref[0], g_ref[...]).astype(BF16)
    kv = _dot(mb, w_ref[...])
    parts = []
    for h in range(n_heads):
        parts.append(_rms(kv[:, h * dh:(h + 1) * dh], kg_ref[...]))
    k_ref[0] = jnp.concatenate(parts, axis=-1).astype(BF16)
    v_ref[0] = kv[:, mw:].astype(BF16)


def _mem_kv(mem, norm_g, wkv, k_g, *, n_heads, dh):
    B, M, D = mem.shape
    mw = n_heads * dh
    full = lambda shape: pl.BlockSpec(shape, lambda b: (0,) * len(shape))
    out = pl.BlockSpec((1, M, mw), lambda b: (b, 0, 0))
    return pl.pallas_call(
        functools.partial(_mem_kv_kernel, n_heads=n_heads, dh=dh),
        grid=(B,),
        in_specs=[pl.BlockSpec((1, M, D), lambda b: (b, 0, 0)), full((1, D)), full((D, 2 * mw)),
                  full((1, dh))],
        out_specs=[out, out],
        out_shape=[jax.ShapeDtypeStruct((B, M, mw), BF16)] * 2,
        compiler_params=pltpu.CompilerParams(
            dimension_semantics=("parallel",), vmem_limit_bytes=VMEM_LIMIT),
        name="mem_kv",
    )(mem, norm_g.reshape(1, D), wkv.astype(BF16), k_g.reshape(1, dh))


def _out_mem_kernel(x_ref, yl_ref, yf_ref, fg_ref, wol_ref, wof_ref, gx_ref, wq_ref, qg_ref,
                    km_ref, vm_ref, wo_ref, gf_ref, wr_ref, br_ref,
                    x2_ref, xn_ref, gates_ref, experts_ref, *, n_heads, dh, n_groups, per_group):
    yf = _rms(yf_ref[0], fg_ref[...]).astype(BF16)
    x1 = x_ref[0] + _dot(yl_ref[0], wol_ref[...]) + _dot(yf, wof_ref[...])

    q = _dot(_rms(x1, gx_ref[...]).astype(BF16), wq_ref[...])
    outs = []
    for h in range(n_heads):
        sl = slice(h * dh, (h + 1) * dh)
        qh = _rms(q[:, sl], qg_ref[...]).astype(BF16)
        s = lax.dot_general(qh, km_ref[0, :, sl], (((1,), (1,)), ((), ())),
                            preferred_element_type=F32)
        p = jnp.exp(s - jnp.max(s, axis=-1, keepdims=True))
        p = p / jnp.sum(p, axis=-1, keepdims=True)
        outs.append(_dot(p.astype(BF16), vm_ref[0, :, sl]))
    x2 = x1 + _dot(jnp.concatenate(outs, axis=-1).astype(BF16), wo_ref[...])
    x2_ref[0] = x2

    xn = _rms(x2, gf_ref[...])
    _store_chunk_rows(xn_ref, xn)
    logits = _dot(xn.astype(BF16), wr_ref[...]) + br_ref[...]
    lane = lax.broadcasted_iota(jnp.int32, logits.shape, 1).astype(F32)
    neg = -jnp.inf

    def top(vals):
        mx = jnp.max(vals, axis=-1, keepdims=True)
        idx = jnp.min(jnp.where(vals == mx, lane, float(LANES)), axis=-1, keepdims=True)
        return mx, idx

    gl = jnp.where(lane < n_groups, logits, neg)
    g_max, g_idx = top(gl)
    g_w = 1.0 / jnp.sum(jnp.exp(gl - g_max), axis=-1, keepdims=True)
    lo = n_groups + per_group * g_idx
    el = jnp.where((lane >= lo) & (lane < lo + per_group), logits, neg)
    e1, i1 = top(el)
    e2, i2 = top(jnp.where(lane == i1, neg, el))
    t = jnp.exp(e2 - e1)
    w1 = g_w / (1.0 + t)
    w2 = g_w * t / (1.0 + t)
    gates_ref[0] = jnp.where(lane == 0, w1, jnp.where(lane == 1, w2, 0.0))
    experts_ref[0] = jnp.where(lane == 0, i1 - n_groups,
                               jnp.where(lane == 1, i2 - n_groups, 0.0)).astype(jnp.int32)


def _out_mem(x, y_lru, y_fox, fox_out_g, w_out, norm_mem_x_g, mem_wq, mem_q_g, k_mem, v_mem, mem_wo,
             norm_ffn_g, router_group_w, router_group_b, router_expert_w, router_expert_b, *,
             n_heads, dh, tm):
    B, S, D = x.shape
    lw = y_lru.shape[-1]
    fw = y_fox.shape[-1]
    M = k_mem.shape[1]
    mw = n_heads * dh
    n_groups = router_group_w.shape[-1]
    n_experts = router_expert_w.shape[-1]
    w_r = jnp.concatenate([router_group_w, router_expert_w], axis=1)
    w_r = jnp.pad(w_r, ((0, 0), (0, LANES - w_r.shape[1]))).astype(BF16)
    b_r = jnp.concatenate([router_group_b, router_expert_b])
    b_r = jnp.pad(b_r, (0, LANES - b_r.shape[0])).reshape(1, LANES)
    q_gain = (mem_q_g * dh ** -0.5).reshape(1, dh)

    full = lambda shape: pl.BlockSpec(shape, lambda b, s: (0,) * len(shape))
    row = lambda width: pl.BlockSpec((1, tm, width), lambda b, s: (b, s, 0))
    mem_spec = pl.BlockSpec((1, M, mw), lambda b, s: (b, 0, 0))
    return pl.pallas_call(
        functools.partial(_out_mem_kernel, n_heads=n_heads, dh=dh, n_groups=n_groups,
                          per_group=n_experts // n_groups),
        grid=(B, S // tm),
        in_specs=[row(D), row(lw), row(fw), full((1, fw)), full((lw, D)), full((fw, D)),
                  full((1, D)), full((D, mw)), full((1, dh)), mem_spec, mem_spec, full((mw, D)),
                  full((1, D)), full((D, LANES)), full((1, LANES))],
        out_specs=[row(D), pl.BlockSpec((tm * D // LANES, LANES), lambda b, s: (b * (S // tm) + s, 0)),
                   row(LANES), row(LANES)],
        out_shape=[jax.ShapeDtypeStruct((B, S, D), F32),
                   jax.ShapeDtypeStruct((B * S * D // LANES, LANES), F32),
                   jax.ShapeDtypeStruct((B, S, LANES), F32),
                   jax.ShapeDtypeStruct((B, S, LANES), jnp.int32)],
        compiler_params=pltpu.CompilerParams(
            dimension_semantics=("parallel", "parallel"), vmem_limit_bytes=VMEM_LIMIT),
        name="out_mem",
    )(x, y_lru, y_fox, fox_out_g.reshape(1, fw), w_out[:lw].astype(BF16), w_out[lw:].astype(BF16),
      norm_mem_x_g.reshape(1, D), mem_wq.astype(BF16), q_gain, k_mem, v_mem, mem_wo.astype(BF16),
      norm_ffn_g.reshape(1, D), w_r, b_r)


DMA_UNROLL = 8


def _scatter_rows_kernel(zb_ref, idx_ref, x_ref, buf_ref, zero_ref, sem, zsem, *, top_k, chunks):
    tokens = x_ref.shape[0] // chunks

    @pl.when(pl.program_id(0) == 0)
    def _():
        rows = zero_ref.shape[0]
        zero_ref[...] = jnp.zeros_like(zero_ref)

        def copy(j):
            start = pl.multiple_of(j * rows, rows)
            return pltpu.make_async_copy(zero_ref, buf_ref.at[pl.ds(start, rows), :], zsem)

        def issue_zero(j, carry):
            @pl.when(zb_ref[j] == 1)
            def _():
                copy(j).start()
            return carry

        def drain_zero(j, carry):
            @pl.when(zb_ref[j] == 1)
            def _():
                copy(j).wait()
            return carry

        lax.fori_loop(0, zb_ref.shape[0], issue_zero, 0)
        lax.fori_loop(0, zb_ref.shape[0], drain_zero, 0)

    def issue(g, carry):
        for u in range(DMA_UNROLL):
            r = g * DMA_UNROLL + u
            src = x_ref.at[pl.ds(pl.multiple_of(r * chunks, chunks), chunks), :]
            for k in range(top_k):
                row = pl.multiple_of(idx_ref[0, 0, r * top_k + k], chunks)
                pltpu.make_async_copy(src, buf_ref.at[pl.ds(row, chunks), :], sem).start()
        return carry

    lax.fori_loop(0, tokens // DMA_UNROLL, issue, 0)
    for k in range(top_k):
        pltpu.make_async_copy(x_ref, buf_ref.at[pl.ds(0, tokens * chunks), :], sem).wait()


def _scatter_rows(x, dest, zero_blocks, *, tm, blk, chunks):
    nt = dest.shape[0]
    top_k = dest.shape[2] // tm
    n_blocks = zero_blocks.shape[0]
    grid_spec = pltpu.PrefetchScalarGridSpec(
        num_scalar_prefetch=1,
        grid=(nt,),
        in_specs=[pl.BlockSpec((1, 1, tm * top_k), lambda i, zb: (i, 0, 0), memory_space=pltpu.SMEM),
                  pl.BlockSpec((tm * chunks, LANES), lambda i, zb: (i, 0))],
        out_specs=pl.BlockSpec(memory_space=pl.ANY),
        scratch_shapes=[pltpu.VMEM((blk * chunks, LANES), x.dtype), pltpu.SemaphoreType.DMA,
                        pltpu.SemaphoreType.DMA],
    )
    return pl.pallas_call(
        functools.partial(_scatter_rows_kernel, top_k=top_k, chunks=chunks),
        grid_spec=grid_spec,
        out_shape=jax.ShapeDtypeStruct((n_blocks * blk * chunks, LANES), x.dtype),
        compiler_params=pltpu.CompilerParams(
            dimension_semantics=("arbitrary",), vmem_limit_bytes=VMEM_LIMIT),
        name="scatter_rows",
    )(zero_blocks, dest, x)


def _experts_kernel(be_ref, x_ref, wg_ref, wu_ref, wd_ref, o_ref, wgb_ref, wub_ref, wdb_ref, *, blk):
    i = pl.program_id(0)

    @pl.when((i == 0) | (be_ref[i] != be_ref[jnp.maximum(i - 1, 0)]))
    def _():
        wgb_ref[...] = wg_ref[0].astype(BF16)
        wub_ref[...] = wu_ref[0].astype(BF16)
        wdb_ref[...] = wd_ref[0].astype(BF16)

    d = wg_ref.shape[1]
    xb = _load_chunk_rows(x_ref, blk, d).astype(BF16)
    hdn = jax.nn.silu(_dot(xb, wgb_ref[...])) * _dot(xb, wub_ref[...])
    _store_chunk_rows(o_ref, _dot(hdn.astype(BF16), wdb_ref[...]))


def _experts(x_buf, blk_expert, w_gate, w_up, w_down, *, blk):
    _, D, de = w_gate.shape
    chunks = D // LANES
    n_blocks = x_buf.shape[0] // (blk * chunks)
    rows_spec = pl.BlockSpec((blk * chunks, LANES), lambda i, be: (i, 0))
    grid_spec = pltpu.PrefetchScalarGridSpec(
        num_scalar_prefetch=1,
        grid=(n_blocks,),
        in_specs=[rows_spec,
                  pl.BlockSpec((1, D, de), lambda i, be: (be[i], 0, 0)),
                  pl.BlockSpec((1, D, de), lambda i, be: (be[i], 0, 0)),
                  pl.BlockSpec((1, de, D), lambda i, be: (be[i], 0, 0))],
        out_specs=rows_spec,
        scratch_shapes=[pltpu.VMEM((D, de), BF16), pltpu.VMEM((D, de), BF16),
                        pltpu.VMEM((de, D), BF16)],
    )
    return pl.pallas_call(
        functools.partial(_experts_kernel, blk=blk),
        grid_spec=grid_spec,
        out_shape=jax.ShapeDtypeStruct(x_buf.shape, F32),
        compiler_params=pltpu.CompilerParams(
            dimension_semantics=("arbitrary",), vmem_limit_bytes=VMEM_LIMIT),
        name="experts",
    )(blk_expert, x_buf, w_gate, w_up, w_down)


def _combine_kernel(idx_ref, idx_next_ref, x_ref, g_ref, y_ref, o_ref, buf_ref, sem, *, top_k, chunks):
    i = pl.program_id(0)
    n = pl.num_programs(0)
    slot = lax.rem(i, 2)
    tokens, d = x_ref.shape

    def request(ids_ref, s):
        def issue(g, carry):
            for u in range(DMA_UNROLL):
                r = g * DMA_UNROLL + u
                for k in range(top_k):
                    row = pl.multiple_of(ids_ref[0, 0, r * top_k + k], chunks)
                    dst = buf_ref.at[s, k, pl.ds(pl.multiple_of(r * chunks, chunks), chunks), :]
                    pltpu.make_async_copy(y_ref.at[pl.ds(row, chunks), :], dst, sem.at[s]).start()
            return carry
        lax.fori_loop(0, tokens // DMA_UNROLL, issue, 0)

    @pl.when(i == 0)
    def _():
        request(idx_ref, 0)

    @pl.when(i + 1 < n)
    def _():
        request(idx_next_ref, 1 - slot)

    for k in range(top_k):
        pltpu.make_async_copy(y_ref.at[pl.ds(0, tokens * chunks), :], buf_ref.at[slot, k],
                              sem.at[slot]).wait()
    out = x_ref[...]
    for k in range(top_k):
        out = out + g_ref[:, k:k + 1] * _load_chunk_rows(buf_ref.at[slot, k], tokens, d)
    o_ref[...] = out


def _combine(x2, gates, y_buf, dest, *, tm):
    T, D = x2.shape
    chunks = D // LANES
    nt = dest.shape[0]
    top_k = dest.shape[2] // tm
    idx_spec = lambda step: pl.BlockSpec(
        (1, 1, tm * top_k), lambda i: (jnp.minimum(i + step, nt - 1), 0, 0), memory_space=pltpu.SMEM)
    return pl.pallas_call(
        functools.partial(_combine_kernel, top_k=top_k, chunks=chunks),
        grid=(nt,),
        in_specs=[idx_spec(0), idx_spec(1),
                  pl.BlockSpec((tm, D), lambda i: (i, 0)),
                  pl.BlockSpec((tm, LANES), lambda i: (i, 0)),
                  pl.BlockSpec(memory_space=pl.ANY)],
        out_specs=pl.BlockSpec((tm, D), lambda i: (i, 0)),
        out_shape=jax.ShapeDtypeStruct((T, D), F32),
        scratch_shapes=[pltpu.VMEM((2, top_k, tm * chunks, LANES), F32),
                        pltpu.SemaphoreType.DMA((2,))],
        compiler_params=pltpu.CompilerParams(
            dimension_semantics=("arbitrary",), vmem_limit_bytes=VMEM_LIMIT),
        name="combine",
    )(dest, dest, x2, gates, y_buf)


def _dispatch(experts, n_experts, blk):
    T, K = experts.shape
    A = T * K
    e_flat = experts.reshape(A)
    onehot = (e_flat[:, None] == jnp.arange(n_experts, dtype=jnp.int32)[None, :]).astype(jnp.int32)
    ranks = jnp.cumsum(onehot, axis=0)
    counts = ranks[-1]
    padded = (counts + blk - 1) // blk * blk
    pends = jnp.cumsum(padded)
    pstarts = pends - padded
    dest = jnp.sum(onehot * (ranks - 1 + pstarts[None, :]), axis=1)
    n_blocks = (A + n_experts * blk) // blk
    blk_start = jnp.arange(n_blocks, dtype=jnp.int32) * blk
    blk_expert = jnp.minimum(jnp.sum(blk_start[:, None] >= pends[None, :], axis=1), n_experts - 1)
    last_of_segment = jnp.any((blk_start[:, None] + blk == pends[None, :]) & (padded[None, :] > 0),
                              axis=1)
    zero_blocks = (last_of_segment | (blk_start >= pends[-1])).astype(jnp.int32)
    return dest.reshape(T, K), blk_expert.astype(jnp.int32), zero_blocks


def kernel(x, mem, norm_mix_g, w_in, b_forget, conv_w, conv_b, lru_wa, lru_ba, lru_wx, lru_bx,
           lru_a_param, fox_q_g, fox_k_g, lru_out_g, fox_out_g, w_out, norm_mem_x_g, norm_mem_g,
           mem_wq, mem_wkv, mem_q_g, mem_k_g, mem_wo, norm_ffn_g, router_group_w, router_group_b,
           router_expert_w, router_expert_b, exp_w_gate, exp_w_up, exp_w_down):
    B, S, D = x.shape
    depth = norm_mix_g.shape[0]
    lw = conv_w.shape[-1]
    n_heads = b_forget.shape[-1]
    dh = fox_q_g.shape[-1]
    mem_dh = mem_q_g.shape[-1]
    mem_heads = mem_wq.shape[-1] // mem_dh
    n_experts = router_expert_w.shape[-1]
    T = B * S
    tm = _pick_tile(S, 1024)
    tq = _pick_tile(S, 512)
    blk = 512
    scatter_tm = _pick_tile(T, 1024)
    combine_tm = _pick_tile(T, 512)

    for l in range(depth):
        y_lru, q, k, v = _in_proj(
            x, norm_mix_g[l], w_in[l], b_forget[l], fox_q_g[l], fox_k_g[l], conv_w[l], conv_b[l],
            lru_wa[l], lru_ba[l], lru_wx[l], lru_bx[l], lru_a_param[l], lru_out_g[l],
            lw=lw, n_heads=n_heads, dh=dh, tm=tm)
        y_fox = _fox_attn(q, k, v, n_heads=n_heads, dh=dh, tq=tq)
        k_mem, v_mem = _mem_kv(mem, norm_mem_g[l], mem_wkv[l], mem_k_g[l], n_heads=mem_heads,
                               dh=mem_dh)
        x2, xn, gates, experts = _out_mem(
            x, y_lru, y_fox, fox_out_g[l], w_out[l], norm_mem_x_g[l], mem_wq[l], mem_q_g[l], k_mem,
            v_mem, mem_wo[l], norm_ffn_g[l], router_group_w[l], router_group_b[l],
            router_expert_w[l], router_expert_b[l], n_heads=mem_heads, dh=mem_dh, tm=tm)

        dest, blk_expert, zero_blocks = _dispatch(
            experts.reshape(T, LANES)[:, :TOP_K], n_experts, blk)
        chunks = D // LANES
        tiled = lambda t: (dest * chunks).reshape(T // t, 1, t * TOP_K)
        x_buf = _scatter_rows(xn, tiled(scatter_tm), zero_blocks, tm=scatter_tm, blk=blk,
                              chunks=chunks)
        y_buf = _experts(x_buf, blk_expert, exp_w_gate[l], exp_w_up[l], exp_w_down[l], blk=blk)
        x = _combine(x2.reshape(T, D), gates.reshape(T, LANES), y_buf, tiled(combine_tm),
                     tm=combine_tm)
        x = x.reshape(B, S, D)
    return x
```

```python
import functools

import jax
import jax.numpy as jnp
import numpy as np
from jax import lax
from jax.experimental import pallas as pl
from jax.experimental.pallas import tpu as pltpu

EPS = 1e-6
LRU_C = 8.0
CONV_WIDTH = 4
TOP_K = 2
LANES = 128
SUBLANES = 8
MXU_DIM = 256
HEAD_SLOT = LANES
VMEM_LIMIT = 56 * 1024 * 1024

F32 = jnp.float32
BF16 = jnp.bfloat16
MIN_NORMAL = float(np.finfo(np.float32).tiny)


def _dot(a, b):
    return jnp.dot(a, b, preferred_element_type=F32)


def _rms(x, g):
    return x * lax.rsqrt(jnp.mean(x * x, axis=-1, keepdims=True) + EPS) * g


def _softplus(x):
    return jnp.maximum(x, 0.0) + jnp.log1p(jnp.exp(-jnp.abs(x)))


def _pick_tile(n, target):
    t = min(n, target)
    while n % t:
        t //= 2
    return t


def _store_chunk_rows(ref, val):
    n, d = val.shape
    chunks = d // LANES
    for c in range(chunks):
        ref[pl.ds(c, n, stride=chunks), :] = val[:, c * LANES:(c + 1) * LANES]


def _load_chunk_rows(ref, n, d):
    chunks = d // LANES
    return jnp.concatenate([ref[pl.ds(c, n, stride=chunks), :] for c in range(chunks)], axis=1)


def _in_proj_kernel(x_ref, g_ref, w_ref, wf_ref, bf_ref, gq_ref, gk_ref, pe_ref, cv_ref,
                    cw_ref, cb_ref, wl_ref, ba_ref, bx_ref, ap_ref, og_ref,
                    ylru_ref, q_ref, k_ref, v_ref, carry_ref, tail_ref, h_ref, *, lw, n_heads, dh):
    @pl.when(pl.program_id(1) == 0)
    def _():
        carry_ref[...] = jnp.zeros_like(carry_ref)
        tail_ref[...] = jnp.zeros_like(tail_ref)
        h_ref[...] = jnp.zeros_like(h_ref)

    hs = n_heads * HEAD_SLOT
    x = x_ref[0]
    tm = x.shape[0]
    hb = _rms(x, g_ref[...]).astype(BF16)

    u = _dot(hb, w_ref[:, 0:lw])
    gate = _dot(hb, w_ref[:, lw:2 * lw])
    ylru_ref[0] = _rg_lru_tile(u, gate, cw_ref, cb_ref, wl_ref, ba_ref, bx_ref, ap_ref, og_ref,
                               tail_ref, h_ref).astype(ylru_ref.dtype)

    z = _dot(hb, wf_ref[...]) + bf_ref[...]
    lane = lax.broadcasted_iota(jnp.int32, z.shape, 1)
    row = lax.broadcasted_iota(jnp.int32, z.shape, 0)
    c = jnp.where(lane < n_heads, -_softplus(-z), 0.0)
    d = 1
    while d < tm:
        c = c + jnp.where(row >= d, pltpu.roll(c, d, 0), 0.0)
        d *= 2
    c = c + carry_ref[...]
    carry_ref[...] = c[tm - 1:tm, :]
    c1 = c.astype(BF16).astype(F32)
    r1 = c - c1
    c2 = r1.astype(BF16).astype(F32)
    c3 = (r1 - c2).astype(BF16).astype(F32)
    e = c1 + pltpu.roll(c2, n_heads, 1) + pltpu.roll(c3, 2 * n_heads, 1)
    e = jnp.where(lane == 3 * n_heads, 1.0, e).astype(BF16)
    ext = _dot(e, pe_ref[...])

    fw = n_heads * dh
    lane_t = lax.broadcasted_iota(jnp.int32, (tm, LANES), 1)
    own = (lane_t < dh, lane_t >= dh)

    def to_slots(y):
        parts = []
        for c in range(fw // LANES):
            col = y[:, c * LANES:(c + 1) * LANES]
            parts += [jnp.where(own[0], col, 0.0), jnp.where(own[1], col, 0.0)]
        return jnp.concatenate(parts, axis=-1)

    def head_norm(y, gain):
        parts = []
        for h in range(n_heads):
            blk = y[:, h * HEAD_SLOT:(h + 1) * HEAD_SLOT]
            ss = jnp.sum(blk * blk, axis=-1, keepdims=True) * (1.0 / dh)
            parts.append(blk * lax.rsqrt(ss + EPS))
        return jnp.concatenate(parts, axis=-1) * gain

    q = to_slots(_dot(hb, w_ref[:, 2 * lw:2 * lw + fw]))
    q_ref[0] = (head_norm(q, gq_ref[...]) + ext[:, 0:hs]).astype(BF16)
    k = to_slots(_dot(hb, w_ref[:, 2 * lw + fw:2 * lw + 2 * fw]))
    k_ref[0] = (head_norm(k, gk_ref[...]) + ext[:, hs:2 * hs]).astype(BF16)
    v = to_slots(_dot(hb, w_ref[:, 2 * lw + 2 * fw:2 * lw + 3 * fw]))
    v_ref[0] = (v + cv_ref[...]).astype(BF16)


def _in_proj(x, norm_g, w_in, b_forget, fox_q_g, fox_k_g, conv_w, conv_b, wa, ba, wx, bx, a_param,
             lru_out_g, *, lw, n_heads, dh, tm):
    B, S, D = x.shape
    fw = n_heads * dh
    hs = n_heads * HEAD_SLOT
    assert 2 * dh == HEAD_SLOT and fw % LANES == 0 and 3 * n_heads <= LANES
    n_main = 2 * lw + 3 * fw
    w_all = w_in[:, :n_main].astype(BF16)
    w_f = jnp.pad(w_in[:, n_main:], ((0, 0), (0, LANES - n_heads))).astype(BF16)
    b_f = jnp.pad(b_forget, (0, LANES - n_heads)).reshape(1, LANES)
    scale = dh ** -0.5

    def slot_gains(g):
        even = jnp.pad(g, (0, HEAD_SLOT - dh))
        odd = jnp.pad(g, (HEAD_SLOT - dh, 0))
        return jnp.tile(jnp.concatenate([even, odd]), n_heads // 2).reshape(1, hs)

    gq = slot_gains(fox_q_g * scale)
    gk = slot_gains(fox_k_g)
    ones_row = 3 * n_heads
    pe = np.zeros((LANES, 2 * hs), np.float32)
    cv = np.zeros((1, hs), np.float32)
    for h in range(n_heads):
        x0 = h * HEAD_SLOT + (dh if h % 2 == 0 else 0)
        for j in range(3):
            pe[j * n_heads + h, x0 + j] = 1.0
            pe[ones_row, x0 + 3 + j] = 1.0
            pe[ones_row, hs + x0 + j] = 1.0
            pe[j * n_heads + h, hs + x0 + 3 + j] = -1.0
        cv[0, x0] = 1.0
    pe = jnp.asarray(pe, BF16)

    w_lru = _lru_gate_weights(wa, wx, lw)
    vec = lambda a: a.reshape(1, lw)
    full = lambda shape: pl.BlockSpec(shape, lambda b, s: (0,) * len(shape))
    row = lambda width: pl.BlockSpec((1, tm, width), lambda b, s: (b, s, 0))
    return pl.pallas_call(
        functools.partial(_in_proj_kernel, lw=lw, n_heads=n_heads, dh=dh),
        grid=(B, S // tm),
        in_specs=[row(D), full((1, D)), full(w_all.shape), full(w_f.shape), full((1, LANES)),
                  full((1, hs)), full((1, hs)), full(pe.shape), full((1, hs)),
                  full((CONV_WIDTH, lw)), full((1, lw)), full(w_lru.shape), full((1, lw)),
                  full((1, lw)), full((1, lw)), full((1, lw))],
        out_specs=[row(lw), row(hs), row(hs), row(hs)],
        out_shape=[jax.ShapeDtypeStruct((B, S, lw), BF16),
                   jax.ShapeDtypeStruct((B, S, hs), BF16), jax.ShapeDtypeStruct((B, S, hs), BF16),
                   jax.ShapeDtypeStruct((B, S, hs), BF16)],
        scratch_shapes=[pltpu.VMEM((1, LANES), F32), pltpu.VMEM((SUBLANES, lw), F32),
                        pltpu.VMEM((1, lw), F32)],
        compiler_params=pltpu.CompilerParams(
            dimension_semantics=("parallel", "arbitrary"), vmem_limit_bytes=VMEM_LIMIT),
        name="in_proj",
    )(x, norm_g.reshape(1, D), w_all, w_f, b_f, gq, gk, pe, jnp.asarray(cv),
      conv_w, vec(conv_b), w_lru, vec(ba), vec(bx), vec(a_param), vec(lru_out_g))


def _rg_lru_tile(u, gate, cw_ref, cb_ref, w_ref, ba_ref, bx_ref, ap_ref, og_ref, tail_ref, h_ref):
    ts, lw = u.shape
    ext = jnp.concatenate([tail_ref[...], u], axis=0)
    tail_ref[...] = u[ts - SUBLANES:, :]
    xc = cb_ref[...] + u * cw_ref[CONV_WIDTH - 1:CONV_WIDTH, :]
    for back in range(1, CONV_WIDTH):
        tap = CONV_WIDTH - 1 - back
        xc = xc + pltpu.roll(ext, back, 0)[SUBLANES:, :] * cw_ref[tap:tap + 1, :]

    xb = xc.astype(BF16)
    r_parts, i_parts = [], []
    for j in range(lw // MXU_DIM):
        y = _dot(xb[:, j * MXU_DIM:(j + 1) * MXU_DIM], w_ref[j])
        r_parts.append(y[:, :MXU_DIM])
        i_parts.append(y[:, MXU_DIM:])
    r = jax.nn.sigmoid(jnp.concatenate(r_parts, axis=-1) + ba_ref[...])
    i = jax.nn.sigmoid(jnp.concatenate(i_parts, axis=-1) + bx_ref[...])
    log_a = (-LRU_C) * r * _softplus(-ap_ref[...])
    a = jnp.exp(log_a)
    gap = jnp.maximum(-jnp.tanh(log_a) * (a * a + 1.0), 0.0)
    mult = gap * lax.rsqrt(jnp.maximum(gap, MIN_NORMAL))
    b = mult * (i * xc)

    groups = ts // SUBLANES
    a = a.reshape(groups, SUBLANES, lw)
    b = b.reshape(groups, SUBLANES, lw)
    row = lax.broadcasted_iota(jnp.int32, a.shape, 1)
    d = 1
    while d < SUBLANES:
        keep = row >= d
        a_prev = jnp.where(keep, pltpu.roll(a, d, 1), 1.0)
        b_prev = jnp.where(keep, pltpu.roll(b, d, 1), 0.0)
        b = a * b_prev + b
        a = a * a_prev
        d *= 2
    state = h_ref[...]
    parts = []
    for g in range(groups):
        hg = b[g] + a[g] * state
        state = hg[SUBLANES - 1:, :]
        parts.append(hg)
    h = jnp.concatenate(parts, axis=0)
    h_ref[...] = state

    y = h * jax.nn.gelu(gate)
    return _rms(y, og_ref[...])


def _lru_gate_weights(wa, wx, lw):
    _, bd, _ = wa.shape
    per = MXU_DIM // bd
    n_tiles = lw // MXU_DIM

    def tiles(w):
        w = w.reshape(n_tiles, per, bd, bd)
        eye = jnp.eye(per, dtype=w.dtype)
        return jnp.einsum('tpij,pq->tpiqj', w, eye).reshape(n_tiles, MXU_DIM, MXU_DIM)

    return jnp.concatenate([tiles(wa), tiles(wx)], axis=-1).astype(BF16)


OVERFLOW_GUARD = 1e30


def _fox_attn_kernel(q_ref, k_ref, v_ref, o_ref, vt_ref, m_ref, acc_ref, *, dh, tq):
    n_chunks = vt_ref.shape[0]
    for c in range(n_chunks):
        vt_ref[c] = v_ref[0, c * tq:(c + 1) * tq, :].astype(F32).T.astype(BF16)

    heads = [slice(hh * HEAD_SLOT, (hh + 1) * HEAD_SLOT) for hh in range(2)]
    lax.fori_loop(0, n_chunks, functools.partial(
        _fox_attn_tile, q_ref=q_ref, k_ref=k_ref, o_ref=o_ref, vt_ref=vt_ref, m_ref=m_ref,
        acc_ref=acc_ref, heads=heads, dh=dh, tq=tq), 0)


def _fox_attn_tile(qi, carry, *, q_ref, k_ref, o_ref, vt_ref, m_ref, acc_ref, heads, dh, tq):
    q_rows = pl.ds(pl.multiple_of(qi * tq, tq), tq)

    def scores(j, q, sl, masked):
        start = pl.multiple_of(j * tq, tq)
        st = lax.dot_general(k_ref[0, pl.ds(start, tq), sl], q, (((1,), (1,)), ((), ())),
                             preferred_element_type=F32)
        if masked:
            kpos = lax.broadcasted_iota(jnp.int32, st.shape, 0)
            qpos = lax.broadcasted_iota(jnp.int32, st.shape, 1)
            st = jnp.where(kpos <= qpos, st, -jnp.inf)
        return st

    for hh, sl in enumerate(heads):
        st = scores(qi, q_ref[0, q_rows, sl], sl, True)
        m = jnp.max(st, axis=0, keepdims=True)
        m_ref[hh] = m
        acc_ref[hh] = _dot(vt_ref[qi, sl, :], jnp.exp(st - m).astype(BF16))

    def fast_chunk(j):
        for hh, sl in enumerate(heads):
            pt = jnp.exp(scores(j, q_ref[0, q_rows, sl], sl, False) - m_ref[hh]).astype(BF16)
            acc_ref[hh] += _dot(vt_ref[j, sl, :], pt)

    def fast_pair(i, carry):
        start = pl.multiple_of(2 * i * tq, 2 * tq)
        for hh, sl in enumerate(heads):
            st = lax.dot_general(k_ref[0, pl.ds(start, 2 * tq), sl], q_ref[0, q_rows, sl],
                                 (((1,), (1,)), ((), ())), preferred_element_type=F32)
            pt = jnp.exp(st - m_ref[hh]).astype(BF16)
            vt = jnp.concatenate([vt_ref[2 * i, sl, :], vt_ref[2 * i + 1, sl, :]], axis=1)
            acc_ref[hh] += _dot(vt, pt)
        return carry

    lax.fori_loop(0, qi // 2, fast_pair, 0)

    @pl.when(qi % 2 == 1)
    def _():
        fast_chunk(qi - 1)

    def softmax_sums():
        return jnp.concatenate([acc_ref[0][dh:dh + 1, :], acc_ref[1][0:1, :]], axis=0)

    def normalised():
        sums = softmax_sums()
        return jnp.concatenate([acc_ref[0][0:dh, :] / sums[0:1, :],
                                acc_ref[1][dh:2 * dh, :] / sums[1:2, :]], axis=0)

    out = normalised()
    o_ref[0, q_rows, :] = out.T.astype(o_ref.dtype)
    sums = softmax_sums()
    flag = lambda x: jnp.max(jnp.where(jnp.abs(x) < OVERFLOW_GUARD, 0.0, 1.0),
                             axis=0, keepdims=True)
    bad = jnp.max(jnp.maximum(flag(out), flag(sums)), axis=1, keepdims=True)

    @pl.when(bad[0, 0] > 0.0)
    def _():
        m_ref[...] = jnp.full_like(m_ref, -jnp.inf)
        acc_ref[...] = jnp.zeros_like(acc_ref)

        def chunk(j, masked):
            for hh, sl in enumerate(heads):
                st = scores(j, q_ref[0, q_rows, sl], sl, masked)
                m_old = m_ref[hh]
                m_new = jnp.maximum(m_old, jnp.max(st, axis=0, keepdims=True))
                pt = jnp.exp(st - m_new).astype(BF16)
                acc_ref[hh] = jnp.exp(m_old - m_new) * acc_ref[hh] + _dot(vt_ref[j, sl, :], pt)
                m_ref[hh] = m_new

        def body(j, carry):
            chunk(j, False)
            return carry

        lax.fori_loop(0, qi, body, 0)
        chunk(qi, True)
        o_ref[0, q_rows, :] = normalised().T.astype(o_ref.dtype)

    return carry


def _fox_attn(q, k, v, *, n_heads, dh, tq):
    B, S, _ = q.shape
    nq = S // tq
    pair = 2 * HEAD_SLOT
    qkv_spec = pl.BlockSpec((1, S, pair), lambda b, h: (b, 0, h))
    return pl.pallas_call(
        functools.partial(_fox_attn_kernel, dh=dh, tq=tq),
        grid=(B, n_heads // 2),
        in_specs=[qkv_spec, qkv_spec, qkv_spec],
        out_specs=pl.BlockSpec((1, S, 2 * dh), lambda b, h: (b, 0, h)),
        out_shape=jax.ShapeDtypeStruct((B, S, n_heads * dh), F32),
        scratch_shapes=[pltpu.VMEM((nq, pair, tq), BF16), pltpu.VMEM((2, 1, tq), F32),
                        pltpu.VMEM((2, HEAD_SLOT, tq), F32)],
        compiler_params=pltpu.CompilerParams(
            dimension_semantics=("parallel", "parallel"), vmem_limit_bytes=VMEM_LIMIT),
        name="fox_attn",
    )(q, k, v)


def _mem_kv_kernel(m_ref, g_ref, w_ref, kg_ref, k_ref, v_ref, *, n_heads, dh):
    mw = n_heads * dh
    mb = _rms(m_ref[0], g_ref[...]).astype(BF16)
    kv = _dot(mb, w_ref[...])
    parts = []
    for h in range(n_heads):
        parts.append(_rms(kv[:, h * dh:(h + 1) * dh], kg_ref[...]))
    k_ref[0] = jnp.concatenate(parts, axis=-1).astype(BF16)
    v_ref[0] = kv[:, mw:].astype(BF16)


def _mem_kv(mem, norm_g, wkv, k_g, *, n_heads, dh):
    B, M, D = mem.shape
    mw = n_heads * dh
    full = lambda shape: pl.BlockSpec(shape, lambda b: (0,) * len(shape))
    out = pl.BlockSpec((1, M, mw), lambda b: (b, 0, 0))
    return pl.pallas_call(
        functools.partial(_mem_kv_kernel, n_heads=n_heads, dh=dh),
        grid=(B,),
        in_specs=[pl.BlockSpec((1, M, D), lambda b: (b, 0, 0)), full((1, D)), full((D, 2 * mw)),
                  full((1, dh))],
        out_specs=[out, out],
        out_shape=[jax.ShapeDtypeStruct((B, M, mw), BF16)] * 2,
        compiler_params=pltpu.CompilerParams(
            dimension_semantics=("parallel",), vmem_limit_bytes=VMEM_LIMIT),
        name="mem_kv",
    )(mem, norm_g.reshape(1, D), wkv.astype(BF16), k_g.reshape(1, dh))


def _out_mem_kernel(x_ref, yl_ref, yf_ref, fg_ref, wol_ref, wof_ref, gx_ref, wq_ref, qg_ref,
                    km_ref, vm_ref, wo_ref, gf_ref, wr_ref, br_ref,
                    x2_ref, xn_ref, gates_ref, experts_ref, *, n_heads, dh, n_groups, per_group):
    yf = _rms(yf_ref[0], fg_ref[...]).astype(BF16)
    x1 = x_ref[0] + _dot(yl_ref[0], wol_ref[...]) + _dot(yf, wof_ref[...])

    q = _dot(_rms(x1, gx_ref[...]).astype(BF16), wq_ref[...])
    outs = []
    for h in range(n_heads):
        sl = slice(h * dh, (h + 1) * dh)
        qh = _rms(q[:, sl], qg_ref[...]).astype(BF16)
        s = lax.dot_general(qh, km_ref[0, :, sl], (((1,), (1,)), ((), ())),
                            preferred_element_type=F32)
        p = jnp.exp(s - jnp.max(s, axis=-1, keepdims=True))
        inv = 1.0 / jnp.sum(p, axis=-1, keepdims=True)
        outs.append(_dot(p.astype(BF16), vm_ref[0, :, sl]) * inv)
    x2 = x1 + _dot(jnp.concatenate(outs, axis=-1).astype(BF16), wo_ref[...])
    x2_ref[0] = x2

    xn = _rms(x2, gf_ref[...])
    _store_chunk_rows(xn_ref, xn)
    logits = _dot(xn.astype(BF16), wr_ref[...]) + br_ref[...]
    lane = lax.broadcasted_iota(jnp.int32, logits.shape, 1).astype(F32)
    neg = -jnp.inf

    def top(vals):
        mx = jnp.max(vals, axis=-1, keepdims=True)
        idx = jnp.min(jnp.where(vals == mx, lane, float(LANES)), axis=-1, keepdims=True)
        return mx, idx

    gl = jnp.where(lane < n_groups, logits, neg)
    g_max, g_idx = top(gl)
    g_w = 1.0 / jnp.sum(jnp.exp(gl - g_max), axis=-1, keepdims=True)
    lo = n_groups + per_group * g_idx
    el = jnp.where((lane >= lo) & (lane < lo + per_group), logits, neg)
    e1, i1 = top(el)
    e2, i2 = top(jnp.where(lane == i1, neg, el))
    t = jnp.exp(e2 - e1)
    w1 = g_w / (1.0 + t)
    w2 = g_w * t / (1.0 + t)
    gates_ref[0] = jnp.where(lane == 0, w1, jnp.where(lane == 1, w2, 0.0))
    experts_ref[0] = jnp.where(lane == 0, i1 - n_groups,
                               jnp.where(lane == 1, i2 - n_groups, 0.0)).astype(jnp.int32)


def _out_mem(x, y_lru, y_fox, fox_out_g, w_out, norm_mem_x_g, mem_wq, mem_q_g, k_mem, v_mem, mem_wo,
             norm_ffn_g, router_group_w, router_group_b, router_expert_w, router_expert_b, *,
             n_heads, dh, tm):
    B, S, D = x.shape
    lw = y_lru.shape[-1]
    fw = y_fox.shape[-1]
    M = k_mem.shape[1]
    mw = n_heads * dh
    n_groups = router_group_w.shape[-1]
    n_experts = router_expert_w.shape[-1]
    w_r = jnp.concatenate([router_group_w, router_expert_w], axis=1)
    w_r = jnp.pad(w_r, ((0, 0), (0, LANES - w_r.shape[1]))).astype(BF16)
    b_r = jnp.concatenate([router_group_b, router_expert_b])
    b_r = jnp.pad(b_r, (0, LANES - b_r.shape[0])).reshape(1, LANES)
    q_gain = (mem_q_g * dh ** -0.5).reshape(1, dh)

    full = lambda shape: pl.BlockSpec(shape, lambda b, s: (0,) * len(shape))
    row = lambda width: pl.BlockSpec((1, tm, width), lambda b, s: (b, s, 0))
    mem_spec = pl.BlockSpec((1, M, mw), lambda b, s: (b, 0, 0))
    return pl.pallas_call(
        functools.partial(_out_mem_kernel, n_heads=n_heads, dh=dh, n_groups=n_groups,
                          per_group=n_experts // n_groups),
        grid=(B, S // tm),
        in_specs=[row(D), row(lw), row(fw), full((1, fw)), full((lw, D)), full((fw, D)),
                  full((1, D)), full((D, mw)), full((1, dh)), mem_spec, mem_spec, full((mw, D)),
                  full((1, D)), full((D, LANES)), full((1, LANES))],
        out_specs=[row(D), pl.BlockSpec((tm * D // LANES, LANES), lambda b, s: (b * (S // tm) + s, 0)),
                   row(LANES), row(LANES)],
        out_shape=[jax.ShapeDtypeStruct((B, S, D), F32),
                   jax.ShapeDtypeStruct((B * S * D // LANES, LANES), F32),
                   jax.ShapeDtypeStruct((B, S, LANES), F32),
                   jax.ShapeDtypeStruct((B, S, LANES), jnp.int32)],
        compiler_params=pltpu.CompilerParams(
            dimension_semantics=("parallel", "parallel"), vmem_limit_bytes=VMEM_LIMIT),
        name="out_mem",
    )(x, y_lru, y_fox, fox_out_g.reshape(1, fw), w_out[:lw].astype(BF16), w_out[lw:].astype(BF16),
      norm_mem_x_g.reshape(1, D), mem_wq.astype(BF16), q_gain, k_mem, v_mem, mem_wo.astype(BF16),
      norm_ffn_g.reshape(1, D), w_r, b_r)


DMA_UNROLL = 8


def _scatter_rows_kernel(zb_ref, idx_ref, x_ref, buf_ref, zero_ref, sem, zsem, *, top_k, chunks):
    tokens = x_ref.shape[0] // chunks

    @pl.when(pl.program_id(0) == 0)
    def _():
        rows = zero_ref.shape[0]
        zero_ref[...] = jnp.zeros_like(zero_ref)

        def copy(j):
            start = pl.multiple_of(j * rows, rows)
            return pltpu.make_async_copy(zero_ref, buf_ref.at[pl.ds(start, rows), :], zsem)

        def issue_zero(j, carry):
            @pl.when(zb_ref[j] == 1)
            def _():
                copy(j).start()
            return carry

        def drain_zero(j, carry):
            @pl.when(zb_ref[j] == 1)
            def _():
                copy(j).wait()
            return carry

        lax.fori_loop(0, zb_ref.shape[0], issue_zero, 0)
        lax.fori_loop(0, zb_ref.shape[0], drain_zero, 0)

    def issue(g, carry):
        for u in range(DMA_UNROLL):
            r = g * DMA_UNROLL + u
            src = x_ref.at[pl.ds(pl.multiple_of(r * chunks, chunks), chunks), :]
            for k in range(top_k):
                row = pl.multiple_of(idx_ref[0, 0, r * top_k + k], chunks)
                pltpu.make_async_copy(src, buf_ref.at[pl.ds(row, chunks), :], sem).start()
        return carry

    lax.fori_loop(0, tokens // DMA_UNROLL, issue, 0)
    for k in range(top_k):
        pltpu.make_async_copy(x_ref, buf_ref.at[pl.ds(0, tokens * chunks), :], sem).wait()


def _scatter_rows(x, dest, zero_blocks, *, tm, blk, chunks):
    nt = dest.shape[0]
    top_k = dest.shape[2] // tm
    n_blocks = zero_blocks.shape[0]
    grid_spec = pltpu.PrefetchScalarGridSpec(
        num_scalar_prefetch=1,
        grid=(nt,),
        in_specs=[pl.BlockSpec((1, 1, tm * top_k), lambda i, zb: (i, 0, 0), memory_space=pltpu.SMEM),
                  pl.BlockSpec((tm * chunks, LANES), lambda i, zb: (i, 0))],
        out_specs=pl.BlockSpec(memory_space=pl.ANY),
        scratch_shapes=[pltpu.VMEM((blk * chunks, LANES), x.dtype), pltpu.SemaphoreType.DMA,
                        pltpu.SemaphoreType.DMA],
    )
    return pl.pallas_call(
        functools.partial(_scatter_rows_kernel, top_k=top_k, chunks=chunks),
        grid_spec=grid_spec,
        out_shape=jax.ShapeDtypeStruct((n_blocks * blk * chunks, LANES), x.dtype),
        compiler_params=pltpu.CompilerParams(
            dimension_semantics=("arbitrary",), vmem_limit_bytes=VMEM_LIMIT),
        name="scatter_rows",
    )(zero_blocks, dest, x)


def _experts_kernel(be_ref, x_ref, wg_ref, wu_ref, wd_ref, o_ref, wgb_ref, wub_ref, wdb_ref, *, blk):
    i = pl.program_id(0)

    @pl.when((i == 0) | (be_ref[i] != be_ref[jnp.maximum(i - 1, 0)]))
    def _():
        wgb_ref[...] = wg_ref[0].astype(BF16)
        wub_ref[...] = wu_ref[0].astype(BF16)
        wdb_ref[...] = wd_ref[0].astype(BF16)

    d = wg_ref.shape[1]
    xb = _load_chunk_rows(x_ref, blk, d).astype(BF16)
    hdn = jax.nn.silu(_dot(xb, wgb_ref[...])) * _dot(xb, wub_ref[...])
    _store_chunk_rows(o_ref, _dot(hdn.astype(BF16), wdb_ref[...]))


def _experts(x_buf, blk_expert, w_gate, w_up, w_down, *, blk):
    _, D, de = w_gate.shape
    chunks = D // LANES
    n_blocks = x_buf.shape[0] // (blk * chunks)
    rows_spec = pl.BlockSpec((blk * chunks, LANES), lambda i, be: (i, 0))
    grid_spec = pltpu.PrefetchScalarGridSpec(
        num_scalar_prefetch=1,
        grid=(n_blocks,),
        in_specs=[rows_spec,
                  pl.BlockSpec((1, D, de), lambda i, be: (be[i], 0, 0)),
                  pl.BlockSpec((1, D, de), lambda i, be: (be[i], 0, 0)),
                  pl.BlockSpec((1, de, D), lambda i, be: (be[i], 0, 0))],
        out_specs=rows_spec,
        scratch_shapes=[pltpu.VMEM((D, de), BF16), pltpu.VMEM((D, de), BF16),
                        pltpu.VMEM((de, D), BF16)],
    )
    return pl.pallas_call(
        functools.partial(_experts_kernel, blk=blk),
        grid_spec=grid_spec,
        out_shape=jax.ShapeDtypeStruct(x_buf.shape, F32),
        compiler_params=pltpu.CompilerParams(
            dimension_semantics=("arbitrary",), vmem_limit_bytes=VMEM_LIMIT),
        name="experts",
    )(blk_expert, x_buf, w_gate, w_up, w_down)


def _combine_kernel(idx_ref, idx_next_ref, x_ref, g_ref, y_ref, o_ref, buf_ref, sem, *, top_k, chunks):
    i = pl.program_id(0)
    n = pl.num_programs(0)
    slot = lax.rem(i, 2)
    tokens, d = x_ref.shape

    def request(ids_ref, s):
        def issue(g, carry):
            for u in range(DMA_UNROLL):
                r = g * DMA_UNROLL + u
                for k in range(top_k):
                    row = pl.multiple_of(ids_ref[0, 0, r * top_k + k], chunks)
                    dst = buf_ref.at[s, k, pl.ds(pl.multiple_of(r * chunks, chunks), chunks), :]
                    pltpu.make_async_copy(y_ref.at[pl.ds(row, chunks), :], dst, sem.at[s]).start()
            return carry
        lax.fori_loop(0, tokens // DMA_UNROLL, issue, 0)

    @pl.when(i == 0)
    def _():
        request(idx_ref, 0)

    @pl.when(i + 1 < n)
    def _():
        request(idx_next_ref, 1 - slot)

    for k in range(top_k):
        pltpu.make_async_copy(y_ref.at[pl.ds(0, tokens * chunks), :], buf_ref.at[slot, k],
                              sem.at[slot]).wait()
    out = x_ref[...]
    for k in range(top_k):
        out = out + g_ref[:, k:k + 1] * _load_chunk_rows(buf_ref.at[slot, k], tokens, d)
    o_ref[...] = out


def _combine(x2, gates, y_buf, dest, *, tm):
    T, D = x2.shape
    chunks = D // LANES
    nt = dest.shape[0]
    top_k = dest.shape[2] // tm
    idx_spec = lambda step: pl.BlockSpec(
        (1, 1, tm * top_k), lambda i: (jnp.minimum(i + step, nt - 1), 0, 0), memory_space=pltpu.SMEM)
    return pl.pallas_call(
        functools.partial(_combine_kernel, top_k=top_k, chunks=chunks),
        grid=(nt,),
        in_specs=[idx_spec(0), idx_spec(1),
                  pl.BlockSpec((tm, D), lambda i: (i, 0)),
                  pl.BlockSpec((tm, LANES), lambda i: (i, 0)),
                  pl.BlockSpec(memory_space=pl.ANY)],
        out_specs=pl.BlockSpec((tm, D), lambda i: (i, 0)),
        out_shape=jax.ShapeDtypeStruct((T, D), F32),
        scratch_shapes=[pltpu.VMEM((2, top_k, tm * chunks, LANES), F32),
                        pltpu.SemaphoreType.DMA((2,))],
        compiler_params=pltpu.CompilerParams(
            dimension_semantics=("arbitrary",), vmem_limit_bytes=VMEM_LIMIT),
        name="combine",
    )(dest, dest, x2, gates, y_buf)


def _dispatch(experts, n_experts, blk):
    T, K = experts.shape
    A = T * K
    e_flat = experts.reshape(A)
    onehot = (e_flat[:, None] == jnp.arange(n_experts, dtype=jnp.int32)[None, :]).astype(jnp.int32)
    ranks = jnp.cumsum(onehot, axis=0)
    counts = ranks[-1]
    padded = (counts + blk - 1) // blk * blk
    pends = jnp.cumsum(padded)
    pstarts = pends - padded
    dest = jnp.sum(onehot * (ranks - 1 + pstarts[None, :]), axis=1)
    n_blocks = (A + n_experts * blk) // blk
    blk_start = jnp.arange(n_blocks, dtype=jnp.int32) * blk
    blk_expert = jnp.minimum(jnp.sum(blk_start[:, None] >= pends[None, :], axis=1), n_experts - 1)
    last_of_segment = jnp.any((blk_start[:, None] + blk == pends[None, :]) & (padded[None, :] > 0),
                              axis=1)
    zero_blocks = (last_of_segment | (blk_start >= pends[-1])).astype(jnp.int32)
    return dest.reshape(T, K), blk_expert.astype(jnp.int32), zero_blocks


def kernel(x, mem, norm_mix_g, w_in, b_forget, conv_w, conv_b, lru_wa, lru_ba, lru_wx, lru_bx,
           lru_a_param, fox_q_g, fox_k_g, lru_out_g, fox_out_g, w_out, norm_mem_x_g, norm_mem_g,
           mem_wq, mem_wkv, mem_q_g, mem_k_g, mem_wo, norm_ffn_g, router_group_w, router_group_b,
           router_expert_w, router_expert_b, exp_w_gate, exp_w_up, exp_w_down):
    B, S, D = x.shape
    depth = norm_mix_g.shape[0]
    lw = conv_w.shape[-1]
    n_heads = b_forget.shape[-1]
    dh = fox_q_g.shape[-1]
    mem_dh = mem_q_g.shape[-1]
    mem_heads = mem_wq.shape[-1] // mem_dh
    n_experts = router_expert_w.shape[-1]
    T = B * S
    tm = _pick_tile(S, 1024)
    tq = _pick_tile(S, 1024)
    blk = 512
    scatter_tm = _pick_tile(T, 1024)
    combine_tm = _pick_tile(T, 512)

    for l in range(depth):
        y_lru, q, k, v = _in_proj(
            x, norm_mix_g[l], w_in[l], b_forget[l], fox_q_g[l], fox_k_g[l], conv_w[l], conv_b[l],
            lru_wa[l], lru_ba[l], lru_wx[l], lru_bx[l], lru_a_param[l], lru_out_g[l],
            lw=lw, n_heads=n_heads, dh=dh, tm=tm)
        y_fox = _fox_attn(q, k, v, n_heads=n_heads, dh=dh, tq=tq)
        k_mem, v_mem = _mem_kv(mem, norm_mem_g[l], mem_wkv[l], mem_k_g[l], n_heads=mem_heads,
                               dh=mem_dh)
        x2, xn, gates, experts = _out_mem(
            x, y_lru, y_fox, fox_out_g[l], w_out[l], norm_mem_x_g[l], mem_wq[l], mem_q_g[l], k_mem,
            v_mem, mem_wo[l], norm_ffn_g[l], router_group_w[l], router_group_b[l],
            router_expert_w[l], router_expert_b[l], n_heads=mem_heads, dh=mem_dh, tm=tm)

        dest, blk_expert, zero_blocks = _dispatch(
            experts.reshape(T, LANES)[:, :TOP_K], n_experts, blk)
        chunks = D // LANES
        tiled = lambda t: (dest * chunks).reshape(T // t, 1, t * TOP_K)
        x_buf = _scatter_rows(xn, tiled(scatter_tm), zero_blocks, tm=scatter_tm, blk=blk,
                              chunks=chunks)
        y_buf = _experts(x_buf, blk_expert, exp_w_gate[l], exp_w_up[l], exp_w_down[l], blk=blk)
        x = _combine(x2.reshape(T, D), gates.reshape(T, LANES), y_buf, tiled(combine_tm),
                     tm=combine_tm)
        x = x.reshape(B, S, D)
    return x
```

```python
import functools
from typing import NamedTuple

import jax
import jax.numpy as jnp
import numpy as np
from jax import lax
from jax.experimental import pallas as pl
from jax.experimental.pallas import tpu as pltpu

EPS = 1e-6
LRU_C = 8.0
CONV_WIDTH = 4
TOP_K = 2
LANES = 128
SUBLANES = 8
MXU_DIM = 256
HEAD_SLOT = LANES
VMEM_LIMIT = 56 * 1024 * 1024

F32 = jnp.float32
BF16 = jnp.bfloat16
MIN_NORMAL = float(np.finfo(np.float32).tiny)


def _dot(a, b):
    return jnp.dot(a, b, preferred_element_type=F32)


def _rms(x, g):
    return x * lax.rsqrt(jnp.mean(x * x, axis=-1, keepdims=True) + EPS) * g


def _softplus(x):
    return jnp.maximum(x, 0.0) + jnp.log1p(jnp.exp(-jnp.abs(x)))


def _pick_tile(n, target):
    t = min(n, target)
    while n % t:
        t //= 2
    return t


def _store_chunk_rows(ref, val):
    n, d = val.shape
    chunks = d // LANES
    for c in range(chunks):
        ref[pl.ds(c, n, stride=chunks), :] = val[:, c * LANES:(c + 1) * LANES]


def _load_chunk_rows(ref, n, d):
    chunks = d // LANES
    return jnp.concatenate([ref[pl.ds(c, n, stride=chunks), :] for c in range(chunks)], axis=1)


def _in_proj_kernel(x_ref, g_ref, w_ref, wf_ref, bf_ref, gq_ref, gk_ref, pe_ref, cv_ref,
                    cw_ref, cb_ref, wl_ref, ba_ref, bx_ref, ap_ref, og_ref,
                    ylru_ref, q_ref, k_ref, v_ref, carry_ref, tail_ref, h_ref, *, lw, n_heads, dh):
    @pl.when(pl.program_id(1) == 0)
    def _():
        carry_ref[...] = jnp.zeros_like(carry_ref)
        tail_ref[...] = jnp.zeros_like(tail_ref)
        h_ref[...] = jnp.zeros_like(h_ref)

    hs = n_heads * HEAD_SLOT
    x = x_ref[0]
    tm = x.shape[0]
    hb = _rms(x, g_ref[...]).astype(BF16)

    u = _dot(hb, w_ref[:, 0:lw])
    gate = _dot(hb, w_ref[:, lw:2 * lw])
    ylru_ref[0] = _rg_lru_tile(u, gate, cw_ref, cb_ref, wl_ref, ba_ref, bx_ref, ap_ref, og_ref,
                               tail_ref, h_ref).astype(ylru_ref.dtype)

    z = _dot(hb, wf_ref[...]) + bf_ref[...]
    lane = lax.broadcasted_iota(jnp.int32, z.shape, 1)
    row = lax.broadcasted_iota(jnp.int32, z.shape, 0)
    c = jnp.where(lane < n_heads, -_softplus(-z), 0.0)
    d = 1
    while d < tm:
        c = c + jnp.where(row >= d, pltpu.roll(c, d, 0), 0.0)
        d *= 2
    c = c + carry_ref[...]
    carry_ref[...] = c[tm - 1:tm, :]
    c1 = c.astype(BF16).astype(F32)
    r1 = c - c1
    c2 = r1.astype(BF16).astype(F32)
    c3 = (r1 - c2).astype(BF16).astype(F32)
    e = c1 + pltpu.roll(c2, n_heads, 1) + pltpu.roll(c3, 2 * n_heads, 1)
    e = jnp.where(lane == 3 * n_heads, 1.0, e).astype(BF16)
    ext = _dot(e, pe_ref[...])

    fw = n_heads * dh
    lane_t = lax.broadcasted_iota(jnp.int32, (tm, LANES), 1)
    own = (lane_t < dh, lane_t >= dh)

    def to_slots(y):
        parts = []
        for c in range(fw // LANES):
            col = y[:, c * LANES:(c + 1) * LANES]
            parts += [jnp.where(own[0], col, 0.0), jnp.where(own[1], col, 0.0)]
        return jnp.concatenate(parts, axis=-1)

    def head_norm(y, gain):
        parts = []
        for h in range(n_heads):
            blk = y[:, h * HEAD_SLOT:(h + 1) * HEAD_SLOT]
            ss = jnp.sum(blk * blk, axis=-1, keepdims=True) * (1.0 / dh)
            parts.append(blk * lax.rsqrt(ss + EPS))
        return jnp.concatenate(parts, axis=-1) * gain

    q = to_slots(_dot(hb, w_ref[:, 2 * lw:2 * lw + fw]))
    q_ref[0] = (head_norm(q, gq_ref[...]) + ext[:, 0:hs]).astype(BF16)
    k = to_slots(_dot(hb, w_ref[:, 2 * lw + fw:2 * lw + 2 * fw]))
    k_ref[0] = (head_norm(k, gk_ref[...]) + ext[:, hs:2 * hs]).astype(BF16)
    v = to_slots(_dot(hb, w_ref[:, 2 * lw + 2 * fw:2 * lw + 3 * fw]))
    v_ref[0] = (v + cv_ref[...]).astype(BF16)


def _in_proj(x, norm_g, w_in, b_forget, fox_q_g, fox_k_g, conv_w, conv_b, wa, ba, wx, bx, a_param,
             lru_out_g, *, lw, n_heads, dh, tm):
    B, S, D = x.shape
    fw = n_heads * dh
    hs = n_heads * HEAD_SLOT
    assert 2 * dh == HEAD_SLOT and fw % LANES == 0 and 3 * n_heads <= LANES
    n_main = 2 * lw + 3 * fw
    w_all = w_in[:, :n_main].astype(BF16)
    w_f = jnp.pad(w_in[:, n_main:], ((0, 0), (0, LANES - n_heads))).astype(BF16)
    b_f = jnp.pad(b_forget, (0, LANES - n_heads)).reshape(1, LANES)
    scale = dh ** -0.5

    def slot_gains(g):
        even = jnp.pad(g, (0, HEAD_SLOT - dh))
        odd = jnp.pad(g, (HEAD_SLOT - dh, 0))
        return jnp.tile(jnp.concatenate([even, odd]), n_heads // 2).reshape(1, hs)

    gq = slot_gains(fox_q_g * scale)
    gk = slot_gains(fox_k_g)
    ones_row = 3 * n_heads
    pe = np.zeros((LANES, 2 * hs), np.float32)
    cv = np.zeros((1, hs), np.float32)
    for h in range(n_heads):
        x0 = h * HEAD_SLOT + (dh if h % 2 == 0 else 0)
        for j in range(3):
            pe[j * n_heads + h, x0 + j] = 1.0
            pe[ones_row, x0 + 3 + j] = 1.0
            pe[ones_row, hs + x0 + j] = 1.0
            pe[j * n_heads + h, hs + x0 + 3 + j] = -1.0
        cv[0, x0] = 1.0
    pe = jnp.asarray(pe, BF16)

    w_lru = _lru_gate_weights(wa, wx, lw)
    vec = lambda a: a.reshape(1, lw)
    full = lambda shape: pl.BlockSpec(shape, lambda b, s: (0,) * len(shape))
    row = lambda width: pl.BlockSpec((1, tm, width), lambda b, s: (b, s, 0))
    return pl.pallas_call(
        functools.partial(_in_proj_kernel, lw=lw, n_heads=n_heads, dh=dh),
        grid=(B, S // tm),
        in_specs=[row(D), full((1, D)), full(w_all.shape), full(w_f.shape), full((1, LANES)),
                  full((1, hs)), full((1, hs)), full(pe.shape), full((1, hs)),
                  full((CONV_WIDTH, lw)), full((1, lw)), full(w_lru.shape), full((1, lw)),
                  full((1, lw)), full((1, lw)), full((1, lw))],
        out_specs=[row(lw), row(hs), row(hs), row(hs)],
        out_shape=[jax.ShapeDtypeStruct((B, S, lw), BF16),
                   jax.ShapeDtypeStruct((B, S, hs), BF16), jax.ShapeDtypeStruct((B, S, hs), BF16),
                   jax.ShapeDtypeStruct((B, S, hs), BF16)],
        scratch_shapes=[pltpu.VMEM((1, LANES), F32), pltpu.VMEM((SUBLANES, lw), F32),
                        pltpu.VMEM((1, lw), F32)],
        compiler_params=pltpu.CompilerParams(
            dimension_semantics=("parallel", "arbitrary"), vmem_limit_bytes=VMEM_LIMIT),
        name="in_proj",
    )(x, norm_g.reshape(1, D), w_all, w_f, b_f, gq, gk, pe, jnp.asarray(cv),
      conv_w, vec(conv_b), w_lru, vec(ba), vec(bx), vec(a_param), vec(lru_out_g))


def _rg_lru_tile(u, gate, cw_ref, cb_ref, w_ref, ba_ref, bx_ref, ap_ref, og_ref, tail_ref, h_ref):
    ts, lw = u.shape
    ext = jnp.concatenate([tail_ref[...], u], axis=0)
    tail_ref[...] = u[ts - SUBLANES:, :]
    xc = cb_ref[...] + u * cw_ref[CONV_WIDTH - 1:CONV_WIDTH, :]
    for back in range(1, CONV_WIDTH):
        tap = CONV_WIDTH - 1 - back
        xc = xc + pltpu.roll(ext, back, 0)[SUBLANES:, :] * cw_ref[tap:tap + 1, :]

    xb = xc.astype(BF16)
    r_parts, i_parts = [], []
    for j in range(lw // MXU_DIM):
        y = _dot(xb[:, j * MXU_DIM:(j + 1) * MXU_DIM], w_ref[j])
        r_parts.append(y[:, :MXU_DIM])
        i_parts.append(y[:, MXU_DIM:])
    r = jax.nn.sigmoid(jnp.concatenate(r_parts, axis=-1) + ba_ref[...])
    i = jax.nn.sigmoid(jnp.concatenate(i_parts, axis=-1) + bx_ref[...])
    log_a = (-LRU_C) * r * _softplus(-ap_ref[...])
    a = jnp.exp(log_a)
    gap = jnp.maximum(-jnp.tanh(log_a) * (a * a + 1.0), 0.0)
    mult = gap * lax.rsqrt(jnp.maximum(gap, MIN_NORMAL))
    b = mult * (i * xc)

    groups = ts // SUBLANES
    a = a.reshape(groups, SUBLANES, lw)
    b = b.reshape(groups, SUBLANES, lw)
    row = lax.broadcasted_iota(jnp.int32, a.shape, 1)
    d = 1
    while d < SUBLANES:
        keep = row >= d
        a_prev = jnp.where(keep, pltpu.roll(a, d, 1), 1.0)
        b_prev = jnp.where(keep, pltpu.roll(b, d, 1), 0.0)
        b = a * b_prev + b
        a = a * a_prev
        d *= 2
    state = h_ref[...]
    parts = []
    for g in range(groups):
        hg = b[g] + a[g] * state
        state = hg[SUBLANES - 1:, :]
        parts.append(hg)
    h = jnp.concatenate(parts, axis=0)
    h_ref[...] = state

    y = h * jax.nn.gelu(gate)
    return _rms(y, og_ref[...])


def _lru_gate_weights(wa, wx, lw):
    _, bd, _ = wa.shape
    per = MXU_DIM // bd
    n_tiles = lw // MXU_DIM

    def tiles(w):
        w = w.reshape(n_tiles, per, bd, bd)
        eye = jnp.eye(per, dtype=w.dtype)
        return jnp.einsum('tpij,pq->tpiqj', w, eye).reshape(n_tiles, MXU_DIM, MXU_DIM)

    return jnp.concatenate([tiles(wa), tiles(wx)], axis=-1).astype(BF16)


OVERFLOW_GUARD = 1e30


def _fox_attn_kernel(q_ref, k_ref, v_ref, o_ref, vt_ref, m_ref, acc_ref, *, dh, tq):
    n_chunks = vt_ref.shape[0]
    for c in range(n_chunks):
        vt_ref[c] = v_ref[0, c * tq:(c + 1) * tq, :].astype(F32).T.astype(BF16)

    heads = [slice(hh * HEAD_SLOT, (hh + 1) * HEAD_SLOT) for hh in range(2)]
    lax.fori_loop(0, n_chunks, functools.partial(
        _fox_attn_tile, q_ref=q_ref, k_ref=k_ref, o_ref=o_ref, vt_ref=vt_ref, m_ref=m_ref,
        acc_ref=acc_ref, heads=heads, dh=dh, tq=tq), 0)


def _fox_attn_tile(qi, carry, *, q_ref, k_ref, o_ref, vt_ref, m_ref, acc_ref, heads, dh, tq):
    q_rows = pl.ds(pl.multiple_of(qi * tq, tq), tq)

    def scores(j, q, sl, masked):
        start = pl.multiple_of(j * tq, tq)
        st = lax.dot_general(k_ref[0, pl.ds(start, tq), sl], q, (((1,), (1,)), ((), ())),
                             preferred_element_type=F32)
        if masked:
            kpos = lax.broadcasted_iota(jnp.int32, st.shape, 0)
            qpos = lax.broadcasted_iota(jnp.int32, st.shape, 1)
            st = jnp.where(kpos <= qpos, st, -jnp.inf)
        return st

    for hh, sl in enumerate(heads):
        st = scores(qi, q_ref[0, q_rows, sl], sl, True)
        m = jnp.max(st, axis=0, keepdims=True)
        m_ref[hh] = m
        acc_ref[hh] = _dot(vt_ref[qi, sl, :], jnp.exp(st - m).astype(BF16))

    def fast_chunk(j):
        for hh, sl in enumerate(heads):
            pt = jnp.exp(scores(j, q_ref[0, q_rows, sl], sl, False) - m_ref[hh]).astype(BF16)
            acc_ref[hh] += _dot(vt_ref[j, sl, :], pt)

    def fast_pair(i, carry):
        start = pl.multiple_of(2 * i * tq, 2 * tq)
        for hh, sl in enumerate(heads):
            st = lax.dot_general(k_ref[0, pl.ds(start, 2 * tq), sl], q_ref[0, q_rows, sl],
                                 (((1,), (1,)), ((), ())), preferred_element_type=F32)
            pt = jnp.exp(st - m_ref[hh]).astype(BF16)
            vt = jnp.concatenate([vt_ref[2 * i, sl, :], vt_ref[2 * i + 1, sl, :]], axis=1)
            acc_ref[hh] += _dot(vt, pt)
        return carry

    lax.fori_loop(0, qi // 2, fast_pair, 0)

    @pl.when(qi % 2 == 1)
    def _():
        fast_chunk(qi - 1)

    def softmax_sums():
        return jnp.concatenate([acc_ref[0][dh:dh + 1, :], acc_ref[1][0:1, :]], axis=0)

    def normalised():
        sums = softmax_sums()
        return jnp.concatenate([acc_ref[0][0:dh, :] / sums[0:1, :],
                                acc_ref[1][dh:2 * dh, :] / sums[1:2, :]], axis=0)

    out = normalised()
    o_ref[0, q_rows, :] = out.T.astype(o_ref.dtype)
    sums = softmax_sums()
    flag = lambda x: jnp.max(jnp.where(jnp.abs(x) < OVERFLOW_GUARD, 0.0, 1.0),
                             axis=0, keepdims=True)
    bad = jnp.max(jnp.maximum(flag(out), flag(sums)), axis=1, keepdims=True)

    @pl.when(bad[0, 0] > 0.0)
    def _():
        m_ref[...] = jnp.full_like(m_ref, -jnp.inf)
        acc_ref[...] = jnp.zeros_like(acc_ref)

        def chunk(j, masked):
            for hh, sl in enumerate(heads):
                st = scores(j, q_ref[0, q_rows, sl], sl, masked)
                m_old = m_ref[hh]
                m_new = jnp.maximum(m_old, jnp.max(st, axis=0, keepdims=True))
                pt = jnp.exp(st - m_new).astype(BF16)
                acc_ref[hh] = jnp.exp(m_old - m_new) * acc_ref[hh] + _dot(vt_ref[j, sl, :], pt)
                m_ref[hh] = m_new

        def body(j, carry):
            chunk(j, False)
            return carry

        lax.fori_loop(0, qi, body, 0)
        chunk(qi, True)
        o_ref[0, q_rows, :] = normalised().T.astype(o_ref.dtype)

    return carry


def _fox_attn(q, k, v, *, n_heads, dh, tq):
    B, S, _ = q.shape
    nq = S // tq
    pair = 2 * HEAD_SLOT
    qkv_spec = pl.BlockSpec((1, S, pair), lambda b, h: (b, 0, h))
    return pl.pallas_call(
        functools.partial(_fox_attn_kernel, dh=dh, tq=tq),
        grid=(B, n_heads // 2),
        in_specs=[qkv_spec, qkv_spec, qkv_spec],
        out_specs=pl.BlockSpec((1, S, 2 * dh), lambda b, h: (b, 0, h)),
        out_shape=jax.ShapeDtypeStruct((B, S, n_heads * dh), BF16),
        scratch_shapes=[pltpu.VMEM((nq, pair, tq), BF16), pltpu.VMEM((2, 1, tq), F32),
                        pltpu.VMEM((2, HEAD_SLOT, tq), F32)],
        compiler_params=pltpu.CompilerParams(
            dimension_semantics=("parallel", "parallel"), vmem_limit_bytes=VMEM_LIMIT),
        name="fox_attn",
    )(q, k, v)


def _mem_kv_kernel(m_ref, g_ref, w_ref, kg_ref, k_ref, v_ref, *, n_heads, dh):
    mw = n_heads * dh
    mb = _rms(m_ref[0], g_ref[...]).astype(BF16)
    kv = _dot(mb, w_ref[...])
    parts = []
    for h in range(n_heads):
        parts.append(_rms(kv[:, h * dh:(h + 1) * dh], kg_ref[...]))
    k_ref[0] = jnp.concatenate(parts, axis=-1).astype(BF16)
    v_ref[0] = kv[:, mw:].astype(BF16)


def _mem_kv(mem, norm_g, wkv, k_g, *, n_heads, dh):
    B, M, D = mem.shape
    mw = n_heads * dh
    full = lambda shape: pl.BlockSpec(shape, lambda b: (0,) * len(shape))
    out = pl.BlockSpec((1, M, mw), lambda b: (b, 0, 0))
    return pl.pallas_call(
        functools.partial(_mem_kv_kernel, n_heads=n_heads, dh=dh),
        grid=(B,),
        in_specs=[pl.BlockSpec((1, M, D), lambda b: (b, 0, 0)), full((1, D)), full((D, 2 * mw)),
                  full((1, dh))],
        out_specs=[out, out],
        out_shape=[jax.ShapeDtypeStruct((B, M, mw), BF16)] * 2,
        compiler_params=pltpu.CompilerParams(
            dimension_semantics=("parallel",), vmem_limit_bytes=VMEM_LIMIT),
        name="mem_kv",
    )(mem, norm_g.reshape(1, D), wkv.astype(BF16), k_g.reshape(1, dh))


def _out_mem_kernel(x_ref, yl_ref, yf_ref, fg_ref, wol_ref, wof_ref, gx_ref, wq_ref, qg_ref,
                    km_ref, vm_ref, wo_ref, gf_ref, wr_ref, br_ref,
                    x2_ref, xn_ref, gates_ref, experts_ref, *, n_heads, dh, n_groups, per_group):
    yf = _rms(yf_ref[0].astype(F32), fg_ref[...]).astype(BF16)
    x1 = x_ref[0] + _dot(yl_ref[0], wol_ref[...]) + _dot(yf, wof_ref[...])

    q = _dot(_rms(x1, gx_ref[...]).astype(BF16), wq_ref[...])
    outs = []
    for h in range(n_heads):
        sl = slice(h * dh, (h + 1) * dh)
        qh = _rms(q[:, sl], qg_ref[...]).astype(BF16)
        s = lax.dot_general(qh, km_ref[0, :, sl], (((1,), (1,)), ((), ())),
                            preferred_element_type=F32)
        p = jnp.exp(s - jnp.max(s, axis=-1, keepdims=True))
        inv = 1.0 / jnp.sum(p, axis=-1, keepdims=True)
        outs.append(_dot(p.astype(BF16), vm_ref[0, :, sl]) * inv)
    x2 = x1 + _dot(jnp.concatenate(outs, axis=-1).astype(BF16), wo_ref[...])
    x2_ref[0] = x2

    xn = _rms(x2, gf_ref[...])
    _store_chunk_rows(xn_ref, xn)
    logits = _dot(xn.astype(BF16), wr_ref[...]) + br_ref[...]
    lane = lax.broadcasted_iota(jnp.int32, logits.shape, 1).astype(F32)
    neg = -jnp.inf

    def top(vals):
        mx = jnp.max(vals, axis=-1, keepdims=True)
        idx = jnp.min(jnp.where(vals == mx, lane, float(LANES)), axis=-1, keepdims=True)
        return mx, idx

    gl = jnp.where(lane < n_groups, logits, neg)
    g_max, g_idx = top(gl)
    g_w = 1.0 / jnp.sum(jnp.exp(gl - g_max), axis=-1, keepdims=True)
    lo = n_groups + per_group * g_idx
    el = jnp.where((lane >= lo) & (lane < lo + per_group), logits, neg)
    e1, i1 = top(el)
    e2, i2 = top(jnp.where(lane == i1, neg, el))
    t = jnp.exp(e2 - e1)
    w1 = g_w / (1.0 + t)
    w2 = g_w * t / (1.0 + t)
    gates_ref[0] = jnp.where(lane == 0, w1, jnp.where(lane == 1, w2, 0.0))
    experts_ref[0] = jnp.where(lane == 0, i1 - n_groups,
                               jnp.where(lane == 1, i2 - n_groups, 0.0)).astype(jnp.int32)


def _out_mem(x, y_lru, y_fox, fox_out_g, w_out, norm_mem_x_g, mem_wq, mem_q_g, k_mem, v_mem, mem_wo,
             norm_ffn_g, router_group_w, router_group_b, router_expert_w, router_expert_b, *,
             n_heads, dh, tm):
    B, S, D = x.shape
    lw = y_lru.shape[-1]
    fw = y_fox.shape[-1]
    M = k_mem.shape[1]
    mw = n_heads * dh
    n_groups = router_group_w.shape[-1]
    n_experts = router_expert_w.shape[-1]
    w_r = jnp.concatenate([router_group_w, router_expert_w], axis=1)
    w_r = jnp.pad(w_r, ((0, 0), (0, LANES - w_r.shape[1]))).astype(BF16)
    b_r = jnp.concatenate([router_group_b, router_expert_b])
    b_r = jnp.pad(b_r, (0, LANES - b_r.shape[0])).reshape(1, LANES)
    q_gain = (mem_q_g * dh ** -0.5).reshape(1, dh)

    full = lambda shape: pl.BlockSpec(shape, lambda b, s: (0,) * len(shape))
    row = lambda width: pl.BlockSpec((1, tm, width), lambda b, s: (b, s, 0))
    mem_spec = pl.BlockSpec((1, M, mw), lambda b, s: (b, 0, 0))
    return pl.pallas_call(
        functools.partial(_out_mem_kernel, n_heads=n_heads, dh=dh, n_groups=n_groups,
                          per_group=n_experts // n_groups),
        grid=(B, S // tm),
        in_specs=[row(D), row(lw), row(fw), full((1, fw)), full((lw, D)), full((fw, D)),
                  full((1, D)), full((D, mw)), full((1, dh)), mem_spec, mem_spec, full((mw, D)),
                  full((1, D)), full((D, LANES)), full((1, LANES))],
        out_specs=[row(D), pl.BlockSpec((tm * D // LANES, LANES), lambda b, s: (b * (S // tm) + s, 0)),
                   row(LANES), row(LANES)],
        out_shape=[jax.ShapeDtypeStruct((B, S, D), F32),
                   jax.ShapeDtypeStruct((B * S * D // LANES, LANES), F32),
                   jax.ShapeDtypeStruct((B, S, LANES), F32),
                   jax.ShapeDtypeStruct((B, S, LANES), jnp.int32)],
        compiler_params=pltpu.CompilerParams(
            dimension_semantics=("parallel", "parallel"), vmem_limit_bytes=VMEM_LIMIT),
        name="out_mem",
    )(x, y_lru, y_fox, fox_out_g.reshape(1, fw), w_out[:lw].astype(BF16), w_out[lw:].astype(BF16),
      norm_mem_x_g.reshape(1, D), mem_wq.astype(BF16), q_gain, k_mem, v_mem, mem_wo.astype(BF16),
      norm_ffn_g.reshape(1, D), w_r, b_r)


DMA_UNROLL = 8
RANK_GROUP = 256


def _scatter_rows_kernel(zb_ref, idx_ref, x_ref, buf_ref, zero_ref, sem, zsem, *, top_k, chunks):
    tokens = x_ref.shape[0] // chunks

    @pl.when(pl.program_id(0) == 0)
    def _():
        rows = zero_ref.shape[0]
        zero_ref[...] = jnp.zeros_like(zero_ref)

        def copy(j):
            start = pl.multiple_of(j * rows, rows)
            return pltpu.make_async_copy(zero_ref, buf_ref.at[pl.ds(start, rows), :], zsem)

        def issue_zero(j, carry):
            @pl.when(zb_ref[j] == 1)
            def _():
                copy(j).start()
            return carry

        def drain_zero(j, carry):
            @pl.when(zb_ref[j] == 1)
            def _():
                copy(j).wait()
            return carry

        lax.fori_loop(0, zb_ref.shape[0], issue_zero, 0)
        lax.fori_loop(0, zb_ref.shape[0], drain_zero, 0)

    def issue(g, carry):
        for u in range(DMA_UNROLL):
            r = g * DMA_UNROLL + u
            src = x_ref.at[pl.ds(pl.multiple_of(r * chunks, chunks), chunks), :]
            for k in range(top_k):
                row = pl.multiple_of(idx_ref[0, 0, r * top_k + k], chunks)
                pltpu.make_async_copy(src, buf_ref.at[pl.ds(row, chunks), :], sem).start()
        return carry

    lax.fori_loop(0, tokens // DMA_UNROLL, issue, 0)
    for k in range(top_k):
        pltpu.make_async_copy(x_ref, buf_ref.at[pl.ds(0, tokens * chunks), :], sem).wait()


def _scatter_rows(x, dest, zero_blocks, *, tm, blk, chunks):
    nt = dest.shape[0]
    top_k = dest.shape[2] // tm
    n_blocks = zero_blocks.shape[0]
    grid_spec = pltpu.PrefetchScalarGridSpec(
        num_scalar_prefetch=1,
        grid=(nt,),
        in_specs=[pl.BlockSpec((1, 1, tm * top_k), lambda i, zb: (i, 0, 0), memory_space=pltpu.SMEM),
                  pl.BlockSpec((tm * chunks, LANES), lambda i, zb: (i, 0))],
        out_specs=pl.BlockSpec(memory_space=pl.ANY),
        scratch_shapes=[pltpu.VMEM((blk * chunks, LANES), x.dtype), pltpu.SemaphoreType.DMA,
                        pltpu.SemaphoreType.DMA],
    )
    return pl.pallas_call(
        functools.partial(_scatter_rows_kernel, top_k=top_k, chunks=chunks),
        grid_spec=grid_spec,
        out_shape=jax.ShapeDtypeStruct((n_blocks * blk * chunks, LANES), x.dtype),
        compiler_params=pltpu.CompilerParams(
            dimension_semantics=("arbitrary",), vmem_limit_bytes=VMEM_LIMIT),
        name="scatter_rows",
    )(zero_blocks, dest, x)


def _experts_kernel(be_ref, x_ref, wg_ref, wu_ref, wd_ref, o_ref, wgb_ref, wub_ref, wdb_ref, *, blk):
    i = pl.program_id(0)

    @pl.when((i == 0) | (be_ref[i] != be_ref[jnp.maximum(i - 1, 0)]))
    def _():
        wgb_ref[...] = wg_ref[0].astype(BF16)
        wub_ref[...] = wu_ref[0].astype(BF16)
        wdb_ref[...] = wd_ref[0].astype(BF16)

    d = wg_ref.shape[1]
    xb = _load_chunk_rows(x_ref, blk, d).astype(BF16)
    hdn = jax.nn.silu(_dot(xb, wgb_ref[...])) * _dot(xb, wub_ref[...])
    _store_chunk_rows(o_ref, _dot(hdn.astype(BF16), wdb_ref[...]))


def _experts(x_buf, blk_expert, w_gate, w_up, w_down, *, blk):
    _, D, de = w_gate.shape
    chunks = D // LANES
    n_blocks = x_buf.shape[0] // (blk * chunks)
    rows_spec = pl.BlockSpec((blk * chunks, LANES), lambda i, be: (i, 0))
    grid_spec = pltpu.PrefetchScalarGridSpec(
        num_scalar_prefetch=1,
        grid=(n_blocks,),
        in_specs=[rows_spec,
                  pl.BlockSpec((1, D, de), lambda i, be: (be[i], 0, 0)),
                  pl.BlockSpec((1, D, de), lambda i, be: (be[i], 0, 0)),
                  pl.BlockSpec((1, de, D), lambda i, be: (be[i], 0, 0))],
        out_specs=rows_spec,
        scratch_shapes=[pltpu.VMEM((D, de), BF16), pltpu.VMEM((D, de), BF16),
                        pltpu.VMEM((de, D), BF16)],
    )
    return pl.pallas_call(
        functools.partial(_experts_kernel, blk=blk),
        grid_spec=grid_spec,
        out_shape=jax.ShapeDtypeStruct(x_buf.shape, F32),
        compiler_params=pltpu.CompilerParams(
            dimension_semantics=("arbitrary",), vmem_limit_bytes=VMEM_LIMIT),
        name="experts",
    )(blk_expert, x_buf, w_gate, w_up, w_down)


def _combine_kernel(idx_ref, idx_next_ref, x_ref, g_ref, y_ref, o_ref, buf_ref, sem, *, top_k, chunks):
    i = pl.program_id(0)
    n = pl.num_programs(0)
    slot = lax.rem(i, 2)
    tokens, d = x_ref.shape

    def request(ids_ref, s):
        def issue(g, carry):
            for u in range(DMA_UNROLL):
                r = g * DMA_UNROLL + u
                for k in range(top_k):
                    row = pl.multiple_of(ids_ref[0, 0, r * top_k + k], chunks)
                    dst = buf_ref.at[s, k, pl.ds(pl.multiple_of(r * chunks, chunks), chunks), :]
                    pltpu.make_async_copy(y_ref.at[pl.ds(row, chunks), :], dst, sem.at[s]).start()
            return carry
        lax.fori_loop(0, tokens // DMA_UNROLL, issue, 0)

    @pl.when(i == 0)
    def _():
        request(idx_ref, 0)

    @pl.when(i + 1 < n)
    def _():
        request(idx_next_ref, 1 - slot)

    for k in range(top_k):
        pltpu.make_async_copy(y_ref.at[pl.ds(0, tokens * chunks), :], buf_ref.at[slot, k],
                              sem.at[slot]).wait()
    out = x_ref[...]
    for k in range(top_k):
        out = out + g_ref[:, k:k + 1] * _load_chunk_rows(buf_ref.at[slot, k], tokens, d)
    o_ref[...] = out


def _combine(x2, gates, y_buf, dest, *, tm):
    T, D = x2.shape
    chunks = D // LANES
    nt = dest.shape[0]
    top_k = dest.shape[2] // tm
    idx_spec = lambda step: pl.BlockSpec(
        (1, 1, tm * top_k), lambda i: (jnp.minimum(i + step, nt - 1), 0, 0), memory_space=pltpu.SMEM)
    return pl.pallas_call(
        functools.partial(_combine_kernel, top_k=top_k, chunks=chunks),
        grid=(nt,),
        in_specs=[idx_spec(0), idx_spec(1),
                  pl.BlockSpec((tm, D), lambda i: (i, 0)),
                  pl.BlockSpec((tm, LANES), lambda i: (i, 0)),
                  pl.BlockSpec(memory_space=pl.ANY)],
        out_specs=pl.BlockSpec((tm, D), lambda i: (i, 0)),
        out_shape=jax.ShapeDtypeStruct((T, D), F32),
        scratch_shapes=[pltpu.VMEM((2, top_k, tm * chunks, LANES), F32),
                        pltpu.SemaphoreType.DMA((2,))],
        compiler_params=pltpu.CompilerParams(
            dimension_semantics=("arbitrary",), vmem_limit_bytes=VMEM_LIMIT),
        name="combine",
    )(dest, dest, x2, gates, y_buf)


def _dispatch(experts, n_experts, blk):
    T, K = experts.shape
    A = T * K
    e_flat = experts.reshape(A)
    g = _pick_tile(A, RANK_GROUP)
    onehot = (e_flat.reshape(A // g, g, 1) == jnp.arange(n_experts, dtype=jnp.int32)).astype(BF16)
    tri = jnp.tril(jnp.ones((g, g), BF16))
    within = jnp.einsum('ij,gje->gie', tri, onehot, preferred_element_type=F32)
    group_counts = within[:, -1, :]
    before = jnp.cumsum(group_counts, axis=0) - group_counts
    counts = jnp.sum(group_counts, axis=0).astype(jnp.int32)
    padded = (counts + blk - 1) // blk * blk
    pends = jnp.cumsum(padded)
    pstarts = pends - padded
    row = within + before[:, None, :] - 1.0 + pstarts.astype(F32)
    dest = jnp.sum(onehot.astype(F32) * row, axis=-1).astype(jnp.int32).reshape(A)
    n_blocks = (A + n_experts * blk) // blk
    blk_start = jnp.arange(n_blocks, dtype=jnp.int32) * blk
    blk_expert = jnp.minimum(jnp.sum(blk_start[:, None] >= pends[None, :], axis=1), n_experts - 1)
    last_of_segment = jnp.any((blk_start[:, None] + blk == pends[None, :]) & (padded[None, :] > 0),
                              axis=1)
    zero_blocks = (last_of_segment | (blk_start >= pends[-1])).astype(jnp.int32)
    return dest.reshape(T, K), blk_expert.astype(jnp.int32), zero_blocks


class _Tiles(NamedTuple):
    rows: int
    q_rows: int
    expert_rows: int
    scatter_rows: int
    combine_rows: int


def _tiles(seq, tokens):
    return _Tiles(rows=_pick_tile(seq, 1024), q_rows=_pick_tile(seq, 1024), expert_rows=512,
                  scatter_rows=_pick_tile(tokens, 1024), combine_rows=_pick_tile(tokens, 512))


def kernel(x, mem, norm_mix_g, w_in, b_forget, conv_w, conv_b, lru_wa, lru_ba, lru_wx, lru_bx,
           lru_a_param, fox_q_g, fox_k_g, lru_out_g, fox_out_g, w_out, norm_mem_x_g, norm_mem_g,
           mem_wq, mem_wkv, mem_q_g, mem_k_g, mem_wo, norm_ffn_g, router_group_w, router_group_b,
           router_expert_w, router_expert_b, exp_w_gate, exp_w_up, exp_w_down):
    B, S, D = x.shape
    depth = norm_mix_g.shape[0]
    lw = conv_w.shape[-1]
    n_heads = b_forget.shape[-1]
    dh = fox_q_g.shape[-1]
    mem_dh = mem_q_g.shape[-1]
    mem_heads = mem_wq.shape[-1] // mem_dh
    n_experts = router_expert_w.shape[-1]
    T = B * S
    tm, tq, blk, scatter_tm, combine_tm = _tiles(S, T)

    for l in range(depth):
        y_lru, q, k, v = _in_proj(
            x, norm_mix_g[l], w_in[l], b_forget[l], fox_q_g[l], fox_k_g[l], conv_w[l], conv_b[l],
            lru_wa[l], lru_ba[l], lru_wx[l], lru_bx[l], lru_a_param[l], lru_out_g[l],
            lw=lw, n_heads=n_heads, dh=dh, tm=tm)
        y_fox = _fox_attn(q, k, v, n_heads=n_heads, dh=dh, tq=tq)
        k_mem, v_mem = _mem_kv(mem, norm_mem_g[l], mem_wkv[l], mem_k_g[l], n_heads=mem_heads,
                               dh=mem_dh)
        x2, xn, gates, experts = _out_mem(
            x, y_lru, y_fox, fox_out_g[l], w_out[l], norm_mem_x_g[l], mem_wq[l], mem_q_g[l], k_mem,
            v_mem, mem_wo[l], norm_ffn_g[l], router_group_w[l], router_group_b[l],
            router_expert_w[l], router_expert_b[l], n_heads=mem_heads, dh=mem_dh, tm=tm)

        dest, blk_expert, zero_blocks = _dispatch(
            experts.reshape(T, LANES)[:, :TOP_K], n_experts, blk)
        chunks = D // LANES
        tiled = lambda t: (dest * chunks).reshape(T // t, 1, t * TOP_K)
        x_buf = _scatter_rows(xn, tiled(scatter_tm), zero_blocks, tm=scatter_tm, blk=blk,
                              chunks=chunks)
        y_buf = _experts(x_buf, blk_expert, exp_w_gate[l], exp_w_up[l], exp_w_down[l], blk=blk)
        x = _combine(x2.reshape(T, D), gates.reshape(T, LANES), y_buf, tiled(combine_tm),
                     tm=combine_tm)
        x = x.reshape(B, S, D)
    return x
```

```python
import functools
from typing import NamedTuple

import jax
import jax.numpy as jnp
import numpy as np
from jax import lax
from jax.experimental import pallas as pl
from jax.experimental.pallas import tpu as pltpu

EPS = 1e-6
LRU_C = 8.0
CONV_WIDTH = 4
TOP_K = 2
LANES = 128
SUBLANES = 8
MXU_DIM = 256
HEAD_SLOT = LANES
VMEM_LIMIT = 56 * 1024 * 1024

F32 = jnp.float32
BF16 = jnp.bfloat16
MIN_NORMAL = float(np.finfo(np.float32).tiny)


def _dot(a, b):
    return jnp.dot(a, b, preferred_element_type=F32)


def _rms(x, g):
    return x * lax.rsqrt(jnp.mean(x * x, axis=-1, keepdims=True) + EPS) * g


def _softplus(x):
    return jnp.maximum(x, 0.0) + jnp.log1p(jnp.exp(-jnp.abs(x)))


def _pick_tile(n, target):
    t = min(n, target)
    while n % t:
        t //= 2
    return t


def _store_chunk_rows(ref, val):
    n, d = val.shape
    chunks = d // LANES
    for c in range(chunks):
        ref[pl.ds(c, n, stride=chunks), :] = val[:, c * LANES:(c + 1) * LANES]


def _load_chunk_rows(ref, n, d):
    chunks = d // LANES
    return jnp.concatenate([ref[pl.ds(c, n, stride=chunks), :] for c in range(chunks)], axis=1)


def _in_proj_kernel(x_ref, g_ref, w_ref, wf_ref, bf_ref, gq_ref, gk_ref, pe_ref, cv_ref,
                    cw_ref, cb_ref, wl_ref, ba_ref, bx_ref, ap_ref, og_ref,
                    ylru_ref, q_ref, k_ref, v_ref, carry_ref, tail_ref, h_ref, *, lw, n_heads, dh):
    @pl.when(pl.program_id(1) == 0)
    def _():
        carry_ref[...] = jnp.zeros_like(carry_ref)
        tail_ref[...] = jnp.zeros_like(tail_ref)
        h_ref[...] = jnp.zeros_like(h_ref)

    hs = n_heads * HEAD_SLOT
    x = x_ref[0]
    tm = x.shape[0]
    hb = _rms(x, g_ref[...]).astype(BF16)

    u = _dot(hb, w_ref[:, 0:lw])
    gate = _dot(hb, w_ref[:, lw:2 * lw])
    ylru_ref[0] = _rg_lru_tile(u, gate, cw_ref, cb_ref, wl_ref, ba_ref, bx_ref, ap_ref, og_ref,
                               tail_ref, h_ref).astype(ylru_ref.dtype)

    z = _dot(hb, wf_ref[...]) + bf_ref[...]
    lane = lax.broadcasted_iota(jnp.int32, z.shape, 1)
    row = lax.broadcasted_iota(jnp.int32, z.shape, 0)
    c = jnp.where(lane < n_heads, -_softplus(-z), 0.0)
    d = 1
    while d < tm:
        c = c + jnp.where(row >= d, pltpu.roll(c, d, 0), 0.0)
        d *= 2
    c = c + carry_ref[...]
    carry_ref[...] = c[tm - 1:tm, :]
    c1 = c.astype(BF16).astype(F32)
    r1 = c - c1
    c2 = r1.astype(BF16).astype(F32)
    c3 = (r1 - c2).astype(BF16).astype(F32)
    e = c1 + pltpu.roll(c2, n_heads, 1) + pltpu.roll(c3, 2 * n_heads, 1)
    e = jnp.where(lane == 3 * n_heads, 1.0, e).astype(BF16)
    ext = _dot(e, pe_ref[...])

    fw = n_heads * dh
    lane_t = lax.broadcasted_iota(jnp.int32, (tm, LANES), 1)
    own = (lane_t < dh, lane_t >= dh)

    def to_slots(y):
        parts = []
        for c in range(fw // LANES):
            col = y[:, c * LANES:(c + 1) * LANES]
            parts += [jnp.where(own[0], col, 0.0), jnp.where(own[1], col, 0.0)]
        return jnp.concatenate(parts, axis=-1)

    def head_norm(y, gain):
        parts = []
        for h in range(n_heads):
            blk = y[:, h * HEAD_SLOT:(h + 1) * HEAD_SLOT]
            ss = jnp.sum(blk * blk, axis=-1, keepdims=True) * (1.0 / dh)
            parts.append(blk * lax.rsqrt(ss + EPS))
        return jnp.concatenate(parts, axis=-1) * gain

    q = to_slots(_dot(hb, w_ref[:, 2 * lw:2 * lw + fw]))
    q_ref[0] = (head_norm(q, gq_ref[...]) + ext[:, 0:hs]).astype(BF16)
    k = to_slots(_dot(hb, w_ref[:, 2 * lw + fw:2 * lw + 2 * fw]))
    k_ref[0] = (head_norm(k, gk_ref[...]) + ext[:, hs:2 * hs]).astype(BF16)
    v = to_slots(_dot(hb, w_ref[:, 2 * lw + 2 * fw:2 * lw + 3 * fw]))
    v_ref[0] = (v + cv_ref[...]).astype(BF16)


def _in_proj(x, norm_g, w_in, b_forget, fox_q_g, fox_k_g, conv_w, conv_b, wa, ba, wx, bx, a_param,
             lru_out_g, *, lw, n_heads, dh, tm):
    B, S, D = x.shape
    fw = n_heads * dh
    hs = n_heads * HEAD_SLOT
    assert 2 * dh == HEAD_SLOT and fw % LANES == 0 and 3 * n_heads <= LANES
    n_main = 2 * lw + 3 * fw
    w_all = w_in[:, :n_main].astype(BF16)
    w_f = jnp.pad(w_in[:, n_main:], ((0, 0), (0, LANES - n_heads))).astype(BF16)
    b_f = jnp.pad(b_forget, (0, LANES - n_heads)).reshape(1, LANES)
    scale = dh ** -0.5

    def slot_gains(g):
        even = jnp.pad(g, (0, HEAD_SLOT - dh))
        odd = jnp.pad(g, (HEAD_SLOT - dh, 0))
        return jnp.tile(jnp.concatenate([even, odd]), n_heads // 2).reshape(1, hs)

    gq = slot_gains(fox_q_g * scale)
    gk = slot_gains(fox_k_g)
    ones_row = 3 * n_heads
    pe = np.zeros((LANES, 2 * hs), np.float32)
    cv = np.zeros((1, hs), np.float32)
    for h in range(n_heads):
        x0 = h * HEAD_SLOT + (dh if h % 2 == 0 else 0)
        for j in range(3):
            pe[j * n_heads + h, x0 + j] = 1.0
            pe[ones_row, x0 + 3 + j] = 1.0
            pe[ones_row, hs + x0 + j] = 1.0
            pe[j * n_heads + h, hs + x0 + 3 + j] = -1.0
        cv[0, x0] = 1.0
    pe = jnp.asarray(pe, BF16)

    w_lru = _lru_gate_weights(wa, wx, lw)
    vec = lambda a: a.reshape(1, lw)
    full = lambda shape: pl.BlockSpec(shape, lambda b, s: (0,) * len(shape))
    row = lambda width: pl.BlockSpec((1, tm, width), lambda b, s: (b, s, 0))
    return pl.pallas_call(
        functools.partial(_in_proj_kernel, lw=lw, n_heads=n_heads, dh=dh),
        grid=(B, S // tm),
        in_specs=[row(D), full((1, D)), full(w_all.shape), full(w_f.shape), full((1, LANES)),
                  full((1, hs)), full((1, hs)), full(pe.shape), full((1, hs)),
                  full((CONV_WIDTH, lw)), full((1, lw)), full(w_lru.shape), full((1, lw)),
                  full((1, lw)), full((1, lw)), full((1, lw))],
        out_specs=[row(lw), row(hs), row(hs), row(hs)],
        out_shape=[jax.ShapeDtypeStruct((B, S, lw), BF16),
                   jax.ShapeDtypeStruct((B, S, hs), BF16), jax.ShapeDtypeStruct((B, S, hs), BF16),
                   jax.ShapeDtypeStruct((B, S, hs), BF16)],
        scratch_shapes=[pltpu.VMEM((1, LANES), F32), pltpu.VMEM((SUBLANES, lw), F32),
                        pltpu.VMEM((1, lw), F32)],
        compiler_params=pltpu.CompilerParams(
            dimension_semantics=("parallel", "arbitrary"), vmem_limit_bytes=VMEM_LIMIT),
        name="in_proj",
    )(x, norm_g.reshape(1, D), w_all, w_f, b_f, gq, gk, pe, jnp.asarray(cv),
      conv_w, vec(conv_b), w_lru, vec(ba), vec(bx), vec(a_param), vec(lru_out_g))


def _rg_lru_tile(u, gate, cw_ref, cb_ref, w_ref, ba_ref, bx_ref, ap_ref, og_ref, tail_ref, h_ref):
    ts, lw = u.shape
    ext = jnp.concatenate([tail_ref[...], u], axis=0)
    tail_ref[...] = u[ts - SUBLANES:, :]
    xc = cb_ref[...] + u * cw_ref[CONV_WIDTH - 1:CONV_WIDTH, :]
    for back in range(1, CONV_WIDTH):
        tap = CONV_WIDTH - 1 - back
        xc = xc + pltpu.roll(ext, back, 0)[SUBLANES:, :] * cw_ref[tap:tap + 1, :]

    xb = xc.astype(BF16)
    r_parts, i_parts = [], []
    for j in range(lw // MXU_DIM):
        y = _dot(xb[:, j * MXU_DIM:(j + 1) * MXU_DIM], w_ref[j])
        r_parts.append(y[:, :MXU_DIM])
        i_parts.append(y[:, MXU_DIM:])
    r = jax.nn.sigmoid(jnp.concatenate(r_parts, axis=-1) + ba_ref[...])
    i = jax.nn.sigmoid(jnp.concatenate(i_parts, axis=-1) + bx_ref[...])
    log_a = (-LRU_C) * r * _softplus(-ap_ref[...])
    a = jnp.exp(log_a)
    gap = jnp.maximum(-jnp.tanh(log_a) * (a * a + 1.0), 0.0)
    mult = gap * lax.rsqrt(jnp.maximum(gap, MIN_NORMAL))
    b = mult * (i * xc)

    groups = ts // SUBLANES
    a = a.reshape(groups, SUBLANES, lw)
    b = b.reshape(groups, SUBLANES, lw)
    row = lax.broadcasted_iota(jnp.int32, a.shape, 1)
    d = 1
    while d < SUBLANES:
        keep = row >= d
        a_prev = jnp.where(keep, pltpu.roll(a, d, 1), 1.0)
        b_prev = jnp.where(keep, pltpu.roll(b, d, 1), 0.0)
        b = a * b_prev + b
        a = a * a_prev
        d *= 2
    state = h_ref[...]
    parts = []
    for g in range(groups):
        hg = b[g] + a[g] * state
        state = hg[SUBLANES - 1:, :]
        parts.append(hg)
    h = jnp.concatenate(parts, axis=0)
    h_ref[...] = state

    y = h * jax.nn.gelu(gate)
    return _rms(y, og_ref[...])


def _lru_gate_weights(wa, wx, lw):
    _, bd, _ = wa.shape
    per = MXU_DIM // bd
    n_tiles = lw // MXU_DIM

    def tiles(w):
        w = w.reshape(n_tiles, per, bd, bd)
        eye = jnp.eye(per, dtype=w.dtype)
        return jnp.einsum('tpij,pq->tpiqj', w, eye).reshape(n_tiles, MXU_DIM, MXU_DIM)

    return jnp.concatenate([tiles(wa), tiles(wx)], axis=-1).astype(BF16)


OVERFLOW_GUARD = 1e30


def _fox_attn_kernel(q_ref, k_ref, v_ref, o_ref, vt_ref, m_ref, acc_ref, *, dh, tq):
    n_chunks = vt_ref.shape[0]
    for c in range(n_chunks):
        vt_ref[c] = v_ref[0, c * tq:(c + 1) * tq, :].astype(F32).T.astype(BF16)

    heads = [slice(hh * HEAD_SLOT, (hh + 1) * HEAD_SLOT) for hh in range(2)]
    lax.fori_loop(0, n_chunks, functools.partial(
        _fox_attn_tile, q_ref=q_ref, k_ref=k_ref, o_ref=o_ref, vt_ref=vt_ref, m_ref=m_ref,
        acc_ref=acc_ref, heads=heads, dh=dh, tq=tq), 0)


def _fox_attn_tile(qi, carry, *, q_ref, k_ref, o_ref, vt_ref, m_ref, acc_ref, heads, dh, tq):
    q_rows = pl.ds(pl.multiple_of(qi * tq, tq), tq)

    def scores(j, q, sl, masked):
        start = pl.multiple_of(j * tq, tq)
        st = lax.dot_general(k_ref[0, pl.ds(start, tq), sl], q, (((1,), (1,)), ((), ())),
                             preferred_element_type=F32)
        if masked:
            kpos = lax.broadcasted_iota(jnp.int32, st.shape, 0)
            qpos = lax.broadcasted_iota(jnp.int32, st.shape, 1)
            st = jnp.where(kpos <= qpos, st, -jnp.inf)
        return st

    for hh, sl in enumerate(heads):
        st = scores(qi, q_ref[0, q_rows, sl], sl, True)
        m = jnp.max(st, axis=0, keepdims=True)
        m_ref[hh] = m
        acc_ref[hh] = _dot(vt_ref[qi, sl, :], jnp.exp(st - m).astype(BF16))

    def fast_chunk(j):
        for hh, sl in enumerate(heads):
            pt = jnp.exp(scores(j, q_ref[0, q_rows, sl], sl, False) - m_ref[hh]).astype(BF16)
            acc_ref[hh] += _dot(vt_ref[j, sl, :], pt)

    def fast_pair(i, carry):
        start = pl.multiple_of(2 * i * tq, 2 * tq)
        for hh, sl in enumerate(heads):
            st = lax.dot_general(k_ref[0, pl.ds(start, 2 * tq), sl], q_ref[0, q_rows, sl],
                                 (((1,), (1,)), ((), ())), preferred_element_type=F32)
            pt = jnp.exp(st - m_ref[hh]).astype(BF16)
            vt = jnp.concatenate([vt_ref[2 * i, sl, :], vt_ref[2 * i + 1, sl, :]], axis=1)
            acc_ref[hh] += _dot(vt, pt)
        return carry

    lax.fori_loop(0, qi // 2, fast_pair, 0)

    @pl.when(qi % 2 == 1)
    def _():
        fast_chunk(qi - 1)

    def softmax_sums():
        return jnp.concatenate([acc_ref[0][dh:dh + 1, :], acc_ref[1][0:1, :]], axis=0)

    def normalised():
        sums = softmax_sums()
        return jnp.concatenate([acc_ref[0][0:dh, :] / sums[0:1, :],
                                acc_ref[1][dh:2 * dh, :] / sums[1:2, :]], axis=0)

    out = normalised()
    o_ref[0, q_rows, :] = out.T.astype(o_ref.dtype)
    sums = softmax_sums()
    flag = lambda x: jnp.max(jnp.where(jnp.abs(x) < OVERFLOW_GUARD, 0.0, 1.0),
                             axis=0, keepdims=True)
    bad = jnp.max(jnp.maximum(flag(out), flag(sums)), axis=1, keepdims=True)

    @pl.when(bad[0, 0] > 0.0)
    def _():
        m_ref[...] = jnp.full_like(m_ref, -jnp.inf)
        acc_ref[...] = jnp.zeros_like(acc_ref)

        def chunk(j, masked):
            for hh, sl in enumerate(heads):
                st = scores(j, q_ref[0, q_rows, sl], sl, masked)
                m_old = m_ref[hh]
                m_new = jnp.maximum(m_old, jnp.max(st, axis=0, keepdims=True))
                pt = jnp.exp(st - m_new).astype(BF16)
                acc_ref[hh] = jnp.exp(m_old - m_new) * acc_ref[hh] + _dot(vt_ref[j, sl, :], pt)
                m_ref[hh] = m_new

        def body(j, carry):
            chunk(j, False)
            return carry

        lax.fori_loop(0, qi, body, 0)
        chunk(qi, True)
        o_ref[0, q_rows, :] = normalised().T.astype(o_ref.dtype)

    return carry


def _fox_attn(q, k, v, *, n_heads, dh, tq):
    B, S, _ = q.shape
    nq = S // tq
    pair = 2 * HEAD_SLOT
    qkv_spec = pl.BlockSpec((1, S, pair), lambda b, h: (b, 0, h))
    return pl.pallas_call(
        functools.partial(_fox_attn_kernel, dh=dh, tq=tq),
        grid=(B, n_heads // 2),
        in_specs=[qkv_spec, qkv_spec, qkv_spec],
        out_specs=pl.BlockSpec((1, S, 2 * dh), lambda b, h: (b, 0, h)),
        out_shape=jax.ShapeDtypeStruct((B, S, n_heads * dh), BF16),
        scratch_shapes=[pltpu.VMEM((nq, pair, tq), BF16), pltpu.VMEM((2, 1, tq), F32),
                        pltpu.VMEM((2, HEAD_SLOT, tq), F32)],
        compiler_params=pltpu.CompilerParams(
            dimension_semantics=("parallel", "parallel"), vmem_limit_bytes=VMEM_LIMIT),
        name="fox_attn",
    )(q, k, v)


def _mem_kv_kernel(m_ref, g_ref, w_ref, kg_ref, k_ref, v_ref, *, n_heads, dh):
    mw = n_heads * dh
    mb = _rms(m_ref[0], g_ref[...]).astype(BF16)
    kv = _dot(mb, w_ref[...])
    parts = []
    for h in range(n_heads):
        parts.append(_rms(kv[:, h * dh:(h + 1) * dh], kg_ref[...]))
    k_ref[0] = jnp.concatenate(parts, axis=-1).astype(BF16)
    v_ref[0] = kv[:, mw:].astype(BF16)


def _mem_kv(mem, norm_g, wkv, k_g, *, n_heads, dh):
    B, M, D = mem.shape
    mw = n_heads * dh
    full = lambda shape: pl.BlockSpec(shape, lambda b: (0,) * len(shape))
    out = pl.BlockSpec((1, M, mw), lambda b: (b, 0, 0))
    return pl.pallas_call(
        functools.partial(_mem_kv_kernel, n_heads=n_heads, dh=dh),
        grid=(B,),
        in_specs=[pl.BlockSpec((1, M, D), lambda b: (b, 0, 0)), full((1, D)), full((D, 2 * mw)),
                  full((1, dh))],
        out_specs=[out, out],
        out_shape=[jax.ShapeDtypeStruct((B, M, mw), BF16)] * 2,
        compiler_params=pltpu.CompilerParams(
            dimension_semantics=("parallel",), vmem_limit_bytes=VMEM_LIMIT),
        name="mem_kv",
    )(mem, norm_g.reshape(1, D), wkv.astype(BF16), k_g.reshape(1, dh))


def _out_mem_kernel(x_ref, yl_ref, yf_ref, fg_ref, wol_ref, wof_ref, gx_ref, wq_ref, qg_ref,
                    km_ref, vm_ref, wo_ref, gf_ref, wr_ref, br_ref,
                    x2_ref, xn_ref, gates_ref, experts_ref, *, n_heads, dh, n_groups, per_group):
    yf = _rms(yf_ref[0].astype(F32), fg_ref[...]).astype(BF16)
    x1 = x_ref[0] + _dot(yl_ref[0], wol_ref[...]) + _dot(yf, wof_ref[...])

    q = _dot(_rms(x1, gx_ref[...]).astype(BF16), wq_ref[...])
    outs = []
    for h in range(n_heads):
        sl = slice(h * dh, (h + 1) * dh)
        qh = _rms(q[:, sl], qg_ref[...]).astype(BF16)
        s = lax.dot_general(qh, km_ref[0, :, sl], (((1,), (1,)), ((), ())),
                            preferred_element_type=F32)
        p = jnp.exp(s - jnp.max(s, axis=-1, keepdims=True))
        inv = 1.0 / jnp.sum(p, axis=-1, keepdims=True)
        outs.append(_dot(p.astype(BF16), vm_ref[0, :, sl]) * inv)
    x2 = x1 + _dot(jnp.concatenate(outs, axis=-1).astype(BF16), wo_ref[...])
    x2_ref[0] = x2

    xn = _rms(x2, gf_ref[...])
    _store_chunk_rows(xn_ref, xn)
    logits = _dot(xn.astype(BF16), wr_ref[...]) + br_ref[...]
    lane = lax.broadcasted_iota(jnp.int32, logits.shape, 1).astype(F32)
    neg = -jnp.inf

    def top(vals):
        mx = jnp.max(vals, axis=-1, keepdims=True)
        idx = jnp.min(jnp.where(vals == mx, lane, float(LANES)), axis=-1, keepdims=True)
        return mx, idx

    gl = jnp.where(lane < n_groups, logits, neg)
    g_max, g_idx = top(gl)
    g_w = 1.0 / jnp.sum(jnp.exp(gl - g_max), axis=-1, keepdims=True)
    lo = n_groups + per_group * g_idx
    el = jnp.where((lane >= lo) & (lane < lo + per_group), logits, neg)
    e1, i1 = top(el)
    e2, i2 = top(jnp.where(lane == i1, neg, el))
    t = jnp.exp(e2 - e1)
    w1 = g_w / (1.0 + t)
    w2 = g_w * t / (1.0 + t)
    gates_ref[0] = jnp.where(lane == 0, w1, jnp.where(lane == 1, w2, 0.0))
    experts_ref[0] = jnp.where(lane == 0, i1 - n_groups,
                               jnp.where(lane == 1, i2 - n_groups, 0.0)).astype(jnp.int32)


def _out_mem(x, y_lru, y_fox, fox_out_g, w_out, norm_mem_x_g, mem_wq, mem_q_g, k_mem, v_mem, mem_wo,
             norm_ffn_g, router_group_w, router_group_b, router_expert_w, router_expert_b, *,
             n_heads, dh, tm):
    B, S, D = x.shape
    lw = y_lru.shape[-1]
    fw = y_fox.shape[-1]
    M = k_mem.shape[1]
    mw = n_heads * dh
    n_groups = router_group_w.shape[-1]
    n_experts = router_expert_w.shape[-1]
    w_r = jnp.concatenate([router_group_w, router_expert_w], axis=1)
    w_r = jnp.pad(w_r, ((0, 0), (0, LANES - w_r.shape[1]))).astype(BF16)
    b_r = jnp.concatenate([router_group_b, router_expert_b])
    b_r = jnp.pad(b_r, (0, LANES - b_r.shape[0])).reshape(1, LANES)
    q_gain = (mem_q_g * dh ** -0.5).reshape(1, dh)

    full = lambda shape: pl.BlockSpec(shape, lambda b, s: (0,) * len(shape))
    row = lambda width: pl.BlockSpec((1, tm, width), lambda b, s: (b, s, 0))
    mem_spec = pl.BlockSpec((1, M, mw), lambda b, s: (b, 0, 0))
    return pl.pallas_call(
        functools.partial(_out_mem_kernel, n_heads=n_heads, dh=dh, n_groups=n_groups,
                          per_group=n_experts // n_groups),
        grid=(B, S // tm),
        in_specs=[row(D), row(lw), row(fw), full((1, fw)), full((lw, D)), full((fw, D)),
                  full((1, D)), full((D, mw)), full((1, dh)), mem_spec, mem_spec, full((mw, D)),
                  full((1, D)), full((D, LANES)), full((1, LANES))],
        out_specs=[row(D), pl.BlockSpec((tm * D // LANES, LANES), lambda b, s: (b * (S // tm) + s, 0)),
                   row(LANES), row(LANES)],
        out_shape=[jax.ShapeDtypeStruct((B, S, D), F32),
                   jax.ShapeDtypeStruct((B * S * D // LANES, LANES), F32),
                   jax.ShapeDtypeStruct((B, S, LANES), F32),
                   jax.ShapeDtypeStruct((B, S, LANES), jnp.int32)],
        compiler_params=pltpu.CompilerParams(
            dimension_semantics=("parallel", "parallel"), vmem_limit_bytes=VMEM_LIMIT),
        name="out_mem",
    )(x, y_lru, y_fox, fox_out_g.reshape(1, fw), w_out[:lw].astype(BF16), w_out[lw:].astype(BF16),
      norm_mem_x_g.reshape(1, D), mem_wq.astype(BF16), q_gain, k_mem, v_mem, mem_wo.astype(BF16),
      norm_ffn_g.reshape(1, D), w_r, b_r)


DMA_UNROLL = 8
DMA_THREADS = 2
RANK_GROUP = 256


def _scatter_rows_kernel(zb_ref, idx_ref, x_ref, buf_ref, zero_ref, sem, zsem, *, top_k, chunks):
    tokens = x_ref.shape[0] // chunks

    @pl.when(pl.program_id(0) == 0)
    def _():
        rows = zero_ref.shape[0]
        zero_ref[...] = jnp.zeros_like(zero_ref)

        def copy(j):
            start = pl.multiple_of(j * rows, rows)
            return pltpu.make_async_copy(zero_ref, buf_ref.at[pl.ds(start, rows), :], zsem)

        def issue_zero(j, carry):
            @pl.when(zb_ref[j] == 1)
            def _():
                copy(j).start()
            return carry

        def drain_zero(j, carry):
            @pl.when(zb_ref[j] == 1)
            def _():
                copy(j).wait()
            return carry

        lax.fori_loop(0, zb_ref.shape[0], issue_zero, 0)
        lax.fori_loop(0, zb_ref.shape[0], drain_zero, 0)

    def issue(g, carry):
        for u in range(DMA_UNROLL):
            r = g * DMA_UNROLL + u
            src = x_ref.at[pl.ds(pl.multiple_of(r * chunks, chunks), chunks), :]
            for k in range(top_k):
                row = pl.multiple_of(idx_ref[0, 0, r * top_k + k], chunks)
                pltpu.make_async_copy(src, buf_ref.at[pl.ds(row, chunks), :], sem).start(
                    priority=(u * top_k + k) % DMA_THREADS)
        return carry

    lax.fori_loop(0, tokens // DMA_UNROLL, issue, 0)
    for k in range(top_k):
        pltpu.make_async_copy(x_ref, buf_ref.at[pl.ds(0, tokens * chunks), :], sem).wait()


def _scatter_rows(x, dest, zero_blocks, *, tm, blk, chunks):
    nt = dest.shape[0]
    top_k = dest.shape[2] // tm
    n_blocks = zero_blocks.shape[0]
    grid_spec = pltpu.PrefetchScalarGridSpec(
        num_scalar_prefetch=1,
        grid=(nt,),
        in_specs=[pl.BlockSpec((1, 1, tm * top_k), lambda i, zb: (i, 0, 0), memory_space=pltpu.SMEM),
                  pl.BlockSpec((tm * chunks, LANES), lambda i, zb: (i, 0))],
        out_specs=pl.BlockSpec(memory_space=pl.ANY),
        scratch_shapes=[pltpu.VMEM((blk * chunks, LANES), x.dtype), pltpu.SemaphoreType.DMA,
                        pltpu.SemaphoreType.DMA],
    )
    return pl.pallas_call(
        functools.partial(_scatter_rows_kernel, top_k=top_k, chunks=chunks),
        grid_spec=grid_spec,
        out_shape=jax.ShapeDtypeStruct((n_blocks * blk * chunks, LANES), x.dtype),
        compiler_params=pltpu.CompilerParams(
            dimension_semantics=("arbitrary",), vmem_limit_bytes=VMEM_LIMIT),
        name="scatter_rows",
    )(zero_blocks, dest, x)


def _experts_kernel(be_ref, x_ref, wg_ref, wu_ref, wd_ref, o_ref, wgb_ref, wub_ref, wdb_ref, *, blk):
    i = pl.program_id(0)

    @pl.when((i == 0) | (be_ref[i] != be_ref[jnp.maximum(i - 1, 0)]))
    def _():
        wgb_ref[...] = wg_ref[0].astype(BF16)
        wub_ref[...] = wu_ref[0].astype(BF16)
        wdb_ref[...] = wd_ref[0].astype(BF16)

    d = wg_ref.shape[1]
    xb = _load_chunk_rows(x_ref, blk, d).astype(BF16)
    hdn = jax.nn.silu(_dot(xb, wgb_ref[...])) * _dot(xb, wub_ref[...])
    _store_chunk_rows(o_ref, _dot(hdn.astype(BF16), wdb_ref[...]))


def _experts(x_buf, blk_expert, w_gate, w_up, w_down, *, blk):
    _, D, de = w_gate.shape
    chunks = D // LANES
    n_blocks = x_buf.shape[0] // (blk * chunks)
    rows_spec = pl.BlockSpec((blk * chunks, LANES), lambda i, be: (i, 0))
    grid_spec = pltpu.PrefetchScalarGridSpec(
        num_scalar_prefetch=1,
        grid=(n_blocks,),
        in_specs=[rows_spec,
                  pl.BlockSpec((1, D, de), lambda i, be: (be[i], 0, 0)),
                  pl.BlockSpec((1, D, de), lambda i, be: (be[i], 0, 0)),
                  pl.BlockSpec((1, de, D), lambda i, be: (be[i], 0, 0))],
        out_specs=rows_spec,
        scratch_shapes=[pltpu.VMEM((D, de), BF16), pltpu.VMEM((D, de), BF16),
                        pltpu.VMEM((de, D), BF16)],
    )
    return pl.pallas_call(
        functools.partial(_experts_kernel, blk=blk),
        grid_spec=grid_spec,
        out_shape=jax.ShapeDtypeStruct(x_buf.shape, F32),
        compiler_params=pltpu.CompilerParams(
            dimension_semantics=("arbitrary",), vmem_limit_bytes=VMEM_LIMIT),
        name="experts",
    )(blk_expert, x_buf, w_gate, w_up, w_down)


def _combine_kernel(idx_ref, idx_next_ref, x_ref, g_ref, y_ref, o_ref, buf_ref, sem, *, top_k, chunks):
    i = pl.program_id(0)
    n = pl.num_programs(0)
    slot = lax.rem(i, 2)
    tokens, d = x_ref.shape

    def request(ids_ref, s):
        def issue(g, carry):
            for u in range(DMA_UNROLL):
                r = g * DMA_UNROLL + u
                for k in range(top_k):
                    row = pl.multiple_of(ids_ref[0, 0, r * top_k + k], chunks)
                    dst = buf_ref.at[s, k, pl.ds(pl.multiple_of(r * chunks, chunks), chunks), :]
                    pltpu.make_async_copy(y_ref.at[pl.ds(row, chunks), :], dst, sem.at[s]).start(
                        priority=(u * top_k + k) % DMA_THREADS)
            return carry
        lax.fori_loop(0, tokens // DMA_UNROLL, issue, 0)

    @pl.when(i == 0)
    def _():
        request(idx_ref, 0)

    @pl.when(i + 1 < n)
    def _():
        request(idx_next_ref, 1 - slot)

    for k in range(top_k):
        pltpu.make_async_copy(y_ref.at[pl.ds(0, tokens * chunks), :], buf_ref.at[slot, k],
                              sem.at[slot]).wait()
    out = x_ref[...]
    for k in range(top_k):
        out = out + g_ref[:, k:k + 1] * _load_chunk_rows(buf_ref.at[slot, k], tokens, d)
    o_ref[...] = out


def _combine(x2, gates, y_buf, dest, *, tm):
    T, D = x2.shape
    chunks = D // LANES
    nt = dest.shape[0]
    top_k = dest.shape[2] // tm
    idx_spec = lambda step: pl.BlockSpec(
        (1, 1, tm * top_k), lambda i: (jnp.minimum(i + step, nt - 1), 0, 0), memory_space=pltpu.SMEM)
    return pl.pallas_call(
        functools.partial(_combine_kernel, top_k=top_k, chunks=chunks),
        grid=(nt,),
        in_specs=[idx_spec(0), idx_spec(1),
                  pl.BlockSpec((tm, D), lambda i: (i, 0)),
                  pl.BlockSpec((tm, LANES), lambda i: (i, 0)),
                  pl.BlockSpec(memory_space=pl.ANY)],
        out_specs=pl.BlockSpec((tm, D), lambda i: (i, 0)),
        out_shape=jax.ShapeDtypeStruct((T, D), F32),
        scratch_shapes=[pltpu.VMEM((2, top_k, tm * chunks, LANES), F32),
                        pltpu.SemaphoreType.DMA((2,))],
        compiler_params=pltpu.CompilerParams(
            dimension_semantics=("arbitrary",), vmem_limit_bytes=VMEM_LIMIT),
        name="combine",
    )(dest, dest, x2, gates, y_buf)


def _dispatch(experts, n_experts, blk):
    T, K = experts.shape
    A = T * K
    e_flat = experts.reshape(A)
    g = _pick_tile(A, RANK_GROUP)
    onehot = (e_flat.reshape(A // g, g, 1) == jnp.arange(n_experts, dtype=jnp.int32)).astype(BF16)
    tri = jnp.tril(jnp.ones((g, g), BF16))
    within = jnp.einsum('ij,gje->gie', tri, onehot, preferred_element_type=F32)
    group_counts = within[:, -1, :]
    before = jnp.cumsum(group_counts, axis=0) - group_counts
    counts = jnp.sum(group_counts, axis=0).astype(jnp.int32)
    padded = (counts + blk - 1) // blk * blk
    pends = jnp.cumsum(padded)
    pstarts = pends - padded
    row = within + before[:, None, :] - 1.0 + pstarts.astype(F32)
    dest = jnp.sum(onehot.astype(F32) * row, axis=-1).astype(jnp.int32).reshape(A)
    n_blocks = (A + n_experts * blk) // blk
    blk_start = jnp.arange(n_blocks, dtype=jnp.int32) * blk
    blk_expert = jnp.minimum(jnp.sum(blk_start[:, None] >= pends[None, :], axis=1), n_experts - 1)
    last_of_segment = jnp.any((blk_start[:, None] + blk == pends[None, :]) & (padded[None, :] > 0),
                              axis=1)
    zero_blocks = (last_of_segment | (blk_start >= pends[-1])).astype(jnp.int32)
    return dest.reshape(T, K), blk_expert.astype(jnp.int32), zero_blocks


class _Tiles(NamedTuple):
    rows: int
    q_rows: int
    expert_rows: int
    scatter_rows: int
    combine_rows: int


def _tiles(seq, tokens):
    return _Tiles(rows=_pick_tile(seq, 1024), q_rows=_pick_tile(seq, 1024), expert_rows=512,
                  scatter_rows=_pick_tile(tokens, 1024), combine_rows=_pick_tile(tokens, 512))


def kernel(x, mem, norm_mix_g, w_in, b_forget, conv_w, conv_b, lru_wa, lru_ba, lru_wx, lru_bx,
           lru_a_param, fox_q_g, fox_k_g, lru_out_g, fox_out_g, w_out, norm_mem_x_g, norm_mem_g,
           mem_wq, mem_wkv, mem_q_g, mem_k_g, mem_wo, norm_ffn_g, router_group_w, router_group_b,
           router_expert_w, router_expert_b, exp_w_gate, exp_w_up, exp_w_down):
    B, S, D = x.shape
    depth = norm_mix_g.shape[0]
    lw = conv_w.shape[-1]
    n_heads = b_forget.shape[-1]
    dh = fox_q_g.shape[-1]
    mem_dh = mem_q_g.shape[-1]
    mem_heads = mem_wq.shape[-1] // mem_dh
    n_experts = router_expert_w.shape[-1]
    T = B * S
    tm, tq, blk, scatter_tm, combine_tm = _tiles(S, T)

    for l in range(depth):
        y_lru, q, k, v = _in_proj(
            x, norm_mix_g[l], w_in[l], b_forget[l], fox_q_g[l], fox_k_g[l], conv_w[l], conv_b[l],
            lru_wa[l], lru_ba[l], lru_wx[l], lru_bx[l], lru_a_param[l], lru_out_g[l],
            lw=lw, n_heads=n_heads, dh=dh, tm=tm)
        y_fox = _fox_attn(q, k, v, n_heads=n_heads, dh=dh, tq=tq)
        k_mem, v_mem = _mem_kv(mem, norm_mem_g[l], mem_wkv[l], mem_k_g[l], n_heads=mem_heads,
                               dh=mem_dh)
        x2, xn, gates, experts = _out_mem(
            x, y_lru, y_fox, fox_out_g[l], w_out[l], norm_mem_x_g[l], mem_wq[l], mem_q_g[l], k_mem,
            v_mem, mem_wo[l], norm_ffn_g[l], router_group_w[l], router_group_b[l],
            router_expert_w[l], router_expert_b[l], n_heads=mem_heads, dh=mem_dh, tm=tm)

        dest, blk_expert, zero_blocks = _dispatch(
            experts.reshape(T, LANES)[:, :TOP_K], n_experts, blk)
        chunks = D // LANES
        tiled = lambda t: (dest * chunks).reshape(T // t, 1, t * TOP_K)
        x_buf = _scatter_rows(xn, tiled(scatter_tm), zero_blocks, tm=scatter_tm, blk=blk,
                              chunks=chunks)
        y_buf = _experts(x_buf, blk_expert, exp_w_gate[l], exp_w_up[l], exp_w_down[l], blk=blk)
        x = _combine(x2.reshape(T, D), gates.reshape(T, LANES), y_buf, tiled(combine_tm),
                     tm=combine_tm)
        x = x.reshape(B, S, D)
    return x
```

```python
import functools
from typing import NamedTuple

import jax
import jax.numpy as jnp
import numpy as np
from jax import lax
from jax.experimental import pallas as pl
from jax.experimental.pallas import tpu as pltpu

EPS = 1e-6
LRU_C = 8.0
CONV_WIDTH = 4
TOP_K = 2
LANES = 128
SUBLANES = 8
MXU_DIM = 256
HEAD_SLOT = LANES
VMEM_LIMIT = 56 * 1024 * 1024

F32 = jnp.float32
BF16 = jnp.bfloat16
MIN_NORMAL = float(np.finfo(np.float32).tiny)


def _dot(a, b):
    return jnp.dot(a, b, preferred_element_type=F32)


def _rms(x, g):
    return x * lax.rsqrt(jnp.mean(x * x, axis=-1, keepdims=True) + EPS) * g


def _softplus(x):
    return jnp.maximum(x, 0.0) + jnp.log1p(jnp.exp(-jnp.abs(x)))


def _pick_tile(n, target):
    t = min(n, target)
    while n % t:
        t //= 2
    return t


def _store_chunk_rows(ref, val):
    n, d = val.shape
    chunks = d // LANES
    for c in range(chunks):
        ref[pl.ds(c, n, stride=chunks), :] = val[:, c * LANES:(c + 1) * LANES]


def _load_chunk_rows(ref, n, d):
    chunks = d // LANES
    return jnp.concatenate([ref[pl.ds(c, n, stride=chunks), :] for c in range(chunks)], axis=1)


def _in_proj_kernel(x_ref, g_ref, w_ref, wf_ref, bf_ref, gq_ref, gk_ref, pe_ref, cv_ref,
                    cw_ref, cb_ref, wl_ref, ba_ref, bx_ref, ap_ref, og_ref,
                    ylru_ref, q_ref, k_ref, v_ref, carry_ref, tail_ref, h_ref, *, lw, n_heads, dh):
    @pl.when(pl.program_id(1) == 0)
    def _():
        carry_ref[...] = jnp.zeros_like(carry_ref)
        tail_ref[...] = jnp.zeros_like(tail_ref)
        h_ref[...] = jnp.zeros_like(h_ref)

    hs = n_heads * HEAD_SLOT
    x = x_ref[0]
    tm = x.shape[0]
    hb = _rms(x, g_ref[...]).astype(BF16)

    u = _dot(hb, w_ref[:, 0:lw])
    gate = _dot(hb, w_ref[:, lw:2 * lw])
    ylru_ref[0] = _rg_lru_tile(u, gate, cw_ref, cb_ref, wl_ref, ba_ref, bx_ref, ap_ref, og_ref,
                               tail_ref, h_ref).astype(ylru_ref.dtype)

    z = _dot(hb, wf_ref[...]) + bf_ref[...]
    lane = lax.broadcasted_iota(jnp.int32, z.shape, 1)
    row = lax.broadcasted_iota(jnp.int32, z.shape, 0)
    c = jnp.where(lane < n_heads, -_softplus(-z), 0.0)
    d = 1
    while d < tm:
        c = c + jnp.where(row >= d, pltpu.roll(c, d, 0), 0.0)
        d *= 2
    c = c + carry_ref[...]
    carry_ref[...] = c[tm - 1:tm, :]
    c1 = c.astype(BF16).astype(F32)
    r1 = c - c1
    c2 = r1.astype(BF16).astype(F32)
    c3 = (r1 - c2).astype(BF16).astype(F32)
    e = c1 + pltpu.roll(c2, n_heads, 1) + pltpu.roll(c3, 2 * n_heads, 1)
    e = jnp.where(lane == 3 * n_heads, 1.0, e).astype(BF16)
    ext = _dot(e, pe_ref[...])

    fw = n_heads * dh
    lane_t = lax.broadcasted_iota(jnp.int32, (tm, LANES), 1)
    own = (lane_t < dh, lane_t >= dh)

    def to_slots(y):
        parts = []
        for c in range(fw // LANES):
            col = y[:, c * LANES:(c + 1) * LANES]
            parts += [jnp.where(own[0], col, 0.0), jnp.where(own[1], col, 0.0)]
        return jnp.concatenate(parts, axis=-1)

    def head_norm(y, gain):
        parts = []
        for h in range(n_heads):
            blk = y[:, h * HEAD_SLOT:(h + 1) * HEAD_SLOT]
            ss = jnp.sum(blk * blk, axis=-1, keepdims=True) * (1.0 / dh)
            parts.append(blk * lax.rsqrt(ss + EPS))
        return jnp.concatenate(parts, axis=-1) * gain

    q = to_slots(_dot(hb, w_ref[:, 2 * lw:2 * lw + fw]))
    q_ref[0] = (head_norm(q, gq_ref[...]) + ext[:, 0:hs]).astype(BF16)
    k = to_slots(_dot(hb, w_ref[:, 2 * lw + fw:2 * lw + 2 * fw]))
    k_ref[0] = (head_norm(k, gk_ref[...]) + ext[:, hs:2 * hs]).astype(BF16)
    v = to_slots(_dot(hb, w_ref[:, 2 * lw + 2 * fw:2 * lw + 3 * fw]))
    v_ref[0] = (v + cv_ref[...]).astype(BF16)


def _in_proj(x, norm_g, w_in, b_forget, fox_q_g, fox_k_g, conv_w, conv_b, wa, ba, wx, bx, a_param,
             lru_out_g, *, lw, n_heads, dh, tm):
    B, S, D = x.shape
    fw = n_heads * dh
    hs = n_heads * HEAD_SLOT
    assert 2 * dh == HEAD_SLOT and fw % LANES == 0 and 3 * n_heads <= LANES
    n_main = 2 * lw + 3 * fw
    w_all = w_in[:, :n_main].astype(BF16)
    w_f = jnp.pad(w_in[:, n_main:], ((0, 0), (0, LANES - n_heads))).astype(BF16)
    b_f = jnp.pad(b_forget, (0, LANES - n_heads)).reshape(1, LANES)
    scale = dh ** -0.5

    def slot_gains(g):
        even = jnp.pad(g, (0, HEAD_SLOT - dh))
        odd = jnp.pad(g, (HEAD_SLOT - dh, 0))
        return jnp.tile(jnp.concatenate([even, odd]), n_heads // 2).reshape(1, hs)

    gq = slot_gains(fox_q_g * scale)
    gk = slot_gains(fox_k_g)
    ones_row = 3 * n_heads
    pe = np.zeros((LANES, 2 * hs), np.float32)
    cv = np.zeros((1, hs), np.float32)
    for h in range(n_heads):
        x0 = h * HEAD_SLOT + (dh if h % 2 == 0 else 0)
        for j in range(3):
            pe[j * n_heads + h, x0 + j] = 1.0
            pe[ones_row, x0 + 3 + j] = 1.0
            pe[ones_row, hs + x0 + j] = 1.0
            pe[j * n_heads + h, hs + x0 + 3 + j] = -1.0
        cv[0, x0] = 1.0
    pe = jnp.asarray(pe, BF16)

    w_lru = _lru_gate_weights(wa, wx, lw)
    vec = lambda a: a.reshape(1, lw)
    full = lambda shape: pl.BlockSpec(shape, lambda b, s: (0,) * len(shape))
    row = lambda width: pl.BlockSpec((1, tm, width), lambda b, s: (b, s, 0))
    return pl.pallas_call(
        functools.partial(_in_proj_kernel, lw=lw, n_heads=n_heads, dh=dh),
        grid=(B, S // tm),
        in_specs=[row(D), full((1, D)), full(w_all.shape), full(w_f.shape), full((1, LANES)),
                  full((1, hs)), full((1, hs)), full(pe.shape), full((1, hs)),
                  full((CONV_WIDTH, lw)), full((1, lw)), full(w_lru.shape), full((1, lw)),
                  full((1, lw)), full((1, lw)), full((1, lw))],
        out_specs=[row(lw), row(hs), row(hs), row(hs)],
        out_shape=[jax.ShapeDtypeStruct((B, S, lw), BF16),
                   jax.ShapeDtypeStruct((B, S, hs), BF16), jax.ShapeDtypeStruct((B, S, hs), BF16),
                   jax.ShapeDtypeStruct((B, S, hs), BF16)],
        scratch_shapes=[pltpu.VMEM((1, LANES), F32), pltpu.VMEM((SUBLANES, lw), F32),
                        pltpu.VMEM((1, lw), F32)],
        compiler_params=pltpu.CompilerParams(
            dimension_semantics=("parallel", "arbitrary"), vmem_limit_bytes=VMEM_LIMIT),
        name="in_proj",
    )(x, norm_g.reshape(1, D), w_all, w_f, b_f, gq, gk, pe, jnp.asarray(cv),
      conv_w, vec(conv_b), w_lru, vec(ba), vec(bx), vec(a_param), vec(lru_out_g))


def _rg_lru_tile(u, gate, cw_ref, cb_ref, w_ref, ba_ref, bx_ref, ap_ref, og_ref, tail_ref, h_ref):
    ts, lw = u.shape
    ext = jnp.concatenate([tail_ref[...], u], axis=0)
    tail_ref[...] = u[ts - SUBLANES:, :]
    xc = cb_ref[...] + u * cw_ref[CONV_WIDTH - 1:CONV_WIDTH, :]
    for back in range(1, CONV_WIDTH):
        tap = CONV_WIDTH - 1 - back
        xc = xc + pltpu.roll(ext, back, 0)[SUBLANES:, :] * cw_ref[tap:tap + 1, :]

    xb = xc.astype(BF16)
    r_parts, i_parts = [], []
    for j in range(lw // MXU_DIM):
        y = _dot(xb[:, j * MXU_DIM:(j + 1) * MXU_DIM], w_ref[j])
        r_parts.append(y[:, :MXU_DIM])
        i_parts.append(y[:, MXU_DIM:])
    r = jax.nn.sigmoid(jnp.concatenate(r_parts, axis=-1) + ba_ref[...])
    i = jax.nn.sigmoid(jnp.concatenate(i_parts, axis=-1) + bx_ref[...])
    log_a = (-LRU_C) * r * _softplus(-ap_ref[...])
    a = jnp.exp(log_a)
    gap = jnp.maximum(-jnp.tanh(log_a) * (a * a + 1.0), 0.0)
    mult = gap * lax.rsqrt(jnp.maximum(gap, MIN_NORMAL))
    b = mult * (i * xc)

    groups = ts // SUBLANES
    a = a.reshape(groups, SUBLANES, lw)
    b = b.reshape(groups, SUBLANES, lw)
    row = lax.broadcasted_iota(jnp.int32, a.shape, 1)
    d = 1
    while d < SUBLANES:
        keep = row >= d
        a_prev = jnp.where(keep, pltpu.roll(a, d, 1), 1.0)
        b_prev = jnp.where(keep, pltpu.roll(b, d, 1), 0.0)
        b = a * b_prev + b
        a = a * a_prev
        d *= 2
    state = h_ref[...]
    parts = []
    for g in range(groups):
        hg = b[g] + a[g] * state
        state = hg[SUBLANES - 1:, :]
        parts.append(hg)
    h = jnp.concatenate(parts, axis=0)
    h_ref[...] = state

    y = h * jax.nn.gelu(gate)
    return _rms(y, og_ref[...])


def _lru_gate_weights(wa, wx, lw):
    _, bd, _ = wa.shape
    per = MXU_DIM // bd
    n_tiles = lw // MXU_DIM

    def tiles(w):
        w = w.reshape(n_tiles, per, bd, bd)
        eye = jnp.eye(per, dtype=w.dtype)
        return jnp.einsum('tpij,pq->tpiqj', w, eye).reshape(n_tiles, MXU_DIM, MXU_DIM)

    return jnp.concatenate([tiles(wa), tiles(wx)], axis=-1).astype(BF16)


OVERFLOW_GUARD = 1e30


def _fox_attn_kernel(q_ref, k_ref, v_ref, o_ref, vt_ref, m_ref, acc_ref, *, dh, tq):
    n_chunks = vt_ref.shape[0]
    for c in range(n_chunks):
        vt_ref[c] = v_ref[0, c * tq:(c + 1) * tq, :].astype(F32).T.astype(BF16)

    heads = [slice(hh * HEAD_SLOT, (hh + 1) * HEAD_SLOT) for hh in range(2)]
    lax.fori_loop(0, n_chunks, functools.partial(
        _fox_attn_tile, q_ref=q_ref, k_ref=k_ref, o_ref=o_ref, vt_ref=vt_ref, m_ref=m_ref,
        acc_ref=acc_ref, heads=heads, dh=dh, tq=tq), 0)


def _fox_attn_tile(qi, carry, *, q_ref, k_ref, o_ref, vt_ref, m_ref, acc_ref, heads, dh, tq):
    q_rows = pl.ds(pl.multiple_of(qi * tq, tq), tq)

    def scores(j, q, sl, masked):
        start = pl.multiple_of(j * tq, tq)
        st = lax.dot_general(k_ref[0, pl.ds(start, tq), sl], q, (((1,), (1,)), ((), ())),
                             preferred_element_type=F32)
        if masked:
            kpos = lax.broadcasted_iota(jnp.int32, st.shape, 0)
            qpos = lax.broadcasted_iota(jnp.int32, st.shape, 1)
            st = jnp.where(kpos <= qpos, st, -jnp.inf)
        return st

    for hh, sl in enumerate(heads):
        st = scores(qi, q_ref[0, q_rows, sl], sl, True)
        m = jnp.max(st, axis=0, keepdims=True)
        m_ref[hh] = m
        acc_ref[hh] = _dot(vt_ref[qi, sl, :], jnp.exp(st - m).astype(BF16))

    def fast_chunk(j):
        for hh, sl in enumerate(heads):
            pt = jnp.exp(scores(j, q_ref[0, q_rows, sl], sl, False) - m_ref[hh]).astype(BF16)
            acc_ref[hh] += _dot(vt_ref[j, sl, :], pt)

    def fast_pair(i, carry):
        start = pl.multiple_of(2 * i * tq, 2 * tq)
        for hh, sl in enumerate(heads):
            st = lax.dot_general(k_ref[0, pl.ds(start, 2 * tq), sl], q_ref[0, q_rows, sl],
                                 (((1,), (1,)), ((), ())), preferred_element_type=F32)
            pt = jnp.exp(st - m_ref[hh]).astype(BF16)
            vt = jnp.concatenate([vt_ref[2 * i, sl, :], vt_ref[2 * i + 1, sl, :]], axis=1)
            acc_ref[hh] += _dot(vt, pt)
        return carry

    lax.fori_loop(0, qi // 2, fast_pair, 0)

    @pl.when(qi % 2 == 1)
    def _():
        fast_chunk(qi - 1)

    def softmax_sums():
        return jnp.concatenate([acc_ref[0][dh:dh + 1, :], acc_ref[1][0:1, :]], axis=0)

    def normalised():
        sums = softmax_sums()
        return jnp.concatenate([acc_ref[0][0:dh, :] / sums[0:1, :],
                                acc_ref[1][dh:2 * dh, :] / sums[1:2, :]], axis=0)

    out = normalised()
    o_ref[0, q_rows, :] = out.T.astype(o_ref.dtype)
    sums = softmax_sums()
    flag = lambda x: jnp.max(jnp.where(jnp.abs(x) < OVERFLOW_GUARD, 0.0, 1.0),
                             axis=0, keepdims=True)
    bad = jnp.max(jnp.maximum(flag(out), flag(sums)), axis=1, keepdims=True)

    @pl.when(bad[0, 0] > 0.0)
    def _():
        m_ref[...] = jnp.full_like(m_ref, -jnp.inf)
        acc_ref[...] = jnp.zeros_like(acc_ref)

        def chunk(j, masked):
            for hh, sl in enumerate(heads):
                st = scores(j, q_ref[0, q_rows, sl], sl, masked)
                m_old = m_ref[hh]
                m_new = jnp.maximum(m_old, jnp.max(st, axis=0, keepdims=True))
                pt = jnp.exp(st - m_new).astype(BF16)
                acc_ref[hh] = jnp.exp(m_old - m_new) * acc_ref[hh] + _dot(vt_ref[j, sl, :], pt)
                m_ref[hh] = m_new

        def body(j, carry):
            chunk(j, False)
            return carry

        lax.fori_loop(0, qi, body, 0)
        chunk(qi, True)
        o_ref[0, q_rows, :] = normalised().T.astype(o_ref.dtype)

    return carry


def _fox_attn(q, k, v, *, n_heads, dh, tq):
    B, S, _ = q.shape
    nq = S // tq
    pair = 2 * HEAD_SLOT
    qkv_spec = pl.BlockSpec((1, S, pair), lambda b, h: (b, 0, h))
    return pl.pallas_call(
        functools.partial(_fox_attn_kernel, dh=dh, tq=tq),
        grid=(B, n_heads // 2),
        in_specs=[qkv_spec, qkv_spec, qkv_spec],
        out_specs=pl.BlockSpec((1, S, 2 * dh), lambda b, h: (b, 0, h)),
        out_shape=jax.ShapeDtypeStruct((B, S, n_heads * dh), BF16),
        scratch_shapes=[pltpu.VMEM((nq, pair, tq), BF16), pltpu.VMEM((2, 1, tq), F32),
                        pltpu.VMEM((2, HEAD_SLOT, tq), F32)],
        compiler_params=pltpu.CompilerParams(
            dimension_semantics=("parallel", "parallel"), vmem_limit_bytes=VMEM_LIMIT),
        name="fox_attn",
    )(q, k, v)


def _mem_kv_kernel(m_ref, g_ref, w_ref, kg_ref, k_ref, v_ref, *, n_heads, dh):
    mw = n_heads * dh
    mb = _rms(m_ref[0], g_ref[...]).astype(BF16)
    kv = _dot(mb, w_ref[...])
    parts = []
    for h in range(n_heads):
        parts.append(_rms(kv[:, h * dh:(h + 1) * dh], kg_ref[...]))
    k_ref[0] = jnp.concatenate(parts, axis=-1).astype(BF16)
    v_ref[0] = kv[:, mw:].astype(BF16)


def _mem_kv(mem, norm_g, wkv, k_g, *, n_heads, dh):
    B, M, D = mem.shape
    mw = n_heads * dh
    full = lambda shape: pl.BlockSpec(shape, lambda b: (0,) * len(shape))
    out = pl.BlockSpec((1, M, mw), lambda b: (b, 0, 0))
    return pl.pallas_call(
        functools.partial(_mem_kv_kernel, n_heads=n_heads, dh=dh),
        grid=(B,),
        in_specs=[pl.BlockSpec((1, M, D), lambda b: (b, 0, 0)), full((1, D)), full((D, 2 * mw)),
                  full((1, dh))],
        out_specs=[out, out],
        out_shape=[jax.ShapeDtypeStruct((B, M, mw), BF16)] * 2,
        compiler_params=pltpu.CompilerParams(
            dimension_semantics=("parallel",), vmem_limit_bytes=VMEM_LIMIT),
        name="mem_kv",
    )(mem, norm_g.reshape(1, D), wkv.astype(BF16), k_g.reshape(1, dh))


def _out_mem_kernel(x_ref, yl_ref, yf_ref, fg_ref, wol_ref, wof_ref, gx_ref, wq_ref, qg_ref,
                    km_ref, vm_ref, wo_ref, gf_ref, wr_ref, br_ref,
                    x2_ref, xn_ref, gates_ref, experts_ref, *, n_heads, dh, n_groups, per_group):
    yf = _rms(yf_ref[0].astype(F32), fg_ref[...]).astype(BF16)
    x1 = x_ref[0] + _dot(yl_ref[0], wol_ref[...]) + _dot(yf, wof_ref[...])

    q = _dot(_rms(x1, gx_ref[...]).astype(BF16), wq_ref[...])
    outs = []
    for h in range(n_heads):
        sl = slice(h * dh, (h + 1) * dh)
        qh = _rms(q[:, sl], qg_ref[...]).astype(BF16)
        s = lax.dot_general(qh, km_ref[0, :, sl], (((1,), (1,)), ((), ())),
                            preferred_element_type=F32)
        p = jnp.exp(s - jnp.max(s, axis=-1, keepdims=True))
        inv = 1.0 / jnp.sum(p, axis=-1, keepdims=True)
        outs.append(_dot(p.astype(BF16), vm_ref[0, :, sl]) * inv)
    x2 = x1 + _dot(jnp.concatenate(outs, axis=-1).astype(BF16), wo_ref[...])
    x2_ref[0] = x2

    xn = _rms(x2, gf_ref[...])
    _store_chunk_rows(xn_ref, xn)
    logits = _dot(xn.astype(BF16), wr_ref[...]) + br_ref[...]
    lane = lax.broadcasted_iota(jnp.int32, logits.shape, 1).astype(F32)
    neg = -jnp.inf

    def top(vals):
        mx = jnp.max(vals, axis=-1, keepdims=True)
        idx = jnp.min(jnp.where(vals == mx, lane, float(LANES)), axis=-1, keepdims=True)
        return mx, idx

    gl = jnp.where(lane < n_groups, logits, neg)
    g_max, g_idx = top(gl)
    g_w = 1.0 / jnp.sum(jnp.exp(gl - g_max), axis=-1, keepdims=True)
    lo = n_groups + per_group * g_idx
    el = jnp.where((lane >= lo) & (lane < lo + per_group), logits, neg)
    e1, i1 = top(el)
    e2, i2 = top(jnp.where(lane == i1, neg, el))
    t = jnp.exp(e2 - e1)
    w1 = g_w / (1.0 + t)
    w2 = g_w * t / (1.0 + t)
    gates_ref[0] = jnp.where(lane == 0, w1, jnp.where(lane == 1, w2, 0.0))
    experts_ref[0] = jnp.where(lane == 0, i1 - n_groups,
                               jnp.where(lane == 1, i2 - n_groups, 0.0)).astype(jnp.int32)


def _out_mem(x, y_lru, y_fox, fox_out_g, w_out, norm_mem_x_g, mem_wq, mem_q_g, k_mem, v_mem, mem_wo,
             norm_ffn_g, router_group_w, router_group_b, router_expert_w, router_expert_b, *,
             n_heads, dh, tm):
    B, S, D = x.shape
    lw = y_lru.shape[-1]
    fw = y_fox.shape[-1]
    M = k_mem.shape[1]
    mw = n_heads * dh
    n_groups = router_group_w.shape[-1]
    n_experts = router_expert_w.shape[-1]
    w_r = jnp.concatenate([router_group_w, router_expert_w], axis=1)
    w_r = jnp.pad(w_r, ((0, 0), (0, LANES - w_r.shape[1]))).astype(BF16)
    b_r = jnp.concatenate([router_group_b, router_expert_b])
    b_r = jnp.pad(b_r, (0, LANES - b_r.shape[0])).reshape(1, LANES)
    q_gain = (mem_q_g * dh ** -0.5).reshape(1, dh)

    full = lambda shape: pl.BlockSpec(shape, lambda b, s: (0,) * len(shape))
    row = lambda width: pl.BlockSpec((1, tm, width), lambda b, s: (b, s, 0))
    mem_spec = pl.BlockSpec((1, M, mw), lambda b, s: (b, 0, 0))
    return pl.pallas_call(
        functools.partial(_out_mem_kernel, n_heads=n_heads, dh=dh, n_groups=n_groups,
                          per_group=n_experts // n_groups),
        grid=(B, S // tm),
        in_specs=[row(D), row(lw), row(fw), full((1, fw)), full((lw, D)), full((fw, D)),
                  full((1, D)), full((D, mw)), full((1, dh)), mem_spec, mem_spec, full((mw, D)),
                  full((1, D)), full((D, LANES)), full((1, LANES))],
        out_specs=[row(D), pl.BlockSpec((tm * D // LANES, LANES), lambda b, s: (b * (S // tm) + s, 0)),
                   row(LANES), row(LANES)],
        out_shape=[jax.ShapeDtypeStruct((B, S, D), F32),
                   jax.ShapeDtypeStruct((B * S * D // LANES, LANES), F32),
                   jax.ShapeDtypeStruct((B, S, LANES), F32),
                   jax.ShapeDtypeStruct((B, S, LANES), jnp.int32)],
        compiler_params=pltpu.CompilerParams(
            dimension_semantics=("parallel", "parallel"), vmem_limit_bytes=VMEM_LIMIT),
        name="out_mem",
    )(x, y_lru, y_fox, fox_out_g.reshape(1, fw), w_out[:lw].astype(BF16), w_out[lw:].astype(BF16),
      norm_mem_x_g.reshape(1, D), mem_wq.astype(BF16), q_gain, k_mem, v_mem, mem_wo.astype(BF16),
      norm_ffn_g.reshape(1, D), w_r, b_r)


DMA_UNROLL = 8
DMA_THREADS = 2
RANK_GROUP = 256


def _scatter_rows_kernel(zb_ref, idx_ref, x_ref, buf_ref, zero_ref, sem, zsem, *, top_k, chunks):
    tokens = x_ref.shape[0] // chunks

    @pl.when(pl.program_id(0) == 0)
    def _():
        rows = zero_ref.shape[0]
        zero_ref[...] = jnp.zeros_like(zero_ref)

        def copy(j):
            start = pl.multiple_of(j * rows, rows)
            return pltpu.make_async_copy(zero_ref, buf_ref.at[pl.ds(start, rows), :], zsem)

        def issue_zero(j, carry):
            @pl.when(zb_ref[j] == 1)
            def _():
                copy(j).start()
            return carry

        def drain_zero(j, carry):
            @pl.when(zb_ref[j] == 1)
            def _():
                copy(j).wait()
            return carry

        lax.fori_loop(0, zb_ref.shape[0], issue_zero, 0)
        lax.fori_loop(0, zb_ref.shape[0], drain_zero, 0)

    def issue(g, carry):
        for u in range(DMA_UNROLL):
            r = g * DMA_UNROLL + u
            src = x_ref.at[pl.ds(pl.multiple_of(r * chunks, chunks), chunks), :]
            for k in range(top_k):
                row = pl.multiple_of(idx_ref[0, 0, r * top_k + k], chunks)
                pltpu.make_async_copy(src, buf_ref.at[pl.ds(row, chunks), :], sem).start(
                    priority=(u * top_k + k) % DMA_THREADS)
        return carry

    lax.fori_loop(0, tokens // DMA_UNROLL, issue, 0)
    for k in range(top_k):
        pltpu.make_async_copy(x_ref, buf_ref.at[pl.ds(0, tokens * chunks), :], sem).wait()


def _scatter_rows(x, dest, zero_blocks, *, tm, blk, chunks):
    nt = dest.shape[0]
    top_k = dest.shape[2] // tm
    n_blocks = zero_blocks.shape[0]
    grid_spec = pltpu.PrefetchScalarGridSpec(
        num_scalar_prefetch=1,
        grid=(nt,),
        in_specs=[pl.BlockSpec((1, 1, tm * top_k), lambda i, zb: (i, 0, 0), memory_space=pltpu.SMEM),
                  pl.BlockSpec((tm * chunks, LANES), lambda i, zb: (i, 0))],
        out_specs=pl.BlockSpec(memory_space=pl.ANY),
        scratch_shapes=[pltpu.VMEM((blk * chunks, LANES), x.dtype), pltpu.SemaphoreType.DMA,
                        pltpu.SemaphoreType.DMA],
    )
    return pl.pallas_call(
        functools.partial(_scatter_rows_kernel, top_k=top_k, chunks=chunks),
        grid_spec=grid_spec,
        out_shape=jax.ShapeDtypeStruct((n_blocks * blk * chunks, LANES), x.dtype),
        compiler_params=pltpu.CompilerParams(
            dimension_semantics=("arbitrary",), vmem_limit_bytes=VMEM_LIMIT),
        name="scatter_rows",
    )(zero_blocks, dest, x)


X_RING = 3


def _experts_kernel(be_ref, x_hbm, wg_ref, wu_ref, wd_ref, o_ref, xbuf_ref, wgb_ref, wub_ref, wdb_ref,
                    sem, *, blk):
    i = pl.program_id(0)
    n = pl.num_programs(0)
    rows = xbuf_ref.shape[1]

    def fetch(step):
        slot = lax.rem(step, X_RING)
        src = x_hbm.at[pl.ds(pl.multiple_of(step * rows, rows), rows), :]
        return pltpu.make_async_copy(src, xbuf_ref.at[slot], sem.at[slot])

    @pl.when(i == 0)
    def _():
        for step in range(X_RING - 1):
            @pl.when(step < n)
            def _():
                fetch(step).start()

    @pl.when(i + X_RING - 1 < n)
    def _():
        fetch(i + X_RING - 1).start()

    @pl.when((i == 0) | (be_ref[i] != be_ref[jnp.maximum(i - 1, 0)]))
    def _():
        wgb_ref[...] = wg_ref[0].astype(BF16)
        wub_ref[...] = wu_ref[0].astype(BF16)
        wdb_ref[...] = wd_ref[0].astype(BF16)

    fetch(i).wait()
    d = wg_ref.shape[1]
    xb = _load_chunk_rows(xbuf_ref.at[lax.rem(i, X_RING)], blk, d).astype(BF16)
    hdn = jax.nn.silu(_dot(xb, wgb_ref[...])) * _dot(xb, wub_ref[...])
    _store_chunk_rows(o_ref, _dot(hdn.astype(BF16), wdb_ref[...]))


def _experts(x_buf, blk_expert, w_gate, w_up, w_down, *, blk):
    _, D, de = w_gate.shape
    chunks = D // LANES
    n_blocks = x_buf.shape[0] // (blk * chunks)
    rows_spec = pl.BlockSpec((blk * chunks, LANES), lambda i, be: (i, 0))
    grid_spec = pltpu.PrefetchScalarGridSpec(
        num_scalar_prefetch=1,
        grid=(n_blocks,),
        in_specs=[pl.BlockSpec(memory_space=pl.ANY),
                  pl.BlockSpec((1, D, de), lambda i, be: (be[i], 0, 0)),
                  pl.BlockSpec((1, D, de), lambda i, be: (be[i], 0, 0)),
                  pl.BlockSpec((1, de, D), lambda i, be: (be[i], 0, 0))],
        out_specs=rows_spec,
        scratch_shapes=[pltpu.VMEM((X_RING, blk * chunks, LANES), F32),
                        pltpu.VMEM((D, de), BF16), pltpu.VMEM((D, de), BF16),
                        pltpu.VMEM((de, D), BF16), pltpu.SemaphoreType.DMA((X_RING,))],
    )
    return pl.pallas_call(
        functools.partial(_experts_kernel, blk=blk),
        grid_spec=grid_spec,
        out_shape=jax.ShapeDtypeStruct(x_buf.shape, F32),
        compiler_params=pltpu.CompilerParams(
            dimension_semantics=("arbitrary",), vmem_limit_bytes=VMEM_LIMIT),
        name="experts",
    )(blk_expert, x_buf, w_gate, w_up, w_down)


def _combine_kernel(idx_ref, idx_next_ref, x_ref, g_ref, y_ref, o_ref, buf_ref, sem, *, top_k, chunks):
    i = pl.program_id(0)
    n = pl.num_programs(0)
    slot = lax.rem(i, 2)
    tokens, d = x_ref.shape

    def request(ids_ref, s):
        def issue(g, carry):
            for u in range(DMA_UNROLL):
                r = g * DMA_UNROLL + u
                for k in range(top_k):
                    row = pl.multiple_of(ids_ref[0, 0, r * top_k + k], chunks)
                    dst = buf_ref.at[s, k, pl.ds(pl.multiple_of(r * chunks, chunks), chunks), :]
                    pltpu.make_async_copy(y_ref.at[pl.ds(row, chunks), :], dst, sem.at[s]).start(
                        priority=(u * top_k + k) % DMA_THREADS)
            return carry
        lax.fori_loop(0, tokens // DMA_UNROLL, issue, 0)

    @pl.when(i == 0)
    def _():
        request(idx_ref, 0)

    @pl.when(i + 1 < n)
    def _():
        request(idx_next_ref, 1 - slot)

    for k in range(top_k):
        pltpu.make_async_copy(y_ref.at[pl.ds(0, tokens * chunks), :], buf_ref.at[slot, k],
                              sem.at[slot]).wait()
    out = x_ref[...]
    for k in range(top_k):
        out = out + g_ref[:, k:k + 1] * _load_chunk_rows(buf_ref.at[slot, k], tokens, d)
    o_ref[...] = out


def _combine(x2, gates, y_buf, dest, *, tm):
    T, D = x2.shape
    chunks = D // LANES
    nt = dest.shape[0]
    top_k = dest.shape[2] // tm
    idx_spec = lambda step: pl.BlockSpec(
        (1, 1, tm * top_k), lambda i: (jnp.minimum(i + step, nt - 1), 0, 0), memory_space=pltpu.SMEM)
    return pl.pallas_call(
        functools.partial(_combine_kernel, top_k=top_k, chunks=chunks),
        grid=(nt,),
        in_specs=[idx_spec(0), idx_spec(1),
                  pl.BlockSpec((tm, D), lambda i: (i, 0)),
                  pl.BlockSpec((tm, LANES), lambda i: (i, 0)),
                  pl.BlockSpec(memory_space=pl.ANY)],
        out_specs=pl.BlockSpec((tm, D), lambda i: (i, 0)),
        out_shape=jax.ShapeDtypeStruct((T, D), F32),
        scratch_shapes=[pltpu.VMEM((2, top_k, tm * chunks, LANES), F32),
                        pltpu.SemaphoreType.DMA((2,))],
        compiler_params=pltpu.CompilerParams(
            dimension_semantics=("arbitrary",), vmem_limit_bytes=VMEM_LIMIT),
        name="combine",
    )(dest, dest, x2, gates, y_buf)


def _dispatch(experts, n_experts, blk):
    T, K = experts.shape
    A = T * K
    e_flat = experts.reshape(A)
    g = _pick_tile(A, RANK_GROUP)
    onehot = (e_flat.reshape(A // g, g, 1) == jnp.arange(n_experts, dtype=jnp.int32)).astype(BF16)
    tri = jnp.tril(jnp.ones((g, g), BF16))
    within = jnp.einsum('ij,gje->gie', tri, onehot, preferred_element_type=F32)
    group_counts = within[:, -1, :]
    before = jnp.cumsum(group_counts, axis=0) - group_counts
    counts = jnp.sum(group_counts, axis=0).astype(jnp.int32)
    padded = (counts + blk - 1) // blk * blk
    pends = jnp.cumsum(padded)
    pstarts = pends - padded
    row = within + before[:, None, :] - 1.0 + pstarts.astype(F32)
    dest = jnp.sum(onehot.astype(F32) * row, axis=-1).astype(jnp.int32).reshape(A)
    n_blocks = (A + n_experts * blk) // blk
    blk_start = jnp.arange(n_blocks, dtype=jnp.int32) * blk
    blk_expert = jnp.minimum(jnp.sum(blk_start[:, None] >= pends[None, :], axis=1), n_experts - 1)
    last_of_segment = jnp.any((blk_start[:, None] + blk == pends[None, :]) & (padded[None, :] > 0),
                              axis=1)
    zero_blocks = (last_of_segment | (blk_start >= pends[-1])).astype(jnp.int32)
    return dest.reshape(T, K), blk_expert.astype(jnp.int32), zero_blocks


class _Tiles(NamedTuple):
    rows: int
    q_rows: int
    expert_rows: int
    scatter_rows: int
    combine_rows: int


def _tiles(seq, tokens):
    return _Tiles(rows=_pick_tile(seq, 1024), q_rows=_pick_tile(seq, 1024), expert_rows=512,
                  scatter_rows=_pick_tile(tokens, 1024), combine_rows=_pick_tile(tokens, 512))


def kernel(x, mem, norm_mix_g, w_in, b_forget, conv_w, conv_b, lru_wa, lru_ba, lru_wx, lru_bx,
           lru_a_param, fox_q_g, fox_k_g, lru_out_g, fox_out_g, w_out, norm_mem_x_g, norm_mem_g,
           mem_wq, mem_wkv, mem_q_g, mem_k_g, mem_wo, norm_ffn_g, router_group_w, router_group_b,
           router_expert_w, router_expert_b, exp_w_gate, exp_w_up, exp_w_down):
    B, S, D = x.shape
    depth = norm_mix_g.shape[0]
    lw = conv_w.shape[-1]
    n_heads = b_forget.shape[-1]
    dh = fox_q_g.shape[-1]
    mem_dh = mem_q_g.shape[-1]
    mem_heads = mem_wq.shape[-1] // mem_dh
    n_experts = router_expert_w.shape[-1]
    T = B * S
    tm, tq, blk, scatter_tm, combine_tm = _tiles(S, T)

    for l in range(depth):
        y_lru, q, k, v = _in_proj(
            x, norm_mix_g[l], w_in[l], b_forget[l], fox_q_g[l], fox_k_g[l], conv_w[l], conv_b[l],
            lru_wa[l], lru_ba[l], lru_wx[l], lru_bx[l], lru_a_param[l], lru_out_g[l],
            lw=lw, n_heads=n_heads, dh=dh, tm=tm)
        y_fox = _fox_attn(q, k, v, n_heads=n_heads, dh=dh, tq=tq)
        k_mem, v_mem = _mem_kv(mem, norm_mem_g[l], mem_wkv[l], mem_k_g[l], n_heads=mem_heads,
                               dh=mem_dh)
        x2, xn, gates, experts = _out_mem(
            x, y_lru, y_fox, fox_out_g[l], w_out[l], norm_mem_x_g[l], mem_wq[l], mem_q_g[l], k_mem,
            v_mem, mem_wo[l], norm_ffn_g[l], router_group_w[l], router_group_b[l],
            router_expert_w[l], router_expert_b[l], n_heads=mem_heads, dh=mem_dh, tm=tm)

        dest, blk_expert, zero_blocks = _dispatch(
            experts.reshape(T, LANES)[:, :TOP_K], n_experts, blk)
        chunks = D // LANES
        tiled = lambda t: (dest * chunks).reshape(T // t, 1, t * TOP_K)
        x_buf = _scatter_rows(xn, tiled(scatter_tm), zero_blocks, tm=scatter_tm, blk=blk,
                              chunks=chunks)
        y_buf = _experts(x_buf, blk_expert, exp_w_gate[l], exp_w_up[l], exp_w_down[l], blk=blk)
        x = _combine(x2.reshape(T, D), gates.reshape(T, LANES), y_buf, tiled(combine_tm),
                     tm=combine_tm)
        x = x.reshape(B, S, D)
    return x
```

```python
import functools
from typing import NamedTuple

import jax
import jax.numpy as jnp
import numpy as np
from jax import lax
from jax.experimental import pallas as pl
from jax.experimental.pallas import tpu as pltpu

EPS = 1e-6
LRU_C = 8.0
CONV_WIDTH = 4
TOP_K = 2
LANES = 128
SUBLANES = 8
MXU_DIM = 256
HEAD_SLOT = LANES
VMEM_LIMIT = 56 * 1024 * 1024

F32 = jnp.float32
BF16 = jnp.bfloat16
MIN_NORMAL = float(np.finfo(np.float32).tiny)


def _dot(a, b):
    return jnp.dot(a, b, preferred_element_type=F32)


def _rms(x, g):
    return x * lax.rsqrt(jnp.mean(x * x, axis=-1, keepdims=True) + EPS) * g


def _softplus(x):
    return jnp.maximum(x, 0.0) + jnp.log1p(jnp.exp(-jnp.abs(x)))


def _pick_tile(n, target):
    t = min(n, target)
    while n % t:
        t //= 2
    return t


def _store_chunk_rows(ref, val):
    n, d = val.shape
    chunks = d // LANES
    for c in range(chunks):
        ref[pl.ds(c, n, stride=chunks), :] = val[:, c * LANES:(c + 1) * LANES]


def _load_chunk_rows(ref, n, d):
    chunks = d // LANES
    return jnp.concatenate([ref[pl.ds(c, n, stride=chunks), :] for c in range(chunks)], axis=1)


def _in_proj_kernel(x_ref, g_ref, w_ref, wf_ref, bf_ref, gq_ref, gk_ref, pe_ref, cv_ref,
                    cw_ref, cb_ref, wl_ref, ba_ref, bx_ref, ap_ref, og_ref,
                    ylru_ref, q_ref, k_ref, v_ref, carry_ref, tail_ref, h_ref, *, lw, n_heads, dh):
    @pl.when(pl.program_id(1) == 0)
    def _():
        carry_ref[...] = jnp.zeros_like(carry_ref)
        tail_ref[...] = jnp.zeros_like(tail_ref)
        h_ref[...] = jnp.zeros_like(h_ref)

    hs = n_heads * HEAD_SLOT
    x = x_ref[0]
    tm = x.shape[0]
    hb = _rms(x, g_ref[...]).astype(BF16)

    u = _dot(hb, w_ref[:, 0:lw])
    gate = _dot(hb, w_ref[:, lw:2 * lw])
    ylru_ref[0] = _rg_lru_tile(u, gate, cw_ref, cb_ref, wl_ref, ba_ref, bx_ref, ap_ref, og_ref,
                               tail_ref, h_ref).astype(ylru_ref.dtype)

    z = _dot(hb, wf_ref[...]) + bf_ref[...]
    lane = lax.broadcasted_iota(jnp.int32, z.shape, 1)
    row = lax.broadcasted_iota(jnp.int32, z.shape, 0)
    c = jnp.where(lane < n_heads, -_softplus(-z), 0.0)
    d = 1
    while d < tm:
        c = c + jnp.where(row >= d, pltpu.roll(c, d, 0), 0.0)
        d *= 2
    c = c + carry_ref[...]
    carry_ref[...] = c[tm - 1:tm, :]
    c1 = c.astype(BF16).astype(F32)
    r1 = c - c1
    c2 = r1.astype(BF16).astype(F32)
    c3 = (r1 - c2).astype(BF16).astype(F32)
    e = c1 + pltpu.roll(c2, n_heads, 1) + pltpu.roll(c3, 2 * n_heads, 1)
    e = jnp.where(lane == 3 * n_heads, 1.0, e).astype(BF16)
    ext = _dot(e, pe_ref[...])

    fw = n_heads * dh
    lane_t = lax.broadcasted_iota(jnp.int32, (tm, LANES), 1)
    own = (lane_t < dh, lane_t >= dh)

    def to_slots(y):
        parts = []
        for c in range(fw // LANES):
            col = y[:, c * LANES:(c + 1) * LANES]
            parts += [jnp.where(own[0], col, 0.0), jnp.where(own[1], col, 0.0)]
        return jnp.concatenate(parts, axis=-1)

    def head_norm(y, gain):
        parts = []
        for h in range(n_heads):
            blk = y[:, h * HEAD_SLOT:(h + 1) * HEAD_SLOT]
            ss = jnp.sum(blk * blk, axis=-1, keepdims=True) * (1.0 / dh)
            parts.append(blk * lax.rsqrt(ss + EPS))
        return jnp.concatenate(parts, axis=-1) * gain

    q = to_slots(_dot(hb, w_ref[:, 2 * lw:2 * lw + fw]))
    q_ref[0] = (head_norm(q, gq_ref[...]) + ext[:, 0:hs]).astype(BF16)
    k = to_slots(_dot(hb, w_ref[:, 2 * lw + fw:2 * lw + 2 * fw]))
    k_ref[0] = (head_norm(k, gk_ref[...]) + ext[:, hs:2 * hs]).astype(BF16)
    v = to_slots(_dot(hb, w_ref[:, 2 * lw + 2 * fw:2 * lw + 3 * fw]))
    v_ref[0] = (v + cv_ref[...]).astype(BF16)


def _in_proj(x, norm_g, w_in, b_forget, fox_q_g, fox_k_g, conv_w, conv_b, wa, ba, wx, bx, a_param,
             lru_out_g, *, lw, n_heads, dh, tm):
    B, S, D = x.shape
    fw = n_heads * dh
    hs = n_heads * HEAD_SLOT
    assert 2 * dh == HEAD_SLOT and fw % LANES == 0 and 3 * n_heads <= LANES
    n_main = 2 * lw + 3 * fw
    w_all = w_in[:, :n_main].astype(BF16)
    w_f = jnp.pad(w_in[:, n_main:], ((0, 0), (0, LANES - n_heads))).astype(BF16)
    b_f = jnp.pad(b_forget, (0, LANES - n_heads)).reshape(1, LANES)
    scale = dh ** -0.5

    def slot_gains(g):
        even = jnp.pad(g, (0, HEAD_SLOT - dh))
        odd = jnp.pad(g, (HEAD_SLOT - dh, 0))
        return jnp.tile(jnp.concatenate([even, odd]), n_heads // 2).reshape(1, hs)

    gq = slot_gains(fox_q_g * scale)
    gk = slot_gains(fox_k_g)
    ones_row = 3 * n_heads
    pe = np.zeros((LANES, 2 * hs), np.float32)
    cv = np.zeros((1, hs), np.float32)
    for h in range(n_heads):
        x0 = h * HEAD_SLOT + (dh if h % 2 == 0 else 0)
        for j in range(3):
            pe[j * n_heads + h, x0 + j] = 1.0
            pe[ones_row, x0 + 3 + j] = 1.0
            pe[ones_row, hs + x0 + j] = 1.0
            pe[j * n_heads + h, hs + x0 + 3 + j] = -1.0
        cv[0, x0] = 1.0
    pe = jnp.asarray(pe, BF16)

    w_lru = _lru_gate_weights(wa, wx, lw)
    vec = lambda a: a.reshape(1, lw)
    full = lambda shape: pl.BlockSpec(shape, lambda b, s: (0,) * len(shape))
    row = lambda width: pl.BlockSpec((1, tm, width), lambda b, s: (b, s, 0))
    return pl.pallas_call(
        functools.partial(_in_proj_kernel, lw=lw, n_heads=n_heads, dh=dh),
        grid=(B, S // tm),
        in_specs=[row(D), full((1, D)), full(w_all.shape), full(w_f.shape), full((1, LANES)),
                  full((1, hs)), full((1, hs)), full(pe.shape), full((1, hs)),
                  full((CONV_WIDTH, lw)), full((1, lw)), full(w_lru.shape), full((1, lw)),
                  full((1, lw)), full((1, lw)), full((1, lw))],
        out_specs=[row(lw), row(hs), row(hs), row(hs)],
        out_shape=[jax.ShapeDtypeStruct((B, S, lw), BF16),
                   jax.ShapeDtypeStruct((B, S, hs), BF16), jax.ShapeDtypeStruct((B, S, hs), BF16),
                   jax.ShapeDtypeStruct((B, S, hs), BF16)],
        scratch_shapes=[pltpu.VMEM((1, LANES), F32), pltpu.VMEM((SUBLANES, lw), F32),
                        pltpu.VMEM((1, lw), F32)],
        compiler_params=pltpu.CompilerParams(
            dimension_semantics=("parallel", "arbitrary"), vmem_limit_bytes=VMEM_LIMIT),
        name="in_proj",
    )(x, norm_g.reshape(1, D), w_all, w_f, b_f, gq, gk, pe, jnp.asarray(cv),
      conv_w, vec(conv_b), w_lru, vec(ba), vec(bx), vec(a_param), vec(lru_out_g))


def _rg_lru_tile(u, gate, cw_ref, cb_ref, w_ref, ba_ref, bx_ref, ap_ref, og_ref, tail_ref, h_ref):
    ts, lw = u.shape
    ext = jnp.concatenate([tail_ref[...], u], axis=0)
    tail_ref[...] = u[ts - SUBLANES:, :]
    xc = cb_ref[...] + u * cw_ref[CONV_WIDTH - 1:CONV_WIDTH, :]
    for back in range(1, CONV_WIDTH):
        tap = CONV_WIDTH - 1 - back
        xc = xc + pltpu.roll(ext, back, 0)[SUBLANES:, :] * cw_ref[tap:tap + 1, :]

    xb = xc.astype(BF16)
    r_parts, i_parts = [], []
    for j in range(lw // MXU_DIM):
        y = _dot(xb[:, j * MXU_DIM:(j + 1) * MXU_DIM], w_ref[j])
        r_parts.append(y[:, :MXU_DIM])
        i_parts.append(y[:, MXU_DIM:])
    r = jax.nn.sigmoid(jnp.concatenate(r_parts, axis=-1) + ba_ref[...])
    i = jax.nn.sigmoid(jnp.concatenate(i_parts, axis=-1) + bx_ref[...])
    log_a = (-LRU_C) * r * _softplus(-ap_ref[...])
    a = jnp.exp(log_a)
    gap = jnp.maximum(-jnp.tanh(log_a) * (a * a + 1.0), 0.0)
    mult = gap * lax.rsqrt(jnp.maximum(gap, MIN_NORMAL))
    b = mult * (i * xc)

    groups = ts // SUBLANES
    a = a.reshape(groups, SUBLANES, lw)
    b = b.reshape(groups, SUBLANES, lw)
    row = lax.broadcasted_iota(jnp.int32, a.shape, 1)
    d = 1
    while d < SUBLANES:
        keep = row >= d
        a_prev = jnp.where(keep, pltpu.roll(a, d, 1), 1.0)
        b_prev = jnp.where(keep, pltpu.roll(b, d, 1), 0.0)
        b = a * b_prev + b
        a = a * a_prev
        d *= 2
    state = h_ref[...]
    parts = []
    for g in range(groups):
        hg = b[g] + a[g] * state
        state = hg[SUBLANES - 1:, :]
        parts.append(hg)
    h = jnp.concatenate(parts, axis=0)
    h_ref[...] = state

    y = h * jax.nn.gelu(gate)
    return _rms(y, og_ref[...])


def _lru_gate_weights(wa, wx, lw):
    _, bd, _ = wa.shape
    per = MXU_DIM // bd
    n_tiles = lw // MXU_DIM

    def tiles(w):
        w = w.reshape(n_tiles, per, bd, bd)
        eye = jnp.eye(per, dtype=w.dtype)
        return jnp.einsum('tpij,pq->tpiqj', w, eye).reshape(n_tiles, MXU_DIM, MXU_DIM)

    return jnp.concatenate([tiles(wa), tiles(wx)], axis=-1).astype(BF16)


OVERFLOW_GUARD = 1e30


def _fox_attn_kernel(q_ref, k_ref, v_ref, o_ref, vt_ref, m_ref, acc_ref, *, dh, tq):
    n_chunks = vt_ref.shape[0]
    for c in range(n_chunks):
        vt_ref[c] = v_ref[0, c * tq:(c + 1) * tq, :].astype(F32).T.astype(BF16)

    heads = [slice(hh * HEAD_SLOT, (hh + 1) * HEAD_SLOT) for hh in range(2)]
    lax.fori_loop(0, n_chunks, functools.partial(
        _fox_attn_tile, q_ref=q_ref, k_ref=k_ref, o_ref=o_ref, vt_ref=vt_ref, m_ref=m_ref,
        acc_ref=acc_ref, heads=heads, dh=dh, tq=tq), 0)


def _fox_attn_tile(qi, carry, *, q_ref, k_ref, o_ref, vt_ref, m_ref, acc_ref, heads, dh, tq):
    q_rows = pl.ds(pl.multiple_of(qi * tq, tq), tq)

    def scores(j, q, sl, masked):
        start = pl.multiple_of(j * tq, tq)
        st = lax.dot_general(k_ref[0, pl.ds(start, tq), sl], q, (((1,), (1,)), ((), ())),
                             preferred_element_type=F32)
        if masked:
            kpos = lax.broadcasted_iota(jnp.int32, st.shape, 0)
            qpos = lax.broadcasted_iota(jnp.int32, st.shape, 1)
            st = jnp.where(kpos <= qpos, st, -jnp.inf)
        return st

    for hh, sl in enumerate(heads):
        st = scores(qi, q_ref[0, q_rows, sl], sl, True)
        m = jnp.max(st, axis=0, keepdims=True)
        m_ref[hh] = m
        acc_ref[hh] = _dot(vt_ref[qi, sl, :], jnp.exp(st - m).astype(BF16))

    def fast_chunk(j):
        for hh, sl in enumerate(heads):
            pt = jnp.exp(scores(j, q_ref[0, q_rows, sl], sl, False) - m_ref[hh]).astype(BF16)
            acc_ref[hh] += _dot(vt_ref[j, sl, :], pt)

    def fast_pair(i, carry):
        start = pl.multiple_of(2 * i * tq, 2 * tq)
        for hh, sl in enumerate(heads):
            st = lax.dot_general(k_ref[0, pl.ds(start, 2 * tq), sl], q_ref[0, q_rows, sl],
                                 (((1,), (1,)), ((), ())), preferred_element_type=F32)
            pt = jnp.exp(st - m_ref[hh]).astype(BF16)
            vt = jnp.concatenate([vt_ref[2 * i, sl, :], vt_ref[2 * i + 1, sl, :]], axis=1)
            acc_ref[hh] += _dot(vt, pt)
        return carry

    lax.fori_loop(0, qi // 2, fast_pair, 0)

    @pl.when(qi % 2 == 1)
    def _():
        fast_chunk(qi - 1)

    def softmax_sums():
        return jnp.concatenate([acc_ref[0][dh:dh + 1, :], acc_ref[1][0:1, :]], axis=0)

    def normalised():
        sums = softmax_sums()
        return jnp.concatenate([acc_ref[0][0:dh, :] / sums[0:1, :],
                                acc_ref[1][dh:2 * dh, :] / sums[1:2, :]], axis=0)

    out = normalised()
    o_ref[0, q_rows, :] = out.T.astype(o_ref.dtype)
    sums = softmax_sums()
    flag = lambda x: jnp.max(jnp.where(jnp.abs(x) < OVERFLOW_GUARD, 0.0, 1.0),
                             axis=0, keepdims=True)
    bad = jnp.max(jnp.maximum(flag(out), flag(sums)), axis=1, keepdims=True)

    @pl.when(bad[0, 0] > 0.0)
    def _():
        m_ref[...] = jnp.full_like(m_ref, -jnp.inf)
        acc_ref[...] = jnp.zeros_like(acc_ref)

        def chunk(j, masked):
            for hh, sl in enumerate(heads):
                st = scores(j, q_ref[0, q_rows, sl], sl, masked)
                m_old = m_ref[hh]
                m_new = jnp.maximum(m_old, jnp.max(st, axis=0, keepdims=True))
                pt = jnp.exp(st - m_new).astype(BF16)
                acc_ref[hh] = jnp.exp(m_old - m_new) * acc_ref[hh] + _dot(vt_ref[j, sl, :], pt)
                m_ref[hh] = m_new

        def body(j, carry):
            chunk(j, False)
            return carry

        lax.fori_loop(0, qi, body, 0)
        chunk(qi, True)
        o_ref[0, q_rows, :] = normalised().T.astype(o_ref.dtype)

    return carry


def _fox_attn(q, k, v, *, n_heads, dh, tq):
    B, S, _ = q.shape
    nq = S // tq
    pair = 2 * HEAD_SLOT
    qkv_spec = pl.BlockSpec((1, S, pair), lambda b, h: (b, 0, h))
    return pl.pallas_call(
        functools.partial(_fox_attn_kernel, dh=dh, tq=tq),
        grid=(B, n_heads // 2),
        in_specs=[qkv_spec, qkv_spec, qkv_spec],
        out_specs=pl.BlockSpec((1, S, 2 * dh), lambda b, h: (b, 0, h)),
        out_shape=jax.ShapeDtypeStruct((B, S, n_heads * dh), BF16),
        scratch_shapes=[pltpu.VMEM((nq, pair, tq), BF16), pltpu.VMEM((2, 1, tq), F32),
                        pltpu.VMEM((2, HEAD_SLOT, tq), F32)],
        compiler_params=pltpu.CompilerParams(
            dimension_semantics=("parallel", "parallel"), vmem_limit_bytes=VMEM_LIMIT),
        name="fox_attn",
    )(q, k, v)


def _mem_kv_kernel(m_ref, g_ref, w_ref, kg_ref, k_ref, v_ref, *, n_heads, dh):
    mw = n_heads * dh
    mb = _rms(m_ref[0], g_ref[...]).astype(BF16)
    kv = _dot(mb, w_ref[...])
    parts = []
    for h in range(n_heads):
        parts.append(_rms(kv[:, h * dh:(h + 1) * dh], kg_ref[...]))
    k_ref[0] = jnp.concatenate(parts, axis=-1).astype(BF16)
    v_ref[0] = kv[:, mw:].astype(BF16)


def _mem_kv(mem, norm_g, wkv, k_g, *, n_heads, dh):
    B, M, D = mem.shape
    mw = n_heads * dh
    full = lambda shape: pl.BlockSpec(shape, lambda b: (0,) * len(shape))
    out = pl.BlockSpec((1, M, mw), lambda b: (b, 0, 0))
    return pl.pallas_call(
        functools.partial(_mem_kv_kernel, n_heads=n_heads, dh=dh),
        grid=(B,),
        in_specs=[pl.BlockSpec((1, M, D), lambda b: (b, 0, 0)), full((1, D)), full((D, 2 * mw)),
                  full((1, dh))],
        out_specs=[out, out],
        out_shape=[jax.ShapeDtypeStruct((B, M, mw), BF16)] * 2,
        compiler_params=pltpu.CompilerParams(
            dimension_semantics=("parallel",), vmem_limit_bytes=VMEM_LIMIT),
        name="mem_kv",
    )(mem, norm_g.reshape(1, D), wkv.astype(BF16), k_g.reshape(1, dh))


def _out_mem_kernel(x_ref, yl_ref, yf_ref, fg_ref, wol_ref, wof_ref, gx_ref, wq_ref, qg_ref,
                    km_ref, vm_ref, wo_ref, gf_ref, wr_ref, br_ref,
                    x2_ref, xn_ref, gates_ref, experts_ref, *, n_heads, dh, n_groups, per_group):
    yf = _rms(yf_ref[0].astype(F32), fg_ref[...]).astype(BF16)
    x1 = x_ref[0] + _dot(yl_ref[0], wol_ref[...]) + _dot(yf, wof_ref[...])

    q = _dot(_rms(x1, gx_ref[...]).astype(BF16), wq_ref[...])
    outs = []
    for h in range(n_heads):
        sl = slice(h * dh, (h + 1) * dh)
        qh = _rms(q[:, sl], qg_ref[...]).astype(BF16)
        s = lax.dot_general(qh, km_ref[0, :, sl], (((1,), (1,)), ((), ())),
                            preferred_element_type=F32)
        p = jnp.exp(s - jnp.max(s, axis=-1, keepdims=True))
        inv = 1.0 / jnp.sum(p, axis=-1, keepdims=True)
        outs.append(_dot(p.astype(BF16), vm_ref[0, :, sl]) * inv)
    x2 = x1 + _dot(jnp.concatenate(outs, axis=-1).astype(BF16), wo_ref[...])
    x2_ref[0] = x2

    xn = _rms(x2, gf_ref[...])
    _store_chunk_rows(xn_ref, xn)
    logits = _dot(xn.astype(BF16), wr_ref[...]) + br_ref[...]
    lane = lax.broadcasted_iota(jnp.int32, logits.shape, 1).astype(F32)
    neg = -jnp.inf

    def top(vals):
        mx = jnp.max(vals, axis=-1, keepdims=True)
        idx = jnp.min(jnp.where(vals == mx, lane, float(LANES)), axis=-1, keepdims=True)
        return mx, idx

    gl = jnp.where(lane < n_groups, logits, neg)
    g_max, g_idx = top(gl)
    g_w = 1.0 / jnp.sum(jnp.exp(gl - g_max), axis=-1, keepdims=True)
    lo = n_groups + per_group * g_idx
    el = jnp.where((lane >= lo) & (lane < lo + per_group), logits, neg)
    e1, i1 = top(el)
    e2, i2 = top(jnp.where(lane == i1, neg, el))
    t = jnp.exp(e2 - e1)
    w1 = g_w / (1.0 + t)
    w2 = g_w * t / (1.0 + t)
    gates_ref[0] = jnp.where(lane == 0, w1, jnp.where(lane == 1, w2, 0.0))
    experts_ref[0] = jnp.where(lane == 0, i1 - n_groups,
                               jnp.where(lane == 1, i2 - n_groups, 0.0)).astype(jnp.int32)


def _out_mem(x, y_lru, y_fox, fox_out_g, w_out, norm_mem_x_g, mem_wq, mem_q_g, k_mem, v_mem, mem_wo,
             norm_ffn_g, router_group_w, router_group_b, router_expert_w, router_expert_b, *,
             n_heads, dh, tm):
    B, S, D = x.shape
    lw = y_lru.shape[-1]
    fw = y_fox.shape[-1]
    M = k_mem.shape[1]
    mw = n_heads * dh
    n_groups = router_group_w.shape[-1]
    n_experts = router_expert_w.shape[-1]
    w_r = jnp.concatenate([router_group_w, router_expert_w], axis=1)
    w_r = jnp.pad(w_r, ((0, 0), (0, LANES - w_r.shape[1]))).astype(BF16)
    b_r = jnp.concatenate([router_group_b, router_expert_b])
    b_r = jnp.pad(b_r, (0, LANES - b_r.shape[0])).reshape(1, LANES)
    q_gain = (mem_q_g * dh ** -0.5).reshape(1, dh)

    full = lambda shape: pl.BlockSpec(shape, lambda b, s: (0,) * len(shape))
    row = lambda width: pl.BlockSpec((1, tm, width), lambda b, s: (b, s, 0))
    mem_spec = pl.BlockSpec((1, M, mw), lambda b, s: (b, 0, 0))
    return pl.pallas_call(
        functools.partial(_out_mem_kernel, n_heads=n_heads, dh=dh, n_groups=n_groups,
                          per_group=n_experts // n_groups),
        grid=(B, S // tm),
        in_specs=[row(D), row(lw), row(fw), full((1, fw)), full((lw, D)), full((fw, D)),
                  full((1, D)), full((D, mw)), full((1, dh)), mem_spec, mem_spec, full((mw, D)),
                  full((1, D)), full((D, LANES)), full((1, LANES))],
        out_specs=[row(D), pl.BlockSpec((tm * D // LANES, LANES), lambda b, s: (b * (S // tm) + s, 0)),
                   row(LANES), row(LANES)],
        out_shape=[jax.ShapeDtypeStruct((B, S, D), F32),
                   jax.ShapeDtypeStruct((B * S * D // LANES, LANES), F32),
                   jax.ShapeDtypeStruct((B, S, LANES), F32),
                   jax.ShapeDtypeStruct((B, S, LANES), jnp.int32)],
        compiler_params=pltpu.CompilerParams(
            dimension_semantics=("parallel", "parallel"), vmem_limit_bytes=VMEM_LIMIT),
        name="out_mem",
    )(x, y_lru, y_fox, fox_out_g.reshape(1, fw), w_out[:lw].astype(BF16), w_out[lw:].astype(BF16),
      norm_mem_x_g.reshape(1, D), mem_wq.astype(BF16), q_gain, k_mem, v_mem, mem_wo.astype(BF16),
      norm_ffn_g.reshape(1, D), w_r, b_r)


DMA_UNROLL = 16
DMA_THREADS = 2
RANK_GROUP = 256


def _scatter_rows_kernel(zb_ref, idx_ref, x_ref, buf_ref, zero_ref, sem, zsem, *, top_k, chunks):
    tokens = x_ref.shape[0] // chunks

    @pl.when(pl.program_id(0) == 0)
    def _():
        rows = zero_ref.shape[0]
        zero_ref[...] = jnp.zeros_like(zero_ref)

        def copy(j):
            start = pl.multiple_of(j * rows, rows)
            return pltpu.make_async_copy(zero_ref, buf_ref.at[pl.ds(start, rows), :], zsem)

        def issue_zero(j, carry):
            @pl.when(zb_ref[j] == 1)
            def _():
                copy(j).start()
            return carry

        def drain_zero(j, carry):
            @pl.when(zb_ref[j] == 1)
            def _():
                copy(j).wait()
            return carry

        lax.fori_loop(0, zb_ref.shape[0], issue_zero, 0)
        lax.fori_loop(0, zb_ref.shape[0], drain_zero, 0)

    def issue(g, carry):
        for u in range(DMA_UNROLL):
            r = g * DMA_UNROLL + u
            src = x_ref.at[pl.ds(pl.multiple_of(r * chunks, chunks), chunks), :]
            for k in range(top_k):
                row = pl.multiple_of(idx_ref[0, 0, r * top_k + k], chunks)
                pltpu.make_async_copy(src, buf_ref.at[pl.ds(row, chunks), :], sem).start(
                    priority=(u * top_k + k) % DMA_THREADS)
        return carry

    lax.fori_loop(0, tokens // DMA_UNROLL, issue, 0)
    for k in range(top_k):
        pltpu.make_async_copy(x_ref, buf_ref.at[pl.ds(0, tokens * chunks), :], sem).wait()


def _scatter_rows(x, dest, zero_blocks, *, tm, blk, chunks):
    nt = dest.shape[0]
    top_k = dest.shape[2] // tm
    n_blocks = zero_blocks.shape[0]
    grid_spec = pltpu.PrefetchScalarGridSpec(
        num_scalar_prefetch=1,
        grid=(nt,),
        in_specs=[pl.BlockSpec((1, 1, tm * top_k), lambda i, zb: (i, 0, 0), memory_space=pltpu.SMEM),
                  pl.BlockSpec((tm * chunks, LANES), lambda i, zb: (i, 0))],
        out_specs=pl.BlockSpec(memory_space=pl.ANY),
        scratch_shapes=[pltpu.VMEM((blk * chunks, LANES), x.dtype), pltpu.SemaphoreType.DMA,
                        pltpu.SemaphoreType.DMA],
    )
    return pl.pallas_call(
        functools.partial(_scatter_rows_kernel, top_k=top_k, chunks=chunks),
        grid_spec=grid_spec,
        out_shape=jax.ShapeDtypeStruct((n_blocks * blk * chunks, LANES), x.dtype),
        compiler_params=pltpu.CompilerParams(
            dimension_semantics=("arbitrary",), vmem_limit_bytes=VMEM_LIMIT),
        name="scatter_rows",
    )(zero_blocks, dest, x)


X_RING = 3
X_PRIORITY = 1


def _experts_kernel(be_ref, x_hbm, wg_ref, wu_ref, wd_ref, o_ref, xbuf_ref, wgb_ref, wub_ref, wdb_ref,
                    sem, *, blk):
    i = pl.program_id(0)
    n = pl.num_programs(0)
    rows = xbuf_ref.shape[1]

    def fetch(step):
        slot = lax.rem(step, X_RING)
        src = x_hbm.at[pl.ds(pl.multiple_of(step * rows, rows), rows), :]
        return pltpu.make_async_copy(src, xbuf_ref.at[slot], sem.at[slot])

    @pl.when(i == 0)
    def _():
        for step in range(X_RING - 1):
            @pl.when(step < n)
            def _():
                fetch(step).start(priority=X_PRIORITY)

    @pl.when(i + X_RING - 1 < n)
    def _():
        fetch(i + X_RING - 1).start(priority=X_PRIORITY)

    @pl.when((i == 0) | (be_ref[i] != be_ref[jnp.maximum(i - 1, 0)]))
    def _():
        wgb_ref[...] = wg_ref[0].astype(BF16)
        wub_ref[...] = wu_ref[0].astype(BF16)
        wdb_ref[...] = wd_ref[0].astype(BF16)

    fetch(i).wait()
    d = wg_ref.shape[1]
    xb = _load_chunk_rows(xbuf_ref.at[lax.rem(i, X_RING)], blk, d).astype(BF16)
    hdn = jax.nn.silu(_dot(xb, wgb_ref[...])) * _dot(xb, wub_ref[...])
    _store_chunk_rows(o_ref, _dot(hdn.astype(BF16), wdb_ref[...]))


def _experts(x_buf, blk_expert, w_gate, w_up, w_down, *, blk):
    _, D, de = w_gate.shape
    chunks = D // LANES
    n_blocks = x_buf.shape[0] // (blk * chunks)
    rows_spec = pl.BlockSpec((blk * chunks, LANES), lambda i, be: (i, 0))
    grid_spec = pltpu.PrefetchScalarGridSpec(
        num_scalar_prefetch=1,
        grid=(n_blocks,),
        in_specs=[pl.BlockSpec(memory_space=pl.ANY),
                  pl.BlockSpec((1, D, de), lambda i, be: (be[i], 0, 0)),
                  pl.BlockSpec((1, D, de), lambda i, be: (be[i], 0, 0)),
                  pl.BlockSpec((1, de, D), lambda i, be: (be[i], 0, 0))],
        out_specs=rows_spec,
        scratch_shapes=[pltpu.VMEM((X_RING, blk * chunks, LANES), F32),
                        pltpu.VMEM((D, de), BF16), pltpu.VMEM((D, de), BF16),
                        pltpu.VMEM((de, D), BF16), pltpu.SemaphoreType.DMA((X_RING,))],
    )
    return pl.pallas_call(
        functools.partial(_experts_kernel, blk=blk),
        grid_spec=grid_spec,
        out_shape=jax.ShapeDtypeStruct(x_buf.shape, F32),
        compiler_params=pltpu.CompilerParams(
            dimension_semantics=("arbitrary",), vmem_limit_bytes=VMEM_LIMIT),
        name="experts",
    )(blk_expert, x_buf, w_gate, w_up, w_down)


def _combine_kernel(idx_ref, idx_next_ref, x_ref, g_ref, y_ref, o_ref, buf_ref, sem, *, top_k, chunks):
    i = pl.program_id(0)
    n = pl.num_programs(0)
    slot = lax.rem(i, 2)
    tokens, d = x_ref.shape

    def request(ids_ref, s):
        def issue(g, carry):
            for u in range(DMA_UNROLL):
                r = g * DMA_UNROLL + u
                for k in range(top_k):
                    row = pl.multiple_of(ids_ref[0, 0, r * top_k + k], chunks)
                    dst = buf_ref.at[s, k, pl.ds(pl.multiple_of(r * chunks, chunks), chunks), :]
                    pltpu.make_async_copy(y_ref.at[pl.ds(row, chunks), :], dst, sem.at[s]).start(
                        priority=(u * top_k + k) % DMA_THREADS)
            return carry
        lax.fori_loop(0, tokens // DMA_UNROLL, issue, 0)

    @pl.when(i == 0)
    def _():
        request(idx_ref, 0)

    @pl.when(i + 1 < n)
    def _():
        request(idx_next_ref, 1 - slot)

    for k in range(top_k):
        pltpu.make_async_copy(y_ref.at[pl.ds(0, tokens * chunks), :], buf_ref.at[slot, k],
                              sem.at[slot]).wait()
    out = x_ref[...]
    for k in range(top_k):
        out = out + g_ref[:, k:k + 1] * _load_chunk_rows(buf_ref.at[slot, k], tokens, d)
    o_ref[...] = out


def _combine(x2, gates, y_buf, dest, *, tm):
    T, D = x2.shape
    chunks = D // LANES
    nt = dest.shape[0]
    top_k = dest.shape[2] // tm
    idx_spec = lambda step: pl.BlockSpec(
        (1, 1, tm * top_k), lambda i: (jnp.minimum(i + step, nt - 1), 0, 0), memory_space=pltpu.SMEM)
    return pl.pallas_call(
        functools.partial(_combine_kernel, top_k=top_k, chunks=chunks),
        grid=(nt,),
        in_specs=[idx_spec(0), idx_spec(1),
                  pl.BlockSpec((tm, D), lambda i: (i, 0)),
                  pl.BlockSpec((tm, LANES), lambda i: (i, 0)),
                  pl.BlockSpec(memory_space=pl.ANY)],
        out_specs=pl.BlockSpec((tm, D), lambda i: (i, 0)),
        out_shape=jax.ShapeDtypeStruct((T, D), F32),
        scratch_shapes=[pltpu.VMEM((2, top_k, tm * chunks, LANES), F32),
                        pltpu.SemaphoreType.DMA((2,))],
        compiler_params=pltpu.CompilerParams(
            dimension_semantics=("arbitrary",), vmem_limit_bytes=VMEM_LIMIT),
        name="combine",
    )(dest, dest, x2, gates, y_buf)


def _dispatch(experts, n_experts, blk):
    T, K = experts.shape
    A = T * K
    e_flat = experts.reshape(A)
    g = _pick_tile(A, RANK_GROUP)
    onehot = (e_flat.reshape(A // g, g, 1) == jnp.arange(n_experts, dtype=jnp.int32)).astype(BF16)
    tri = jnp.tril(jnp.ones((g, g), BF16))
    within = jnp.einsum('ij,gje->gie', tri, onehot, preferred_element_type=F32)
    group_counts = within[:, -1, :]
    before = jnp.cumsum(group_counts, axis=0) - group_counts
    counts = jnp.sum(group_counts, axis=0).astype(jnp.int32)
    padded = (counts + blk - 1) // blk * blk
    pends = jnp.cumsum(padded)
    pstarts = pends - padded
    row = within + before[:, None, :] - 1.0 + pstarts.astype(F32)
    dest = jnp.sum(onehot.astype(F32) * row, axis=-1).astype(jnp.int32).reshape(A)
    n_blocks = (A + n_experts * blk) // blk
    blk_start = jnp.arange(n_blocks, dtype=jnp.int32) * blk
    blk_expert = jnp.minimum(jnp.sum(blk_start[:, None] >= pends[None, :], axis=1), n_experts - 1)
    last_of_segment = jnp.any((blk_start[:, None] + blk == pends[None, :]) & (padded[None, :] > 0),
                              axis=1)
    zero_blocks = (last_of_segment | (blk_start >= pends[-1])).astype(jnp.int32)
    return dest.reshape(T, K), blk_expert.astype(jnp.int32), zero_blocks


class _Tiles(NamedTuple):
    rows: int
    q_rows: int
    expert_rows: int
    scatter_rows: int
    combine_rows: int


def _tiles(seq, tokens):
    return _Tiles(rows=_pick_tile(seq, 1024), q_rows=_pick_tile(seq, 1024), expert_rows=512,
                  scatter_rows=_pick_tile(tokens, 1024), combine_rows=_pick_tile(tokens, 512))


def kernel(x, mem, norm_mix_g, w_in, b_forget, conv_w, conv_b, lru_wa, lru_ba, lru_wx, lru_bx,
           lru_a_param, fox_q_g, fox_k_g, lru_out_g, fox_out_g, w_out, norm_mem_x_g, norm_mem_g,
           mem_wq, mem_wkv, mem_q_g, mem_k_g, mem_wo, norm_ffn_g, router_group_w, router_group_b,
           router_expert_w, router_expert_b, exp_w_gate, exp_w_up, exp_w_down):
    B, S, D = x.shape
    depth = norm_mix_g.shape[0]
    lw = conv_w.shape[-1]
    n_heads = b_forget.shape[-1]
    dh = fox_q_g.shape[-1]
    mem_dh = mem_q_g.shape[-1]
    mem_heads = mem_wq.shape[-1] // mem_dh
    n_experts = router_expert_w.shape[-1]
    T = B * S
    tm, tq, blk, scatter_tm, combine_tm = _tiles(S, T)

    for l in range(depth):
        y_lru, q, k, v = _in_proj(
            x, norm_mix_g[l], w_in[l], b_forget[l], fox_q_g[l], fox_k_g[l], conv_w[l], conv_b[l],
            lru_wa[l], lru_ba[l], lru_wx[l], lru_bx[l], lru_a_param[l], lru_out_g[l],
            lw=lw, n_heads=n_heads, dh=dh, tm=tm)
        y_fox = _fox_attn(q, k, v, n_heads=n_heads, dh=dh, tq=tq)
        k_mem, v_mem = _mem_kv(mem, norm_mem_g[l], mem_wkv[l], mem_k_g[l], n_heads=mem_heads,
                               dh=mem_dh)
        x2, xn, gates, experts = _out_mem(
            x, y_lru, y_fox, fox_out_g[l], w_out[l], norm_mem_x_g[l], mem_wq[l], mem_q_g[l], k_mem,
            v_mem, mem_wo[l], norm_ffn_g[l], router_group_w[l], router_group_b[l],
            router_expert_w[l], router_expert_b[l], n_heads=mem_heads, dh=mem_dh, tm=tm)

        dest, blk_expert, zero_blocks = _dispatch(
            experts.reshape(T, LANES)[:, :TOP_K], n_experts, blk)
        chunks = D // LANES
        tiled = lambda t: (dest * chunks).reshape(T // t, 1, t * TOP_K)
        x_buf = _scatter_rows(xn, tiled(scatter_tm), zero_blocks, tm=scatter_tm, blk=blk,
                              chunks=chunks)
        y_buf = _experts(x_buf, blk_expert, exp_w_gate[l], exp_w_up[l], exp_w_down[l], blk=blk)
        x = _combine(x2.reshape(T, D), gates.reshape(T, LANES), y_buf, tiled(combine_tm),
                     tm=combine_tm)
        x = x.reshape(B, S, D)
    return x
```

```python
import functools
from typing import NamedTuple

import jax
import jax.numpy as jnp
import numpy as np
from jax import lax
from jax.experimental import pallas as pl
from jax.experimental.pallas import tpu as pltpu

EPS = 1e-6
LRU_C = 8.0
CONV_WIDTH = 4
TOP_K = 2
LANES = 128
SUBLANES = 8
MXU_DIM = 256
HEAD_SLOT = LANES
VMEM_LIMIT = 56 * 1024 * 1024

F32 = jnp.float32
BF16 = jnp.bfloat16
MIN_NORMAL = float(np.finfo(np.float32).tiny)


def _dot(a, b):
    return jnp.dot(a, b, preferred_element_type=F32)


def _rms(x, g):
    return x * lax.rsqrt(jnp.mean(x * x, axis=-1, keepdims=True) + EPS) * g


def _softplus(x):
    return jnp.maximum(x, 0.0) + jnp.log1p(jnp.exp(-jnp.abs(x)))


def _pick_tile(n, target):
    t = min(n, target)
    while n % t:
        t //= 2
    return t


def _store_chunk_rows(ref, val):
    n, d = val.shape
    chunks = d // LANES
    for c in range(chunks):
        ref[pl.ds(c, n, stride=chunks), :] = val[:, c * LANES:(c + 1) * LANES]


def _load_chunk_rows(ref, n, d):
    chunks = d // LANES
    return jnp.concatenate([ref[pl.ds(c, n, stride=chunks), :] for c in range(chunks)], axis=1)


def _in_proj_kernel(x_ref, g_ref, w_ref, wf_ref, bf_ref, gq_ref, gk_ref, pe_ref, cv_ref,
                    cw_ref, cb_ref, wl_ref, ba_ref, bx_ref, ap_ref, og_ref,
                    ylru_ref, q_ref, k_ref, v_ref, carry_ref, tail_ref, h_ref, *, lw, n_heads, dh):
    @pl.when(pl.program_id(1) == 0)
    def _():
        carry_ref[...] = jnp.zeros_like(carry_ref)
        tail_ref[...] = jnp.zeros_like(tail_ref)
        h_ref[...] = jnp.zeros_like(h_ref)

    hs = n_heads * HEAD_SLOT
    x = x_ref[0]
    tm = x.shape[0]
    hb = _rms(x, g_ref[...]).astype(BF16)

    u = _dot(hb, w_ref[:, 0:lw])
    gate = _dot(hb, w_ref[:, lw:2 * lw])
    ylru_ref[0] = _rg_lru_tile(u, gate, cw_ref, cb_ref, wl_ref, ba_ref, bx_ref, ap_ref, og_ref,
                               tail_ref, h_ref).astype(ylru_ref.dtype)

    z = _dot(hb, wf_ref[...]) + bf_ref[...]
    lane = lax.broadcasted_iota(jnp.int32, z.shape, 1)
    row = lax.broadcasted_iota(jnp.int32, z.shape, 0)
    c = jnp.where(lane < n_heads, -_softplus(-z), 0.0)
    d = 1
    while d < tm:
        c = c + jnp.where(row >= d, pltpu.roll(c, d, 0), 0.0)
        d *= 2
    c = c + carry_ref[...]
    carry_ref[...] = c[tm - 1:tm, :]
    c1 = c.astype(BF16).astype(F32)
    r1 = c - c1
    c2 = r1.astype(BF16).astype(F32)
    c3 = (r1 - c2).astype(BF16).astype(F32)
    e = c1 + pltpu.roll(c2, n_heads, 1) + pltpu.roll(c3, 2 * n_heads, 1)
    e = jnp.where(lane == 3 * n_heads, 1.0, e).astype(BF16)
    ext = _dot(e, pe_ref[...])

    fw = n_heads * dh
    lane_t = lax.broadcasted_iota(jnp.int32, (tm, LANES), 1)
    own = (lane_t < dh, lane_t >= dh)

    def to_slots(y):
        parts = []
        for c in range(fw // LANES):
            col = y[:, c * LANES:(c + 1) * LANES]
            parts += [jnp.where(own[0], col, 0.0), jnp.where(own[1], col, 0.0)]
        return jnp.concatenate(parts, axis=-1)

    def head_norm(y, gain):
        parts = []
        for h in range(n_heads):
            blk = y[:, h * HEAD_SLOT:(h + 1) * HEAD_SLOT]
            ss = jnp.sum(blk * blk, axis=-1, keepdims=True) * (1.0 / dh)
            parts.append(blk * lax.rsqrt(ss + EPS))
        return jnp.concatenate(parts, axis=-1) * gain

    q = to_slots(_dot(hb, w_ref[:, 2 * lw:2 * lw + fw]))
    q_ref[0] = (head_norm(q, gq_ref[...]) + ext[:, 0:hs]).astype(BF16)
    k = to_slots(_dot(hb, w_ref[:, 2 * lw + fw:2 * lw + 2 * fw]))
    k_ref[0] = (head_norm(k, gk_ref[...]) + ext[:, hs:2 * hs]).astype(BF16)
    v = to_slots(_dot(hb, w_ref[:, 2 * lw + 2 * fw:2 * lw + 3 * fw]))
    v_ref[0] = (v + cv_ref[...]).astype(BF16)


def _in_proj(x, norm_g, w_in, b_forget, fox_q_g, fox_k_g, conv_w, conv_b, wa, ba, wx, bx, a_param,
             lru_out_g, *, lw, n_heads, dh, tm):
    B, S, D = x.shape
    fw = n_heads * dh
    hs = n_heads * HEAD_SLOT
    assert 2 * dh == HEAD_SLOT and fw % LANES == 0 and 3 * n_heads <= LANES
    n_main = 2 * lw + 3 * fw
    w_all = w_in[:, :n_main].astype(BF16)
    w_f = jnp.pad(w_in[:, n_main:], ((0, 0), (0, LANES - n_heads))).astype(BF16)
    b_f = jnp.pad(b_forget, (0, LANES - n_heads)).reshape(1, LANES)
    scale = dh ** -0.5

    def slot_gains(g):
        even = jnp.pad(g, (0, HEAD_SLOT - dh))
        odd = jnp.pad(g, (HEAD_SLOT - dh, 0))
        return jnp.tile(jnp.concatenate([even, odd]), n_heads // 2).reshape(1, hs)

    gq = slot_gains(fox_q_g * scale)
    gk = slot_gains(fox_k_g)
    ones_row = 3 * n_heads
    pe = np.zeros((LANES, 2 * hs), np.float32)
    cv = np.zeros((1, hs), np.float32)
    for h in range(n_heads):
        x0 = h * HEAD_SLOT + (dh if h % 2 == 0 else 0)
        for j in range(3):
            pe[j * n_heads + h, x0 + j] = 1.0
            pe[ones_row, x0 + 3 + j] = 1.0
            pe[ones_row, hs + x0 + j] = 1.0
            pe[j * n_heads + h, hs + x0 + 3 + j] = -1.0
        cv[0, x0] = 1.0
    pe = jnp.asarray(pe, BF16)

    w_lru = _lru_gate_weights(wa, wx, lw)
    vec = lambda a: a.reshape(1, lw)
    full = lambda shape: pl.BlockSpec(shape, lambda b, s: (0,) * len(shape))
    row = lambda width: pl.BlockSpec((1, tm, width), lambda b, s: (b, s, 0))
    return pl.pallas_call(
        functools.partial(_in_proj_kernel, lw=lw, n_heads=n_heads, dh=dh),
        grid=(B, S // tm),
        in_specs=[row(D), full((1, D)), full(w_all.shape), full(w_f.shape), full((1, LANES)),
                  full((1, hs)), full((1, hs)), full(pe.shape), full((1, hs)),
                  full((CONV_WIDTH, lw)), full((1, lw)), full(w_lru.shape), full((1, lw)),
                  full((1, lw)), full((1, lw)), full((1, lw))],
        out_specs=[row(lw), row(hs), row(hs), row(hs)],
        out_shape=[jax.ShapeDtypeStruct((B, S, lw), BF16),
                   jax.ShapeDtypeStruct((B, S, hs), BF16), jax.ShapeDtypeStruct((B, S, hs), BF16),
                   jax.ShapeDtypeStruct((B, S, hs), BF16)],
        scratch_shapes=[pltpu.VMEM((1, LANES), F32), pltpu.VMEM((SUBLANES, lw), F32),
                        pltpu.VMEM((1, lw), F32)],
        compiler_params=pltpu.CompilerParams(
            dimension_semantics=("parallel", "arbitrary"), vmem_limit_bytes=VMEM_LIMIT),
        name="in_proj",
    )(x, norm_g.reshape(1, D), w_all, w_f, b_f, gq, gk, pe, jnp.asarray(cv),
      conv_w, vec(conv_b), w_lru, vec(ba), vec(bx), vec(a_param), vec(lru_out_g))


def _rg_lru_tile(u, gate, cw_ref, cb_ref, w_ref, ba_ref, bx_ref, ap_ref, og_ref, tail_ref, h_ref):
    ts, lw = u.shape
    ext = jnp.concatenate([tail_ref[...], u], axis=0)
    tail_ref[...] = u[ts - SUBLANES:, :]
    xc = cb_ref[...] + u * cw_ref[CONV_WIDTH - 1:CONV_WIDTH, :]
    for back in range(1, CONV_WIDTH):
        tap = CONV_WIDTH - 1 - back
        xc = xc + pltpu.roll(ext, back, 0)[SUBLANES:, :] * cw_ref[tap:tap + 1, :]

    xb = xc.astype(BF16)
    r_parts, i_parts = [], []
    for j in range(lw // MXU_DIM):
        y = _dot(xb[:, j * MXU_DIM:(j + 1) * MXU_DIM], w_ref[j])
        r_parts.append(y[:, :MXU_DIM])
        i_parts.append(y[:, MXU_DIM:])
    r = jax.nn.sigmoid(jnp.concatenate(r_parts, axis=-1) + ba_ref[...])
    i = jax.nn.sigmoid(jnp.concatenate(i_parts, axis=-1) + bx_ref[...])
    log_a = (-LRU_C) * r * _softplus(-ap_ref[...])
    a = jnp.exp(log_a)
    gap = jnp.maximum(-jnp.tanh(log_a) * (a * a + 1.0), 0.0)
    mult = gap * lax.rsqrt(jnp.maximum(gap, MIN_NORMAL))
    b = mult * (i * xc)

    groups = ts // SUBLANES
    a = a.reshape(groups, SUBLANES, lw)
    b = b.reshape(groups, SUBLANES, lw)
    row = lax.broadcasted_iota(jnp.int32, a.shape, 1)
    d = 1
    while d < SUBLANES:
        keep = row >= d
        a_prev = jnp.where(keep, pltpu.roll(a, d, 1), 1.0)
        b_prev = jnp.where(keep, pltpu.roll(b, d, 1), 0.0)
        b = a * b_prev + b
        a = a * a_prev
        d *= 2
    state = h_ref[...]
    parts = []
    for g in range(groups):
        hg = b[g] + a[g] * state
        state = hg[SUBLANES - 1:, :]
        parts.append(hg)
    h = jnp.concatenate(parts, axis=0)
    h_ref[...] = state

    y = h * jax.nn.gelu(gate)
    return _rms(y, og_ref[...])


def _lru_gate_weights(wa, wx, lw):
    _, bd, _ = wa.shape
    per = MXU_DIM // bd
    n_tiles = lw // MXU_DIM

    def tiles(w):
        w = w.reshape(n_tiles, per, bd, bd)
        eye = jnp.eye(per, dtype=w.dtype)
        return jnp.einsum('tpij,pq->tpiqj', w, eye).reshape(n_tiles, MXU_DIM, MXU_DIM)

    return jnp.concatenate([tiles(wa), tiles(wx)], axis=-1).astype(BF16)


OVERFLOW_GUARD = 1e30


def _fox_attn_kernel(q_ref, k_ref, v_ref, o_ref, vt_ref, m_ref, acc_ref, *, dh, tq):
    n_chunks = vt_ref.shape[0]
    for c in range(n_chunks):
        vt_ref[c] = v_ref[0, c * tq:(c + 1) * tq, :].astype(F32).T.astype(BF16)

    heads = [slice(hh * HEAD_SLOT, (hh + 1) * HEAD_SLOT) for hh in range(2)]
    lax.fori_loop(0, n_chunks, functools.partial(
        _fox_attn_tile, q_ref=q_ref, k_ref=k_ref, o_ref=o_ref, vt_ref=vt_ref, m_ref=m_ref,
        acc_ref=acc_ref, heads=heads, dh=dh, tq=tq), 0)


def _fox_attn_tile(qi, carry, *, q_ref, k_ref, o_ref, vt_ref, m_ref, acc_ref, heads, dh, tq):
    q_rows = pl.ds(pl.multiple_of(qi * tq, tq), tq)

    def scores(j, q, sl, masked):
        start = pl.multiple_of(j * tq, tq)
        st = lax.dot_general(k_ref[0, pl.ds(start, tq), sl], q, (((1,), (1,)), ((), ())),
                             preferred_element_type=F32)
        if masked:
            kpos = lax.broadcasted_iota(jnp.int32, st.shape, 0)
            qpos = lax.broadcasted_iota(jnp.int32, st.shape, 1)
            st = jnp.where(kpos <= qpos, st, -jnp.inf)
        return st

    for hh, sl in enumerate(heads):
        st = scores(qi, q_ref[0, q_rows, sl], sl, True)
        m = jnp.max(st, axis=0, keepdims=True)
        m_ref[hh] = m
        acc_ref[hh] = _dot(vt_ref[qi, sl, :], jnp.exp(st - m).astype(BF16))

    def fast_chunk(j):
        for hh, sl in enumerate(heads):
            pt = jnp.exp(scores(j, q_ref[0, q_rows, sl], sl, False) - m_ref[hh]).astype(BF16)
            acc_ref[hh] += _dot(vt_ref[j, sl, :], pt)

    def fast_pair(i, carry):
        start = pl.multiple_of(2 * i * tq, 2 * tq)
        for hh, sl in enumerate(heads):
            st = lax.dot_general(k_ref[0, pl.ds(start, 2 * tq), sl], q_ref[0, q_rows, sl],
                                 (((1,), (1,)), ((), ())), preferred_element_type=F32)
            pt = jnp.exp(st - m_ref[hh]).astype(BF16)
            vt = jnp.concatenate([vt_ref[2 * i, sl, :], vt_ref[2 * i + 1, sl, :]], axis=1)
            acc_ref[hh] += _dot(vt, pt)
        return carry

    lax.fori_loop(0, qi // 2, fast_pair, 0)

    @pl.when(qi % 2 == 1)
    def _():
        fast_chunk(qi - 1)

    def softmax_sums():
        return jnp.concatenate([acc_ref[0][dh:dh + 1, :], acc_ref[1][0:1, :]], axis=0)

    def normalised():
        sums = softmax_sums()
        return jnp.concatenate([acc_ref[0][0:dh, :] / sums[0:1, :],
                                acc_ref[1][dh:2 * dh, :] / sums[1:2, :]], axis=0)

    out = normalised()
    o_ref[0, q_rows, :] = out.T.astype(o_ref.dtype)
    sums = softmax_sums()
    flag = lambda x: jnp.max(jnp.where(jnp.abs(x) < OVERFLOW_GUARD, 0.0, 1.0),
                             axis=0, keepdims=True)
    bad = jnp.max(jnp.maximum(flag(out), flag(sums)), axis=1, keepdims=True)

    @pl.when(bad[0, 0] > 0.0)
    def _():
        m_ref[...] = jnp.full_like(m_ref, -jnp.inf)
        acc_ref[...] = jnp.zeros_like(acc_ref)

        def chunk(j, masked):
            for hh, sl in enumerate(heads):
                st = scores(j, q_ref[0, q_rows, sl], sl, masked)
                m_old = m_ref[hh]
                m_new = jnp.maximum(m_old, jnp.max(st, axis=0, keepdims=True))
                pt = jnp.exp(st - m_new).astype(BF16)
                acc_ref[hh] = jnp.exp(m_old - m_new) * acc_ref[hh] + _dot(vt_ref[j, sl, :], pt)
                m_ref[hh] = m_new

        def body(j, carry):
            chunk(j, False)
            return carry

        lax.fori_loop(0, qi, body, 0)
        chunk(qi, True)
        o_ref[0, q_rows, :] = normalised().T.astype(o_ref.dtype)

    return carry


def _fox_attn(q, k, v, *, n_heads, dh, tq):
    B, S, _ = q.shape
    nq = S // tq
    pair = 2 * HEAD_SLOT
    qkv_spec = pl.BlockSpec((1, S, pair), lambda b, h: (b, 0, h))
    return pl.pallas_call(
        functools.partial(_fox_attn_kernel, dh=dh, tq=tq),
        grid=(B, n_heads // 2),
        in_specs=[qkv_spec, qkv_spec, qkv_spec],
        out_specs=pl.BlockSpec((1, S, 2 * dh), lambda b, h: (b, 0, h)),
        out_shape=jax.ShapeDtypeStruct((B, S, n_heads * dh), BF16),
        scratch_shapes=[pltpu.VMEM((nq, pair, tq), BF16), pltpu.VMEM((2, 1, tq), F32),
                        pltpu.VMEM((2, HEAD_SLOT, tq), F32)],
        compiler_params=pltpu.CompilerParams(
            dimension_semantics=("parallel", "parallel"), vmem_limit_bytes=VMEM_LIMIT),
        name="fox_attn",
    )(q, k, v)


def _mem_kv_kernel(m_ref, g_ref, w_ref, kg_ref, k_ref, v_ref, *, n_heads, dh):
    mw = n_heads * dh
    mb = _rms(m_ref[0], g_ref[...]).astype(BF16)
    kv = _dot(mb, w_ref[...])
    parts = []
    for h in range(n_heads):
        parts.append(_rms(kv[:, h * dh:(h + 1) * dh], kg_ref[...]))
    k_ref[0] = jnp.concatenate(parts, axis=-1).astype(BF16)
    v_ref[0] = kv[:, mw:].astype(BF16)


def _mem_kv(mem, norm_g, wkv, k_g, *, n_heads, dh):
    B, M, D = mem.shape
    mw = n_heads * dh
    full = lambda shape: pl.BlockSpec(shape, lambda b: (0,) * len(shape))
    out = pl.BlockSpec((1, M, mw), lambda b: (b, 0, 0))
    return pl.pallas_call(
        functools.partial(_mem_kv_kernel, n_heads=n_heads, dh=dh),
        grid=(B,),
        in_specs=[pl.BlockSpec((1, M, D), lambda b: (b, 0, 0)), full((1, D)), full((D, 2 * mw)),
                  full((1, dh))],
        out_specs=[out, out],
        out_shape=[jax.ShapeDtypeStruct((B, M, mw), BF16)] * 2,
        compiler_params=pltpu.CompilerParams(
            dimension_semantics=("parallel",), vmem_limit_bytes=VMEM_LIMIT),
        name="mem_kv",
    )(mem, norm_g.reshape(1, D), wkv.astype(BF16), k_g.reshape(1, dh))


def _out_mem_kernel(x_ref, yl_ref, yf_ref, fg_ref, wol_ref, wof_ref, gx_ref, wq_ref, qg_ref,
                    km_ref, vm_ref, wo_ref, gf_ref, wr_ref, br_ref,
                    x2_ref, xn_ref, gates_ref, experts_ref, *, n_heads, dh, n_groups, per_group):
    yf = _rms(yf_ref[0].astype(F32), fg_ref[...]).astype(BF16)
    x1 = x_ref[0] + _dot(yl_ref[0], wol_ref[...]) + _dot(yf, wof_ref[...])

    q = _dot(_rms(x1, gx_ref[...]).astype(BF16), wq_ref[...])
    outs = []
    for h in range(n_heads):
        sl = slice(h * dh, (h + 1) * dh)
        qh = _rms(q[:, sl], qg_ref[...]).astype(BF16)
        s = lax.dot_general(qh, km_ref[0, :, sl], (((1,), (1,)), ((), ())),
                            preferred_element_type=F32)
        p = jnp.exp(s - jnp.max(s, axis=-1, keepdims=True))
        inv = 1.0 / jnp.sum(p, axis=-1, keepdims=True)
        outs.append(_dot(p.astype(BF16), vm_ref[0, :, sl]) * inv)
    x2 = x1 + _dot(jnp.concatenate(outs, axis=-1).astype(BF16), wo_ref[...])
    x2_ref[0] = x2

    xn = _rms(x2, gf_ref[...])
    _store_chunk_rows(xn_ref, xn)
    logits = _dot(xn.astype(BF16), wr_ref[...]) + br_ref[...]
    lane = lax.broadcasted_iota(jnp.int32, logits.shape, 1).astype(F32)
    neg = -jnp.inf

    def top(vals):
        mx = jnp.max(vals, axis=-1, keepdims=True)
        idx = jnp.min(jnp.where(vals == mx, lane, float(LANES)), axis=-1, keepdims=True)
        return mx, idx

    gl = jnp.where(lane < n_groups, logits, neg)
    g_max, g_idx = top(gl)
    g_w = 1.0 / jnp.sum(jnp.exp(gl - g_max), axis=-1, keepdims=True)
    lo = n_groups + per_group * g_idx
    el = jnp.where((lane >= lo) & (lane < lo + per_group), logits, neg)
    e1, i1 = top(el)
    e2, i2 = top(jnp.where(lane == i1, neg, el))
    t = jnp.exp(e2 - e1)
    w1 = g_w / (1.0 + t)
    w2 = g_w * t / (1.0 + t)
    gates_ref[0] = jnp.where(lane == 0, w1, jnp.where(lane == 1, w2, 0.0))
    experts_ref[0] = jnp.where(lane == 0, i1 - n_groups,
                               jnp.where(lane == 1, i2 - n_groups, 0.0)).astype(jnp.int32)


def _out_mem(x, y_lru, y_fox, fox_out_g, w_out, norm_mem_x_g, mem_wq, mem_q_g, k_mem, v_mem, mem_wo,
             norm_ffn_g, router_group_w, router_group_b, router_expert_w, router_expert_b, *,
             n_heads, dh, tm):
    B, S, D = x.shape
    lw = y_lru.shape[-1]
    fw = y_fox.shape[-1]
    M = k_mem.shape[1]
    mw = n_heads * dh
    n_groups = router_group_w.shape[-1]
    n_experts = router_expert_w.shape[-1]
    w_r = jnp.concatenate([router_group_w, router_expert_w], axis=1)
    w_r = jnp.pad(w_r, ((0, 0), (0, LANES - w_r.shape[1]))).astype(BF16)
    b_r = jnp.concatenate([router_group_b, router_expert_b])
    b_r = jnp.pad(b_r, (0, LANES - b_r.shape[0])).reshape(1, LANES)
    q_gain = (mem_q_g * dh ** -0.5).reshape(1, dh)

    full = lambda shape: pl.BlockSpec(shape, lambda b, s: (0,) * len(shape))
    row = lambda width: pl.BlockSpec((1, tm, width), lambda b, s: (b, s, 0))
    mem_spec = pl.BlockSpec((1, M, mw), lambda b, s: (b, 0, 0))
    return pl.pallas_call(
        functools.partial(_out_mem_kernel, n_heads=n_heads, dh=dh, n_groups=n_groups,
                          per_group=n_experts // n_groups),
        grid=(B, S // tm),
        in_specs=[row(D), row(lw), row(fw), full((1, fw)), full((lw, D)), full((fw, D)),
                  full((1, D)), full((D, mw)), full((1, dh)), mem_spec, mem_spec, full((mw, D)),
                  full((1, D)), full((D, LANES)), full((1, LANES))],
        out_specs=[row(D), pl.BlockSpec((tm * D // LANES, LANES), lambda b, s: (b * (S // tm) + s, 0)),
                   row(LANES), row(LANES)],
        out_shape=[jax.ShapeDtypeStruct((B, S, D), F32),
                   jax.ShapeDtypeStruct((B * S * D // LANES, LANES), F32),
                   jax.ShapeDtypeStruct((B, S, LANES), F32),
                   jax.ShapeDtypeStruct((B, S, LANES), jnp.int32)],
        compiler_params=pltpu.CompilerParams(
            dimension_semantics=("parallel", "parallel"), vmem_limit_bytes=VMEM_LIMIT),
        name="out_mem",
    )(x, y_lru, y_fox, fox_out_g.reshape(1, fw), w_out[:lw].astype(BF16), w_out[lw:].astype(BF16),
      norm_mem_x_g.reshape(1, D), mem_wq.astype(BF16), q_gain, k_mem, v_mem, mem_wo.astype(BF16),
      norm_ffn_g.reshape(1, D), w_r, b_r)


DMA_UNROLL = 16
DMA_THREADS = 2
RANK_GROUP = 256


def _scatter_rows_kernel(zb_ref, idx_ref, x_ref, buf_ref, zero_ref, sem, zsem, *, top_k, chunks):
    tokens = x_ref.shape[0] // chunks

    @pl.when(pl.program_id(0) == 0)
    def _():
        rows = zero_ref.shape[0]
        zero_ref[...] = jnp.zeros_like(zero_ref)

        def copy(j):
            start = pl.multiple_of(j * rows, rows)
            return pltpu.make_async_copy(zero_ref, buf_ref.at[pl.ds(start, rows), :], zsem)

        def issue_zero(j, carry):
            @pl.when(zb_ref[j] == 1)
            def _():
                copy(j).start()
            return carry

        def drain_zero(j, carry):
            @pl.when(zb_ref[j] == 1)
            def _():
                copy(j).wait()
            return carry

        lax.fori_loop(0, zb_ref.shape[0], issue_zero, 0)
        lax.fori_loop(0, zb_ref.shape[0], drain_zero, 0)

    def issue(g, carry):
        for u in range(DMA_UNROLL):
            r = g * DMA_UNROLL + u
            src = x_ref.at[pl.ds(pl.multiple_of(r * chunks, chunks), chunks), :]
            for k in range(top_k):
                row = pl.multiple_of(idx_ref[0, 0, r * top_k + k], chunks)
                pltpu.make_async_copy(src, buf_ref.at[pl.ds(row, chunks), :], sem).start(
                    priority=(u * top_k + k) % DMA_THREADS)
        return carry

    lax.fori_loop(0, tokens // DMA_UNROLL, issue, 0)
    for k in range(top_k):
        pltpu.make_async_copy(x_ref, buf_ref.at[pl.ds(0, tokens * chunks), :], sem).wait()


def _scatter_rows(x, dest, zero_blocks, *, tm, blk, chunks):
    nt = dest.shape[0]
    top_k = dest.shape[2] // tm
    n_blocks = zero_blocks.shape[0]
    grid_spec = pltpu.PrefetchScalarGridSpec(
        num_scalar_prefetch=1,
        grid=(nt,),
        in_specs=[pl.BlockSpec((1, 1, tm * top_k), lambda i, zb: (i, 0, 0), memory_space=pltpu.SMEM),
                  pl.BlockSpec((tm * chunks, LANES), lambda i, zb: (i, 0))],
        out_specs=pl.BlockSpec(memory_space=pl.ANY),
        scratch_shapes=[pltpu.VMEM((blk * chunks, LANES), x.dtype), pltpu.SemaphoreType.DMA,
                        pltpu.SemaphoreType.DMA],
    )
    return pl.pallas_call(
        functools.partial(_scatter_rows_kernel, top_k=top_k, chunks=chunks),
        grid_spec=grid_spec,
        out_shape=jax.ShapeDtypeStruct((n_blocks * blk * chunks, LANES), x.dtype),
        compiler_params=pltpu.CompilerParams(
            dimension_semantics=("arbitrary",), vmem_limit_bytes=VMEM_LIMIT),
        name="scatter_rows",
    )(zero_blocks, dest, x)


X_RING = 3
X_PRIORITY = 1


def _experts_kernel(be_ref, x_hbm, wg_ref, wu_ref, wd_ref, o_ref, xbuf_ref, wgb_ref, wub_ref, wdb_ref,
                    sem, *, blk):
    i = pl.program_id(0)
    n = pl.num_programs(0)
    rows = xbuf_ref.shape[1]

    def fetch(step):
        slot = lax.rem(step, X_RING)
        src = x_hbm.at[pl.ds(pl.multiple_of(step * rows, rows), rows), :]
        return pltpu.make_async_copy(src, xbuf_ref.at[slot], sem.at[slot])

    @pl.when(i == 0)
    def _():
        for step in range(X_RING - 1):
            @pl.when(step < n)
            def _():
                fetch(step).start(priority=X_PRIORITY)

    @pl.when(i + X_RING - 1 < n)
    def _():
        fetch(i + X_RING - 1).start(priority=X_PRIORITY)

    @pl.when((i == 0) | (be_ref[i] != be_ref[jnp.maximum(i - 1, 0)]))
    def _():
        wgb_ref[...] = wg_ref[0].astype(BF16)
        wub_ref[...] = wu_ref[0].astype(BF16)
        wdb_ref[...] = wd_ref[0].astype(BF16)

    fetch(i).wait()
    d = wg_ref.shape[1]
    xb = _load_chunk_rows(xbuf_ref.at[lax.rem(i, X_RING)], blk, d).astype(BF16)
    hdn = jax.nn.silu(_dot(xb, wgb_ref[...])) * _dot(xb, wub_ref[...])
    _store_chunk_rows(o_ref, _dot(hdn.astype(BF16), wdb_ref[...]))


def _experts(x_buf, blk_expert, w_gate, w_up, w_down, *, blk):
    _, D, de = w_gate.shape
    chunks = D // LANES
    n_blocks = x_buf.shape[0] // (blk * chunks)
    rows_spec = pl.BlockSpec((blk * chunks, LANES), lambda i, be: (i, 0))
    grid_spec = pltpu.PrefetchScalarGridSpec(
        num_scalar_prefetch=1,
        grid=(n_blocks,),
        in_specs=[pl.BlockSpec(memory_space=pl.ANY),
                  pl.BlockSpec((1, D, de), lambda i, be: (be[i], 0, 0)),
                  pl.BlockSpec((1, D, de), lambda i, be: (be[i], 0, 0)),
                  pl.BlockSpec((1, de, D), lambda i, be: (be[i], 0, 0))],
        out_specs=rows_spec,
        scratch_shapes=[pltpu.VMEM((X_RING, blk * chunks, LANES), F32),
                        pltpu.VMEM((D, de), BF16), pltpu.VMEM((D, de), BF16),
                        pltpu.VMEM((de, D), BF16), pltpu.SemaphoreType.DMA((X_RING,))],
    )
    return pl.pallas_call(
        functools.partial(_experts_kernel, blk=blk),
        grid_spec=grid_spec,
        out_shape=jax.ShapeDtypeStruct(x_buf.shape, F32),
        compiler_params=pltpu.CompilerParams(
            dimension_semantics=("arbitrary",), vmem_limit_bytes=VMEM_LIMIT),
        name="experts",
    )(blk_expert, x_buf, w_gate, w_up, w_down)


def _combine_kernel(idx_ref, idx_next_ref, x_ref, g_ref, y_ref, o_ref, buf_ref, sem, *, top_k, chunks):
    i = pl.program_id(0)
    n = pl.num_programs(0)
    slot = lax.rem(i, 2)
    tokens, d = x_ref.shape

    def request(ids_ref, s):
        def issue(g, carry):
            for u in range(DMA_UNROLL):
                r = g * DMA_UNROLL + u
                for k in range(top_k):
                    row = pl.multiple_of(ids_ref[0, 0, r * top_k + k], chunks)
                    dst = buf_ref.at[s, k, pl.ds(pl.multiple_of(r * chunks, chunks), chunks), :]
                    pltpu.make_async_copy(y_ref.at[pl.ds(row, chunks), :], dst, sem.at[s]).start(
                        priority=(u * top_k + k) % DMA_THREADS)
            return carry
        lax.fori_loop(0, tokens // DMA_UNROLL, issue, 0)

    @pl.when(i == 0)
    def _():
        request(idx_ref, 0)

    @pl.when(i + 1 < n)
    def _():
        request(idx_next_ref, 1 - slot)

    for k in range(top_k):
        pltpu.make_async_copy(y_ref.at[pl.ds(0, tokens * chunks), :], buf_ref.at[slot, k],
                              sem.at[slot]).wait()
    out = x_ref[...]
    for k in range(top_k):
        out = out + g_ref[:, k:k + 1] * _load_chunk_rows(buf_ref.at[slot, k], tokens, d)
    o_ref[...] = out


def _combine(x2, gates, y_buf, dest, *, tm):
    T, D = x2.shape
    chunks = D // LANES
    nt = dest.shape[0]
    top_k = dest.shape[2] // tm
    idx_spec = lambda step: pl.BlockSpec(
        (1, 1, tm * top_k), lambda i: (jnp.minimum(i + step, nt - 1), 0, 0), memory_space=pltpu.SMEM)
    return pl.pallas_call(
        functools.partial(_combine_kernel, top_k=top_k, chunks=chunks),
        grid=(nt,),
        in_specs=[idx_spec(0), idx_spec(1),
                  pl.BlockSpec((tm, D), lambda i: (i, 0)),
                  pl.BlockSpec((tm, LANES), lambda i: (i, 0)),
                  pl.BlockSpec(memory_space=pl.ANY)],
        out_specs=pl.BlockSpec((tm, D), lambda i: (i, 0)),
        out_shape=jax.ShapeDtypeStruct((T, D), F32),
        scratch_shapes=[pltpu.VMEM((2, top_k, tm * chunks, LANES), F32),
                        pltpu.SemaphoreType.DMA((2,))],
        compiler_params=pltpu.CompilerParams(
            dimension_semantics=("arbitrary",), vmem_limit_bytes=VMEM_LIMIT),
        name="combine",
    )(dest, dest, x2, gates, y_buf)


def _dispatch(experts, n_experts, blk):
    T, K = experts.shape
    A = T * K
    e_flat = experts.reshape(A)
    g = _pick_tile(A, RANK_GROUP)
    onehot = (e_flat.reshape(A // g, g, 1) == jnp.arange(n_experts, dtype=jnp.int32)).astype(BF16)
    tri = jnp.tril(jnp.ones((g, g), BF16))
    within = jnp.einsum('ij,gje->gie', tri, onehot, preferred_element_type=F32)
    group_counts = within[:, -1, :]
    before = jnp.cumsum(group_counts, axis=0) - group_counts
    counts = jnp.sum(group_counts, axis=0).astype(jnp.int32)
    padded = (counts + blk - 1) // blk * blk
    pends = jnp.cumsum(padded)
    pstarts = pends - padded
    row = within + before[:, None, :] - 1.0 + pstarts.astype(F32)
    dest = jnp.sum(onehot.astype(F32) * row, axis=-1).astype(jnp.int32).reshape(A)
    n_blocks = (A + n_experts * blk) // blk
    blk_start = jnp.arange(n_blocks, dtype=jnp.int32) * blk
    blk_expert = jnp.minimum(jnp.sum(blk_start[:, None] >= pends[None, :], axis=1), n_experts - 1)
    last_of_segment = jnp.any((blk_start[:, None] + blk == pends[None, :]) & (padded[None, :] > 0),
                              axis=1)
    zero_blocks = (last_of_segment | (blk_start >= pends[-1])).astype(jnp.int32)
    return dest.reshape(T, K), blk_expert.astype(jnp.int32), zero_blocks


class _Tiles(NamedTuple):
    rows: int
    q_rows: int
    expert_rows: int
    scatter_rows: int
    combine_rows: int


def _tiles(seq, tokens):
    return _Tiles(rows=_pick_tile(seq, 1024), q_rows=_pick_tile(seq, 1024), expert_rows=512,
                  scatter_rows=_pick_tile(tokens, 1024), combine_rows=_pick_tile(tokens, 512))


def kernel(x, mem, norm_mix_g, w_in, b_forget, conv_w, conv_b, lru_wa, lru_ba, lru_wx, lru_bx,
           lru_a_param, fox_q_g, fox_k_g, lru_out_g, fox_out_g, w_out, norm_mem_x_g, norm_mem_g,
           mem_wq, mem_wkv, mem_q_g, mem_k_g, mem_wo, norm_ffn_g, router_group_w, router_group_b,
           router_expert_w, router_expert_b, exp_w_gate, exp_w_up, exp_w_down):
    B, S, D = x.shape
    depth = norm_mix_g.shape[0]
    lw = conv_w.shape[-1]
    n_heads = b_forget.shape[-1]
    dh = fox_q_g.shape[-1]
    mem_dh = mem_q_g.shape[-1]
    mem_heads = mem_wq.shape[-1] // mem_dh
    n_experts = router_expert_w.shape[-1]
    T = B * S
    tm, tq, blk, scatter_tm, combine_tm = _tiles(S, T)
    assert D % (2 * LANES) == 0 and lw % MXU_DIM == 0 and MXU_DIM % lru_wa.shape[-1] == 0
    assert n_heads % 2 == 0 and mem_dh % LANES == 0 and mem.shape[1] % SUBLANES == 0
    assert router_group_w.shape[-1] + n_experts <= LANES and n_experts % router_group_w.shape[-1] == 0
    assert tm % SUBLANES == 0 and tq % MXU_DIM == 0 and (T * TOP_K) % blk == 0

    for l in range(depth):
        y_lru, q, k, v = _in_proj(
            x, norm_mix_g[l], w_in[l], b_forget[l], fox_q_g[l], fox_k_g[l], conv_w[l], conv_b[l],
            lru_wa[l], lru_ba[l], lru_wx[l], lru_bx[l], lru_a_param[l], lru_out_g[l],
            lw=lw, n_heads=n_heads, dh=dh, tm=tm)
        y_fox = _fox_attn(q, k, v, n_heads=n_heads, dh=dh, tq=tq)
        k_mem, v_mem = _mem_kv(mem, norm_mem_g[l], mem_wkv[l], mem_k_g[l], n_heads=mem_heads,
                               dh=mem_dh)
        x2, xn, gates, experts = _out_mem(
            x, y_lru, y_fox, fox_out_g[l], w_out[l], norm_mem_x_g[l], mem_wq[l], mem_q_g[l], k_mem,
            v_mem, mem_wo[l], norm_ffn_g[l], router_group_w[l], router_group_b[l],
            router_expert_w[l], router_expert_b[l], n_heads=mem_heads, dh=mem_dh, tm=tm)

        dest, blk_expert, zero_blocks = _dispatch(
            experts.reshape(T, LANES)[:, :TOP_K], n_experts, blk)
        chunks = D // LANES
        tiled = lambda t: (dest * chunks).reshape(T // t, 1, t * TOP_K)
        x_buf = _scatter_rows(xn, tiled(scatter_tm), zero_blocks, tm=scatter_tm, blk=blk,
                              chunks=chunks)
        y_buf = _experts(x_buf, blk_expert, exp_w_gate[l], exp_w_up[l], exp_w_down[l], blk=blk)
        x = _combine(x2.reshape(T, D), gates.reshape(T, LANES), y_buf, tiled(combine_tm),
                     tm=combine_tm)
        x = x.reshape(B, S, D)
    return x
```

```python
import functools
from typing import NamedTuple

import jax
import jax.numpy as jnp
import numpy as np
from jax import lax
from jax.experimental import pallas as pl
from jax.experimental.pallas import tpu as pltpu

EPS = 1e-6
LRU_C = 8.0
CONV_WIDTH = 4
TOP_K = 2
LANES = 128
SUBLANES = 8
MXU_DIM = 256
HEAD_SLOT = LANES
VMEM_LIMIT = 56 * 1024 * 1024

F32 = jnp.float32
BF16 = jnp.bfloat16
MIN_NORMAL = float(np.finfo(np.float32).tiny)


def _dot(a, b):
    return jnp.dot(a, b, preferred_element_type=F32)


def _rms(x, g):
    return x * lax.rsqrt(jnp.mean(x * x, axis=-1, keepdims=True) + EPS) * g


def _softplus(x):
    return jnp.maximum(x, 0.0) + jnp.log1p(jnp.exp(-jnp.abs(x)))


def _pick_tile(n, target):
    t = min(n, target)
    while n % t:
        t //= 2
    return t


def _store_chunk_rows(ref, val):
    n, d = val.shape
    chunks = d // LANES
    for c in range(chunks):
        ref[pl.ds(c, n, stride=chunks), :] = val[:, c * LANES:(c + 1) * LANES]


def _load_chunk_rows(ref, n, d):
    chunks = d // LANES
    return jnp.concatenate([ref[pl.ds(c, n, stride=chunks), :] for c in range(chunks)], axis=1)


def _in_proj_kernel(x_ref, g_ref, w_ref, wf_ref, bf_ref, gq_ref, gk_ref, pe_ref, cv_ref,
                    cw_ref, cb_ref, wl_ref, ba_ref, bx_ref, ap_ref, og_ref,
                    ylru_ref, q_ref, k_ref, v_ref, carry_ref, tail_ref, h_ref, *, lw, n_heads, dh):
    @pl.when(pl.program_id(1) == 0)
    def _():
        carry_ref[...] = jnp.zeros_like(carry_ref)
        tail_ref[...] = jnp.zeros_like(tail_ref)
        h_ref[...] = jnp.zeros_like(h_ref)

    hs = n_heads * HEAD_SLOT
    x = x_ref[0]
    tm = x.shape[0]
    hb = _rms(x, g_ref[...]).astype(BF16)

    u = _dot(hb, w_ref[:, 0:lw])
    gate = _dot(hb, w_ref[:, lw:2 * lw])
    ylru_ref[0] = _rg_lru_tile(u, gate, cw_ref, cb_ref, wl_ref, ba_ref, bx_ref, ap_ref, og_ref,
                               tail_ref, h_ref).astype(ylru_ref.dtype)

    z = _dot(hb, wf_ref[...]) + bf_ref[...]
    lane = lax.broadcasted_iota(jnp.int32, z.shape, 1)
    row = lax.broadcasted_iota(jnp.int32, z.shape, 0)
    c = jnp.where(lane < n_heads, -_softplus(-z), 0.0)
    d = 1
    while d < tm:
        c = c + jnp.where(row >= d, pltpu.roll(c, d, 0), 0.0)
        d *= 2
    c = c + carry_ref[...]
    carry_ref[...] = c[tm - 1:tm, :]
    c1 = c.astype(BF16).astype(F32)
    r1 = c - c1
    c2 = r1.astype(BF16).astype(F32)
    c3 = (r1 - c2).astype(BF16).astype(F32)
    e = c1 + pltpu.roll(c2, n_heads, 1) + pltpu.roll(c3, 2 * n_heads, 1)
    e = jnp.where(lane == 3 * n_heads, 1.0, e).astype(BF16)
    ext = _dot(e, pe_ref[...])

    fw = n_heads * dh
    lane_t = lax.broadcasted_iota(jnp.int32, (tm, LANES), 1)
    own = (lane_t < dh, lane_t >= dh)

    def to_slots(y):
        parts = []
        for c in range(fw // LANES):
            col = y[:, c * LANES:(c + 1) * LANES]
            parts += [jnp.where(own[0], col, 0.0), jnp.where(own[1], col, 0.0)]
        return jnp.concatenate(parts, axis=-1)

    def head_norm(y, gain):
        parts = []
        for h in range(n_heads):
            blk = y[:, h * HEAD_SLOT:(h + 1) * HEAD_SLOT]
            ss = jnp.sum(blk * blk, axis=-1, keepdims=True) * (1.0 / dh)
            parts.append(blk * lax.rsqrt(ss + EPS))
        return jnp.concatenate(parts, axis=-1) * gain

    q = to_slots(_dot(hb, w_ref[:, 2 * lw:2 * lw + fw]))
    q_ref[0] = (head_norm(q, gq_ref[...]) + ext[:, 0:hs]).astype(BF16)
    k = to_slots(_dot(hb, w_ref[:, 2 * lw + fw:2 * lw + 2 * fw]))
    k_ref[0] = (head_norm(k, gk_ref[...]) + ext[:, hs:2 * hs]).astype(BF16)
    v = to_slots(_dot(hb, w_ref[:, 2 * lw + 2 * fw:2 * lw + 3 * fw]))
    v_ref[0] = (v + cv_ref[...]).astype(BF16)


def _in_proj(x, norm_g, w_in, b_forget, fox_q_g, fox_k_g, conv_w, conv_b, wa, ba, wx, bx, a_param,
             lru_out_g, *, lw, n_heads, dh, tm):
    B, S, D = x.shape
    fw = n_heads * dh
    hs = n_heads * HEAD_SLOT
    assert 2 * dh == HEAD_SLOT and fw % LANES == 0 and 3 * n_heads <= LANES
    n_main = 2 * lw + 3 * fw
    w_all = w_in[:, :n_main].astype(BF16)
    w_f = jnp.pad(w_in[:, n_main:], ((0, 0), (0, LANES - n_heads))).astype(BF16)
    b_f = jnp.pad(b_forget, (0, LANES - n_heads)).reshape(1, LANES)
    scale = dh ** -0.5

    def slot_gains(g):
        even = jnp.pad(g, (0, HEAD_SLOT - dh))
        odd = jnp.pad(g, (HEAD_SLOT - dh, 0))
        return jnp.tile(jnp.concatenate([even, odd]), n_heads // 2).reshape(1, hs)

    gq = slot_gains(fox_q_g * scale)
    gk = slot_gains(fox_k_g)
    ones_row = 3 * n_heads
    pe = np.zeros((LANES, 2 * hs), np.float32)
    cv = np.zeros((1, hs), np.float32)
    for h in range(n_heads):
        x0 = h * HEAD_SLOT + (dh if h % 2 == 0 else 0)
        for j in range(3):
            pe[j * n_heads + h, x0 + j] = 1.0
            pe[ones_row, x0 + 3 + j] = 1.0
            pe[ones_row, hs + x0 + j] = 1.0
            pe[j * n_heads + h, hs + x0 + 3 + j] = -1.0
        cv[0, x0] = 1.0
    pe = jnp.asarray(pe, BF16)

    w_lru = _lru_gate_weights(wa, wx, lw)
    vec = lambda a: a.reshape(1, lw)
    full = lambda shape: pl.BlockSpec(shape, lambda b, s: (0,) * len(shape))
    row = lambda width: pl.BlockSpec((1, tm, width), lambda b, s: (b, s, 0))
    return pl.pallas_call(
        functools.partial(_in_proj_kernel, lw=lw, n_heads=n_heads, dh=dh),
        grid=(B, S // tm),
        in_specs=[row(D), full((1, D)), full(w_all.shape), full(w_f.shape), full((1, LANES)),
                  full((1, hs)), full((1, hs)), full(pe.shape), full((1, hs)),
                  full((CONV_WIDTH, lw)), full((1, lw)), full(w_lru.shape), full((1, lw)),
                  full((1, lw)), full((1, lw)), full((1, lw))],
        out_specs=[row(lw), row(hs), row(hs), row(hs)],
        out_shape=[jax.ShapeDtypeStruct((B, S, lw), BF16),
                   jax.ShapeDtypeStruct((B, S, hs), BF16), jax.ShapeDtypeStruct((B, S, hs), BF16),
                   jax.ShapeDtypeStruct((B, S, hs), BF16)],
        scratch_shapes=[pltpu.VMEM((1, LANES), F32), pltpu.VMEM((SUBLANES, lw), F32),
                        pltpu.VMEM((1, lw), F32)],
        compiler_params=pltpu.CompilerParams(
            dimension_semantics=("parallel", "arbitrary"), vmem_limit_bytes=VMEM_LIMIT),
        name="in_proj",
    )(x, norm_g.reshape(1, D), w_all, w_f, b_f, gq, gk, pe, jnp.asarray(cv),
      conv_w, vec(conv_b), w_lru, vec(ba), vec(bx), vec(a_param), vec(lru_out_g))


def _rg_lru_tile(u, gate, cw_ref, cb_ref, w_ref, ba_ref, bx_ref, ap_ref, og_ref, tail_ref, h_ref):
    ts, lw = u.shape
    ext = jnp.concatenate([tail_ref[...], u], axis=0)
    tail_ref[...] = u[ts - SUBLANES:, :]
    xc = cb_ref[...] + u * cw_ref[CONV_WIDTH - 1:CONV_WIDTH, :]
    for back in range(1, CONV_WIDTH):
        tap = CONV_WIDTH - 1 - back
        xc = xc + pltpu.roll(ext, back, 0)[SUBLANES:, :] * cw_ref[tap:tap + 1, :]

    xb = xc.astype(BF16)
    r_parts, i_parts = [], []
    for j in range(lw // MXU_DIM):
        y = _dot(xb[:, j * MXU_DIM:(j + 1) * MXU_DIM], w_ref[j])
        r_parts.append(y[:, :MXU_DIM])
        i_parts.append(y[:, MXU_DIM:])
    r = jax.nn.sigmoid(jnp.concatenate(r_parts, axis=-1) + ba_ref[...])
    i = jax.nn.sigmoid(jnp.concatenate(i_parts, axis=-1) + bx_ref[...])
    log_a = (-LRU_C) * r * _softplus(-ap_ref[...])
    a = jnp.exp(log_a)
    gap = jnp.maximum(-jnp.tanh(log_a) * (a * a + 1.0), 0.0)
    mult = gap * lax.rsqrt(jnp.maximum(gap, MIN_NORMAL))
    b = mult * (i * xc)

    groups = ts // SUBLANES
    a = a.reshape(groups, SUBLANES, lw)
    b = b.reshape(groups, SUBLANES, lw)
    row = lax.broadcasted_iota(jnp.int32, a.shape, 1)
    d = 1
    while d < SUBLANES:
        keep = row >= d
        a_prev = jnp.where(keep, pltpu.roll(a, d, 1), 1.0)
        b_prev = jnp.where(keep, pltpu.roll(b, d, 1), 0.0)
        b = a * b_prev + b
        a = a * a_prev
        d *= 2
    state = h_ref[...]
    parts = []
    for g in range(groups):
        hg = b[g] + a[g] * state
        state = hg[SUBLANES - 1:, :]
        parts.append(hg)
    h = jnp.concatenate(parts, axis=0)
    h_ref[...] = state

    y = h * jax.nn.gelu(gate)
    return _rms(y, og_ref[...])


def _lru_gate_weights(wa, wx, lw):
    _, bd, _ = wa.shape
    per = MXU_DIM // bd
    n_tiles = lw // MXU_DIM

    def tiles(w):
        w = w.reshape(n_tiles, per, bd, bd)
        eye = jnp.eye(per, dtype=w.dtype)
        return jnp.einsum('tpij,pq->tpiqj', w, eye).reshape(n_tiles, MXU_DIM, MXU_DIM)

    return jnp.concatenate([tiles(wa), tiles(wx)], axis=-1).astype(BF16)


OVERFLOW_GUARD = 1e30


def _fox_attn_kernel(q_ref, k_ref, v_ref, o_ref, vt_ref, m_ref, acc_ref, *, dh, tq):
    n_chunks = vt_ref.shape[0]
    for c in range(n_chunks):
        vt_ref[c] = v_ref[0, c * tq:(c + 1) * tq, :].astype(F32).T.astype(BF16)

    heads = [slice(hh * HEAD_SLOT, (hh + 1) * HEAD_SLOT) for hh in range(2)]
    lax.fori_loop(0, n_chunks, functools.partial(
        _fox_attn_tile, q_ref=q_ref, k_ref=k_ref, o_ref=o_ref, vt_ref=vt_ref, m_ref=m_ref,
        acc_ref=acc_ref, heads=heads, dh=dh, tq=tq), 0)


def _fox_attn_tile(qi, carry, *, q_ref, k_ref, o_ref, vt_ref, m_ref, acc_ref, heads, dh, tq):
    q_rows = pl.ds(pl.multiple_of(qi * tq, tq), tq)

    def scores(j, q, sl, masked):
        start = pl.multiple_of(j * tq, tq)
        st = lax.dot_general(k_ref[0, pl.ds(start, tq), sl], q, (((1,), (1,)), ((), ())),
                             preferred_element_type=F32)
        if masked:
            kpos = lax.broadcasted_iota(jnp.int32, st.shape, 0)
            qpos = lax.broadcasted_iota(jnp.int32, st.shape, 1)
            st = jnp.where(kpos <= qpos, st, -jnp.inf)
        return st

    for hh, sl in enumerate(heads):
        st = scores(qi, q_ref[0, q_rows, sl], sl, True)
        m = jnp.max(st, axis=0, keepdims=True)
        m_ref[hh] = m
        acc_ref[hh] = _dot(vt_ref[qi, sl, :], jnp.exp(st - m).astype(BF16))

    def fast_chunk(j):
        for hh, sl in enumerate(heads):
            pt = jnp.exp(scores(j, q_ref[0, q_rows, sl], sl, False) - m_ref[hh]).astype(BF16)
            acc_ref[hh] += _dot(vt_ref[j, sl, :], pt)

    def fast_pair(i, carry):
        start = pl.multiple_of(2 * i * tq, 2 * tq)
        for hh, sl in enumerate(heads):
            st = lax.dot_general(k_ref[0, pl.ds(start, 2 * tq), sl], q_ref[0, q_rows, sl],
                                 (((1,), (1,)), ((), ())), preferred_element_type=F32)
            pt = jnp.exp(st - m_ref[hh]).astype(BF16)
            vt = jnp.concatenate([vt_ref[2 * i, sl, :], vt_ref[2 * i + 1, sl, :]], axis=1)
            acc_ref[hh] += _dot(vt, pt)
        return carry

    lax.fori_loop(0, qi // 2, fast_pair, 0)

    @pl.when(qi % 2 == 1)
    def _():
        fast_chunk(qi - 1)

    def softmax_sums():
        return jnp.concatenate([acc_ref[0][dh:dh + 1, :], acc_ref[1][0:1, :]], axis=0)

    def normalised():
        sums = softmax_sums()
        return jnp.concatenate([acc_ref[0][0:dh, :] / sums[0:1, :],
                                acc_ref[1][dh:2 * dh, :] / sums[1:2, :]], axis=0)

    out = normalised()
    o_ref[0, q_rows, :] = out.T.astype(o_ref.dtype)
    sums = softmax_sums()
    flag = lambda x: jnp.max(jnp.where(jnp.abs(x) < OVERFLOW_GUARD, 0.0, 1.0),
                             axis=0, keepdims=True)
    bad = jnp.max(jnp.maximum(flag(out), flag(sums)), axis=1, keepdims=True)

    @pl.when(bad[0, 0] > 0.0)
    def _():
        m_ref[...] = jnp.full_like(m_ref, -jnp.inf)
        acc_ref[...] = jnp.zeros_like(acc_ref)

        def chunk(j, masked):
            for hh, sl in enumerate(heads):
                st = scores(j, q_ref[0, q_rows, sl], sl, masked)
                m_old = m_ref[hh]
                m_new = jnp.maximum(m_old, jnp.max(st, axis=0, keepdims=True))
                pt = jnp.exp(st - m_new).astype(BF16)
                acc_ref[hh] = jnp.exp(m_old - m_new) * acc_ref[hh] + _dot(vt_ref[j, sl, :], pt)
                m_ref[hh] = m_new

        def body(j, carry):
            chunk(j, False)
            return carry

        lax.fori_loop(0, qi, body, 0)
        chunk(qi, True)
        o_ref[0, q_rows, :] = normalised().T.astype(o_ref.dtype)

    return carry


def _fox_attn(q, k, v, *, n_heads, dh, tq):
    B, S, _ = q.shape
    nq = S // tq
    pair = 2 * HEAD_SLOT
    qkv_spec = pl.BlockSpec((1, S, pair), lambda b, h: (b, 0, h))
    return pl.pallas_call(
        functools.partial(_fox_attn_kernel, dh=dh, tq=tq),
        grid=(B, n_heads // 2),
        in_specs=[qkv_spec, qkv_spec, qkv_spec],
        out_specs=pl.BlockSpec((1, S, 2 * dh), lambda b, h: (b, 0, h)),
        out_shape=jax.ShapeDtypeStruct((B, S, n_heads * dh), BF16),
        scratch_shapes=[pltpu.VMEM((nq, pair, tq), BF16), pltpu.VMEM((2, 1, tq), F32),
                        pltpu.VMEM((2, HEAD_SLOT, tq), F32)],
        compiler_params=pltpu.CompilerParams(
            dimension_semantics=("parallel", "parallel"), vmem_limit_bytes=VMEM_LIMIT),
        name="fox_attn",
    )(q, k, v)


def _mem_kv_kernel(m_ref, g_ref, w_ref, kg_ref, k_ref, v_ref, *, n_heads, dh):
    mw = n_heads * dh
    mb = _rms(m_ref[0], g_ref[...]).astype(BF16)
    kv = _dot(mb, w_ref[...])
    parts = []
    for h in range(n_heads):
        parts.append(_rms(kv[:, h * dh:(h + 1) * dh], kg_ref[...]))
    k_ref[0] = jnp.concatenate(parts, axis=-1).astype(BF16)
    v_ref[0] = kv[:, mw:].astype(BF16)


def _mem_kv(mem, norm_g, wkv, k_g, *, n_heads, dh):
    B, M, D = mem.shape
    mw = n_heads * dh
    full = lambda shape: pl.BlockSpec(shape, lambda b: (0,) * len(shape))
    out = pl.BlockSpec((1, M, mw), lambda b: (b, 0, 0))
    return pl.pallas_call(
        functools.partial(_mem_kv_kernel, n_heads=n_heads, dh=dh),
        grid=(B,),
        in_specs=[pl.BlockSpec((1, M, D), lambda b: (b, 0, 0)), full((1, D)), full((D, 2 * mw)),
                  full((1, dh))],
        out_specs=[out, out],
        out_shape=[jax.ShapeDtypeStruct((B, M, mw), BF16)] * 2,
        compiler_params=pltpu.CompilerParams(
            dimension_semantics=("parallel",), vmem_limit_bytes=VMEM_LIMIT),
        name="mem_kv",
    )(mem, norm_g.reshape(1, D), wkv.astype(BF16), k_g.reshape(1, dh))


def _out_mem_kernel(x_ref, yl_ref, yf_ref, fg_ref, wol_ref, wof_ref, gx_ref, wq_ref, qg_ref,
                    km_ref, vm_ref, wo_ref, gf_ref, wr_ref, br_ref,
                    x2_ref, xn_ref, gates_ref, experts_ref, *, n_heads, dh, n_groups, per_group):
    yf = _rms(yf_ref[0].astype(F32), fg_ref[...]).astype(BF16)
    x1 = x_ref[0] + _dot(yl_ref[0], wol_ref[...]) + _dot(yf, wof_ref[...])

    q = _dot(_rms(x1, gx_ref[...]).astype(BF16), wq_ref[...])
    outs = []
    for h in range(n_heads):
        sl = slice(h * dh, (h + 1) * dh)
        qh = _rms(q[:, sl], qg_ref[...]).astype(BF16)
        s = lax.dot_general(qh, km_ref[0, :, sl], (((1,), (1,)), ((), ())),
                            preferred_element_type=F32)
        p = jnp.exp(s - jnp.max(s, axis=-1, keepdims=True))
        inv = 1.0 / jnp.sum(p, axis=-1, keepdims=True)
        outs.append(_dot(p.astype(BF16), vm_ref[0, :, sl]) * inv)
    x2 = x1 + _dot(jnp.concatenate(outs, axis=-1).astype(BF16), wo_ref[...])
    x2_ref[0] = x2

    xn = _rms(x2, gf_ref[...])
    _store_chunk_rows(xn_ref, xn)
    logits = _dot(xn.astype(BF16), wr_ref[...]) + br_ref[...]
    lane = lax.broadcasted_iota(jnp.int32, logits.shape, 1).astype(F32)
    neg = -jnp.inf

    def top(vals):
        mx = jnp.max(vals, axis=-1, keepdims=True)
        idx = jnp.min(jnp.where(vals == mx, lane, float(LANES)), axis=-1, keepdims=True)
        return mx, idx

    gl = jnp.where(lane < n_groups, logits, neg)
    g_max, g_idx = top(gl)
    g_w = 1.0 / jnp.sum(jnp.exp(gl - g_max), axis=-1, keepdims=True)
    lo = n_groups + per_group * g_idx
    el = jnp.where((lane >= lo) & (lane < lo + per_group), logits, neg)
    e1, i1 = top(el)
    e2, i2 = top(jnp.where(lane == i1, neg, el))
    t = jnp.exp(e2 - e1)
    w1 = g_w / (1.0 + t)
    w2 = g_w * t / (1.0 + t)
    gates_ref[0] = jnp.where(lane == 0, w1, jnp.where(lane == 1, w2, 0.0))
    experts_ref[0] = jnp.where(lane == 0, i1 - n_groups,
                               jnp.where(lane == 1, i2 - n_groups, 0.0)).astype(jnp.int32)


def _out_mem(x, y_lru, y_fox, fox_out_g, w_out, norm_mem_x_g, mem_wq, mem_q_g, k_mem, v_mem, mem_wo,
             norm_ffn_g, router_group_w, router_group_b, router_expert_w, router_expert_b, *,
             n_heads, dh, tm):
    B, S, D = x.shape
    lw = y_lru.shape[-1]
    fw = y_fox.shape[-1]
    M = k_mem.shape[1]
    mw = n_heads * dh
    n_groups = router_group_w.shape[-1]
    n_experts = router_expert_w.shape[-1]
    w_r = jnp.concatenate([router_group_w, router_expert_w], axis=1)
    w_r = jnp.pad(w_r, ((0, 0), (0, LANES - w_r.shape[1]))).astype(BF16)
    b_r = jnp.concatenate([router_group_b, router_expert_b])
    b_r = jnp.pad(b_r, (0, LANES - b_r.shape[0])).reshape(1, LANES)
    q_gain = (mem_q_g * dh ** -0.5).reshape(1, dh)

    full = lambda shape: pl.BlockSpec(shape, lambda b, s: (0,) * len(shape))
    row = lambda width: pl.BlockSpec((1, tm, width), lambda b, s: (b, s, 0))
    mem_spec = pl.BlockSpec((1, M, mw), lambda b, s: (b, 0, 0))
    return pl.pallas_call(
        functools.partial(_out_mem_kernel, n_heads=n_heads, dh=dh, n_groups=n_groups,
                          per_group=n_experts // n_groups),
        grid=(B, S // tm),
        in_specs=[row(D), row(lw), row(fw), full((1, fw)), full((lw, D)), full((fw, D)),
                  full((1, D)), full((D, mw)), full((1, dh)), mem_spec, mem_spec, full((mw, D)),
                  full((1, D)), full((D, LANES)), full((1, LANES))],
        out_specs=[row(D), pl.BlockSpec((tm * D // LANES, LANES), lambda b, s: (b * (S // tm) + s, 0)),
                   row(LANES), row(LANES)],
        out_shape=[jax.ShapeDtypeStruct((B, S, D), F32),
                   jax.ShapeDtypeStruct((B * S * D // LANES, LANES), F32),
                   jax.ShapeDtypeStruct((B, S, LANES), F32),
                   jax.ShapeDtypeStruct((B, S, LANES), jnp.int32)],
        compiler_params=pltpu.CompilerParams(
            dimension_semantics=("parallel", "parallel"), vmem_limit_bytes=VMEM_LIMIT),
        name="out_mem",
    )(x, y_lru, y_fox, fox_out_g.reshape(1, fw), w_out[:lw].astype(BF16), w_out[lw:].astype(BF16),
      norm_mem_x_g.reshape(1, D), mem_wq.astype(BF16), q_gain, k_mem, v_mem, mem_wo.astype(BF16),
      norm_ffn_g.reshape(1, D), w_r, b_r)


DMA_UNROLL = 16
DMA_THREADS = 2
RANK_GROUP = 256


def _scatter_rows_kernel(zb_ref, idx_ref, x_ref, buf_ref, zero_ref, sem, zsem, *, top_k, chunks):
    tokens = x_ref.shape[0] // chunks

    @pl.when(pl.program_id(0) == 0)
    def _():
        rows = zero_ref.shape[0]
        zero_ref[...] = jnp.zeros_like(zero_ref)

        def copy(j):
            start = pl.multiple_of(j * rows, rows)
            return pltpu.make_async_copy(zero_ref, buf_ref.at[pl.ds(start, rows), :], zsem)

        def issue_zero(j, carry):
            @pl.when(zb_ref[j] == 1)
            def _():
                copy(j).start()
            return carry

        def drain_zero(j, carry):
            @pl.when(zb_ref[j] == 1)
            def _():
                copy(j).wait()
            return carry

        lax.fori_loop(0, zb_ref.shape[0], issue_zero, 0)
        lax.fori_loop(0, zb_ref.shape[0], drain_zero, 0)

    def issue(g, carry):
        for u in range(DMA_UNROLL):
            r = g * DMA_UNROLL + u
            src = x_ref.at[pl.ds(pl.multiple_of(r * chunks, chunks), chunks), :]
            for k in range(top_k):
                row = pl.multiple_of(idx_ref[0, 0, r * top_k + k], chunks)
                pltpu.make_async_copy(src, buf_ref.at[pl.ds(row, chunks), :], sem).start(
                    priority=(u * top_k + k) % DMA_THREADS)
        return carry

    lax.fori_loop(0, tokens // DMA_UNROLL, issue, 0)
    for k in range(top_k):
        pltpu.make_async_copy(x_ref, buf_ref.at[pl.ds(0, tokens * chunks), :], sem).wait()


def _scatter_rows(x, dest, zero_blocks, *, tm, blk, chunks):
    nt = dest.shape[0]
    top_k = dest.shape[2] // tm
    n_blocks = zero_blocks.shape[0]
    grid_spec = pltpu.PrefetchScalarGridSpec(
        num_scalar_prefetch=1,
        grid=(nt,),
        in_specs=[pl.BlockSpec((1, 1, tm * top_k), lambda i, zb: (i, 0, 0), memory_space=pltpu.SMEM),
                  pl.BlockSpec((tm * chunks, LANES), lambda i, zb: (i, 0))],
        out_specs=pl.BlockSpec(memory_space=pl.ANY),
        scratch_shapes=[pltpu.VMEM((blk * chunks, LANES), x.dtype), pltpu.SemaphoreType.DMA,
                        pltpu.SemaphoreType.DMA],
    )
    return pl.pallas_call(
        functools.partial(_scatter_rows_kernel, top_k=top_k, chunks=chunks),
        grid_spec=grid_spec,
        out_shape=jax.ShapeDtypeStruct((n_blocks * blk * chunks, LANES), x.dtype),
        compiler_params=pltpu.CompilerParams(
            dimension_semantics=("arbitrary",), vmem_limit_bytes=VMEM_LIMIT),
        name="scatter_rows",
    )(zero_blocks, dest, x)


X_RING = 3
X_PRIORITY = 1


def _experts_kernel(be_ref, nu_ref, x_hbm, wg_ref, wu_ref, wd_ref, o_ref, xbuf_ref, wgb_ref, wub_ref,
                    wdb_ref, sem, *, blk):
    i = pl.program_id(0)
    used = nu_ref[0]
    rows = xbuf_ref.shape[1]

    def fetch(step):
        slot = lax.rem(step, X_RING)
        src = x_hbm.at[pl.ds(pl.multiple_of(step * rows, rows), rows), :]
        return pltpu.make_async_copy(src, xbuf_ref.at[slot], sem.at[slot])

    @pl.when(i == 0)
    def _():
        for step in range(X_RING - 1):
            @pl.when(step < used)
            def _():
                fetch(step).start(priority=X_PRIORITY)

    @pl.when(i + X_RING - 1 < used)
    def _():
        fetch(i + X_RING - 1).start(priority=X_PRIORITY)

    @pl.when((i == 0) | (be_ref[i] != be_ref[jnp.maximum(i - 1, 0)]))
    def _():
        wgb_ref[...] = wg_ref[0].astype(BF16)
        wub_ref[...] = wu_ref[0].astype(BF16)
        wdb_ref[...] = wd_ref[0].astype(BF16)

    @pl.when(i < used)
    def _():
        fetch(i).wait()
        d = wg_ref.shape[1]
        xb = _load_chunk_rows(xbuf_ref.at[lax.rem(i, X_RING)], blk, d).astype(BF16)
        hdn = jax.nn.silu(_dot(xb, wgb_ref[...])) * _dot(xb, wub_ref[...])
        _store_chunk_rows(o_ref, _dot(hdn.astype(BF16), wdb_ref[...]))

    @pl.when(i >= used)
    def _():
        o_ref[...] = jnp.zeros_like(o_ref)


def _experts(x_buf, blk_expert, n_used, w_gate, w_up, w_down, *, blk):
    _, D, de = w_gate.shape
    chunks = D // LANES
    n_blocks = x_buf.shape[0] // (blk * chunks)
    grid_spec = pltpu.PrefetchScalarGridSpec(
        num_scalar_prefetch=2,
        grid=(n_blocks,),
        in_specs=[pl.BlockSpec(memory_space=pl.ANY),
                  pl.BlockSpec((1, D, de), lambda i, be, nu: (be[i], 0, 0)),
                  pl.BlockSpec((1, D, de), lambda i, be, nu: (be[i], 0, 0)),
                  pl.BlockSpec((1, de, D), lambda i, be, nu: (be[i], 0, 0))],
        out_specs=pl.BlockSpec((blk * chunks, LANES), lambda i, be, nu: (i, 0)),
        scratch_shapes=[pltpu.VMEM((X_RING, blk * chunks, LANES), F32),
                        pltpu.VMEM((D, de), BF16), pltpu.VMEM((D, de), BF16),
                        pltpu.VMEM((de, D), BF16), pltpu.SemaphoreType.DMA((X_RING,))],
    )
    return pl.pallas_call(
        functools.partial(_experts_kernel, blk=blk),
        grid_spec=grid_spec,
        out_shape=jax.ShapeDtypeStruct(x_buf.shape, F32),
        compiler_params=pltpu.CompilerParams(
            dimension_semantics=("arbitrary",), vmem_limit_bytes=VMEM_LIMIT),
        name="experts",
    )(blk_expert, n_used, x_buf, w_gate, w_up, w_down)


def _combine_kernel(idx_ref, idx_next_ref, x_ref, g_ref, y_ref, o_ref, buf_ref, sem, *, top_k, chunks):
    i = pl.program_id(0)
    n = pl.num_programs(0)
    slot = lax.rem(i, 2)
    tokens, d = x_ref.shape

    def request(ids_ref, s):
        def issue(g, carry):
            for u in range(DMA_UNROLL):
                r = g * DMA_UNROLL + u
                for k in range(top_k):
                    row = pl.multiple_of(ids_ref[0, 0, r * top_k + k], chunks)
                    dst = buf_ref.at[s, k, pl.ds(pl.multiple_of(r * chunks, chunks), chunks), :]
                    pltpu.make_async_copy(y_ref.at[pl.ds(row, chunks), :], dst, sem.at[s]).start(
                        priority=(u * top_k + k) % DMA_THREADS)
            return carry
        lax.fori_loop(0, tokens // DMA_UNROLL, issue, 0)

    @pl.when(i == 0)
    def _():
        request(idx_ref, 0)

    @pl.when(i + 1 < n)
    def _():
        request(idx_next_ref, 1 - slot)

    for k in range(top_k):
        pltpu.make_async_copy(y_ref.at[pl.ds(0, tokens * chunks), :], buf_ref.at[slot, k],
                              sem.at[slot]).wait()
    out = x_ref[...]
    for k in range(top_k):
        out = out + g_ref[:, k:k + 1] * _load_chunk_rows(buf_ref.at[slot, k], tokens, d)
    o_ref[...] = out


def _combine(x2, gates, y_buf, dest, *, tm):
    T, D = x2.shape
    chunks = D // LANES
    nt = dest.shape[0]
    top_k = dest.shape[2] // tm
    idx_spec = lambda step: pl.BlockSpec(
        (1, 1, tm * top_k), lambda i: (jnp.minimum(i + step, nt - 1), 0, 0), memory_space=pltpu.SMEM)
    return pl.pallas_call(
        functools.partial(_combine_kernel, top_k=top_k, chunks=chunks),
        grid=(nt,),
        in_specs=[idx_spec(0), idx_spec(1),
                  pl.BlockSpec((tm, D), lambda i: (i, 0)),
                  pl.BlockSpec((tm, LANES), lambda i: (i, 0)),
                  pl.BlockSpec(memory_space=pl.ANY)],
        out_specs=pl.BlockSpec((tm, D), lambda i: (i, 0)),
        out_shape=jax.ShapeDtypeStruct((T, D), F32),
        scratch_shapes=[pltpu.VMEM((2, top_k, tm * chunks, LANES), F32),
                        pltpu.SemaphoreType.DMA((2,))],
        compiler_params=pltpu.CompilerParams(
            dimension_semantics=("arbitrary",), vmem_limit_bytes=VMEM_LIMIT),
        name="combine",
    )(dest, dest, x2, gates, y_buf)


def _dispatch(experts, n_experts, blk):
    T, K = experts.shape
    A = T * K
    e_flat = experts.reshape(A)
    g = _pick_tile(A, RANK_GROUP)
    onehot = (e_flat.reshape(A // g, g, 1) == jnp.arange(n_experts, dtype=jnp.int32)).astype(BF16)
    tri = jnp.tril(jnp.ones((g, g), BF16))
    within = jnp.einsum('ij,gje->gie', tri, onehot, preferred_element_type=F32)
    group_counts = within[:, -1, :]
    before = jnp.cumsum(group_counts, axis=0) - group_counts
    counts = jnp.sum(group_counts, axis=0).astype(jnp.int32)
    padded = (counts + blk - 1) // blk * blk
    pends = jnp.cumsum(padded)
    pstarts = pends - padded
    row = within + before[:, None, :] - 1.0 + pstarts.astype(F32)
    dest = jnp.sum(onehot.astype(F32) * row, axis=-1).astype(jnp.int32).reshape(A)
    n_blocks = (A + n_experts * blk) // blk
    blk_start = jnp.arange(n_blocks, dtype=jnp.int32) * blk
    blk_expert = jnp.minimum(jnp.sum(blk_start[:, None] >= pends[None, :], axis=1), n_experts - 1)
    last_of_segment = jnp.any((blk_start[:, None] + blk == pends[None, :]) & (padded[None, :] > 0),
                              axis=1)
    zero_blocks = (last_of_segment | (blk_start >= pends[-1])).astype(jnp.int32)
    n_used = (pends[-1:] // blk).astype(jnp.int32)
    return dest.reshape(T, K), blk_expert.astype(jnp.int32), zero_blocks, n_used


class _Tiles(NamedTuple):
    rows: int
    q_rows: int
    expert_rows: int
    scatter_rows: int
    combine_rows: int


def _tiles(seq, tokens):
    return _Tiles(rows=_pick_tile(seq, 1024), q_rows=_pick_tile(seq, 1024), expert_rows=512,
                  scatter_rows=_pick_tile(tokens, 1024), combine_rows=_pick_tile(tokens, 512))


def kernel(x, mem, norm_mix_g, w_in, b_forget, conv_w, conv_b, lru_wa, lru_ba, lru_wx, lru_bx,
           lru_a_param, fox_q_g, fox_k_g, lru_out_g, fox_out_g, w_out, norm_mem_x_g, norm_mem_g,
           mem_wq, mem_wkv, mem_q_g, mem_k_g, mem_wo, norm_ffn_g, router_group_w, router_group_b,
           router_expert_w, router_expert_b, exp_w_gate, exp_w_up, exp_w_down):
    B, S, D = x.shape
    depth = norm_mix_g.shape[0]
    lw = conv_w.shape[-1]
    n_heads = b_forget.shape[-1]
    dh = fox_q_g.shape[-1]
    mem_dh = mem_q_g.shape[-1]
    mem_heads = mem_wq.shape[-1] // mem_dh
    n_experts = router_expert_w.shape[-1]
    T = B * S
    tm, tq, blk, scatter_tm, combine_tm = _tiles(S, T)
    assert D % (2 * LANES) == 0 and lw % MXU_DIM == 0 and MXU_DIM % lru_wa.shape[-1] == 0
    assert n_heads % 2 == 0 and mem_dh % LANES == 0 and mem.shape[1] % SUBLANES == 0
    assert router_group_w.shape[-1] + n_experts <= LANES and n_experts % router_group_w.shape[-1] == 0
    assert tm % SUBLANES == 0 and tq % MXU_DIM == 0 and (T * TOP_K) % blk == 0

    for l in range(depth):
        y_lru, q, k, v = _in_proj(
            x, norm_mix_g[l], w_in[l], b_forget[l], fox_q_g[l], fox_k_g[l], conv_w[l], conv_b[l],
            lru_wa[l], lru_ba[l], lru_wx[l], lru_bx[l], lru_a_param[l], lru_out_g[l],
            lw=lw, n_heads=n_heads, dh=dh, tm=tm)
        y_fox = _fox_attn(q, k, v, n_heads=n_heads, dh=dh, tq=tq)
        k_mem, v_mem = _mem_kv(mem, norm_mem_g[l], mem_wkv[l], mem_k_g[l], n_heads=mem_heads,
                               dh=mem_dh)
        x2, xn, gates, experts = _out_mem(
            x, y_lru, y_fox, fox_out_g[l], w_out[l], norm_mem_x_g[l], mem_wq[l], mem_q_g[l], k_mem,
            v_mem, mem_wo[l], norm_ffn_g[l], router_group_w[l], router_group_b[l],
            router_expert_w[l], router_expert_b[l], n_heads=mem_heads, dh=mem_dh, tm=tm)

        dest, blk_expert, zero_blocks, n_used = _dispatch(
            experts.reshape(T, LANES)[:, :TOP_K], n_experts, blk)
        chunks = D // LANES
        tiled = lambda t: (dest * chunks).reshape(T // t, 1, t * TOP_K)
        x_buf = _scatter_rows(xn, tiled(scatter_tm), zero_blocks, tm=scatter_tm, blk=blk,
                              chunks=chunks)
        y_buf = _experts(x_buf, blk_expert, n_used, exp_w_gate[l], exp_w_up[l], exp_w_down[l],
                         blk=blk)
        x = _combine(x2.reshape(T, D), gates.reshape(T, LANES), y_buf, tiled(combine_tm),
                     tm=combine_tm)
        x = x.reshape(B, S, D)
    return x
```

```python
import functools
from typing import NamedTuple

import jax
import jax.numpy as jnp
import numpy as np
from jax import lax
from jax.experimental import pallas as pl
from jax.experimental.pallas import tpu as pltpu

EPS = 1e-6
LRU_C = 8.0
CONV_WIDTH = 4
TOP_K = 2
LANES = 128
SUBLANES = 8
MXU_DIM = 256
HEAD_SLOT = LANES
VMEM_LIMIT = 56 * 1024 * 1024

F32 = jnp.float32
BF16 = jnp.bfloat16
MIN_NORMAL = float(np.finfo(np.float32).tiny)


def _dot(a, b):
    return jnp.dot(a, b, preferred_element_type=F32)


def _rms(x, g):
    return x * lax.rsqrt(jnp.mean(x * x, axis=-1, keepdims=True) + EPS) * g


def _softplus(x):
    return jnp.maximum(x, 0.0) + jnp.log1p(jnp.exp(-jnp.abs(x)))


def _pick_tile(n, target):
    t = min(n, target)
    while n % t:
        t //= 2
    return t


def _store_chunk_rows(ref, val):
    n, d = val.shape
    chunks = d // LANES
    for c in range(chunks):
        ref[pl.ds(c, n, stride=chunks), :] = val[:, c * LANES:(c + 1) * LANES]


def _load_chunk_rows(ref, n, d):
    chunks = d // LANES
    return jnp.concatenate([ref[pl.ds(c, n, stride=chunks), :] for c in range(chunks)], axis=1)


def _in_proj_kernel(x_ref, g_ref, w_ref, wf_ref, bf_ref, gq_ref, gk_ref, pe_ref, cv_ref,
                    cw_ref, cb_ref, wl_ref, ba_ref, bx_ref, ap_ref, og_ref,
                    ylru_ref, q_ref, k_ref, v_ref, carry_ref, tail_ref, h_ref, *, lw, n_heads, dh):
    @pl.when(pl.program_id(1) == 0)
    def _():
        carry_ref[...] = jnp.zeros_like(carry_ref)
        tail_ref[...] = jnp.zeros_like(tail_ref)
        h_ref[...] = jnp.zeros_like(h_ref)

    hs = n_heads * HEAD_SLOT
    x = x_ref[0]
    tm = x.shape[0]
    hb = _rms(x, g_ref[...]).astype(BF16)

    u = _dot(hb, w_ref[:, 0:lw])
    gate = _dot(hb, w_ref[:, lw:2 * lw])
    ylru_ref[0] = _rg_lru_tile(u, gate, cw_ref, cb_ref, wl_ref, ba_ref, bx_ref, ap_ref, og_ref,
                               tail_ref, h_ref).astype(ylru_ref.dtype)

    z = _dot(hb, wf_ref[...]) + bf_ref[...]
    lane = lax.broadcasted_iota(jnp.int32, z.shape, 1)
    row = lax.broadcasted_iota(jnp.int32, z.shape, 0)
    c = jnp.where(lane < n_heads, -_softplus(-z), 0.0)
    d = 1
    while d < tm:
        c = c + jnp.where(row >= d, pltpu.roll(c, d, 0), 0.0)
        d *= 2
    c = c + carry_ref[...]
    carry_ref[...] = c[tm - 1:tm, :]
    c1 = c.astype(BF16).astype(F32)
    r1 = c - c1
    c2 = r1.astype(BF16).astype(F32)
    c3 = (r1 - c2).astype(BF16).astype(F32)
    e = c1 + pltpu.roll(c2, n_heads, 1) + pltpu.roll(c3, 2 * n_heads, 1)
    e = jnp.where(lane == 3 * n_heads, 1.0, e).astype(BF16)
    ext = _dot(e, pe_ref[...])

    fw = n_heads * dh
    lane_t = lax.broadcasted_iota(jnp.int32, (tm, LANES), 1)
    own = (lane_t < dh, lane_t >= dh)

    def to_slots(y):
        parts = []
        for c in range(fw // LANES):
            col = y[:, c * LANES:(c + 1) * LANES]
            parts += [jnp.where(own[0], col, 0.0), jnp.where(own[1], col, 0.0)]
        return jnp.concatenate(parts, axis=-1)

    def head_norm(y, gain):
        parts = []
        for h in range(n_heads):
            blk = y[:, h * HEAD_SLOT:(h + 1) * HEAD_SLOT]
            ss = jnp.sum(blk * blk, axis=-1, keepdims=True) * (1.0 / dh)
            parts.append(blk * lax.rsqrt(ss + EPS))
        return jnp.concatenate(parts, axis=-1) * gain

    q = to_slots(_dot(hb, w_ref[:, 2 * lw:2 * lw + fw]))
    q_ref[0] = (head_norm(q, gq_ref[...]) + ext[:, 0:hs]).astype(BF16)
    k = to_slots(_dot(hb, w_ref[:, 2 * lw + fw:2 * lw + 2 * fw]))
    k_ref[0] = (head_norm(k, gk_ref[...]) + ext[:, hs:2 * hs]).astype(BF16)
    v = to_slots(_dot(hb, w_ref[:, 2 * lw + 2 * fw:2 * lw + 3 * fw]))
    v_ref[0] = (v + cv_ref[...]).astype(BF16)


def _in_proj(x, norm_g, w_in, b_forget, fox_q_g, fox_k_g, conv_w, conv_b, wa, ba, wx, bx, a_param,
             lru_out_g, *, lw, n_heads, dh, tm):
    B, S, D = x.shape
    fw = n_heads * dh
    hs = n_heads * HEAD_SLOT
    assert 2 * dh == HEAD_SLOT and fw % LANES == 0 and 3 * n_heads <= LANES
    n_main = 2 * lw + 3 * fw
    w_all = w_in[:, :n_main].astype(BF16)
    w_f = jnp.pad(w_in[:, n_main:], ((0, 0), (0, LANES - n_heads))).astype(BF16)
    b_f = jnp.pad(b_forget, (0, LANES - n_heads)).reshape(1, LANES)
    scale = dh ** -0.5

    def slot_gains(g):
        even = jnp.pad(g, (0, HEAD_SLOT - dh))
        odd = jnp.pad(g, (HEAD_SLOT - dh, 0))
        return jnp.tile(jnp.concatenate([even, odd]), n_heads // 2).reshape(1, hs)

    gq = slot_gains(fox_q_g * scale)
    gk = slot_gains(fox_k_g)
    ones_row = 3 * n_heads
    pe = np.zeros((LANES, 2 * hs), np.float32)
    cv = np.zeros((1, hs), np.float32)
    for h in range(n_heads):
        x0 = h * HEAD_SLOT + (dh if h % 2 == 0 else 0)
        for j in range(3):
            pe[j * n_heads + h, x0 + j] = 1.0
            pe[ones_row, x0 + 3 + j] = 1.0
            pe[ones_row, hs + x0 + j] = 1.0
            pe[j * n_heads + h, hs + x0 + 3 + j] = -1.0
        cv[0, x0] = 1.0
    pe = jnp.asarray(pe, BF16)

    w_lru = _lru_gate_weights(wa, wx, lw)
    vec = lambda a: a.reshape(1, lw)
    full = lambda shape: pl.BlockSpec(shape, lambda b, s: (0,) * len(shape))
    row = lambda width: pl.BlockSpec((1, tm, width), lambda b, s: (b, s, 0))
    return pl.pallas_call(
        functools.partial(_in_proj_kernel, lw=lw, n_heads=n_heads, dh=dh),
        grid=(B, S // tm),
        in_specs=[row(D), full((1, D)), full(w_all.shape), full(w_f.shape), full((1, LANES)),
                  full((1, hs)), full((1, hs)), full(pe.shape), full((1, hs)),
                  full((CONV_WIDTH, lw)), full((1, lw)), full(w_lru.shape), full((1, lw)),
                  full((1, lw)), full((1, lw)), full((1, lw))],
        out_specs=[row(lw), row(hs), row(hs), row(hs)],
        out_shape=[jax.ShapeDtypeStruct((B, S, lw), BF16),
                   jax.ShapeDtypeStruct((B, S, hs), BF16), jax.ShapeDtypeStruct((B, S, hs), BF16),
                   jax.ShapeDtypeStruct((B, S, hs), BF16)],
        scratch_shapes=[pltpu.VMEM((1, LANES), F32), pltpu.VMEM((SUBLANES, lw), F32),
                        pltpu.VMEM((1, lw), F32)],
        compiler_params=pltpu.CompilerParams(
            dimension_semantics=("parallel", "arbitrary"), vmem_limit_bytes=VMEM_LIMIT),
        name="in_proj",
    )(x, norm_g.reshape(1, D), w_all, w_f, b_f, gq, gk, pe, jnp.asarray(cv),
      conv_w, vec(conv_b), w_lru, vec(ba), vec(bx), vec(a_param), vec(lru_out_g))


def _rg_lru_tile(u, gate, cw_ref, cb_ref, w_ref, ba_ref, bx_ref, ap_ref, og_ref, tail_ref, h_ref):
    ts, lw = u.shape
    ext = jnp.concatenate([tail_ref[...], u], axis=0)
    tail_ref[...] = u[ts - SUBLANES:, :]
    xc = cb_ref[...] + u * cw_ref[CONV_WIDTH - 1:CONV_WIDTH, :]
    for back in range(1, CONV_WIDTH):
        tap = CONV_WIDTH - 1 - back
        xc = xc + pltpu.roll(ext, back, 0)[SUBLANES:, :] * cw_ref[tap:tap + 1, :]

    xb = xc.astype(BF16)
    r_parts, i_parts = [], []
    for j in range(lw // MXU_DIM):
        y = _dot(xb[:, j * MXU_DIM:(j + 1) * MXU_DIM], w_ref[j])
        r_parts.append(y[:, :MXU_DIM])
        i_parts.append(y[:, MXU_DIM:])
    r = jax.nn.sigmoid(jnp.concatenate(r_parts, axis=-1) + ba_ref[...])
    i = jax.nn.sigmoid(jnp.concatenate(i_parts, axis=-1) + bx_ref[...])
    log_a = (-LRU_C) * r * _softplus(-ap_ref[...])
    a = jnp.exp(log_a)
    gap = jnp.maximum(-jnp.tanh(log_a) * (a * a + 1.0), 0.0)
    mult = gap * lax.rsqrt(jnp.maximum(gap, MIN_NORMAL))
    b = mult * (i * xc)

    groups = ts // SUBLANES
    a = a.reshape(groups, SUBLANES, lw)
    b = b.reshape(groups, SUBLANES, lw)
    row = lax.broadcasted_iota(jnp.int32, a.shape, 1)
    d = 1
    while d < SUBLANES:
        keep = row >= d
        a_prev = jnp.where(keep, pltpu.roll(a, d, 1), 1.0)
        b_prev = jnp.where(keep, pltpu.roll(b, d, 1), 0.0)
        b = a * b_prev + b
        a = a * a_prev
        d *= 2
    state = h_ref[...]
    parts = []
    for g in range(groups):
        hg = b[g] + a[g] * state
        state = hg[SUBLANES - 1:, :]
        parts.append(hg)
    h = jnp.concatenate(parts, axis=0)
    h_ref[...] = state

    y = h * jax.nn.gelu(gate)
    return _rms(y, og_ref[...])


def _lru_gate_weights(wa, wx, lw):
    _, bd, _ = wa.shape
    per = MXU_DIM // bd
    n_tiles = lw // MXU_DIM

    def tiles(w):
        w = w.reshape(n_tiles, per, bd, bd)
        eye = jnp.eye(per, dtype=w.dtype)
        return jnp.einsum('tpij,pq->tpiqj', w, eye).reshape(n_tiles, MXU_DIM, MXU_DIM)

    return jnp.concatenate([tiles(wa), tiles(wx)], axis=-1).astype(BF16)


OVERFLOW_GUARD = 1e30


def _fox_attn_kernel(q_ref, k_ref, v_ref, o_ref, vt_ref, m_ref, acc_ref, *, dh, tq):
    n_chunks = vt_ref.shape[0]
    for c in range(n_chunks):
        vt_ref[c] = v_ref[0, c * tq:(c + 1) * tq, :].astype(F32).T.astype(BF16)

    heads = [slice(hh * HEAD_SLOT, (hh + 1) * HEAD_SLOT) for hh in range(2)]
    lax.fori_loop(0, n_chunks, functools.partial(
        _fox_attn_tile, q_ref=q_ref, k_ref=k_ref, o_ref=o_ref, vt_ref=vt_ref, m_ref=m_ref,
        acc_ref=acc_ref, heads=heads, dh=dh, tq=tq), 0)


def _fox_attn_tile(qi, carry, *, q_ref, k_ref, o_ref, vt_ref, m_ref, acc_ref, heads, dh, tq):
    q_rows = pl.ds(pl.multiple_of(qi * tq, tq), tq)

    def scores(j, q, sl, masked):
        start = pl.multiple_of(j * tq, tq)
        st = lax.dot_general(k_ref[0, pl.ds(start, tq), sl], q, (((1,), (1,)), ((), ())),
                             preferred_element_type=F32)
        if masked:
            kpos = lax.broadcasted_iota(jnp.int32, st.shape, 0)
            qpos = lax.broadcasted_iota(jnp.int32, st.shape, 1)
            st = jnp.where(kpos <= qpos, st, -jnp.inf)
        return st

    for hh, sl in enumerate(heads):
        st = scores(qi, q_ref[0, q_rows, sl], sl, True)
        m = jnp.max(st, axis=0, keepdims=True)
        m_ref[hh] = m
        acc_ref[hh] = _dot(vt_ref[qi, sl, :], jnp.exp(st - m).astype(BF16))

    def fast_chunk(j):
        for hh, sl in enumerate(heads):
            pt = jnp.exp(scores(j, q_ref[0, q_rows, sl], sl, False) - m_ref[hh]).astype(BF16)
            acc_ref[hh] += _dot(vt_ref[j, sl, :], pt)

    def fast_pair(i, carry):
        start = pl.multiple_of(2 * i * tq, 2 * tq)
        for hh, sl in enumerate(heads):
            st = lax.dot_general(k_ref[0, pl.ds(start, 2 * tq), sl], q_ref[0, q_rows, sl],
                                 (((1,), (1,)), ((), ())), preferred_element_type=F32)
            pt = jnp.exp(st - m_ref[hh]).astype(BF16)
            vt = jnp.concatenate([vt_ref[2 * i, sl, :], vt_ref[2 * i + 1, sl, :]], axis=1)
            acc_ref[hh] += _dot(vt, pt)
        return carry

    lax.fori_loop(0, qi // 2, fast_pair, 0)

    @pl.when(qi % 2 == 1)
    def _():
        fast_chunk(qi - 1)

    def softmax_sums():
        return jnp.concatenate([acc_ref[0][dh:dh + 1, :], acc_ref[1][0:1, :]], axis=0)

    def normalised():
        sums = softmax_sums()
        return jnp.concatenate([acc_ref[0][0:dh, :] / sums[0:1, :],
                                acc_ref[1][dh:2 * dh, :] / sums[1:2, :]], axis=0)

    out = normalised()
    o_ref[0, q_rows, :] = out.T.astype(o_ref.dtype)
    sums = softmax_sums()
    flag = lambda x: jnp.max(jnp.where(jnp.abs(x) < OVERFLOW_GUARD, 0.0, 1.0),
                             axis=0, keepdims=True)
    bad = jnp.max(jnp.maximum(flag(out), flag(sums)), axis=1, keepdims=True)

    @pl.when(bad[0, 0] > 0.0)
    def _():
        m_ref[...] = jnp.full_like(m_ref, -jnp.inf)
        acc_ref[...] = jnp.zeros_like(acc_ref)

        def chunk(j, masked):
            for hh, sl in enumerate(heads):
                st = scores(j, q_ref[0, q_rows, sl], sl, masked)
                m_old = m_ref[hh]
                m_new = jnp.maximum(m_old, jnp.max(st, axis=0, keepdims=True))
                pt = jnp.exp(st - m_new).astype(BF16)
                acc_ref[hh] = jnp.exp(m_old - m_new) * acc_ref[hh] + _dot(vt_ref[j, sl, :], pt)
                m_ref[hh] = m_new

        def body(j, carry):
            chunk(j, False)
            return carry

        lax.fori_loop(0, qi, body, 0)
        chunk(qi, True)
        o_ref[0, q_rows, :] = normalised().T.astype(o_ref.dtype)

    return carry


def _fox_attn(q, k, v, *, n_heads, dh, tq):
    B, S, _ = q.shape
    nq = S // tq
    pair = 2 * HEAD_SLOT
    qkv_spec = pl.BlockSpec((1, S, pair), lambda b, h: (b, 0, h))
    return pl.pallas_call(
        functools.partial(_fox_attn_kernel, dh=dh, tq=tq),
        grid=(B, n_heads // 2),
        in_specs=[qkv_spec, qkv_spec, qkv_spec],
        out_specs=pl.BlockSpec((1, S, 2 * dh), lambda b, h: (b, 0, h)),
        out_shape=jax.ShapeDtypeStruct((B, S, n_heads * dh), BF16),
        scratch_shapes=[pltpu.VMEM((nq, pair, tq), BF16), pltpu.VMEM((2, 1, tq), F32),
                        pltpu.VMEM((2, HEAD_SLOT, tq), F32)],
        compiler_params=pltpu.CompilerParams(
            dimension_semantics=("parallel", "parallel"), vmem_limit_bytes=VMEM_LIMIT),
        name="fox_attn",
    )(q, k, v)


def _mem_kv_kernel(m_ref, g_ref, w_ref, kg_ref, k_ref, v_ref, *, n_heads, dh):
    mw = n_heads * dh
    mb = _rms(m_ref[0], g_ref[...]).astype(BF16)
    kv = _dot(mb, w_ref[...])
    parts = []
    for h in range(n_heads):
        parts.append(_rms(kv[:, h * dh:(h + 1) * dh], kg_ref[...]))
    k_ref[0] = jnp.concatenate(parts, axis=-1).astype(BF16)
    v_ref[0] = kv[:, mw:].astype(BF16)


def _mem_kv(mem, norm_g, wkv, k_g, *, n_heads, dh):
    B, M, D = mem.shape
    mw = n_heads * dh
    full = lambda shape: pl.BlockSpec(shape, lambda b: (0,) * len(shape))
    out = pl.BlockSpec((1, M, mw), lambda b: (b, 0, 0))
    return pl.pallas_call(
        functools.partial(_mem_kv_kernel, n_heads=n_heads, dh=dh),
        grid=(B,),
        in_specs=[pl.BlockSpec((1, M, D), lambda b: (b, 0, 0)), full((1, D)), full((D, 2 * mw)),
                  full((1, dh))],
        out_specs=[out, out],
        out_shape=[jax.ShapeDtypeStruct((B, M, mw), BF16)] * 2,
        compiler_params=pltpu.CompilerParams(
            dimension_semantics=("parallel",), vmem_limit_bytes=VMEM_LIMIT),
        name="mem_kv",
    )(mem, norm_g.reshape(1, D), wkv.astype(BF16), k_g.reshape(1, dh))


def _out_mem_kernel(x_ref, yl_ref, yf_ref, fg_ref, wol_ref, wof_ref, gx_ref, wq_ref, qg_ref,
                    km_ref, vm_ref, wo_ref, gf_ref, wr_ref, br_ref,
                    x2_ref, xn_ref, gates_ref, experts_ref, *, n_heads, dh, n_groups, per_group):
    yf = _rms(yf_ref[0].astype(F32), fg_ref[...]).astype(BF16)
    x1 = x_ref[0] + _dot(yl_ref[0], wol_ref[...]) + _dot(yf, wof_ref[...])

    q = _dot(_rms(x1, gx_ref[...]).astype(BF16), wq_ref[...])
    outs = []
    for h in range(n_heads):
        sl = slice(h * dh, (h + 1) * dh)
        qh = _rms(q[:, sl], qg_ref[...]).astype(BF16)
        s = lax.dot_general(qh, km_ref[0, :, sl], (((1,), (1,)), ((), ())),
                            preferred_element_type=F32)
        p = jnp.exp(s - jnp.max(s, axis=-1, keepdims=True))
        inv = 1.0 / jnp.sum(p, axis=-1, keepdims=True)
        outs.append(_dot(p.astype(BF16), vm_ref[0, :, sl]) * inv)
    x2 = x1 + _dot(jnp.concatenate(outs, axis=-1).astype(BF16), wo_ref[...])
    x2_ref[0] = x2

    xn = _rms(x2, gf_ref[...])
    _store_chunk_rows(xn_ref, xn)
    logits = _dot(xn.astype(BF16), wr_ref[...]) + br_ref[...]
    lane = lax.broadcasted_iota(jnp.int32, logits.shape, 1).astype(F32)
    neg = -jnp.inf

    def top(vals):
        mx = jnp.max(vals, axis=-1, keepdims=True)
        idx = jnp.min(jnp.where(vals == mx, lane, float(LANES)), axis=-1, keepdims=True)
        return mx, idx

    gl = jnp.where(lane < n_groups, logits, neg)
    g_max, g_idx = top(gl)
    g_w = 1.0 / jnp.sum(jnp.exp(gl - g_max), axis=-1, keepdims=True)
    lo = n_groups + per_group * g_idx
    el = jnp.where((lane >= lo) & (lane < lo + per_group), logits, neg)
    e1, i1 = top(el)
    e2, i2 = top(jnp.where(lane == i1, neg, el))
    t = jnp.exp(e2 - e1)
    w1 = g_w / (1.0 + t)
    w2 = g_w * t / (1.0 + t)
    gates_ref[0] = jnp.where(lane == 0, w1, jnp.where(lane == 1, w2, 0.0))
    experts_ref[0] = jnp.where(lane == 0, i1 - n_groups,
                               jnp.where(lane == 1, i2 - n_groups, 0.0)).astype(jnp.int32)


def _out_mem(x, y_lru, y_fox, fox_out_g, w_out, norm_mem_x_g, mem_wq, mem_q_g, k_mem, v_mem, mem_wo,
             norm_ffn_g, router_group_w, router_group_b, router_expert_w, router_expert_b, *,
             n_heads, dh, tm):
    B, S, D = x.shape
    lw = y_lru.shape[-1]
    fw = y_fox.shape[-1]
    M = k_mem.shape[1]
    mw = n_heads * dh
    n_groups = router_group_w.shape[-1]
    n_experts = router_expert_w.shape[-1]
    w_r = jnp.concatenate([router_group_w, router_expert_w], axis=1)
    w_r = jnp.pad(w_r, ((0, 0), (0, LANES - w_r.shape[1]))).astype(BF16)
    b_r = jnp.concatenate([router_group_b, router_expert_b])
    b_r = jnp.pad(b_r, (0, LANES - b_r.shape[0])).reshape(1, LANES)
    q_gain = (mem_q_g * dh ** -0.5).reshape(1, dh)

    full = lambda shape: pl.BlockSpec(shape, lambda b, s: (0,) * len(shape))
    row = lambda width: pl.BlockSpec((1, tm, width), lambda b, s: (b, s, 0))
    mem_spec = pl.BlockSpec((1, M, mw), lambda b, s: (b, 0, 0))
    return pl.pallas_call(
        functools.partial(_out_mem_kernel, n_heads=n_heads, dh=dh, n_groups=n_groups,
                          per_group=n_experts // n_groups),
        grid=(B, S // tm),
        in_specs=[row(D), row(lw), row(fw), full((1, fw)), full((lw, D)), full((fw, D)),
                  full((1, D)), full((D, mw)), full((1, dh)), mem_spec, mem_spec, full((mw, D)),
                  full((1, D)), full((D, LANES)), full((1, LANES))],
        out_specs=[row(D), pl.BlockSpec((tm * D // LANES, LANES), lambda b, s: (b * (S // tm) + s, 0)),
                   row(LANES), row(LANES)],
        out_shape=[jax.ShapeDtypeStruct((B, S, D), F32),
                   jax.ShapeDtypeStruct((B * S * D // LANES, LANES), F32),
                   jax.ShapeDtypeStruct((B, S, LANES), F32),
                   jax.ShapeDtypeStruct((B, S, LANES), jnp.int32)],
        compiler_params=pltpu.CompilerParams(
            dimension_semantics=("parallel", "parallel"), vmem_limit_bytes=VMEM_LIMIT),
        name="out_mem",
    )(x, y_lru, y_fox, fox_out_g.reshape(1, fw), w_out[:lw].astype(BF16), w_out[lw:].astype(BF16),
      norm_mem_x_g.reshape(1, D), mem_wq.astype(BF16), q_gain, k_mem, v_mem, mem_wo.astype(BF16),
      norm_ffn_g.reshape(1, D), w_r, b_r)


DMA_UNROLL = 16
DMA_THREADS = 2
RANK_GROUP = 256


def _scatter_rows_kernel(zb_ref, idx_ref, x_ref, buf_ref, zero_ref, sem, zsem, *, top_k, chunks):
    tokens = x_ref.shape[0] // chunks

    @pl.when(pl.program_id(0) == 0)
    def _():
        rows = zero_ref.shape[0]
        zero_ref[...] = jnp.zeros_like(zero_ref)

        def copy(j):
            start = pl.multiple_of(j * rows, rows)
            return pltpu.make_async_copy(zero_ref, buf_ref.at[pl.ds(start, rows), :], zsem)

        def issue_zero(j, carry):
            @pl.when(zb_ref[j] == 1)
            def _():
                copy(j).start()
            return carry

        def drain_zero(j, carry):
            @pl.when(zb_ref[j] == 1)
            def _():
                copy(j).wait()
            return carry

        lax.fori_loop(0, zb_ref.shape[0], issue_zero, 0)
        lax.fori_loop(0, zb_ref.shape[0], drain_zero, 0)

    def issue(g, carry):
        for u in range(DMA_UNROLL):
            r = g * DMA_UNROLL + u
            src = x_ref.at[pl.ds(pl.multiple_of(r * chunks, chunks), chunks), :]
            for k in range(top_k):
                row = pl.multiple_of(idx_ref[0, 0, r * top_k + k], chunks)
                pltpu.make_async_copy(src, buf_ref.at[pl.ds(row, chunks), :], sem).start(
                    priority=(u * top_k + k) % DMA_THREADS)
        return carry

    lax.fori_loop(0, tokens // DMA_UNROLL, issue, 0)
    for k in range(top_k):
        pltpu.make_async_copy(x_ref, buf_ref.at[pl.ds(0, tokens * chunks), :], sem).wait()


def _scatter_rows(x, dest, zero_blocks, *, tm, blk, chunks):
    nt = dest.shape[0]
    top_k = dest.shape[2] // tm
    n_blocks = zero_blocks.shape[0]
    grid_spec = pltpu.PrefetchScalarGridSpec(
        num_scalar_prefetch=1,
        grid=(nt,),
        in_specs=[pl.BlockSpec((1, 1, tm * top_k), lambda i, zb: (i, 0, 0), memory_space=pltpu.SMEM),
                  pl.BlockSpec((tm * chunks, LANES), lambda i, zb: (i, 0))],
        out_specs=pl.BlockSpec(memory_space=pl.ANY),
        scratch_shapes=[pltpu.VMEM((blk * chunks, LANES), x.dtype), pltpu.SemaphoreType.DMA,
                        pltpu.SemaphoreType.DMA],
    )
    return pl.pallas_call(
        functools.partial(_scatter_rows_kernel, top_k=top_k, chunks=chunks),
        grid_spec=grid_spec,
        out_shape=jax.ShapeDtypeStruct((n_blocks * blk * chunks, LANES), x.dtype),
        compiler_params=pltpu.CompilerParams(
            dimension_semantics=("arbitrary",), vmem_limit_bytes=VMEM_LIMIT),
        name="scatter_rows",
    )(zero_blocks, dest, x)


X_RING = 3
X_PRIORITY = 1


def _experts_kernel(be_ref, nv_ref, x_hbm, wg_ref, wu_ref, wd_ref, o_ref, xbuf_ref, wgb_ref, wub_ref,
                    wdb_ref, sem, *, blk):
    i = pl.program_id(0)
    n = pl.num_programs(0)
    rows = xbuf_ref.shape[1]
    half = blk // 2
    nv = nv_ref[i]

    def fetch(step):
        slot = lax.rem(step, X_RING)
        src = x_hbm.at[pl.ds(pl.multiple_of(step * rows, rows), rows), :]
        return pltpu.make_async_copy(src, xbuf_ref.at[slot], sem.at[slot])

    def has_rows(step):
        return (step < n) & (nv_ref[jnp.minimum(step, n - 1)] > 0)

    @pl.when(i == 0)
    def _():
        for step in range(X_RING - 1):
            @pl.when(has_rows(step))
            def _():
                fetch(step).start(priority=X_PRIORITY)

    @pl.when(has_rows(i + X_RING - 1))
    def _():
        fetch(i + X_RING - 1).start(priority=X_PRIORITY)

    @pl.when((i == 0) | (be_ref[i] != be_ref[jnp.maximum(i - 1, 0)]))
    def _():
        wgb_ref[...] = wg_ref[0].astype(BF16)
        wub_ref[...] = wu_ref[0].astype(BF16)
        wdb_ref[...] = wd_ref[0].astype(BF16)

    def mlp(n_rows):
        d = wg_ref.shape[1]
        xb = _load_chunk_rows(xbuf_ref.at[lax.rem(i, X_RING)], n_rows, d).astype(BF16)
        hdn = jax.nn.silu(_dot(xb, wgb_ref[...])) * _dot(xb, wub_ref[...])
        _store_chunk_rows(o_ref, _dot(hdn.astype(BF16), wdb_ref[...]))

    @pl.when(nv > 0)
    def _():
        fetch(i).wait()

    @pl.when(nv > half)
    def _():
        mlp(blk)

    @pl.when((nv > 0) & (nv <= half))
    def _():
        mlp(half)
        o_ref[rows // 2:, :] = jnp.zeros((rows // 2, o_ref.shape[1]), o_ref.dtype)

    @pl.when(nv == 0)
    def _():
        o_ref[...] = jnp.zeros_like(o_ref)


def _experts(x_buf, blk_expert, n_valid, w_gate, w_up, w_down, *, blk):
    _, D, de = w_gate.shape
    chunks = D // LANES
    n_blocks = x_buf.shape[0] // (blk * chunks)
    grid_spec = pltpu.PrefetchScalarGridSpec(
        num_scalar_prefetch=2,
        grid=(n_blocks,),
        in_specs=[pl.BlockSpec(memory_space=pl.ANY),
                  pl.BlockSpec((1, D, de), lambda i, be, nu: (be[i], 0, 0)),
                  pl.BlockSpec((1, D, de), lambda i, be, nu: (be[i], 0, 0)),
                  pl.BlockSpec((1, de, D), lambda i, be, nu: (be[i], 0, 0))],
        out_specs=pl.BlockSpec((blk * chunks, LANES), lambda i, be, nu: (i, 0)),
        scratch_shapes=[pltpu.VMEM((X_RING, blk * chunks, LANES), F32),
                        pltpu.VMEM((D, de), BF16), pltpu.VMEM((D, de), BF16),
                        pltpu.VMEM((de, D), BF16), pltpu.SemaphoreType.DMA((X_RING,))],
    )
    return pl.pallas_call(
        functools.partial(_experts_kernel, blk=blk),
        grid_spec=grid_spec,
        out_shape=jax.ShapeDtypeStruct(x_buf.shape, F32),
        compiler_params=pltpu.CompilerParams(
            dimension_semantics=("arbitrary",), vmem_limit_bytes=VMEM_LIMIT),
        name="experts",
    )(blk_expert, n_valid, x_buf, w_gate, w_up, w_down)


def _combine_kernel(idx_ref, idx_next_ref, x_ref, g_ref, y_ref, o_ref, buf_ref, sem, *, top_k, chunks):
    i = pl.program_id(0)
    n = pl.num_programs(0)
    slot = lax.rem(i, 2)
    tokens, d = x_ref.shape

    def request(ids_ref, s):
        def issue(g, carry):
            for u in range(DMA_UNROLL):
                r = g * DMA_UNROLL + u
                for k in range(top_k):
                    row = pl.multiple_of(ids_ref[0, 0, r * top_k + k], chunks)
                    dst = buf_ref.at[s, k, pl.ds(pl.multiple_of(r * chunks, chunks), chunks), :]
                    pltpu.make_async_copy(y_ref.at[pl.ds(row, chunks), :], dst, sem.at[s]).start(
                        priority=(u * top_k + k) % DMA_THREADS)
            return carry
        lax.fori_loop(0, tokens // DMA_UNROLL, issue, 0)

    @pl.when(i == 0)
    def _():
        request(idx_ref, 0)

    @pl.when(i + 1 < n)
    def _():
        request(idx_next_ref, 1 - slot)

    for k in range(top_k):
        pltpu.make_async_copy(y_ref.at[pl.ds(0, tokens * chunks), :], buf_ref.at[slot, k],
                              sem.at[slot]).wait()
    out = x_ref[...]
    for k in range(top_k):
        out = out + g_ref[:, k:k + 1] * _load_chunk_rows(buf_ref.at[slot, k], tokens, d)
    o_ref[...] = out


def _combine(x2, gates, y_buf, dest, *, tm):
    T, D = x2.shape
    chunks = D // LANES
    nt = dest.shape[0]
    top_k = dest.shape[2] // tm
    idx_spec = lambda step: pl.BlockSpec(
        (1, 1, tm * top_k), lambda i: (jnp.minimum(i + step, nt - 1), 0, 0), memory_space=pltpu.SMEM)
    return pl.pallas_call(
        functools.partial(_combine_kernel, top_k=top_k, chunks=chunks),
        grid=(nt,),
        in_specs=[idx_spec(0), idx_spec(1),
                  pl.BlockSpec((tm, D), lambda i: (i, 0)),
                  pl.BlockSpec((tm, LANES), lambda i: (i, 0)),
                  pl.BlockSpec(memory_space=pl.ANY)],
        out_specs=pl.BlockSpec((tm, D), lambda i: (i, 0)),
        out_shape=jax.ShapeDtypeStruct((T, D), F32),
        scratch_shapes=[pltpu.VMEM((2, top_k, tm * chunks, LANES), F32),
                        pltpu.SemaphoreType.DMA((2,))],
        compiler_params=pltpu.CompilerParams(
            dimension_semantics=("arbitrary",), vmem_limit_bytes=VMEM_LIMIT),
        name="combine",
    )(dest, dest, x2, gates, y_buf)


def _dispatch(experts, n_experts, blk):
    T, K = experts.shape
    A = T * K
    e_flat = experts.reshape(A)
    g = _pick_tile(A, RANK_GROUP)
    onehot = (e_flat.reshape(A // g, g, 1) == jnp.arange(n_experts, dtype=jnp.int32)).astype(BF16)
    tri = jnp.tril(jnp.ones((g, g), BF16))
    within = jnp.einsum('ij,gje->gie', tri, onehot, preferred_element_type=F32)
    group_counts = within[:, -1, :]
    before = jnp.cumsum(group_counts, axis=0) - group_counts
    counts = jnp.sum(group_counts, axis=0).astype(jnp.int32)
    padded = (counts + blk - 1) // blk * blk
    pends = jnp.cumsum(padded)
    pstarts = pends - padded
    row = within + before[:, None, :] - 1.0 + pstarts.astype(F32)
    dest = jnp.sum(onehot.astype(F32) * row, axis=-1).astype(jnp.int32).reshape(A)
    n_blocks = (A + n_experts * blk) // blk
    blk_start = jnp.arange(n_blocks, dtype=jnp.int32) * blk
    blk_expert = jnp.minimum(jnp.sum(blk_start[:, None] >= pends[None, :], axis=1), n_experts - 1)
    last_of_segment = jnp.any((blk_start[:, None] + blk == pends[None, :]) & (padded[None, :] > 0),
                              axis=1)
    zero_blocks = (last_of_segment | (blk_start >= pends[-1])).astype(jnp.int32)
    n_valid = jnp.where(blk_start < pends[-1],
                        jnp.clip(counts[blk_expert] - (blk_start - pstarts[blk_expert]), 0, blk), 0)
    return dest.reshape(T, K), blk_expert.astype(jnp.int32), zero_blocks, n_valid.astype(jnp.int32)


class _Tiles(NamedTuple):
    rows: int
    q_rows: int
    expert_rows: int
    scatter_rows: int
    combine_rows: int


def _tiles(seq, tokens):
    return _Tiles(rows=_pick_tile(seq, 1024), q_rows=_pick_tile(seq, 1024), expert_rows=512,
                  scatter_rows=_pick_tile(tokens, 1024), combine_rows=_pick_tile(tokens, 512))


def kernel(x, mem, norm_mix_g, w_in, b_forget, conv_w, conv_b, lru_wa, lru_ba, lru_wx, lru_bx,
           lru_a_param, fox_q_g, fox_k_g, lru_out_g, fox_out_g, w_out, norm_mem_x_g, norm_mem_g,
           mem_wq, mem_wkv, mem_q_g, mem_k_g, mem_wo, norm_ffn_g, router_group_w, router_group_b,
           router_expert_w, router_expert_b, exp_w_gate, exp_w_up, exp_w_down):
    B, S, D = x.shape
    depth = norm_mix_g.shape[0]
    lw = conv_w.shape[-1]
    n_heads = b_forget.shape[-1]
    dh = fox_q_g.shape[-1]
    mem_dh = mem_q_g.shape[-1]
    mem_heads = mem_wq.shape[-1] // mem_dh
    n_experts = router_expert_w.shape[-1]
    T = B * S
    tm, tq, blk, scatter_tm, combine_tm = _tiles(S, T)
    assert D % (2 * LANES) == 0 and lw % MXU_DIM == 0 and MXU_DIM % lru_wa.shape[-1] == 0
    assert n_heads % 2 == 0 and mem_dh % LANES == 0 and mem.shape[1] % SUBLANES == 0
    assert router_group_w.shape[-1] + n_experts <= LANES and n_experts % router_group_w.shape[-1] == 0
    assert tm % SUBLANES == 0 and tq % MXU_DIM == 0 and (T * TOP_K) % blk == 0

    for l in range(depth):
        y_lru, q, k, v = _in_proj(
            x, norm_mix_g[l], w_in[l], b_forget[l], fox_q_g[l], fox_k_g[l], conv_w[l], conv_b[l],
            lru_wa[l], lru_ba[l], lru_wx[l], lru_bx[l], lru_a_param[l], lru_out_g[l],
            lw=lw, n_heads=n_heads, dh=dh, tm=tm)
        y_fox = _fox_attn(q, k, v, n_heads=n_heads, dh=dh, tq=tq)
        k_mem, v_mem = _mem_kv(mem, norm_mem_g[l], mem_wkv[l], mem_k_g[l], n_heads=mem_heads,
                               dh=mem_dh)
        x2, xn, gates, experts = _out_mem(
            x, y_lru, y_fox, fox_out_g[l], w_out[l], norm_mem_x_g[l], mem_wq[l], mem_q_g[l], k_mem,
            v_mem, mem_wo[l], norm_ffn_g[l], router_group_w[l], router_group_b[l],
            router_expert_w[l], router_expert_b[l], n_heads=mem_heads, dh=mem_dh, tm=tm)

        dest, blk_expert, zero_blocks, n_valid = _dispatch(
            experts.reshape(T, LANES)[:, :TOP_K], n_experts, blk)
        chunks = D // LANES
        tiled = lambda t: (dest * chunks).reshape(T // t, 1, t * TOP_K)
        x_buf = _scatter_rows(xn, tiled(scatter_tm), zero_blocks, tm=scatter_tm, blk=blk,
                              chunks=chunks)
        y_buf = _experts(x_buf, blk_expert, n_valid, exp_w_gate[l], exp_w_up[l], exp_w_down[l],
                         blk=blk)
        x = _combine(x2.reshape(T, D), gates.reshape(T, LANES), y_buf, tiled(combine_tm),
                     tm=combine_tm)
        x = x.reshape(B, S, D)
    return x
```

```python
import functools
from typing import NamedTuple

import jax
import jax.numpy as jnp
import numpy as np
from jax import lax
from jax.experimental import pallas as pl
from jax.experimental.pallas import tpu as pltpu

EPS = 1e-6
LRU_C = 8.0
CONV_WIDTH = 4
TOP_K = 2
LANES = 128
SUBLANES = 8
MXU_DIM = 256
HEAD_SLOT = LANES
VMEM_LIMIT = 56 * 1024 * 1024

F32 = jnp.float32
BF16 = jnp.bfloat16
MIN_NORMAL = float(np.finfo(np.float32).tiny)


def _dot(a, b):
    return jnp.dot(a, b, preferred_element_type=F32)


def _rms(x, g):
    return x * lax.rsqrt(jnp.mean(x * x, axis=-1, keepdims=True) + EPS) * g


def _softplus(x):
    return jnp.maximum(x, 0.0) + jnp.log1p(jnp.exp(-jnp.abs(x)))


def _pick_tile(n, target):
    t = min(n, target)
    while n % t:
        t //= 2
    return t


def _store_chunk_rows(ref, val):
    n, d = val.shape
    chunks = d // LANES
    for c in range(chunks):
        ref[pl.ds(c, n, stride=chunks), :] = val[:, c * LANES:(c + 1) * LANES]


def _load_chunk_rows(ref, n, d):
    chunks = d // LANES
    return jnp.concatenate([ref[pl.ds(c, n, stride=chunks), :] for c in range(chunks)], axis=1)


def _in_proj_kernel(x_ref, g_ref, w_ref, wf_ref, bf_ref, gq_ref, gk_ref, pe_ref, cv_ref,
                    cw_ref, cb_ref, wl_ref, ba_ref, bx_ref, ap_ref, og_ref,
                    ylru_ref, q_ref, k_ref, v_ref, carry_ref, tail_ref, h_ref, *, lw, n_heads, dh):
    @pl.when(pl.program_id(1) == 0)
    def _():
        carry_ref[...] = jnp.zeros_like(carry_ref)
        tail_ref[...] = jnp.zeros_like(tail_ref)
        h_ref[...] = jnp.zeros_like(h_ref)

    hs = n_heads * HEAD_SLOT
    x = x_ref[0]
    tm = x.shape[0]
    hb = _rms(x, g_ref[...]).astype(BF16)

    u = _dot(hb, w_ref[:, 0:lw])
    gate = _dot(hb, w_ref[:, lw:2 * lw])
    ylru_ref[0] = _rg_lru_tile(u, gate, cw_ref, cb_ref, wl_ref, ba_ref, bx_ref, ap_ref, og_ref,
                               tail_ref, h_ref).astype(ylru_ref.dtype)

    z = _dot(hb, wf_ref[...]) + bf_ref[...]
    lane = lax.broadcasted_iota(jnp.int32, z.shape, 1)
    row = lax.broadcasted_iota(jnp.int32, z.shape, 0)
    c = jnp.where(lane < n_heads, -_softplus(-z), 0.0)
    d = 1
    while d < tm:
        c = c + jnp.where(row >= d, pltpu.roll(c, d, 0), 0.0)
        d *= 2
    c = c + carry_ref[...]
    carry_ref[...] = c[tm - 1:tm, :]
    c1 = c.astype(BF16).astype(F32)
    r1 = c - c1
    c2 = r1.astype(BF16).astype(F32)
    c3 = (r1 - c2).astype(BF16).astype(F32)
    e = c1 + pltpu.roll(c2, n_heads, 1) + pltpu.roll(c3, 2 * n_heads, 1)
    e = jnp.where(lane == 3 * n_heads, 1.0, e).astype(BF16)
    ext = _dot(e, pe_ref[...])

    fw = n_heads * dh
    lane_t = lax.broadcasted_iota(jnp.int32, (tm, LANES), 1)
    own = (lane_t < dh, lane_t >= dh)

    def to_slots(y):
        parts = []
        for c in range(fw // LANES):
            col = y[:, c * LANES:(c + 1) * LANES]
            parts += [jnp.where(own[0], col, 0.0), jnp.where(own[1], col, 0.0)]
        return jnp.concatenate(parts, axis=-1)

    def head_norm(y, gain):
        parts = []
        for h in range(n_heads):
            blk = y[:, h * HEAD_SLOT:(h + 1) * HEAD_SLOT]
            ss = jnp.sum(blk * blk, axis=-1, keepdims=True) * (1.0 / dh)
            parts.append(blk * lax.rsqrt(ss + EPS))
        return jnp.concatenate(parts, axis=-1) * gain

    q = to_slots(_dot(hb, w_ref[:, 2 * lw:2 * lw + fw]))
    q_ref[0] = (head_norm(q, gq_ref[...]) + ext[:, 0:hs]).astype(BF16)
    k = to_slots(_dot(hb, w_ref[:, 2 * lw + fw:2 * lw + 2 * fw]))
    k_ref[0] = (head_norm(k, gk_ref[...]) + ext[:, hs:2 * hs]).astype(BF16)
    v = to_slots(_dot(hb, w_ref[:, 2 * lw + 2 * fw:2 * lw + 3 * fw]))
    v_ref[0] = (v + cv_ref[...]).astype(BF16)


def _in_proj(x, norm_g, w_in, b_forget, fox_q_g, fox_k_g, conv_w, conv_b, wa, ba, wx, bx, a_param,
             lru_out_g, *, lw, n_heads, dh, tm):
    B, S, D = x.shape
    fw = n_heads * dh
    hs = n_heads * HEAD_SLOT
    assert 2 * dh == HEAD_SLOT and fw % LANES == 0 and 3 * n_heads <= LANES
    n_main = 2 * lw + 3 * fw
    w_all = w_in[:, :n_main].astype(BF16)
    w_f = jnp.pad(w_in[:, n_main:], ((0, 0), (0, LANES - n_heads))).astype(BF16)
    b_f = jnp.pad(b_forget, (0, LANES - n_heads)).reshape(1, LANES)
    scale = dh ** -0.5

    def slot_gains(g):
        even = jnp.pad(g, (0, HEAD_SLOT - dh))
        odd = jnp.pad(g, (HEAD_SLOT - dh, 0))
        return jnp.tile(jnp.concatenate([even, odd]), n_heads // 2).reshape(1, hs)

    gq = slot_gains(fox_q_g * scale)
    gk = slot_gains(fox_k_g)
    ones_row = 3 * n_heads
    pe = np.zeros((LANES, 2 * hs), np.float32)
    cv = np.zeros((1, hs), np.float32)
    for h in range(n_heads):
        x0 = h * HEAD_SLOT + (dh if h % 2 == 0 else 0)
        for j in range(3):
            pe[j * n_heads + h, x0 + j] = 1.0
            pe[ones_row, x0 + 3 + j] = 1.0
            pe[ones_row, hs + x0 + j] = 1.0
            pe[j * n_heads + h, hs + x0 + 3 + j] = -1.0
        cv[0, x0] = 1.0
    pe = jnp.asarray(pe, BF16)

    w_lru = _lru_gate_weights(wa, wx, lw)
    vec = lambda a: a.reshape(1, lw)
    full = lambda shape: pl.BlockSpec(shape, lambda b, s: (0,) * len(shape))
    row = lambda width: pl.BlockSpec((1, tm, width), lambda b, s: (b, s, 0))
    return pl.pallas_call(
        functools.partial(_in_proj_kernel, lw=lw, n_heads=n_heads, dh=dh),
        grid=(B, S // tm),
        in_specs=[row(D), full((1, D)), full(w_all.shape), full(w_f.shape), full((1, LANES)),
                  full((1, hs)), full((1, hs)), full(pe.shape), full((1, hs)),
                  full((CONV_WIDTH, lw)), full((1, lw)), full(w_lru.shape), full((1, lw)),
                  full((1, lw)), full((1, lw)), full((1, lw))],
        out_specs=[row(lw), row(hs), row(hs), row(hs)],
        out_shape=[jax.ShapeDtypeStruct((B, S, lw), BF16),
                   jax.ShapeDtypeStruct((B, S, hs), BF16), jax.ShapeDtypeStruct((B, S, hs), BF16),
                   jax.ShapeDtypeStruct((B, S, hs), BF16)],
        scratch_shapes=[pltpu.VMEM((1, LANES), F32), pltpu.VMEM((SUBLANES, lw), F32),
                        pltpu.VMEM((1, lw), F32)],
        compiler_params=pltpu.CompilerParams(
            dimension_semantics=("parallel", "arbitrary"), vmem_limit_bytes=VMEM_LIMIT),
        name="in_proj",
    )(x, norm_g.reshape(1, D), w_all, w_f, b_f, gq, gk, pe, jnp.asarray(cv),
      conv_w, vec(conv_b), w_lru, vec(ba), vec(bx), vec(a_param), vec(lru_out_g))


def _rg_lru_tile(u, gate, cw_ref, cb_ref, w_ref, ba_ref, bx_ref, ap_ref, og_ref, tail_ref, h_ref):
    ts, lw = u.shape
    ext = jnp.concatenate([tail_ref[...], u], axis=0)
    tail_ref[...] = u[ts - SUBLANES:, :]
    xc = cb_ref[...] + u * cw_ref[CONV_WIDTH - 1:CONV_WIDTH, :]
    for back in range(1, CONV_WIDTH):
        tap = CONV_WIDTH - 1 - back
        xc = xc + pltpu.roll(ext, back, 0)[SUBLANES:, :] * cw_ref[tap:tap + 1, :]

    xb = xc.astype(BF16)
    r_parts, i_parts = [], []
    for j in range(lw // MXU_DIM):
        y = _dot(xb[:, j * MXU_DIM:(j + 1) * MXU_DIM], w_ref[j])
        r_parts.append(y[:, :MXU_DIM])
        i_parts.append(y[:, MXU_DIM:])
    r = jax.nn.sigmoid(jnp.concatenate(r_parts, axis=-1) + ba_ref[...])
    i = jax.nn.sigmoid(jnp.concatenate(i_parts, axis=-1) + bx_ref[...])
    log_a = (-LRU_C) * r * _softplus(-ap_ref[...])
    a = jnp.exp(log_a)
    gap = jnp.maximum(-jnp.tanh(log_a) * (a * a + 1.0), 0.0)
    mult = gap * lax.rsqrt(jnp.maximum(gap, MIN_NORMAL))
    b = mult * (i * xc)

    groups = ts // SUBLANES
    a = a.reshape(groups, SUBLANES, lw)
    b = b.reshape(groups, SUBLANES, lw)
    row = lax.broadcasted_iota(jnp.int32, a.shape, 1)
    d = 1
    while d < SUBLANES:
        keep = row >= d
        a_prev = jnp.where(keep, pltpu.roll(a, d, 1), 1.0)
        b_prev = jnp.where(keep, pltpu.roll(b, d, 1), 0.0)
        b = a * b_prev + b
        a = a * a_prev
        d *= 2
    state = h_ref[...]
    parts = []
    for g in range(groups):
        hg = b[g] + a[g] * state
        state = hg[SUBLANES - 1:, :]
        parts.append(hg)
    h = jnp.concatenate(parts, axis=0)
    h_ref[...] = state

    y = h * jax.nn.gelu(gate)
    return _rms(y, og_ref[...])


def _lru_gate_weights(wa, wx, lw):
    _, bd, _ = wa.shape
    per = MXU_DIM // bd
    n_tiles = lw // MXU_DIM

    def tiles(w):
        w = w.reshape(n_tiles, per, bd, bd)
        eye = jnp.eye(per, dtype=w.dtype)
        return jnp.einsum('tpij,pq->tpiqj', w, eye).reshape(n_tiles, MXU_DIM, MXU_DIM)

    return jnp.concatenate([tiles(wa), tiles(wx)], axis=-1).astype(BF16)


OVERFLOW_GUARD = 1e30


def _fox_attn_kernel(q_ref, k_ref, v_ref, o_ref, vt_ref, m_ref, acc_ref, *, dh, tq):
    n_chunks = vt_ref.shape[0]
    for c in range(n_chunks):
        vt_ref[c] = v_ref[0, c * tq:(c + 1) * tq, :].astype(F32).T.astype(BF16)

    heads = [slice(hh * HEAD_SLOT, (hh + 1) * HEAD_SLOT) for hh in range(2)]
    lax.fori_loop(0, n_chunks, functools.partial(
        _fox_attn_tile, q_ref=q_ref, k_ref=k_ref, o_ref=o_ref, vt_ref=vt_ref, m_ref=m_ref,
        acc_ref=acc_ref, heads=heads, dh=dh, tq=tq), 0)


def _fox_attn_tile(qi, carry, *, q_ref, k_ref, o_ref, vt_ref, m_ref, acc_ref, heads, dh, tq):
    q_rows = pl.ds(pl.multiple_of(qi * tq, tq), tq)

    def scores(j, q, sl, masked):
        start = pl.multiple_of(j * tq, tq)
        st = lax.dot_general(k_ref[0, pl.ds(start, tq), sl], q, (((1,), (1,)), ((), ())),
                             preferred_element_type=F32)
        if masked:
            kpos = lax.broadcasted_iota(jnp.int32, st.shape, 0)
            qpos = lax.broadcasted_iota(jnp.int32, st.shape, 1)
            st = jnp.where(kpos <= qpos, st, -jnp.inf)
        return st

    half = tq // 2
    diag = pl.multiple_of(qi * tq, tq)
    for hh, sl in enumerate(heads):
        q = q_ref[0, q_rows, sl]
        nt = (((1,), (1,)), ((), ()))
        st_a = lax.dot_general(k_ref[0, pl.ds(diag, half), sl], q, nt, preferred_element_type=F32)
        st_b = lax.dot_general(k_ref[0, pl.ds(diag + half, half), sl], q[half:, :], nt,
                               preferred_element_type=F32)
        causal = lambda st: jnp.where(lax.broadcasted_iota(jnp.int32, st.shape, 0)
                                      <= lax.broadcasted_iota(jnp.int32, st.shape, 1), st, -jnp.inf)
        st_a, st_b = causal(st_a), causal(st_b)
        m_a = jnp.max(st_a, axis=0, keepdims=True)
        m = jnp.concatenate([m_a[:, :half],
                             jnp.maximum(m_a[:, half:], jnp.max(st_b, axis=0, keepdims=True))], axis=1)
        m_ref[hh] = m
        vt = vt_ref[qi, sl, :]
        acc_a = _dot(vt[:, :half], jnp.exp(st_a - m).astype(BF16))
        acc_b = _dot(vt[:, half:], jnp.exp(st_b - m[:, half:]).astype(BF16))
        acc_ref[hh] = acc_a + jnp.concatenate([jnp.zeros_like(acc_b), acc_b], axis=1)

    def fast_chunk(j):
        for hh, sl in enumerate(heads):
            pt = jnp.exp(scores(j, q_ref[0, q_rows, sl], sl, False) - m_ref[hh]).astype(BF16)
            acc_ref[hh] += _dot(vt_ref[j, sl, :], pt)

    def fast_pair(i, carry):
        start = pl.multiple_of(2 * i * tq, 2 * tq)
        for hh, sl in enumerate(heads):
            st = lax.dot_general(k_ref[0, pl.ds(start, 2 * tq), sl], q_ref[0, q_rows, sl],
                                 (((1,), (1,)), ((), ())), preferred_element_type=F32)
            pt = jnp.exp(st - m_ref[hh]).astype(BF16)
            vt = jnp.concatenate([vt_ref[2 * i, sl, :], vt_ref[2 * i + 1, sl, :]], axis=1)
            acc_ref[hh] += _dot(vt, pt)
        return carry

    lax.fori_loop(0, qi // 2, fast_pair, 0)

    @pl.when(qi % 2 == 1)
    def _():
        fast_chunk(qi - 1)

    def softmax_sums():
        return jnp.concatenate([acc_ref[0][dh:dh + 1, :], acc_ref[1][0:1, :]], axis=0)

    def normalised():
        sums = softmax_sums()
        return jnp.concatenate([acc_ref[0][0:dh, :] / sums[0:1, :],
                                acc_ref[1][dh:2 * dh, :] / sums[1:2, :]], axis=0)

    out = normalised()
    o_ref[0, q_rows, :] = out.T.astype(o_ref.dtype)
    sums = softmax_sums()
    flag = lambda x: jnp.max(jnp.where(jnp.abs(x) < OVERFLOW_GUARD, 0.0, 1.0),
                             axis=0, keepdims=True)
    bad = jnp.max(jnp.maximum(flag(out), flag(sums)), axis=1, keepdims=True)

    @pl.when(bad[0, 0] > 0.0)
    def _():
        m_ref[...] = jnp.full_like(m_ref, -jnp.inf)
        acc_ref[...] = jnp.zeros_like(acc_ref)

        def chunk(j, masked):
            for hh, sl in enumerate(heads):
                st = scores(j, q_ref[0, q_rows, sl], sl, masked)
                m_old = m_ref[hh]
                m_new = jnp.maximum(m_old, jnp.max(st, axis=0, keepdims=True))
                pt = jnp.exp(st - m_new).astype(BF16)
                acc_ref[hh] = jnp.exp(m_old - m_new) * acc_ref[hh] + _dot(vt_ref[j, sl, :], pt)
                m_ref[hh] = m_new

        def body(j, carry):
            chunk(j, False)
            return carry

        lax.fori_loop(0, qi, body, 0)
        chunk(qi, True)
        o_ref[0, q_rows, :] = normalised().T.astype(o_ref.dtype)

    return carry


def _fox_attn(q, k, v, *, n_heads, dh, tq):
    B, S, _ = q.shape
    nq = S // tq
    pair = 2 * HEAD_SLOT
    qkv_spec = pl.BlockSpec((1, S, pair), lambda b, h: (b, 0, h))
    return pl.pallas_call(
        functools.partial(_fox_attn_kernel, dh=dh, tq=tq),
        grid=(B, n_heads // 2),
        in_specs=[qkv_spec, qkv_spec, qkv_spec],
        out_specs=pl.BlockSpec((1, S, 2 * dh), lambda b, h: (b, 0, h)),
        out_shape=jax.ShapeDtypeStruct((B, S, n_heads * dh), BF16),
        scratch_shapes=[pltpu.VMEM((nq, pair, tq), BF16), pltpu.VMEM((2, 1, tq), F32),
                        pltpu.VMEM((2, HEAD_SLOT, tq), F32)],
        compiler_params=pltpu.CompilerParams(
            dimension_semantics=("parallel", "parallel"), vmem_limit_bytes=VMEM_LIMIT),
        name="fox_attn",
    )(q, k, v)


def _mem_kv_kernel(m_ref, g_ref, w_ref, kg_ref, k_ref, v_ref, *, n_heads, dh):
    mw = n_heads * dh
    mb = _rms(m_ref[0], g_ref[...]).astype(BF16)
    kv = _dot(mb, w_ref[...])
    parts = []
    for h in range(n_heads):
        parts.append(_rms(kv[:, h * dh:(h + 1) * dh], kg_ref[...]))
    k_ref[0] = jnp.concatenate(parts, axis=-1).astype(BF16)
    v_ref[0] = kv[:, mw:].astype(BF16)


def _mem_kv(mem, norm_g, wkv, k_g, *, n_heads, dh):
    B, M, D = mem.shape
    mw = n_heads * dh
    full = lambda shape: pl.BlockSpec(shape, lambda b: (0,) * len(shape))
    out = pl.BlockSpec((1, M, mw), lambda b: (b, 0, 0))
    return pl.pallas_call(
        functools.partial(_mem_kv_kernel, n_heads=n_heads, dh=dh),
        grid=(B,),
        in_specs=[pl.BlockSpec((1, M, D), lambda b: (b, 0, 0)), full((1, D)), full((D, 2 * mw)),
                  full((1, dh))],
        out_specs=[out, out],
        out_shape=[jax.ShapeDtypeStruct((B, M, mw), BF16)] * 2,
        compiler_params=pltpu.CompilerParams(
            dimension_semantics=("parallel",), vmem_limit_bytes=VMEM_LIMIT),
        name="mem_kv",
    )(mem, norm_g.reshape(1, D), wkv.astype(BF16), k_g.reshape(1, dh))


def _out_mem_kernel(x_ref, yl_ref, yf_ref, fg_ref, wol_ref, wof_ref, gx_ref, wq_ref, qg_ref,
                    km_ref, vm_ref, wo_ref, gf_ref, wr_ref, br_ref,
                    x2_ref, xn_ref, gates_ref, experts_ref, *, n_heads, dh, n_groups, per_group):
    yf = _rms(yf_ref[0].astype(F32), fg_ref[...]).astype(BF16)
    x1 = x_ref[0] + _dot(yl_ref[0], wol_ref[...]) + _dot(yf, wof_ref[...])

    q = _dot(_rms(x1, gx_ref[...]).astype(BF16), wq_ref[...])
    outs = []
    for h in range(n_heads):
        sl = slice(h * dh, (h + 1) * dh)
        qh = _rms(q[:, sl], qg_ref[...]).astype(BF16)
        s = lax.dot_general(qh, km_ref[0, :, sl], (((1,), (1,)), ((), ())),
                            preferred_element_type=F32)
        p = jnp.exp(s - jnp.max(s, axis=-1, keepdims=True))
        inv = 1.0 / jnp.sum(p, axis=-1, keepdims=True)
        outs.append(_dot(p.astype(BF16), vm_ref[0, :, sl]) * inv)
    x2 = x1 + _dot(jnp.concatenate(outs, axis=-1).astype(BF16), wo_ref[...])
    x2_ref[0] = x2

    xn = _rms(x2, gf_ref[...])
    _store_chunk_rows(xn_ref, xn)
    logits = _dot(xn.astype(BF16), wr_ref[...]) + br_ref[...]
    lane = lax.broadcasted_iota(jnp.int32, logits.shape, 1).astype(F32)
    neg = -jnp.inf

    def top(vals):
        mx = jnp.max(vals, axis=-1, keepdims=True)
        idx = jnp.min(jnp.where(vals == mx, lane, float(LANES)), axis=-1, keepdims=True)
        return mx, idx

    gl = jnp.where(lane < n_groups, logits, neg)
    g_max, g_idx = top(gl)
    g_w = 1.0 / jnp.sum(jnp.exp(gl - g_max), axis=-1, keepdims=True)
    lo = n_groups + per_group * g_idx
    el = jnp.where((lane >= lo) & (lane < lo + per_group), logits, neg)
    e1, i1 = top(el)
    e2, i2 = top(jnp.where(lane == i1, neg, el))
    t = jnp.exp(e2 - e1)
    w1 = g_w / (1.0 + t)
    w2 = g_w * t / (1.0 + t)
    gates_ref[0] = jnp.where(lane == 0, w1, jnp.where(lane == 1, w2, 0.0))
    experts_ref[0] = jnp.where(lane == 0, i1 - n_groups,
                               jnp.where(lane == 1, i2 - n_groups, 0.0)).astype(jnp.int32)


def _out_mem(x, y_lru, y_fox, fox_out_g, w_out, norm_mem_x_g, mem_wq, mem_q_g, k_mem, v_mem, mem_wo,
             norm_ffn_g, router_group_w, router_group_b, router_expert_w, router_expert_b, *,
             n_heads, dh, tm):
    B, S, D = x.shape
    lw = y_lru.shape[-1]
    fw = y_fox.shape[-1]
    M = k_mem.shape[1]
    mw = n_heads * dh
    n_groups = router_group_w.shape[-1]
    n_experts = router_expert_w.shape[-1]
    w_r = jnp.concatenate([router_group_w, router_expert_w], axis=1)
    w_r = jnp.pad(w_r, ((0, 0), (0, LANES - w_r.shape[1]))).astype(BF16)
    b_r = jnp.concatenate([router_group_b, router_expert_b])
    b_r = jnp.pad(b_r, (0, LANES - b_r.shape[0])).reshape(1, LANES)
    q_gain = (mem_q_g * dh ** -0.5).reshape(1, dh)

    full = lambda shape: pl.BlockSpec(shape, lambda b, s: (0,) * len(shape))
    row = lambda width: pl.BlockSpec((1, tm, width), lambda b, s: (b, s, 0))
    mem_spec = pl.BlockSpec((1, M, mw), lambda b, s: (b, 0, 0))
    return pl.pallas_call(
        functools.partial(_out_mem_kernel, n_heads=n_heads, dh=dh, n_groups=n_groups,
                          per_group=n_experts // n_groups),
        grid=(B, S // tm),
        in_specs=[row(D), row(lw), row(fw), full((1, fw)), full((lw, D)), full((fw, D)),
                  full((1, D)), full((D, mw)), full((1, dh)), mem_spec, mem_spec, full((mw, D)),
                  full((1, D)), full((D, LANES)), full((1, LANES))],
        out_specs=[row(D), pl.BlockSpec((tm * D // LANES, LANES), lambda b, s: (b * (S // tm) + s, 0)),
                   row(LANES), row(LANES)],
        out_shape=[jax.ShapeDtypeStruct((B, S, D), F32),
                   jax.ShapeDtypeStruct((B * S * D // LANES, LANES), F32),
                   jax.ShapeDtypeStruct((B, S, LANES), F32),
                   jax.ShapeDtypeStruct((B, S, LANES), jnp.int32)],
        compiler_params=pltpu.CompilerParams(
            dimension_semantics=("parallel", "parallel"), vmem_limit_bytes=VMEM_LIMIT),
        name="out_mem",
    )(x, y_lru, y_fox, fox_out_g.reshape(1, fw), w_out[:lw].astype(BF16), w_out[lw:].astype(BF16),
      norm_mem_x_g.reshape(1, D), mem_wq.astype(BF16), q_gain, k_mem, v_mem, mem_wo.astype(BF16),
      norm_ffn_g.reshape(1, D), w_r, b_r)


DMA_UNROLL = 16
DMA_THREADS = 2
RANK_GROUP = 256


def _scatter_rows_kernel(zb_ref, idx_ref, x_ref, buf_ref, zero_ref, sem, zsem, *, top_k, chunks):
    tokens = x_ref.shape[0] // chunks

    @pl.when(pl.program_id(0) == 0)
    def _():
        rows = zero_ref.shape[0]
        zero_ref[...] = jnp.zeros_like(zero_ref)

        def copy(j):
            start = pl.multiple_of(j * rows, rows)
            return pltpu.make_async_copy(zero_ref, buf_ref.at[pl.ds(start, rows), :], zsem)

        def issue_zero(j, carry):
            @pl.when(zb_ref[j] == 1)
            def _():
                copy(j).start()
            return carry

        def drain_zero(j, carry):
            @pl.when(zb_ref[j] == 1)
            def _():
                copy(j).wait()
            return carry

        lax.fori_loop(0, zb_ref.shape[0], issue_zero, 0)
        lax.fori_loop(0, zb_ref.shape[0], drain_zero, 0)

    def issue(g, carry):
        for u in range(DMA_UNROLL):
            r = g * DMA_UNROLL + u
            src = x_ref.at[pl.ds(pl.multiple_of(r * chunks, chunks), chunks), :]
            for k in range(top_k):
                row = pl.multiple_of(idx_ref[0, 0, r * top_k + k], chunks)
                pltpu.make_async_copy(src, buf_ref.at[pl.ds(row, chunks), :], sem).start(
                    priority=(u * top_k + k) % DMA_THREADS)
        return carry

    lax.fori_loop(0, tokens // DMA_UNROLL, issue, 0)
    for k in range(top_k):
        pltpu.make_async_copy(x_ref, buf_ref.at[pl.ds(0, tokens * chunks), :], sem).wait()


def _scatter_rows(x, dest, zero_blocks, *, tm, blk, chunks):
    nt = dest.shape[0]
    top_k = dest.shape[2] // tm
    n_blocks = zero_blocks.shape[0]
    grid_spec = pltpu.PrefetchScalarGridSpec(
        num_scalar_prefetch=1,
        grid=(nt,),
        in_specs=[pl.BlockSpec((1, 1, tm * top_k), lambda i, zb: (i, 0, 0), memory_space=pltpu.SMEM),
                  pl.BlockSpec((tm * chunks, LANES), lambda i, zb: (i, 0))],
        out_specs=pl.BlockSpec(memory_space=pl.ANY),
        scratch_shapes=[pltpu.VMEM((blk * chunks, LANES), x.dtype), pltpu.SemaphoreType.DMA,
                        pltpu.SemaphoreType.DMA],
    )
    return pl.pallas_call(
        functools.partial(_scatter_rows_kernel, top_k=top_k, chunks=chunks),
        grid_spec=grid_spec,
        out_shape=jax.ShapeDtypeStruct((n_blocks * blk * chunks, LANES), x.dtype),
        compiler_params=pltpu.CompilerParams(
            dimension_semantics=("arbitrary",), vmem_limit_bytes=VMEM_LIMIT),
        name="scatter_rows",
    )(zero_blocks, dest, x)


X_RING = 3
X_PRIORITY = 1


def _experts_kernel(be_ref, nu_ref, x_hbm, wg_ref, wu_ref, wd_ref, o_ref, xbuf_ref, wgb_ref, wub_ref,
                    wdb_ref, sem, *, blk):
    i = pl.program_id(0)
    used = nu_ref[0]
    rows = xbuf_ref.shape[1]

    def fetch(step):
        slot = lax.rem(step, X_RING)
        src = x_hbm.at[pl.ds(pl.multiple_of(step * rows, rows), rows), :]
        return pltpu.make_async_copy(src, xbuf_ref.at[slot], sem.at[slot])

    @pl.when(i == 0)
    def _():
        for step in range(X_RING - 1):
            @pl.when(step < used)
            def _():
                fetch(step).start(priority=X_PRIORITY)

    @pl.when(i + X_RING - 1 < used)
    def _():
        fetch(i + X_RING - 1).start(priority=X_PRIORITY)

    @pl.when((i == 0) | (be_ref[i] != be_ref[jnp.maximum(i - 1, 0)]))
    def _():
        wgb_ref[...] = wg_ref[0].astype(BF16)
        wub_ref[...] = wu_ref[0].astype(BF16)
        wdb_ref[...] = wd_ref[0].astype(BF16)

    @pl.when(i < used)
    def _():
        fetch(i).wait()
        d = wg_ref.shape[1]
        xb = _load_chunk_rows(xbuf_ref.at[lax.rem(i, X_RING)], blk, d).astype(BF16)
        hdn = jax.nn.silu(_dot(xb, wgb_ref[...])) * _dot(xb, wub_ref[...])
        _store_chunk_rows(o_ref, _dot(hdn.astype(BF16), wdb_ref[...]))

    @pl.when(i >= used)
    def _():
        o_ref[...] = jnp.zeros_like(o_ref)


def _experts(x_buf, blk_expert, n_used, w_gate, w_up, w_down, *, blk):
    _, D, de = w_gate.shape
    chunks = D // LANES
    n_blocks = x_buf.shape[0] // (blk * chunks)
    grid_spec = pltpu.PrefetchScalarGridSpec(
        num_scalar_prefetch=2,
        grid=(n_blocks,),
        in_specs=[pl.BlockSpec(memory_space=pl.ANY),
                  pl.BlockSpec((1, D, de), lambda i, be, nu: (be[i], 0, 0)),
                  pl.BlockSpec((1, D, de), lambda i, be, nu: (be[i], 0, 0)),
                  pl.BlockSpec((1, de, D), lambda i, be, nu: (be[i], 0, 0))],
        out_specs=pl.BlockSpec((blk * chunks, LANES), lambda i, be, nu: (i, 0)),
        scratch_shapes=[pltpu.VMEM((X_RING, blk * chunks, LANES), F32),
                        pltpu.VMEM((D, de), BF16), pltpu.VMEM((D, de), BF16),
                        pltpu.VMEM((de, D), BF16), pltpu.SemaphoreType.DMA((X_RING,))],
    )
    return pl.pallas_call(
        functools.partial(_experts_kernel, blk=blk),
        grid_spec=grid_spec,
        out_shape=jax.ShapeDtypeStruct(x_buf.shape, F32),
        compiler_params=pltpu.CompilerParams(
            dimension_semantics=("arbitrary",), vmem_limit_bytes=VMEM_LIMIT),
        name="experts",
    )(blk_expert, n_used, x_buf, w_gate, w_up, w_down)


def _combine_kernel(idx_ref, idx_next_ref, x_ref, g_ref, y_ref, o_ref, buf_ref, sem, *, top_k, chunks):
    i = pl.program_id(0)
    n = pl.num_programs(0)
    slot = lax.rem(i, 2)
    tokens, d = x_ref.shape

    def request(ids_ref, s):
        def issue(g, carry):
            for u in range(DMA_UNROLL):
                r = g * DMA_UNROLL + u
                for k in range(top_k):
                    row = pl.multiple_of(ids_ref[0, 0, r * top_k + k], chunks)
                    dst = buf_ref.at[s, k, pl.ds(pl.multiple_of(r * chunks, chunks), chunks), :]
                    pltpu.make_async_copy(y_ref.at[pl.ds(row, chunks), :], dst, sem.at[s]).start(
                        priority=(u * top_k + k) % DMA_THREADS)
            return carry
        lax.fori_loop(0, tokens // DMA_UNROLL, issue, 0)

    @pl.when(i == 0)
    def _():
        request(idx_ref, 0)

    @pl.when(i + 1 < n)
    def _():
        request(idx_next_ref, 1 - slot)

    for k in range(top_k):
        pltpu.make_async_copy(y_ref.at[pl.ds(0, tokens * chunks), :], buf_ref.at[slot, k],
                              sem.at[slot]).wait()
    out = x_ref[...]
    for k in range(top_k):
        out = out + g_ref[:, k:k + 1] * _load_chunk_rows(buf_ref.at[slot, k], tokens, d)
    o_ref[...] = out


def _combine(x2, gates, y_buf, dest, *, tm):
    T, D = x2.shape
    chunks = D // LANES
    nt = dest.shape[0]
    top_k = dest.shape[2] // tm
    idx_spec = lambda step: pl.BlockSpec(
        (1, 1, tm * top_k), lambda i: (jnp.minimum(i + step, nt - 1), 0, 0), memory_space=pltpu.SMEM)
    return pl.pallas_call(
        functools.partial(_combine_kernel, top_k=top_k, chunks=chunks),
        grid=(nt,),
        in_specs=[idx_spec(0), idx_spec(1),
                  pl.BlockSpec((tm, D), lambda i: (i, 0)),
                  pl.BlockSpec((tm, LANES), lambda i: (i, 0)),
                  pl.BlockSpec(memory_space=pl.ANY)],
        out_specs=pl.BlockSpec((tm, D), lambda i: (i, 0)),
        out_shape=jax.ShapeDtypeStruct((T, D), F32),
        scratch_shapes=[pltpu.VMEM((2, top_k, tm * chunks, LANES), F32),
                        pltpu.SemaphoreType.DMA((2,))],
        compiler_params=pltpu.CompilerParams(
            dimension_semantics=("arbitrary",), vmem_limit_bytes=VMEM_LIMIT),
        name="combine",
    )(dest, dest, x2, gates, y_buf)


def _dispatch(experts, n_experts, blk):
    T, K = experts.shape
    A = T * K
    e_flat = experts.reshape(A)
    g = _pick_tile(A, RANK_GROUP)
    onehot = (e_flat.reshape(A // g, g, 1) == jnp.arange(n_experts, dtype=jnp.int32)).astype(BF16)
    tri = jnp.tril(jnp.ones((g, g), BF16))
    within = jnp.einsum('ij,gje->gie', tri, onehot, preferred_element_type=F32)
    group_counts = within[:, -1, :]
    before = jnp.cumsum(group_counts, axis=0) - group_counts
    counts = jnp.sum(group_counts, axis=0).astype(jnp.int32)
    padded = (counts + blk - 1) // blk * blk
    pends = jnp.cumsum(padded)
    pstarts = pends - padded
    row = within + before[:, None, :] - 1.0 + pstarts.astype(F32)
    dest = jnp.sum(onehot.astype(F32) * row, axis=-1).astype(jnp.int32).reshape(A)
    n_blocks = (A + n_experts * blk) // blk
    blk_start = jnp.arange(n_blocks, dtype=jnp.int32) * blk
    last_expert = jnp.max(jnp.where(counts > 0, jnp.arange(n_experts, dtype=jnp.int32), 0))
    blk_expert = jnp.minimum(jnp.sum(blk_start[:, None] >= pends[None, :], axis=1), last_expert)
    last_of_segment = jnp.any((blk_start[:, None] + blk == pends[None, :]) & (padded[None, :] > 0),
                              axis=1)
    zero_blocks = (last_of_segment | (blk_start >= pends[-1])).astype(jnp.int32)
    n_used = (pends[-1:] // blk).astype(jnp.int32)
    return dest.reshape(T, K), blk_expert.astype(jnp.int32), zero_blocks, n_used


class _Tiles(NamedTuple):
    rows: int
    q_rows: int
    expert_rows: int
    scatter_rows: int
    combine_rows: int


def _tiles(seq, tokens):
    return _Tiles(rows=_pick_tile(seq, 1024), q_rows=_pick_tile(seq, 1024), expert_rows=512,
                  scatter_rows=_pick_tile(tokens, 1024), combine_rows=_pick_tile(tokens, 512))


def kernel(x, mem, norm_mix_g, w_in, b_forget, conv_w, conv_b, lru_wa, lru_ba, lru_wx, lru_bx,
           lru_a_param, fox_q_g, fox_k_g, lru_out_g, fox_out_g, w_out, norm_mem_x_g, norm_mem_g,
           mem_wq, mem_wkv, mem_q_g, mem_k_g, mem_wo, norm_ffn_g, router_group_w, router_group_b,
           router_expert_w, router_expert_b, exp_w_gate, exp_w_up, exp_w_down):
    B, S, D = x.shape
    depth = norm_mix_g.shape[0]
    lw = conv_w.shape[-1]
    n_heads = b_forget.shape[-1]
    dh = fox_q_g.shape[-1]
    mem_dh = mem_q_g.shape[-1]
    mem_heads = mem_wq.shape[-1] // mem_dh
    n_experts = router_expert_w.shape[-1]
    T = B * S
    tm, tq, blk, scatter_tm, combine_tm = _tiles(S, T)
    assert D % (2 * LANES) == 0 and lw % MXU_DIM == 0 and MXU_DIM % lru_wa.shape[-1] == 0
    assert n_heads % 2 == 0 and mem_dh % LANES == 0 and mem.shape[1] % SUBLANES == 0
    assert router_group_w.shape[-1] + n_experts <= LANES and n_experts % router_group_w.shape[-1] == 0
    assert tm % SUBLANES == 0 and tq % MXU_DIM == 0 and (T * TOP_K) % blk == 0

    for l in range(depth):
        y_lru, q, k, v = _in_proj(
            x, norm_mix_g[l], w_in[l], b_forget[l], fox_q_g[l], fox_k_g[l], conv_w[l], conv_b[l],
            lru_wa[l], lru_ba[l], lru_wx[l], lru_bx[l], lru_a_param[l], lru_out_g[l],
            lw=lw, n_heads=n_heads, dh=dh, tm=tm)
        y_fox = _fox_attn(q, k, v, n_heads=n_heads, dh=dh, tq=tq)
        k_mem, v_mem = _mem_kv(mem, norm_mem_g[l], mem_wkv[l], mem_k_g[l], n_heads=mem_heads,
                               dh=mem_dh)
        x2, xn, gates, experts = _out_mem(
            x, y_lru, y_fox, fox_out_g[l], w_out[l], norm_mem_x_g[l], mem_wq[l], mem_q_g[l], k_mem,
            v_mem, mem_wo[l], norm_ffn_g[l], router_group_w[l], router_group_b[l],
            router_expert_w[l], router_expert_b[l], n_heads=mem_heads, dh=mem_dh, tm=tm)

        dest, blk_expert, zero_blocks, n_used = _dispatch(
            experts.reshape(T, LANES)[:, :TOP_K], n_experts, blk)
        chunks = D // LANES
        tiled = lambda t: (dest * chunks).reshape(T // t, 1, t * TOP_K)
        x_buf = _scatter_rows(xn, tiled(scatter_tm), zero_blocks, tm=scatter_tm, blk=blk,
                              chunks=chunks)
        y_buf = _experts(x_buf, blk_expert, n_used, exp_w_gate[l], exp_w_up[l], exp_w_down[l],
                         blk=blk)
        x = _combine(x2.reshape(T, D), gates.reshape(T, LANES), y_buf, tiled(combine_tm),
                     tm=combine_tm)
        x = x.reshape(B, S, D)
    return x
```

```python
import functools
from typing import NamedTuple

import jax
import jax.numpy as jnp
import numpy as np
from jax import lax
from jax.experimental import pallas as pl
from jax.experimental.pallas import tpu as pltpu

EPS = 1e-6
LRU_C = 8.0
CONV_WIDTH = 4
TOP_K = 2
LANES = 128
SUBLANES = 8
MXU_DIM = 256
HEAD_SLOT = LANES
VMEM_LIMIT = 56 * 1024 * 1024

F32 = jnp.float32
BF16 = jnp.bfloat16
MIN_NORMAL = float(np.finfo(np.float32).tiny)


def _dot(a, b):
    return jnp.dot(a, b, preferred_element_type=F32)


def _rms(x, g):
    return x * lax.rsqrt(jnp.mean(x * x, axis=-1, keepdims=True) + EPS) * g


def _softplus(x):
    return jnp.maximum(x, 0.0) + jnp.log1p(jnp.exp(-jnp.abs(x)))


def _pick_tile(n, target):
    t = min(n, target)
    while n % t:
        t //= 2
    return t


def _store_chunk_rows(ref, val):
    n, d = val.shape
    chunks = d // LANES
    for c in range(chunks):
        ref[pl.ds(c, n, stride=chunks), :] = val[:, c * LANES:(c + 1) * LANES]


def _load_chunk_rows(ref, n, d):
    chunks = d // LANES
    return jnp.concatenate([ref[pl.ds(c, n, stride=chunks), :] for c in range(chunks)], axis=1)


def _in_proj_kernel(x_ref, g_ref, w_ref, wf_ref, bf_ref, gq_ref, gk_ref, pe_ref, cv_ref,
                    cw_ref, cb_ref, wl_ref, ba_ref, bx_ref, ap_ref, og_ref,
                    ylru_ref, q_ref, k_ref, v_ref, carry_ref, tail_ref, h_ref, *, lw, n_heads, dh):
    @pl.when(pl.program_id(1) == 0)
    def _():
        carry_ref[...] = jnp.zeros_like(carry_ref)
        tail_ref[...] = jnp.zeros_like(tail_ref)
        h_ref[...] = jnp.zeros_like(h_ref)

    hs = n_heads * HEAD_SLOT
    x = x_ref[0]
    tm = x.shape[0]
    hb = _rms(x, g_ref[...]).astype(BF16)

    u = _dot(hb, w_ref[:, 0:lw])
    gate = _dot(hb, w_ref[:, lw:2 * lw])
    ylru_ref[0] = _rg_lru_tile(u, gate, cw_ref, cb_ref, wl_ref, ba_ref, bx_ref, ap_ref, og_ref,
                               tail_ref, h_ref).astype(ylru_ref.dtype)

    z = _dot(hb, wf_ref[...]) + bf_ref[...]
    lane = lax.broadcasted_iota(jnp.int32, z.shape, 1)
    row = lax.broadcasted_iota(jnp.int32, z.shape, 0)
    c = jnp.where(lane < n_heads, -_softplus(-z), 0.0)
    d = 1
    while d < tm:
        c = c + jnp.where(row >= d, pltpu.roll(c, d, 0), 0.0)
        d *= 2
    c = c + carry_ref[...]
    carry_ref[...] = c[tm - 1:tm, :]
    c1 = c.astype(BF16).astype(F32)
    r1 = c - c1
    c2 = r1.astype(BF16).astype(F32)
    c3 = (r1 - c2).astype(BF16).astype(F32)
    e = c1 + pltpu.roll(c2, n_heads, 1) + pltpu.roll(c3, 2 * n_heads, 1)
    e = jnp.where(lane == 3 * n_heads, 1.0, e).astype(BF16)
    ext = _dot(e, pe_ref[...])

    fw = n_heads * dh
    lane_t = lax.broadcasted_iota(jnp.int32, (tm, LANES), 1)
    own = (lane_t < dh, lane_t >= dh)

    def to_slots(y):
        parts = []
        for c in range(fw // LANES):
            col = y[:, c * LANES:(c + 1) * LANES]
            parts += [jnp.where(own[0], col, 0.0), jnp.where(own[1], col, 0.0)]
        return jnp.concatenate(parts, axis=-1)

    def head_norm(y, gain):
        parts = []
        for h in range(n_heads):
            blk = y[:, h * HEAD_SLOT:(h + 1) * HEAD_SLOT]
            ss = jnp.sum(blk * blk, axis=-1, keepdims=True) * (1.0 / dh)
            parts.append(blk * lax.rsqrt(ss + EPS))
        return jnp.concatenate(parts, axis=-1) * gain

    q = to_slots(_dot(hb, w_ref[:, 2 * lw:2 * lw + fw]))
    q_ref[0] = (head_norm(q, gq_ref[...]) + ext[:, 0:hs]).astype(BF16)
    k = to_slots(_dot(hb, w_ref[:, 2 * lw + fw:2 * lw + 2 * fw]))
    k_ref[0] = (head_norm(k, gk_ref[...]) + ext[:, hs:2 * hs]).astype(BF16)
    v = to_slots(_dot(hb, w_ref[:, 2 * lw + 2 * fw:2 * lw + 3 * fw]))
    v_ref[0] = (v + cv_ref[...]).astype(BF16)


def _in_proj(x, norm_g, w_in, b_forget, fox_q_g, fox_k_g, conv_w, conv_b, wa, ba, wx, bx, a_param,
             lru_out_g, *, lw, n_heads, dh, tm):
    B, S, D = x.shape
    fw = n_heads * dh
    hs = n_heads * HEAD_SLOT
    assert 2 * dh == HEAD_SLOT and fw % LANES == 0 and 3 * n_heads <= LANES
    n_main = 2 * lw + 3 * fw
    w_all = w_in[:, :n_main].astype(BF16)
    w_f = jnp.pad(w_in[:, n_main:], ((0, 0), (0, LANES - n_heads))).astype(BF16)
    b_f = jnp.pad(b_forget, (0, LANES - n_heads)).reshape(1, LANES)
    scale = dh ** -0.5

    def slot_gains(g):
        even = jnp.pad(g, (0, HEAD_SLOT - dh))
        odd = jnp.pad(g, (HEAD_SLOT - dh, 0))
        return jnp.tile(jnp.concatenate([even, odd]), n_heads // 2).reshape(1, hs)

    gq = slot_gains(fox_q_g * scale)
    gk = slot_gains(fox_k_g)
    ones_row = 3 * n_heads
    pe = np.zeros((LANES, 2 * hs), np.float32)
    cv = np.zeros((1, hs), np.float32)
    for h in range(n_heads):
        x0 = h * HEAD_SLOT + (dh if h % 2 == 0 else 0)
        for j in range(3):
            pe[j * n_heads + h, x0 + j] = 1.0
            pe[ones_row, x0 + 3 + j] = 1.0
            pe[ones_row, hs + x0 + j] = 1.0
            pe[j * n_heads + h, hs + x0 + 3 + j] = -1.0
        cv[0, x0] = 1.0
    pe = jnp.asarray(pe, BF16)

    w_lru = _lru_gate_weights(wa, wx, lw)
    vec = lambda a: a.reshape(1, lw)
    full = lambda shape: pl.BlockSpec(shape, lambda b, s: (0,) * len(shape))
    row = lambda width: pl.BlockSpec((1, tm, width), lambda b, s: (b, s, 0))
    return pl.pallas_call(
        functools.partial(_in_proj_kernel, lw=lw, n_heads=n_heads, dh=dh),
        grid=(B, S // tm),
        in_specs=[row(D), full((1, D)), full(w_all.shape), full(w_f.shape), full((1, LANES)),
                  full((1, hs)), full((1, hs)), full(pe.shape), full((1, hs)),
                  full((CONV_WIDTH, lw)), full((1, lw)), full(w_lru.shape), full((1, lw)),
                  full((1, lw)), full((1, lw)), full((1, lw))],
        out_specs=[row(lw), row(hs), row(hs), row(hs)],
        out_shape=[jax.ShapeDtypeStruct((B, S, lw), BF16),
                   jax.ShapeDtypeStruct((B, S, hs), BF16), jax.ShapeDtypeStruct((B, S, hs), BF16),
                   jax.ShapeDtypeStruct((B, S, hs), BF16)],
        scratch_shapes=[pltpu.VMEM((1, LANES), F32), pltpu.VMEM((SUBLANES, lw), F32),
                        pltpu.VMEM((1, lw), F32)],
        compiler_params=pltpu.CompilerParams(
            dimension_semantics=("parallel", "arbitrary"), vmem_limit_bytes=VMEM_LIMIT),
        name="in_proj",
    )(x, norm_g.reshape(1, D), w_all, w_f, b_f, gq, gk, pe, jnp.asarray(cv),
      conv_w, vec(conv_b), w_lru, vec(ba), vec(bx), vec(a_param), vec(lru_out_g))


def _rg_lru_tile(u, gate, cw_ref, cb_ref, w_ref, ba_ref, bx_ref, ap_ref, og_ref, tail_ref, h_ref):
    ts, lw = u.shape
    ext = jnp.concatenate([tail_ref[...], u], axis=0)
    tail_ref[...] = u[ts - SUBLANES:, :]
    xc = cb_ref[...] + u * cw_ref[CONV_WIDTH - 1:CONV_WIDTH, :]
    for back in range(1, CONV_WIDTH):
        tap = CONV_WIDTH - 1 - back
        xc = xc + pltpu.roll(ext, back, 0)[SUBLANES:, :] * cw_ref[tap:tap + 1, :]

    xb = xc.astype(BF16)
    r_parts, i_parts = [], []
    for j in range(lw // MXU_DIM):
        y = _dot(xb[:, j * MXU_DIM:(j + 1) * MXU_DIM], w_ref[j])
        r_parts.append(y[:, :MXU_DIM])
        i_parts.append(y[:, MXU_DIM:])
    r = jax.nn.sigmoid(jnp.concatenate(r_parts, axis=-1) + ba_ref[...])
    i = jax.nn.sigmoid(jnp.concatenate(i_parts, axis=-1) + bx_ref[...])
    log_a = (-LRU_C) * r * _softplus(-ap_ref[...])
    a = jnp.exp(log_a)
    gap = jnp.maximum(-jnp.tanh(log_a) * (a * a + 1.0), 0.0)
    mult = gap * lax.rsqrt(jnp.maximum(gap, MIN_NORMAL))
    b = mult * (i * xc)

    groups = ts // SUBLANES
    a = a.reshape(groups, SUBLANES, lw)
    b = b.reshape(groups, SUBLANES, lw)
    row = lax.broadcasted_iota(jnp.int32, a.shape, 1)
    d = 1
    while d < SUBLANES:
        keep = row >= d
        a_prev = jnp.where(keep, pltpu.roll(a, d, 1), 1.0)
        b_prev = jnp.where(keep, pltpu.roll(b, d, 1), 0.0)
        b = a * b_prev + b
        a = a * a_prev
        d *= 2
    state = h_ref[...]
    parts = []
    for g in range(groups):
        hg = b[g] + a[g] * state
        state = hg[SUBLANES - 1:, :]
        parts.append(hg)
    h = jnp.concatenate(parts, axis=0)
    h_ref[...] = state

    y = h * jax.nn.gelu(gate)
    return _rms(y, og_ref[...])


def _lru_gate_weights(wa, wx, lw):
    _, bd, _ = wa.shape
    per = MXU_DIM // bd
    n_tiles = lw // MXU_DIM

    def tiles(w):
        w = w.reshape(n_tiles, per, bd, bd)
        eye = jnp.eye(per, dtype=w.dtype)
        return jnp.einsum('tpij,pq->tpiqj', w, eye).reshape(n_tiles, MXU_DIM, MXU_DIM)

    return jnp.concatenate([tiles(wa), tiles(wx)], axis=-1).astype(BF16)


OVERFLOW_GUARD = 1e30


def _fox_attn_kernel(q_ref, k_ref, v_ref, o_ref, vt_ref, m_ref, acc_ref, *, dh, tq):
    n_chunks = vt_ref.shape[0]
    for c in range(n_chunks):
        vt_ref[c] = v_ref[0, c * tq:(c + 1) * tq, :].astype(F32).T.astype(BF16)

    heads = [slice(hh * HEAD_SLOT, (hh + 1) * HEAD_SLOT) for hh in range(2)]
    lax.fori_loop(0, n_chunks, functools.partial(
        _fox_attn_tile, q_ref=q_ref, k_ref=k_ref, o_ref=o_ref, vt_ref=vt_ref, m_ref=m_ref,
        acc_ref=acc_ref, heads=heads, dh=dh, tq=tq), 0)


def _fox_attn_tile(qi, carry, *, q_ref, k_ref, o_ref, vt_ref, m_ref, acc_ref, heads, dh, tq):
    q_rows = pl.ds(pl.multiple_of(qi * tq, tq), tq)

    def scores(j, q, sl, masked):
        start = pl.multiple_of(j * tq, tq)
        st = lax.dot_general(k_ref[0, pl.ds(start, tq), sl], q, (((1,), (1,)), ((), ())),
                             preferred_element_type=F32)
        if masked:
            kpos = lax.broadcasted_iota(jnp.int32, st.shape, 0)
            qpos = lax.broadcasted_iota(jnp.int32, st.shape, 1)
            st = jnp.where(kpos <= qpos, st, -jnp.inf)
        return st

    half = tq // 2
    diag = pl.multiple_of(qi * tq, tq)
    for hh, sl in enumerate(heads):
        q = q_ref[0, q_rows, sl]
        nt = (((1,), (1,)), ((), ()))
        st_a = lax.dot_general(k_ref[0, pl.ds(diag, half), sl], q, nt, preferred_element_type=F32)
        st_b = lax.dot_general(k_ref[0, pl.ds(diag + half, half), sl], q[half:, :], nt,
                               preferred_element_type=F32)
        causal = lambda st: jnp.where(lax.broadcasted_iota(jnp.int32, st.shape, 0)
                                      <= lax.broadcasted_iota(jnp.int32, st.shape, 1), st, -jnp.inf)
        st_a, st_b = causal(st_a), causal(st_b)
        m_a = jnp.max(st_a, axis=0, keepdims=True)
        m = jnp.concatenate([m_a[:, :half],
                             jnp.maximum(m_a[:, half:], jnp.max(st_b, axis=0, keepdims=True))], axis=1)
        m_ref[hh] = m
        vt = vt_ref[qi, sl, :]
        acc_a = _dot(vt[:, :half], jnp.exp(st_a - m).astype(BF16))
        acc_b = _dot(vt[:, half:], jnp.exp(st_b - m[:, half:]).astype(BF16))
        acc_ref[hh] = acc_a + jnp.concatenate([jnp.zeros_like(acc_b), acc_b], axis=1)

    def fast_chunk(j):
        for hh, sl in enumerate(heads):
            pt = jnp.exp(scores(j, q_ref[0, q_rows, sl], sl, False) - m_ref[hh]).astype(BF16)
            acc_ref[hh] += _dot(vt_ref[j, sl, :], pt)

    def fast_pair(i, carry):
        start = pl.multiple_of(2 * i * tq, 2 * tq)
        for hh, sl in enumerate(heads):
            st = lax.dot_general(k_ref[0, pl.ds(start, 2 * tq), sl], q_ref[0, q_rows, sl],
                                 (((1,), (1,)), ((), ())), preferred_element_type=F32)
            pt = jnp.exp(st - m_ref[hh]).astype(BF16)
            vt = jnp.concatenate([vt_ref[2 * i, sl, :], vt_ref[2 * i + 1, sl, :]], axis=1)
            acc_ref[hh] += _dot(vt, pt)
        return carry

    lax.fori_loop(0, qi // 2, fast_pair, 0)

    @pl.when(qi % 2 == 1)
    def _():
        fast_chunk(qi - 1)

    def softmax_sums():
        return jnp.concatenate([acc_ref[0][dh:dh + 1, :], acc_ref[1][0:1, :]], axis=0)

    def normalised():
        sums = softmax_sums()
        return jnp.concatenate([acc_ref[0][0:dh, :] / sums[0:1, :],
                                acc_ref[1][dh:2 * dh, :] / sums[1:2, :]], axis=0)

    out = normalised()
    o_ref[0, q_rows, :] = out.T.astype(o_ref.dtype)
    sums = softmax_sums()
    flag = lambda x: jnp.max(jnp.where(jnp.abs(x) < OVERFLOW_GUARD, 0.0, 1.0),
                             axis=0, keepdims=True)
    bad = jnp.max(jnp.maximum(flag(out), flag(sums)), axis=1, keepdims=True)

    @pl.when(bad[0, 0] > 0.0)
    def _():
        m_ref[...] = jnp.full_like(m_ref, -jnp.inf)
        acc_ref[...] = jnp.zeros_like(acc_ref)

        def chunk(j, masked):
            for hh, sl in enumerate(heads):
                st = scores(j, q_ref[0, q_rows, sl], sl, masked)
                m_old = m_ref[hh]
                m_new = jnp.maximum(m_old, jnp.max(st, axis=0, keepdims=True))
                pt = jnp.exp(st - m_new).astype(BF16)
                acc_ref[hh] = jnp.exp(m_old - m_new) * acc_ref[hh] + _dot(vt_ref[j, sl, :], pt)
                m_ref[hh] = m_new

        def body(j, carry):
            chunk(j, False)
            return carry

        lax.fori_loop(0, qi, body, 0)
        chunk(qi, True)
        o_ref[0, q_rows, :] = normalised().T.astype(o_ref.dtype)

    return carry


def _fox_attn(q, k, v, *, n_heads, dh, tq):
    B, S, _ = q.shape
    nq = S // tq
    pair = 2 * HEAD_SLOT
    qkv_spec = pl.BlockSpec((1, S, pair), lambda b, h: (b, 0, h))
    return pl.pallas_call(
        functools.partial(_fox_attn_kernel, dh=dh, tq=tq),
        grid=(B, n_heads // 2),
        in_specs=[qkv_spec, qkv_spec, qkv_spec],
        out_specs=pl.BlockSpec((1, S, 2 * dh), lambda b, h: (b, 0, h)),
        out_shape=jax.ShapeDtypeStruct((B, S, n_heads * dh), BF16),
        scratch_shapes=[pltpu.VMEM((nq, pair, tq), BF16), pltpu.VMEM((2, 1, tq), F32),
                        pltpu.VMEM((2, HEAD_SLOT, tq), F32)],
        compiler_params=pltpu.CompilerParams(
            dimension_semantics=("parallel", "parallel"), vmem_limit_bytes=VMEM_LIMIT),
        name="fox_attn",
    )(q, k, v)


def _mem_kv_kernel(m_ref, g_ref, w_ref, kg_ref, k_ref, v_ref, *, n_heads, dh):
    mw = n_heads * dh
    mb = _rms(m_ref[0], g_ref[...]).astype(BF16)
    kv = _dot(mb, w_ref[...])
    parts = []
    for h in range(n_heads):
        parts.append(_rms(kv[:, h * dh:(h + 1) * dh], kg_ref[...]))
    k_ref[0] = jnp.concatenate(parts, axis=-1).astype(BF16)
    v_ref[0] = kv[:, mw:].astype(BF16)


def _mem_kv(mem, norm_g, wkv, k_g, *, n_heads, dh):
    B, M, D = mem.shape
    mw = n_heads * dh
    full = lambda shape: pl.BlockSpec(shape, lambda b: (0,) * len(shape))
    out = pl.BlockSpec((1, M, mw), lambda b: (b, 0, 0))
    return pl.pallas_call(
        functools.partial(_mem_kv_kernel, n_heads=n_heads, dh=dh),
        grid=(B,),
        in_specs=[pl.BlockSpec((1, M, D), lambda b: (b, 0, 0)), full((1, D)), full((D, 2 * mw)),
                  full((1, dh))],
        out_specs=[out, out],
        out_shape=[jax.ShapeDtypeStruct((B, M, mw), BF16)] * 2,
        compiler_params=pltpu.CompilerParams(
            dimension_semantics=("parallel",), vmem_limit_bytes=VMEM_LIMIT),
        name="mem_kv",
    )(mem, norm_g.reshape(1, D), wkv.astype(BF16), k_g.reshape(1, dh))


def _out_mem_kernel(x_ref, yl_ref, yf_ref, fg_ref, wol_ref, wof_ref, gx_ref, wq_ref, qg_ref,
                    km_ref, vm_ref, wo_ref, gf_ref, wr_ref, br_ref,
                    x2_ref, xn_ref, gates_ref, experts_ref, *, n_heads, dh, n_groups, per_group):
    yf = _rms(yf_ref[0].astype(F32), fg_ref[...]).astype(BF16)
    x1 = x_ref[0] + _dot(yl_ref[0], wol_ref[...]) + _dot(yf, wof_ref[...])

    q = _dot(_rms(x1, gx_ref[...]).astype(BF16), wq_ref[...])
    outs = []
    for h in range(n_heads):
        sl = slice(h * dh, (h + 1) * dh)
        qh = _rms(q[:, sl], qg_ref[...]).astype(BF16)
        s = lax.dot_general(qh, km_ref[0, :, sl], (((1,), (1,)), ((), ())),
                            preferred_element_type=F32)
        p = jnp.exp(s - jnp.max(s, axis=-1, keepdims=True))
        inv = 1.0 / jnp.sum(p, axis=-1, keepdims=True)
        outs.append(_dot(p.astype(BF16), vm_ref[0, :, sl]) * inv)
    x2 = x1 + _dot(jnp.concatenate(outs, axis=-1).astype(BF16), wo_ref[...])
    x2_ref[0] = x2

    xn = _rms(x2, gf_ref[...])
    _store_chunk_rows(xn_ref, xn)
    logits = _dot(xn.astype(BF16), wr_ref[...]) + br_ref[...]
    lane = lax.broadcasted_iota(jnp.int32, logits.shape, 1).astype(F32)
    neg = -jnp.inf

    def top(vals):
        mx = jnp.max(vals, axis=-1, keepdims=True)
        idx = jnp.min(jnp.where(vals == mx, lane, float(LANES)), axis=-1, keepdims=True)
        return mx, idx

    gl = jnp.where(lane < n_groups, logits, neg)
    g_max, g_idx = top(gl)
    g_w = 1.0 / jnp.sum(jnp.exp(gl - g_max), axis=-1, keepdims=True)
    lo = n_groups + per_group * g_idx
    el = jnp.where((lane >= lo) & (lane < lo + per_group), logits, neg)
    e1, i1 = top(el)
    e2, i2 = top(jnp.where(lane == i1, neg, el))
    t = jnp.exp(e2 - e1)
    w1 = g_w / (1.0 + t)
    w2 = g_w * t / (1.0 + t)
    gates_ref[0] = jnp.where(lane == 0, w1, jnp.where(lane == 1, w2, 0.0))
    experts_ref[0] = jnp.where(lane == 0, i1 - n_groups,
                               jnp.where(lane == 1, i2 - n_groups, 0.0)).astype(jnp.int32)


def _out_mem(x, y_lru, y_fox, fox_out_g, w_out, norm_mem_x_g, mem_wq, mem_q_g, k_mem, v_mem, mem_wo,
             norm_ffn_g, router_group_w, router_group_b, router_expert_w, router_expert_b, *,
             n_heads, dh, tm):
    B, S, D = x.shape
    lw = y_lru.shape[-1]
    fw = y_fox.shape[-1]
    M = k_mem.shape[1]
    mw = n_heads * dh
    n_groups = router_group_w.shape[-1]
    n_experts = router_expert_w.shape[-1]
    w_r = jnp.concatenate([router_group_w, router_expert_w], axis=1)
    w_r = jnp.pad(w_r, ((0, 0), (0, LANES - w_r.shape[1]))).astype(BF16)
    b_r = jnp.concatenate([router_group_b, router_expert_b])
    b_r = jnp.pad(b_r, (0, LANES - b_r.shape[0])).reshape(1, LANES)
    q_gain = (mem_q_g * dh ** -0.5).reshape(1, dh)

    full = lambda shape: pl.BlockSpec(shape, lambda b, s: (0,) * len(shape))
    row = lambda width: pl.BlockSpec((1, tm, width), lambda b, s: (b, s, 0))
    mem_spec = pl.BlockSpec((1, M, mw), lambda b, s: (b, 0, 0))
    return pl.pallas_call(
        functools.partial(_out_mem_kernel, n_heads=n_heads, dh=dh, n_groups=n_groups,
                          per_group=n_experts // n_groups),
        grid=(B, S // tm),
        in_specs=[row(D), row(lw), row(fw), full((1, fw)), full((lw, D)), full((fw, D)),
                  full((1, D)), full((D, mw)), full((1, dh)), mem_spec, mem_spec, full((mw, D)),
                  full((1, D)), full((D, LANES)), full((1, LANES))],
        out_specs=[row(D), pl.BlockSpec((tm * D // LANES, LANES), lambda b, s: (b * (S // tm) + s, 0)),
                   row(LANES), row(LANES)],
        out_shape=[jax.ShapeDtypeStruct((B, S, D), F32),
                   jax.ShapeDtypeStruct((B * S * D // LANES, LANES), F32),
                   jax.ShapeDtypeStruct((B, S, LANES), F32),
                   jax.ShapeDtypeStruct((B, S, LANES), jnp.int32)],
        compiler_params=pltpu.CompilerParams(
            dimension_semantics=("parallel", "parallel"), vmem_limit_bytes=VMEM_LIMIT),
        name="out_mem",
    )(x, y_lru, y_fox, fox_out_g.reshape(1, fw), w_out[:lw].astype(BF16), w_out[lw:].astype(BF16),
      norm_mem_x_g.reshape(1, D), mem_wq.astype(BF16), q_gain, k_mem, v_mem, mem_wo.astype(BF16),
      norm_ffn_g.reshape(1, D), w_r, b_r)


DMA_UNROLL = 16
DMA_THREADS = 2
RANK_GROUP = 256


def _scatter_rows_kernel(zb_ref, idx_ref, x_ref, buf_ref, zero_ref, sem, zsem, *, top_k, chunks):
    tokens = x_ref.shape[0] // chunks

    @pl.when(pl.program_id(0) == 0)
    def _():
        rows = zero_ref.shape[0]
        zero_ref[...] = jnp.zeros_like(zero_ref)

        def copy(j):
            start = pl.multiple_of(j * rows, rows)
            return pltpu.make_async_copy(zero_ref, buf_ref.at[pl.ds(start, rows), :], zsem)

        def issue_zero(j, carry):
            @pl.when(zb_ref[j] == 1)
            def _():
                copy(j).start()
            return carry

        def drain_zero(j, carry):
            @pl.when(zb_ref[j] == 1)
            def _():
                copy(j).wait()
            return carry

        lax.fori_loop(0, zb_ref.shape[0], issue_zero, 0)
        lax.fori_loop(0, zb_ref.shape[0], drain_zero, 0)

    def issue(g, carry):
        for u in range(DMA_UNROLL):
            r = g * DMA_UNROLL + u
            src = x_ref.at[pl.ds(pl.multiple_of(r * chunks, chunks), chunks), :]
            for k in range(top_k):
                row = pl.multiple_of(idx_ref[0, 0, r * top_k + k], chunks)
                pltpu.make_async_copy(src, buf_ref.at[pl.ds(row, chunks), :], sem).start(
                    priority=(u * top_k + k) % DMA_THREADS)
        return carry

    lax.fori_loop(0, tokens // DMA_UNROLL, issue, 0)
    for k in range(top_k):
        pltpu.make_async_copy(x_ref, buf_ref.at[pl.ds(0, tokens * chunks), :], sem).wait()


def _scatter_rows(x, dest, zero_blocks, *, tm, blk, chunks):
    nt = dest.shape[0]
    top_k = dest.shape[2] // tm
    n_blocks = zero_blocks.shape[0]
    grid_spec = pltpu.PrefetchScalarGridSpec(
        num_scalar_prefetch=1,
        grid=(nt,),
        in_specs=[pl.BlockSpec((1, 1, tm * top_k), lambda i, zb: (i, 0, 0), memory_space=pltpu.SMEM),
                  pl.BlockSpec((tm * chunks, LANES), lambda i, zb: (i, 0))],
        out_specs=pl.BlockSpec(memory_space=pl.ANY),
        scratch_shapes=[pltpu.VMEM((blk * chunks, LANES), x.dtype), pltpu.SemaphoreType.DMA,
                        pltpu.SemaphoreType.DMA],
    )
    return pl.pallas_call(
        functools.partial(_scatter_rows_kernel, top_k=top_k, chunks=chunks),
        grid_spec=grid_spec,
        out_shape=jax.ShapeDtypeStruct((n_blocks * blk * chunks, LANES), x.dtype),
        compiler_params=pltpu.CompilerParams(
            dimension_semantics=("arbitrary",), vmem_limit_bytes=VMEM_LIMIT),
        name="scatter_rows",
    )(zero_blocks, dest, x)


X_RING = 3
X_PRIORITY = 1


def _experts_kernel(be_ref, nu_ref, x_hbm, wg_ref, wu_ref, wd_ref, o_ref, xbuf_ref, wgb_ref, wub_ref,
                    wdb_ref, sem, *, blk):
    i = pl.program_id(0)
    used = nu_ref[0]
    rows = xbuf_ref.shape[1]

    def fetch(step):
        slot = lax.rem(step, X_RING)
        src = x_hbm.at[pl.ds(pl.multiple_of(step * rows, rows), rows), :]
        return pltpu.make_async_copy(src, xbuf_ref.at[slot], sem.at[slot])

    @pl.when(i == 0)
    def _():
        for step in range(X_RING - 1):
            @pl.when(step < used)
            def _():
                fetch(step).start(priority=X_PRIORITY)

    @pl.when(i + X_RING - 1 < used)
    def _():
        fetch(i + X_RING - 1).start(priority=X_PRIORITY)

    @pl.when((i == 0) | (be_ref[i] != be_ref[jnp.maximum(i - 1, 0)]))
    def _():
        wgb_ref[...] = wg_ref[0].astype(BF16)
        wub_ref[...] = wu_ref[0].astype(BF16)
        wdb_ref[...] = wd_ref[0].astype(BF16)

    @pl.when(i < used)
    def _():
        fetch(i).wait()
        d = wg_ref.shape[1]
        xb = _load_chunk_rows(xbuf_ref.at[lax.rem(i, X_RING)], blk, d).astype(BF16)
        hdn = jax.nn.silu(_dot(xb, wgb_ref[...])) * _dot(xb, wub_ref[...])
        _store_chunk_rows(o_ref, _dot(hdn.astype(BF16), wdb_ref[...]))

    @pl.when(i >= used)
    def _():
        o_ref[...] = jnp.zeros_like(o_ref)


def _experts(x_buf, blk_expert, n_used, w_gate, w_up, w_down, *, blk):
    _, D, de = w_gate.shape
    chunks = D // LANES
    n_blocks = x_buf.shape[0] // (blk * chunks)
    grid_spec = pltpu.PrefetchScalarGridSpec(
        num_scalar_prefetch=2,
        grid=(n_blocks,),
        in_specs=[pl.BlockSpec(memory_space=pl.ANY),
                  pl.BlockSpec((1, D, de), lambda i, be, nu: (be[i], 0, 0)),
                  pl.BlockSpec((1, D, de), lambda i, be, nu: (be[i], 0, 0)),
                  pl.BlockSpec((1, de, D), lambda i, be, nu: (be[i], 0, 0))],
        out_specs=pl.BlockSpec((blk * chunks, LANES), lambda i, be, nu: (i, 0)),
        scratch_shapes=[pltpu.VMEM((X_RING, blk * chunks, LANES), F32),
                        pltpu.VMEM((D, de), BF16), pltpu.VMEM((D, de), BF16),
                        pltpu.VMEM((de, D), BF16), pltpu.SemaphoreType.DMA((X_RING,))],
    )
    return pl.pallas_call(
        functools.partial(_experts_kernel, blk=blk),
        grid_spec=grid_spec,
        out_shape=jax.ShapeDtypeStruct(x_buf.shape, F32),
        compiler_params=pltpu.CompilerParams(
            dimension_semantics=("arbitrary",), vmem_limit_bytes=VMEM_LIMIT),
        name="experts",
    )(blk_expert, n_used, x_buf, w_gate, w_up, w_down)


Y_RING = 3
COMBINE_GROUP = 64


def _combine_kernel(idx_ref, idx1_ref, idx2_ref, x_ref, g_ref, y_ref, o_ref, buf_ref, sem, *, top_k,
                    chunks):
    i = pl.program_id(0)
    n = pl.num_programs(0)
    slot = lax.rem(i, Y_RING)
    tokens, d = x_ref.shape

    def issue_group(ids_ref, s, g):
        for u in range(COMBINE_GROUP):
            r = g * COMBINE_GROUP + u
            for k in range(top_k):
                row = pl.multiple_of(ids_ref[0, 0, r * top_k + k], chunks)
                dst = buf_ref.at[s, k, pl.ds(pl.multiple_of(r * chunks, chunks), chunks), :]
                pltpu.make_async_copy(y_ref.at[pl.ds(row, chunks), :], dst, sem.at[s]).start(
                    priority=(u * top_k + k) % DMA_THREADS)

    def request(ids_ref, s):
        def body(g, carry):
            issue_group(ids_ref, s, g)
            return carry
        lax.fori_loop(0, tokens // COMBINE_GROUP, body, 0)

    @pl.when(i == 0)
    def _():
        request(idx_ref, 0)

        @pl.when(n > 1)
        def _():
            request(idx1_ref, 1)

    for k in range(top_k):
        pltpu.make_async_copy(y_ref.at[pl.ds(0, tokens * chunks), :], buf_ref.at[slot, k],
                              sem.at[slot]).wait()

    def combine_group(g):
        first = pl.multiple_of(g * COMBINE_GROUP, COMBINE_GROUP)
        rows = pl.ds(first, COMBINE_GROUP)
        out = x_ref[rows, :]
        for k in range(top_k):
            view = buf_ref.at[slot, k]
            y = jnp.concatenate([view[pl.ds(first * chunks + c, COMBINE_GROUP, stride=chunks), :]
                                 for c in range(chunks)], axis=1)
            out = out + g_ref[rows, k:k + 1] * y
        o_ref[rows, :] = out

    @pl.when(i + Y_RING - 1 < n)
    def _():
        ahead = lax.rem(i + Y_RING - 1, Y_RING)

        def body(g, carry):
            issue_group(idx2_ref, ahead, g)
            combine_group(g)
            return carry
        lax.fori_loop(0, tokens // COMBINE_GROUP, body, 0)

    @pl.when(i + Y_RING - 1 >= n)
    def _():
        def body(g, carry):
            combine_group(g)
            return carry
        lax.fori_loop(0, tokens // COMBINE_GROUP, body, 0)


def _combine(x2, gates, y_buf, dest, *, tm):
    T, D = x2.shape
    chunks = D // LANES
    nt = dest.shape[0]
    top_k = dest.shape[2] // tm
    idx_spec = lambda step: pl.BlockSpec(
        (1, 1, tm * top_k), lambda i: (jnp.minimum(i + step, nt - 1), 0, 0), memory_space=pltpu.SMEM)
    return pl.pallas_call(
        functools.partial(_combine_kernel, top_k=top_k, chunks=chunks),
        grid=(nt,),
        in_specs=[idx_spec(0), idx_spec(1), idx_spec(2),
                  pl.BlockSpec((tm, D), lambda i: (i, 0)),
                  pl.BlockSpec((tm, LANES), lambda i: (i, 0)),
                  pl.BlockSpec(memory_space=pl.ANY)],
        out_specs=pl.BlockSpec((tm, D), lambda i: (i, 0)),
        out_shape=jax.ShapeDtypeStruct((T, D), F32),
        scratch_shapes=[pltpu.VMEM((Y_RING, top_k, tm * chunks, LANES), F32),
                        pltpu.SemaphoreType.DMA((Y_RING,))],
        compiler_params=pltpu.CompilerParams(
            dimension_semantics=("arbitrary",), vmem_limit_bytes=VMEM_LIMIT),
        name="combine",
    )(dest, dest, dest, x2, gates, y_buf)


def _dispatch(experts, n_experts, blk):
    T, K = experts.shape
    A = T * K
    e_flat = experts.reshape(A)
    g = _pick_tile(A, RANK_GROUP)
    onehot = (e_flat.reshape(A // g, g, 1) == jnp.arange(n_experts, dtype=jnp.int32)).astype(BF16)
    tri = jnp.tril(jnp.ones((g, g), BF16))
    within = jnp.einsum('ij,gje->gie', tri, onehot, preferred_element_type=F32)
    group_counts = within[:, -1, :]
    before = jnp.cumsum(group_counts, axis=0) - group_counts
    counts = jnp.sum(group_counts, axis=0).astype(jnp.int32)
    padded = (counts + blk - 1) // blk * blk
    pends = jnp.cumsum(padded)
    pstarts = pends - padded
    row = within + before[:, None, :] - 1.0 + pstarts.astype(F32)
    dest = jnp.sum(onehot.astype(F32) * row, axis=-1).astype(jnp.int32).reshape(A)
    n_blocks = (A + n_experts * blk) // blk
    blk_start = jnp.arange(n_blocks, dtype=jnp.int32) * blk
    last_expert = jnp.max(jnp.where(counts > 0, jnp.arange(n_experts, dtype=jnp.int32), 0))
    blk_expert = jnp.minimum(jnp.sum(blk_start[:, None] >= pends[None, :], axis=1), last_expert)
    last_of_segment = jnp.any((blk_start[:, None] + blk == pends[None, :]) & (padded[None, :] > 0),
                              axis=1)
    zero_blocks = (last_of_segment | (blk_start >= pends[-1])).astype(jnp.int32)
    n_used = (pends[-1:] // blk).astype(jnp.int32)
    return dest.reshape(T, K), blk_expert.astype(jnp.int32), zero_blocks, n_used


class _Tiles(NamedTuple):
    rows: int
    q_rows: int
    expert_rows: int
    scatter_rows: int
    combine_rows: int


def _tiles(seq, tokens):
    return _Tiles(rows=_pick_tile(seq, 1024), q_rows=_pick_tile(seq, 1024), expert_rows=512,
                  scatter_rows=_pick_tile(tokens, 1024), combine_rows=_pick_tile(tokens, 512))


def kernel(x, mem, norm_mix_g, w_in, b_forget, conv_w, conv_b, lru_wa, lru_ba, lru_wx, lru_bx,
           lru_a_param, fox_q_g, fox_k_g, lru_out_g, fox_out_g, w_out, norm_mem_x_g, norm_mem_g,
           mem_wq, mem_wkv, mem_q_g, mem_k_g, mem_wo, norm_ffn_g, router_group_w, router_group_b,
           router_expert_w, router_expert_b, exp_w_gate, exp_w_up, exp_w_down):
    B, S, D = x.shape
    depth = norm_mix_g.shape[0]
    lw = conv_w.shape[-1]
    n_heads = b_forget.shape[-1]
    dh = fox_q_g.shape[-1]
    mem_dh = mem_q_g.shape[-1]
    mem_heads = mem_wq.shape[-1] // mem_dh
    n_experts = router_expert_w.shape[-1]
    T = B * S
    tm, tq, blk, scatter_tm, combine_tm = _tiles(S, T)
    assert D % (2 * LANES) == 0 and lw % MXU_DIM == 0 and MXU_DIM % lru_wa.shape[-1] == 0
    assert n_heads % 2 == 0 and mem_dh % LANES == 0 and mem.shape[1] % SUBLANES == 0
    assert router_group_w.shape[-1] + n_experts <= LANES and n_experts % router_group_w.shape[-1] == 0
    assert tm % SUBLANES == 0 and tq % MXU_DIM == 0 and (T * TOP_K) % blk == 0

    for l in range(depth):
        y_lru, q, k, v = _in_proj(
            x, norm_mix_g[l], w_in[l], b_forget[l], fox_q_g[l], fox_k_g[l], conv_w[l], conv_b[l],
            lru_wa[l], lru_ba[l], lru_wx[l], lru_bx[l], lru_a_param[l], lru_out_g[l],
            lw=lw, n_heads=n_heads, dh=dh, tm=tm)
        y_fox = _fox_attn(q, k, v, n_heads=n_heads, dh=dh, tq=tq)
        k_mem, v_mem = _mem_kv(mem, norm_mem_g[l], mem_wkv[l], mem_k_g[l], n_heads=mem_heads,
                               dh=mem_dh)
        x2, xn, gates, experts = _out_mem(
            x, y_lru, y_fox, fox_out_g[l], w_out[l], norm_mem_x_g[l], mem_wq[l], mem_q_g[l], k_mem,
            v_mem, mem_wo[l], norm_ffn_g[l], router_group_w[l], router_group_b[l],
            router_expert_w[l], router_expert_b[l], n_heads=mem_heads, dh=mem_dh, tm=tm)

        dest, blk_expert, zero_blocks, n_used = _dispatch(
            experts.reshape(T, LANES)[:, :TOP_K], n_experts, blk)
        chunks = D // LANES
        tiled = lambda t: (dest * chunks).reshape(T // t, 1, t * TOP_K)
        x_buf = _scatter_rows(xn, tiled(scatter_tm), zero_blocks, tm=scatter_tm, blk=blk,
                              chunks=chunks)
        y_buf = _experts(x_buf, blk_expert, n_used, exp_w_gate[l], exp_w_up[l], exp_w_down[l],
                         blk=blk)
        x = _combine(x2.reshape(T, D), gates.reshape(T, LANES), y_buf, tiled(combine_tm),
                     tm=combine_tm)
        x = x.reshape(B, S, D)
    return x
```

```python
import functools
from typing import NamedTuple

import jax
import jax.numpy as jnp
import numpy as np
from jax import lax
from jax.experimental import pallas as pl
from jax.experimental.pallas import tpu as pltpu

EPS = 1e-6
LRU_C = 8.0
CONV_WIDTH = 4
TOP_K = 2
LANES = 128
SUBLANES = 8
MXU_DIM = 256
HEAD_SLOT = LANES
VMEM_LIMIT = 56 * 1024 * 1024

F32 = jnp.float32
BF16 = jnp.bfloat16
MIN_NORMAL = float(np.finfo(np.float32).tiny)


def _dot(a, b):
    return jnp.dot(a, b, preferred_element_type=F32)


def _rms(x, g):
    return x * lax.rsqrt(jnp.mean(x * x, axis=-1, keepdims=True) + EPS) * g


def _softplus(x):
    return jnp.maximum(x, 0.0) + jnp.log1p(jnp.exp(-jnp.abs(x)))


def _pick_tile(n, target):
    t = min(n, target)
    while n % t:
        t //= 2
    return t


def _store_chunk_rows(ref, val):
    n, d = val.shape
    chunks = d // LANES
    for c in range(chunks):
        ref[pl.ds(c, n, stride=chunks), :] = val[:, c * LANES:(c + 1) * LANES]


def _load_chunk_rows(ref, n, d):
    chunks = d // LANES
    return jnp.concatenate([ref[pl.ds(c, n, stride=chunks), :] for c in range(chunks)], axis=1)


def _in_proj_kernel(x_ref, g_ref, w_ref, wf_ref, bf_ref, gq_ref, gk_ref, pe_ref, cv_ref,
                    cw_ref, cb_ref, wl_ref, ba_ref, bx_ref, ap_ref, og_ref,
                    ylru_ref, q_ref, k_ref, v_ref, carry_ref, tail_ref, h_ref, *, lw, n_heads, dh):
    @pl.when(pl.program_id(1) == 0)
    def _():
        carry_ref[...] = jnp.zeros_like(carry_ref)
        tail_ref[...] = jnp.zeros_like(tail_ref)
        h_ref[...] = jnp.zeros_like(h_ref)

    hs = n_heads * HEAD_SLOT
    x = x_ref[0]
    tm = x.shape[0]
    hb = _rms(x, g_ref[...]).astype(BF16)

    u = _dot(hb, w_ref[:, 0:lw])
    gate = _dot(hb, w_ref[:, lw:2 * lw])
    ylru_ref[0] = _rg_lru_tile(u, gate, cw_ref, cb_ref, wl_ref, ba_ref, bx_ref, ap_ref, og_ref,
                               tail_ref, h_ref).astype(ylru_ref.dtype)

    z = _dot(hb, wf_ref[...]) + bf_ref[...]
    lane = lax.broadcasted_iota(jnp.int32, z.shape, 1)
    row = lax.broadcasted_iota(jnp.int32, z.shape, 0)
    c = jnp.where(lane < n_heads, -_softplus(-z), 0.0)
    d = 1
    while d < tm:
        c = c + jnp.where(row >= d, pltpu.roll(c, d, 0), 0.0)
        d *= 2
    c = c + carry_ref[...]
    carry_ref[...] = c[tm - 1:tm, :]
    c1 = c.astype(BF16).astype(F32)
    r1 = c - c1
    c2 = r1.astype(BF16).astype(F32)
    c3 = (r1 - c2).astype(BF16).astype(F32)
    e = c1 + pltpu.roll(c2, n_heads, 1) + pltpu.roll(c3, 2 * n_heads, 1)
    e = jnp.where(lane == 3 * n_heads, 1.0, e).astype(BF16)
    ext = _dot(e, pe_ref[...])

    fw = n_heads * dh
    lane_t = lax.broadcasted_iota(jnp.int32, (tm, LANES), 1)
    own = (lane_t < dh, lane_t >= dh)

    def to_slots(y):
        parts = []
        for c in range(fw // LANES):
            col = y[:, c * LANES:(c + 1) * LANES]
            parts += [jnp.where(own[0], col, 0.0), jnp.where(own[1], col, 0.0)]
        return jnp.concatenate(parts, axis=-1)

    def head_norm(y, gain):
        parts = []
        for h in range(n_heads):
            blk = y[:, h * HEAD_SLOT:(h + 1) * HEAD_SLOT]
            ss = jnp.sum(blk * blk, axis=-1, keepdims=True) * (1.0 / dh)
            parts.append(blk * lax.rsqrt(ss + EPS))
        return jnp.concatenate(parts, axis=-1) * gain

    q = to_slots(_dot(hb, w_ref[:, 2 * lw:2 * lw + fw]))
    q_ref[0] = (head_norm(q, gq_ref[...]) + ext[:, 0:hs]).astype(BF16)
    k = to_slots(_dot(hb, w_ref[:, 2 * lw + fw:2 * lw + 2 * fw]))
    k_ref[0] = (head_norm(k, gk_ref[...]) + ext[:, hs:2 * hs]).astype(BF16)
    v = to_slots(_dot(hb, w_ref[:, 2 * lw + 2 * fw:2 * lw + 3 * fw]))
    v_ref[0] = (v + cv_ref[...]).astype(BF16)


def _in_proj(x, norm_g, w_in, b_forget, fox_q_g, fox_k_g, conv_w, conv_b, wa, ba, wx, bx, a_param,
             lru_out_g, *, lw, n_heads, dh, tm):
    B, S, D = x.shape
    fw = n_heads * dh
    hs = n_heads * HEAD_SLOT
    assert 2 * dh == HEAD_SLOT and fw % LANES == 0 and 3 * n_heads <= LANES
    n_main = 2 * lw + 3 * fw
    w_all = w_in[:, :n_main].astype(BF16)
    w_f = jnp.pad(w_in[:, n_main:], ((0, 0), (0, LANES - n_heads))).astype(BF16)
    b_f = jnp.pad(b_forget, (0, LANES - n_heads)).reshape(1, LANES)
    scale = dh ** -0.5

    def slot_gains(g):
        even = jnp.pad(g, (0, HEAD_SLOT - dh))
        odd = jnp.pad(g, (HEAD_SLOT - dh, 0))
        return jnp.tile(jnp.concatenate([even, odd]), n_heads // 2).reshape(1, hs)

    gq = slot_gains(fox_q_g * scale)
    gk = slot_gains(fox_k_g)
    ones_row = 3 * n_heads
    pe = np.zeros((LANES, 2 * hs), np.float32)
    cv = np.zeros((1, hs), np.float32)
    for h in range(n_heads):
        x0 = h * HEAD_SLOT + (dh if h % 2 == 0 else 0)
        for j in range(3):
            pe[j * n_heads + h, x0 + j] = 1.0
            pe[ones_row, x0 + 3 + j] = 1.0
            pe[ones_row, hs + x0 + j] = 1.0
            pe[j * n_heads + h, hs + x0 + 3 + j] = -1.0
        cv[0, x0] = 1.0
    pe = jnp.asarray(pe, BF16)

    w_lru = _lru_gate_weights(wa, wx, lw)
    vec = lambda a: a.reshape(1, lw)
    full = lambda shape: pl.BlockSpec(shape, lambda b, s: (0,) * len(shape))
    row = lambda width: pl.BlockSpec((1, tm, width), lambda b, s: (b, s, 0))
    return pl.pallas_call(
        functools.partial(_in_proj_kernel, lw=lw, n_heads=n_heads, dh=dh),
        grid=(B, S // tm),
        in_specs=[row(D), full((1, D)), full(w_all.shape), full(w_f.shape), full((1, LANES)),
                  full((1, hs)), full((1, hs)), full(pe.shape), full((1, hs)),
                  full((CONV_WIDTH, lw)), full((1, lw)), full(w_lru.shape), full((1, lw)),
                  full((1, lw)), full((1, lw)), full((1, lw))],
        out_specs=[row(lw), row(hs), row(hs), row(hs)],
        out_shape=[jax.ShapeDtypeStruct((B, S, lw), BF16),
                   jax.ShapeDtypeStruct((B, S, hs), BF16), jax.ShapeDtypeStruct((B, S, hs), BF16),
                   jax.ShapeDtypeStruct((B, S, hs), BF16)],
        scratch_shapes=[pltpu.VMEM((1, LANES), F32), pltpu.VMEM((SUBLANES, lw), F32),
                        pltpu.VMEM((1, lw), F32)],
        compiler_params=pltpu.CompilerParams(
            dimension_semantics=("parallel", "arbitrary"), vmem_limit_bytes=VMEM_LIMIT),
        name="in_proj",
    )(x, norm_g.reshape(1, D), w_all, w_f, b_f, gq, gk, pe, jnp.asarray(cv),
      conv_w, vec(conv_b), w_lru, vec(ba), vec(bx), vec(a_param), vec(lru_out_g))


def _rg_lru_tile(u, gate, cw_ref, cb_ref, w_ref, ba_ref, bx_ref, ap_ref, og_ref, tail_ref, h_ref):
    ts, lw = u.shape
    ext = jnp.concatenate([tail_ref[...], u], axis=0)
    tail_ref[...] = u[ts - SUBLANES:, :]
    xc = cb_ref[...] + u * cw_ref[CONV_WIDTH - 1:CONV_WIDTH, :]
    for back in range(1, CONV_WIDTH):
        tap = CONV_WIDTH - 1 - back
        xc = xc + pltpu.roll(ext, back, 0)[SUBLANES:, :] * cw_ref[tap:tap + 1, :]

    xb = xc.astype(BF16)
    r_parts, i_parts = [], []
    for j in range(lw // MXU_DIM):
        y = _dot(xb[:, j * MXU_DIM:(j + 1) * MXU_DIM], w_ref[j])
        r_parts.append(y[:, :MXU_DIM])
        i_parts.append(y[:, MXU_DIM:])
    r = jax.nn.sigmoid(jnp.concatenate(r_parts, axis=-1) + ba_ref[...])
    i = jax.nn.sigmoid(jnp.concatenate(i_parts, axis=-1) + bx_ref[...])
    log_a = (-LRU_C) * r * _softplus(-ap_ref[...])
    a = jnp.exp(log_a)
    gap = jnp.maximum(-jnp.tanh(log_a) * (a * a + 1.0), 0.0)
    mult = gap * lax.rsqrt(jnp.maximum(gap, MIN_NORMAL))
    b = mult * (i * xc)

    groups = ts // SUBLANES
    a = a.reshape(groups, SUBLANES, lw)
    b = b.reshape(groups, SUBLANES, lw)
    row = lax.broadcasted_iota(jnp.int32, a.shape, 1)
    d = 1
    while d < SUBLANES:
        keep = row >= d
        a_prev = jnp.where(keep, pltpu.roll(a, d, 1), 1.0)
        b_prev = jnp.where(keep, pltpu.roll(b, d, 1), 0.0)
        b = a * b_prev + b
        a = a * a_prev
        d *= 2
    state = h_ref[...]
    parts = []
    for g in range(groups):
        hg = b[g] + a[g] * state
        state = hg[SUBLANES - 1:, :]
        parts.append(hg)
    h = jnp.concatenate(parts, axis=0)
    h_ref[...] = state

    y = h * jax.nn.gelu(gate)
    return _rms(y, og_ref[...])


def _lru_gate_weights(wa, wx, lw):
    _, bd, _ = wa.shape
    per = MXU_DIM // bd
    n_tiles = lw // MXU_DIM

    def tiles(w):
        w = w.reshape(n_tiles, per, bd, bd)
        eye = jnp.eye(per, dtype=w.dtype)
        return jnp.einsum('tpij,pq->tpiqj', w, eye).reshape(n_tiles, MXU_DIM, MXU_DIM)

    return jnp.concatenate([tiles(wa), tiles(wx)], axis=-1).astype(BF16)


OVERFLOW_GUARD = 1e30


def _fox_attn_kernel(q_ref, k_ref, v_ref, o_ref, vt_ref, m_ref, acc_ref, *, dh, tq):
    n_chunks = vt_ref.shape[0]
    for c in range(n_chunks):
        vt_ref[c] = v_ref[0, c * tq:(c + 1) * tq, :].astype(F32).T.astype(BF16)

    heads = [slice(hh * HEAD_SLOT, (hh + 1) * HEAD_SLOT) for hh in range(2)]
    lax.fori_loop(0, n_chunks, functools.partial(
        _fox_attn_tile, q_ref=q_ref, k_ref=k_ref, o_ref=o_ref, vt_ref=vt_ref, m_ref=m_ref,
        acc_ref=acc_ref, heads=heads, dh=dh, tq=tq), 0)


def _fox_attn_tile(qi, carry, *, q_ref, k_ref, o_ref, vt_ref, m_ref, acc_ref, heads, dh, tq):
    q_rows = pl.ds(pl.multiple_of(qi * tq, tq), tq)

    def scores(j, q, sl, masked):
        start = pl.multiple_of(j * tq, tq)
        st = lax.dot_general(k_ref[0, pl.ds(start, tq), sl], q, (((1,), (1,)), ((), ())),
                             preferred_element_type=F32)
        if masked:
            kpos = lax.broadcasted_iota(jnp.int32, st.shape, 0)
            qpos = lax.broadcasted_iota(jnp.int32, st.shape, 1)
            st = jnp.where(kpos <= qpos, st, -jnp.inf)
        return st

    half = tq // 2
    diag = pl.multiple_of(qi * tq, tq)
    for hh, sl in enumerate(heads):
        q = q_ref[0, q_rows, sl]
        nt = (((1,), (1,)), ((), ()))
        st_a = lax.dot_general(k_ref[0, pl.ds(diag, half), sl], q, nt, preferred_element_type=F32)
        st_b = lax.dot_general(k_ref[0, pl.ds(diag + half, half), sl], q[half:, :], nt,
                               preferred_element_type=F32)
        causal = lambda st: jnp.where(lax.broadcasted_iota(jnp.int32, st.shape, 0)
                                      <= lax.broadcasted_iota(jnp.int32, st.shape, 1), st, -jnp.inf)
        st_a, st_b = causal(st_a), causal(st_b)
        m_a = jnp.max(st_a, axis=0, keepdims=True)
        m = jnp.concatenate([m_a[:, :half],
                             jnp.maximum(m_a[:, half:], jnp.max(st_b, axis=0, keepdims=True))], axis=1)
        m_ref[hh] = m
        vt = vt_ref[qi, sl, :]
        acc_a = _dot(vt[:, :half], jnp.exp(st_a - m).astype(BF16))
        acc_b = _dot(vt[:, half:], jnp.exp(st_b - m[:, half:]).astype(BF16))
        acc_ref[hh] = acc_a + jnp.concatenate([jnp.zeros_like(acc_b), acc_b], axis=1)

    def fast_chunk(j):
        for hh, sl in enumerate(heads):
            pt = jnp.exp(scores(j, q_ref[0, q_rows, sl], sl, False) - m_ref[hh]).astype(BF16)
            acc_ref[hh] += _dot(vt_ref[j, sl, :], pt)

    def fast_pair(i, carry):
        start = pl.multiple_of(2 * i * tq, 2 * tq)
        for hh, sl in enumerate(heads):
            st = lax.dot_general(k_ref[0, pl.ds(start, 2 * tq), sl], q_ref[0, q_rows, sl],
                                 (((1,), (1,)), ((), ())), preferred_element_type=F32)
            pt = jnp.exp(st - m_ref[hh]).astype(BF16)
            vt = jnp.concatenate([vt_ref[2 * i, sl, :], vt_ref[2 * i + 1, sl, :]], axis=1)
            acc_ref[hh] += _dot(vt, pt)
        return carry

    lax.fori_loop(0, qi // 2, fast_pair, 0)

    @pl.when(qi % 2 == 1)
    def _():
        fast_chunk(qi - 1)

    def softmax_sums():
        return jnp.concatenate([acc_ref[0][dh:dh + 1, :], acc_ref[1][0:1, :]], axis=0)

    def normalised():
        sums = softmax_sums()
        return jnp.concatenate([acc_ref[0][0:dh, :] / sums[0:1, :],
                                acc_ref[1][dh:2 * dh, :] / sums[1:2, :]], axis=0)

    out = normalised()
    o_ref[0, q_rows, :] = out.T.astype(o_ref.dtype)
    sums = softmax_sums()
    flag = lambda x: jnp.max(jnp.where(jnp.abs(x) < OVERFLOW_GUARD, 0.0, 1.0),
                             axis=0, keepdims=True)
    bad = jnp.max(jnp.maximum(flag(out), flag(sums)), axis=1, keepdims=True)

    @pl.when(bad[0, 0] > 0.0)
    def _():
        m_ref[...] = jnp.full_like(m_ref, -jnp.inf)
        acc_ref[...] = jnp.zeros_like(acc_ref)

        def chunk(j, masked):
            for hh, sl in enumerate(heads):
                st = scores(j, q_ref[0, q_rows, sl], sl, masked)
                m_old = m_ref[hh]
                m_new = jnp.maximum(m_old, jnp.max(st, axis=0, keepdims=True))
                pt = jnp.exp(st - m_new).astype(BF16)
                acc_ref[hh] = jnp.exp(m_old - m_new) * acc_ref[hh] + _dot(vt_ref[j, sl, :], pt)
                m_ref[hh] = m_new

        def body(j, carry):
            chunk(j, False)
            return carry

        lax.fori_loop(0, qi, body, 0)
        chunk(qi, True)
        o_ref[0, q_rows, :] = normalised().T.astype(o_ref.dtype)

    return carry


def _fox_attn(q, k, v, *, n_heads, dh, tq):
    B, S, _ = q.shape
    nq = S // tq
    pair = 2 * HEAD_SLOT
    qkv_spec = pl.BlockSpec((1, S, pair), lambda b, h: (b, 0, h))
    return pl.pallas_call(
        functools.partial(_fox_attn_kernel, dh=dh, tq=tq),
        grid=(B, n_heads // 2),
        in_specs=[qkv_spec, qkv_spec, qkv_spec],
        out_specs=pl.BlockSpec((1, S, 2 * dh), lambda b, h: (b, 0, h)),
        out_shape=jax.ShapeDtypeStruct((B, S, n_heads * dh), BF16),
        scratch_shapes=[pltpu.VMEM((nq, pair, tq), BF16), pltpu.VMEM((2, 1, tq), F32),
                        pltpu.VMEM((2, HEAD_SLOT, tq), F32)],
        compiler_params=pltpu.CompilerParams(
            dimension_semantics=("parallel", "parallel"), vmem_limit_bytes=VMEM_LIMIT),
        name="fox_attn",
    )(q, k, v)


def _mem_kv_kernel(m_ref, g_ref, w_ref, kg_ref, k_ref, v_ref, *, n_heads, dh):
    mw = n_heads * dh
    mb = _rms(m_ref[0], g_ref[...]).astype(BF16)
    kv = _dot(mb, w_ref[...])
    parts = []
    for h in range(n_heads):
        parts.append(_rms(kv[:, h * dh:(h + 1) * dh], kg_ref[...]))
    k_ref[0] = jnp.concatenate(parts, axis=-1).astype(BF16)
    v_ref[0] = kv[:, mw:].astype(BF16)


def _mem_kv(mem, norm_g, wkv, k_g, *, n_heads, dh):
    B, M, D = mem.shape
    mw = n_heads * dh
    full = lambda shape: pl.BlockSpec(shape, lambda b: (0,) * len(shape))
    out = pl.BlockSpec((1, M, mw), lambda b: (b, 0, 0))
    return pl.pallas_call(
        functools.partial(_mem_kv_kernel, n_heads=n_heads, dh=dh),
        grid=(B,),
        in_specs=[pl.BlockSpec((1, M, D), lambda b: (b, 0, 0)), full((1, D)), full((D, 2 * mw)),
                  full((1, dh))],
        out_specs=[out, out],
        out_shape=[jax.ShapeDtypeStruct((B, M, mw), BF16)] * 2,
        compiler_params=pltpu.CompilerParams(
            dimension_semantics=("parallel",), vmem_limit_bytes=VMEM_LIMIT),
        name="mem_kv",
    )(mem, norm_g.reshape(1, D), wkv.astype(BF16), k_g.reshape(1, dh))


def _out_mem_kernel(x_ref, yl_ref, yf_ref, fg_ref, wol_ref, wof_ref, gx_ref, wq_ref, qg_ref,
                    km_ref, vm_ref, wo_ref, gf_ref, wr_ref, br_ref,
                    x2_ref, xn_ref, gates_ref, experts_ref, *, n_heads, dh, n_groups, per_group):
    yf = _rms(yf_ref[0].astype(F32), fg_ref[...]).astype(BF16)
    x1 = x_ref[0] + _dot(yl_ref[0], wol_ref[...]) + _dot(yf, wof_ref[...])

    q = _dot(_rms(x1, gx_ref[...]).astype(BF16), wq_ref[...])
    outs = []
    for h in range(n_heads):
        sl = slice(h * dh, (h + 1) * dh)
        qh = _rms(q[:, sl], qg_ref[...]).astype(BF16)
        s = lax.dot_general(qh, km_ref[0, :, sl], (((1,), (1,)), ((), ())),
                            preferred_element_type=F32)
        p = jnp.exp(s - jnp.max(s, axis=-1, keepdims=True))
        inv = 1.0 / jnp.sum(p, axis=-1, keepdims=True)
        outs.append(_dot(p.astype(BF16), vm_ref[0, :, sl]) * inv)
    x2 = x1 + _dot(jnp.concatenate(outs, axis=-1).astype(BF16), wo_ref[...])
    x2_ref[0] = x2

    xn = _rms(x2, gf_ref[...])
    _store_chunk_rows(xn_ref, xn)
    logits = _dot(xn.astype(BF16), wr_ref[...]) + br_ref[...]
    lane = lax.broadcasted_iota(jnp.int32, logits.shape, 1).astype(F32)
    neg = -jnp.inf

    def top(vals):
        mx = jnp.max(vals, axis=-1, keepdims=True)
        idx = jnp.min(jnp.where(vals == mx, lane, float(LANES)), axis=-1, keepdims=True)
        return mx, idx

    gl = jnp.where(lane < n_groups, logits, neg)
    g_max, g_idx = top(gl)
    g_w = 1.0 / jnp.sum(jnp.exp(gl - g_max), axis=-1, keepdims=True)
    lo = n_groups + per_group * g_idx
    el = jnp.where((lane >= lo) & (lane < lo + per_group), logits, neg)
    e1, i1 = top(el)
    e2, i2 = top(jnp.where(lane == i1, neg, el))
    t = jnp.exp(e2 - e1)
    w1 = g_w / (1.0 + t)
    w2 = g_w * t / (1.0 + t)
    gates_ref[0] = jnp.where(lane == 0, w1, jnp.where(lane == 1, w2, 0.0))
    experts_ref[0] = jnp.where(lane == 0, i1 - n_groups,
                               jnp.where(lane == 1, i2 - n_groups, 0.0)).astype(jnp.int32)


def _out_mem(x, y_lru, y_fox, fox_out_g, w_out, norm_mem_x_g, mem_wq, mem_q_g, k_mem, v_mem, mem_wo,
             norm_ffn_g, router_group_w, router_group_b, router_expert_w, router_expert_b, *,
             n_heads, dh, tm):
    B, S, D = x.shape
    lw = y_lru.shape[-1]
    fw = y_fox.shape[-1]
    M = k_mem.shape[1]
    mw = n_heads * dh
    n_groups = router_group_w.shape[-1]
    n_experts = router_expert_w.shape[-1]
    w_r = jnp.concatenate([router_group_w, router_expert_w], axis=1)
    w_r = jnp.pad(w_r, ((0, 0), (0, LANES - w_r.shape[1]))).astype(BF16)
    b_r = jnp.concatenate([router_group_b, router_expert_b])
    b_r = jnp.pad(b_r, (0, LANES - b_r.shape[0])).reshape(1, LANES)
    q_gain = (mem_q_g * dh ** -0.5).reshape(1, dh)

    full = lambda shape: pl.BlockSpec(shape, lambda b, s: (0,) * len(shape))
    row = lambda width: pl.BlockSpec((1, tm, width), lambda b, s: (b, s, 0))
    mem_spec = pl.BlockSpec((1, M, mw), lambda b, s: (b, 0, 0))
    return pl.pallas_call(
        functools.partial(_out_mem_kernel, n_heads=n_heads, dh=dh, n_groups=n_groups,
                          per_group=n_experts // n_groups),
        grid=(B, S // tm),
        in_specs=[row(D), row(lw), row(fw), full((1, fw)), full((lw, D)), full((fw, D)),
                  full((1, D)), full((D, mw)), full((1, dh)), mem_spec, mem_spec, full((mw, D)),
                  full((1, D)), full((D, LANES)), full((1, LANES))],
        out_specs=[row(D), pl.BlockSpec((tm * D // LANES, LANES), lambda b, s: (b * (S // tm) + s, 0)),
                   row(LANES), row(LANES)],
        out_shape=[jax.ShapeDtypeStruct((B, S, D), F32),
                   jax.ShapeDtypeStruct((B * S * D // LANES, LANES), F32),
                   jax.ShapeDtypeStruct((B, S, LANES), F32),
                   jax.ShapeDtypeStruct((B, S, LANES), jnp.int32)],
        compiler_params=pltpu.CompilerParams(
            dimension_semantics=("parallel", "parallel"), vmem_limit_bytes=VMEM_LIMIT),
        name="out_mem",
    )(x, y_lru, y_fox, fox_out_g.reshape(1, fw), w_out[:lw].astype(BF16), w_out[lw:].astype(BF16),
      norm_mem_x_g.reshape(1, D), mem_wq.astype(BF16), q_gain, k_mem, v_mem, mem_wo.astype(BF16),
      norm_ffn_g.reshape(1, D), w_r, b_r)


DMA_UNROLL = 16
DMA_THREADS = 2
RANK_GROUP = 256


def _scatter_rows_kernel(zb_ref, idx_ref, x_ref, buf_ref, zero_ref, sem, zsem, *, top_k, chunks):
    tokens = x_ref.shape[0] // chunks

    @pl.when(pl.program_id(0) == 0)
    def _():
        rows = zero_ref.shape[0]
        zero_ref[...] = jnp.zeros_like(zero_ref)

        def copy(j):
            start = pl.multiple_of(j * rows, rows)
            return pltpu.make_async_copy(zero_ref, buf_ref.at[pl.ds(start, rows), :], zsem)

        def issue_zero(j, carry):
            @pl.when(zb_ref[j] == 1)
            def _():
                copy(j).start()
            return carry

        def drain_zero(j, carry):
            @pl.when(zb_ref[j] == 1)
            def _():
                copy(j).wait()
            return carry

        lax.fori_loop(0, zb_ref.shape[0], issue_zero, 0)
        lax.fori_loop(0, zb_ref.shape[0], drain_zero, 0)

    def issue(g, carry):
        for u in range(DMA_UNROLL):
            r = g * DMA_UNROLL + u
            src = x_ref.at[pl.ds(pl.multiple_of(r * chunks, chunks), chunks), :]
            for k in range(top_k):
                row = pl.multiple_of(idx_ref[0, 0, r * top_k + k], chunks)
                pltpu.make_async_copy(src, buf_ref.at[pl.ds(row, chunks), :], sem).start(
                    priority=(u * top_k + k) % DMA_THREADS)
        return carry

    lax.fori_loop(0, tokens // DMA_UNROLL, issue, 0)
    for k in range(top_k):
        pltpu.make_async_copy(x_ref, buf_ref.at[pl.ds(0, tokens * chunks), :], sem).wait()


def _scatter_rows(x, dest, zero_blocks, *, tm, blk, chunks):
    nt = dest.shape[0]
    top_k = dest.shape[2] // tm
    n_blocks = zero_blocks.shape[0]
    grid_spec = pltpu.PrefetchScalarGridSpec(
        num_scalar_prefetch=1,
        grid=(nt,),
        in_specs=[pl.BlockSpec((1, 1, tm * top_k), lambda i, zb: (i, 0, 0), memory_space=pltpu.SMEM),
                  pl.BlockSpec((tm * chunks, LANES), lambda i, zb: (i, 0))],
        out_specs=pl.BlockSpec(memory_space=pl.ANY),
        scratch_shapes=[pltpu.VMEM((blk * chunks, LANES), x.dtype), pltpu.SemaphoreType.DMA,
                        pltpu.SemaphoreType.DMA],
    )
    return pl.pallas_call(
        functools.partial(_scatter_rows_kernel, top_k=top_k, chunks=chunks),
        grid_spec=grid_spec,
        out_shape=jax.ShapeDtypeStruct((n_blocks * blk * chunks, LANES), x.dtype),
        compiler_params=pltpu.CompilerParams(
            dimension_semantics=("arbitrary",), vmem_limit_bytes=VMEM_LIMIT),
        name="scatter_rows",
    )(zero_blocks, dest, x)


X_RING = 3
X_PRIORITY = 1


def _experts_kernel(be_ref, nu_ref, x_hbm, wg_ref, wu_ref, wd_ref, o_ref, xbuf_ref, wgb_ref, wub_ref,
                    wdb_ref, sem, *, blk):
    i = pl.program_id(0)
    used = nu_ref[0]
    rows = xbuf_ref.shape[1]

    def fetch(step):
        slot = lax.rem(step, X_RING)
        src = x_hbm.at[pl.ds(pl.multiple_of(step * rows, rows), rows), :]
        return pltpu.make_async_copy(src, xbuf_ref.at[slot], sem.at[slot])

    @pl.when(i == 0)
    def _():
        for step in range(X_RING - 1):
            @pl.when(step < used)
            def _():
                fetch(step).start(priority=X_PRIORITY)

    @pl.when(i + X_RING - 1 < used)
    def _():
        fetch(i + X_RING - 1).start(priority=X_PRIORITY)

    @pl.when((i == 0) | (be_ref[i] != be_ref[jnp.maximum(i - 1, 0)]))
    def _():
        wgb_ref[...] = wg_ref[0].astype(BF16)
        wub_ref[...] = wu_ref[0].astype(BF16)
        wdb_ref[...] = wd_ref[0].astype(BF16)

    @pl.when(i < used)
    def _():
        fetch(i).wait()
        d = wg_ref.shape[1]
        xb = _load_chunk_rows(xbuf_ref.at[lax.rem(i, X_RING)], blk, d).astype(BF16)
        hdn = jax.nn.silu(_dot(xb, wgb_ref[...])) * _dot(xb, wub_ref[...])
        _store_chunk_rows(o_ref, _dot(hdn.astype(BF16), wdb_ref[...]))

    @pl.when(i >= used)
    def _():
        o_ref[...] = jnp.zeros_like(o_ref)


def _experts(x_buf, blk_expert, n_used, w_gate, w_up, w_down, *, blk):
    _, D, de = w_gate.shape
    chunks = D // LANES
    n_blocks = x_buf.shape[0] // (blk * chunks)
    grid_spec = pltpu.PrefetchScalarGridSpec(
        num_scalar_prefetch=2,
        grid=(n_blocks,),
        in_specs=[pl.BlockSpec(memory_space=pl.ANY),
                  pl.BlockSpec((1, D, de), lambda i, be, nu: (be[i], 0, 0)),
                  pl.BlockSpec((1, D, de), lambda i, be, nu: (be[i], 0, 0)),
                  pl.BlockSpec((1, de, D), lambda i, be, nu: (be[i], 0, 0))],
        out_specs=pl.BlockSpec((blk * chunks, LANES), lambda i, be, nu: (i, 0)),
        scratch_shapes=[pltpu.VMEM((X_RING, blk * chunks, LANES), F32),
                        pltpu.VMEM((D, de), BF16), pltpu.VMEM((D, de), BF16),
                        pltpu.VMEM((de, D), BF16), pltpu.SemaphoreType.DMA((X_RING,))],
    )
    return pl.pallas_call(
        functools.partial(_experts_kernel, blk=blk),
        grid_spec=grid_spec,
        out_shape=jax.ShapeDtypeStruct(x_buf.shape, F32),
        compiler_params=pltpu.CompilerParams(
            dimension_semantics=("arbitrary",), vmem_limit_bytes=VMEM_LIMIT),
        name="experts",
    )(blk_expert, n_used, x_buf, w_gate, w_up, w_down)


Y_RING = 3
COMBINE_GROUP = 128


def _combine_kernel(idx_ref, idx1_ref, idx2_ref, x_ref, g_ref, y_ref, o_ref, buf_ref, sem, *, top_k,
                    chunks):
    i = pl.program_id(0)
    n = pl.num_programs(0)
    slot = lax.rem(i, Y_RING)
    tokens, d = x_ref.shape

    def issue_group(ids_ref, s, g):
        for u in range(COMBINE_GROUP):
            r = g * COMBINE_GROUP + u
            for k in range(top_k):
                row = pl.multiple_of(ids_ref[0, 0, r * top_k + k], chunks)
                dst = buf_ref.at[s, k, pl.ds(pl.multiple_of(r * chunks, chunks), chunks), :]
                pltpu.make_async_copy(y_ref.at[pl.ds(row, chunks), :], dst, sem.at[s]).start(
                    priority=(u * top_k + k) % DMA_THREADS)

    def request(ids_ref, s):
        def body(g, carry):
            issue_group(ids_ref, s, g)
            return carry
        lax.fori_loop(0, tokens // COMBINE_GROUP, body, 0)

    @pl.when(i == 0)
    def _():
        request(idx_ref, 0)

        @pl.when(n > 1)
        def _():
            request(idx1_ref, 1)

    for k in range(top_k):
        pltpu.make_async_copy(y_ref.at[pl.ds(0, tokens * chunks), :], buf_ref.at[slot, k],
                              sem.at[slot]).wait()

    def combine_group(g):
        first = pl.multiple_of(g * COMBINE_GROUP, COMBINE_GROUP)
        rows = pl.ds(first, COMBINE_GROUP)
        out = x_ref[rows, :]
        for k in range(top_k):
            view = buf_ref.at[slot, k]
            y = jnp.concatenate([view[pl.ds(first * chunks + c, COMBINE_GROUP, stride=chunks), :]
                                 for c in range(chunks)], axis=1)
            out = out + g_ref[rows, k:k + 1] * y
        o_ref[rows, :] = out

    @pl.when(i + Y_RING - 1 < n)
    def _():
        ahead = lax.rem(i + Y_RING - 1, Y_RING)

        def body(g, carry):
            issue_group(idx2_ref, ahead, g)
            combine_group(g)
            return carry
        lax.fori_loop(0, tokens // COMBINE_GROUP, body, 0)

    @pl.when(i + Y_RING - 1 >= n)
    def _():
        def body(g, carry):
            combine_group(g)
            return carry
        lax.fori_loop(0, tokens // COMBINE_GROUP, body, 0)


def _combine(x2, gates, y_buf, dest, *, tm):
    T, D = x2.shape
    chunks = D // LANES
    nt = dest.shape[0]
    top_k = dest.shape[2] // tm
    idx_spec = lambda step: pl.BlockSpec(
        (1, 1, tm * top_k), lambda i: (jnp.minimum(i + step, nt - 1), 0, 0), memory_space=pltpu.SMEM)
    return pl.pallas_call(
        functools.partial(_combine_kernel, top_k=top_k, chunks=chunks),
        grid=(nt,),
        in_specs=[idx_spec(0), idx_spec(1), idx_spec(2),
                  pl.BlockSpec((tm, D), lambda i: (i, 0)),
                  pl.BlockSpec((tm, LANES), lambda i: (i, 0)),
                  pl.BlockSpec(memory_space=pl.ANY)],
        out_specs=pl.BlockSpec((tm, D), lambda i: (i, 0)),
        out_shape=jax.ShapeDtypeStruct((T, D), F32),
        scratch_shapes=[pltpu.VMEM((Y_RING, top_k, tm * chunks, LANES), F32),
                        pltpu.SemaphoreType.DMA((Y_RING,))],
        compiler_params=pltpu.CompilerParams(
            dimension_semantics=("arbitrary",), vmem_limit_bytes=VMEM_LIMIT),
        name="combine",
    )(dest, dest, dest, x2, gates, y_buf)


def _dispatch(experts, n_experts, blk):
    T, K = experts.shape
    A = T * K
    e_flat = experts.reshape(A)
    g = _pick_tile(A, RANK_GROUP)
    onehot = (e_flat.reshape(A // g, g, 1) == jnp.arange(n_experts, dtype=jnp.int32)).astype(BF16)
    tri = jnp.tril(jnp.ones((g, g), BF16))
    within = jnp.einsum('ij,gje->gie', tri, onehot, preferred_element_type=F32)
    group_counts = within[:, -1, :]
    before = jnp.cumsum(group_counts, axis=0) - group_counts
    counts = jnp.sum(group_counts, axis=0).astype(jnp.int32)
    padded = (counts + blk - 1) // blk * blk
    pends = jnp.cumsum(padded)
    pstarts = pends - padded
    row = within + before[:, None, :] - 1.0 + pstarts.astype(F32)
    dest = jnp.sum(onehot.astype(F32) * row, axis=-1).astype(jnp.int32).reshape(A)
    n_blocks = (A + n_experts * blk) // blk
    blk_start = jnp.arange(n_blocks, dtype=jnp.int32) * blk
    last_expert = jnp.max(jnp.where(counts > 0, jnp.arange(n_experts, dtype=jnp.int32), 0))
    blk_expert = jnp.minimum(jnp.sum(blk_start[:, None] >= pends[None, :], axis=1), last_expert)
    last_of_segment = jnp.any((blk_start[:, None] + blk == pends[None, :]) & (padded[None, :] > 0),
                              axis=1)
    zero_blocks = (last_of_segment | (blk_start >= pends[-1])).astype(jnp.int32)
    n_used = (pends[-1:] // blk).astype(jnp.int32)
    return dest.reshape(T, K), blk_expert.astype(jnp.int32), zero_blocks, n_used


class _Tiles(NamedTuple):
    rows: int
    q_rows: int
    expert_rows: int
    scatter_rows: int
    combine_rows: int


def _tiles(seq, tokens):
    return _Tiles(rows=_pick_tile(seq, 1024), q_rows=_pick_tile(seq, 1024), expert_rows=512,
                  scatter_rows=_pick_tile(tokens, 1024), combine_rows=_pick_tile(tokens, 512))


def kernel(x, mem, norm_mix_g, w_in, b_forget, conv_w, conv_b, lru_wa, lru_ba, lru_wx, lru_bx,
           lru_a_param, fox_q_g, fox_k_g, lru_out_g, fox_out_g, w_out, norm_mem_x_g, norm_mem_g,
           mem_wq, mem_wkv, mem_q_g, mem_k_g, mem_wo, norm_ffn_g, router_group_w, router_group_b,
           router_expert_w, router_expert_b, exp_w_gate, exp_w_up, exp_w_down):
    B, S, D = x.shape
    depth = norm_mix_g.shape[0]
    lw = conv_w.shape[-1]
    n_heads = b_forget.shape[-1]
    dh = fox_q_g.shape[-1]
    mem_dh = mem_q_g.shape[-1]
    mem_heads = mem_wq.shape[-1] // mem_dh
    n_experts = router_expert_w.shape[-1]
    T = B * S
    tm, tq, blk, scatter_tm, combine_tm = _tiles(S, T)
    assert D % (2 * LANES) == 0 and lw % MXU_DIM == 0 and MXU_DIM % lru_wa.shape[-1] == 0
    assert n_heads % 2 == 0 and mem_dh % LANES == 0 and mem.shape[1] % SUBLANES == 0
    assert router_group_w.shape[-1] + n_experts <= LANES and n_experts % router_group_w.shape[-1] == 0
    assert tm % SUBLANES == 0 and tq % MXU_DIM == 0 and (T * TOP_K) % blk == 0

    for l in range(depth):
        y_lru, q, k, v = _in_proj(
            x, norm_mix_g[l], w_in[l], b_forget[l], fox_q_g[l], fox_k_g[l], conv_w[l], conv_b[l],
            lru_wa[l], lru_ba[l], lru_wx[l], lru_bx[l], lru_a_param[l], lru_out_g[l],
            lw=lw, n_heads=n_heads, dh=dh, tm=tm)
        y_fox = _fox_attn(q, k, v, n_heads=n_heads, dh=dh, tq=tq)
        k_mem, v_mem = _mem_kv(mem, norm_mem_g[l], mem_wkv[l], mem_k_g[l], n_heads=mem_heads,
                               dh=mem_dh)
        x2, xn, gates, experts = _out_mem(
            x, y_lru, y_fox, fox_out_g[l], w_out[l], norm_mem_x_g[l], mem_wq[l], mem_q_g[l], k_mem,
            v_mem, mem_wo[l], norm_ffn_g[l], router_group_w[l], router_group_b[l],
            router_expert_w[l], router_expert_b[l], n_heads=mem_heads, dh=mem_dh, tm=tm)

        dest, blk_expert, zero_blocks, n_used = _dispatch(
            experts.reshape(T, LANES)[:, :TOP_K], n_experts, blk)
        chunks = D // LANES
        tiled = lambda t: (dest * chunks).reshape(T // t, 1, t * TOP_K)
        x_buf = _scatter_rows(xn, tiled(scatter_tm), zero_blocks, tm=scatter_tm, blk=blk,
                              chunks=chunks)
        y_buf = _experts(x_buf, blk_expert, n_used, exp_w_gate[l], exp_w_up[l], exp_w_down[l],
                         blk=blk)
        x = _combine(x2.reshape(T, D), gates.reshape(T, LANES), y_buf, tiled(combine_tm),
                     tm=combine_tm)
        x = x.reshape(B, S, D)
    return x
```
